```python
import math
import numpy as np
import jax
import jax.numpy as jnp
from jax import lax

D_MODEL = 2048
BATCH = 4
SEQ = 4096
DEPTH = 2

GRID_W = 64
NA_HEADS = 8
NA_HEAD_DIM = 128
NA_WIN_H = 8
NA_WIN_W = 16
MLA_HEADS = 8
MLA_Q_RANK = 512
MLA_KV_RANK = 256
MLA_NOPE_DIM = 128
MLA_ROPE_DIM = 64
MLA_V_DIM = 128
MLA_BLOCK = 128
RET_HEADS = 8
RET_QK_DIM = 256
RET_V_DIM = 512
RET_CHUNK = 128
N_GROUPS = 4
EXPERTS_PER_GROUP = 8
N_EXPERTS = N_GROUPS * EXPERTS_PER_GROUP
TOP_K = 2
D_EXPERT = 512
MOE_BLOCK = 128
PLE_DIM = 256
ROPE_BASE = 10000.0
LN_EPS = 1e-5
RMS_EPS = 1e-6
DN_ALPHA = (2 * DEPTH) ** 0.25
DN_BETA = (8 * DEPTH) ** -0.25
NA_WIDTH = NA_HEADS * NA_HEAD_DIM
AB_IN = 3 * NA_WIDTH + MLA_Q_RANK + MLA_KV_RANK + MLA_ROPE_DIM
AB_OUT = NA_WIDTH + MLA_HEADS * MLA_V_DIM
RET_QK = RET_HEADS * RET_QK_DIM
RET_V = RET_HEADS * RET_V_DIM
C_IN = 2 * RET_QK + 2 * RET_V
C_OUT = RET_V
N_EVEN = (DEPTH + 1) // 2
N_ODD = DEPTH // 2

kernel_name = 'hybrid_natten_mla_retention_hmoe_encoder'


def layer_norm(x, g, b):
    xf = x.astype(jnp.float32)
    xc = xf - jnp.mean(xf, -1, keepdims=True)
    var = jnp.mean(xc * xc, -1, keepdims=True)
    y = xc * lax.rsqrt(var + LN_EPS) * g.astype(jnp.float32) + b.astype(jnp.float32)
    return y.astype(x.dtype)


def rms_norm(x, g):
    xf = x.astype(jnp.float32)
    y = xf * lax.rsqrt(jnp.mean(xf * xf, -1, keepdims=True) + RMS_EPS) * g.astype(jnp.float32)
    return y.astype(x.dtype)


def rope_tables(seq, dim):
    pos = jnp.arange(seq, dtype=jnp.float32)
    inv = jnp.exp(jnp.arange(0, dim, 2, dtype=jnp.float32) * (-math.log(ROPE_BASE) / dim))
    ang = pos[:, None] * inv[None, :]
    return jnp.cos(ang), jnp.sin(ang)


def apply_rope(x, cos, sin):
    xf = x.astype(jnp.float32)
    half = xf.shape[-1] // 2
    x1, x2 = xf[..., :half], xf[..., half:]
    return jnp.concatenate([x1 * cos - x2 * sin, x2 * cos + x1 * sin], -1).astype(x.dtype)


def neighbourhood_indices(rows):
    wr = min(NA_WIN_H, rows)
    ww = NA_WIN_W
    r = np.arange(rows)
    c = np.arange(GRID_W)
    rs = np.clip(r - wr // 2, 0, rows - wr)
    cs = np.clip(c - ww // 2, 0, GRID_W - ww)
    kr = rs[:, None] + np.arange(wr)[None, :]
    kc = cs[:, None] + np.arange(ww)[None, :]
    key = kr[:, None, :, None] * GRID_W + kc[None, :, None, :]
    dr = kr - r[:, None] + (NA_WIN_H - 1)
    dc = kc - c[:, None] + (NA_WIN_W - 1)
    bias = dr[:, None, :, None] * (2 * NA_WIN_W - 1) + dc[None, :, None, :]
    n_keys = wr * ww
    return (key.reshape(rows, GRID_W, n_keys).astype(np.int32),
            bias.reshape(rows, GRID_W, n_keys).astype(np.int32))


def neighbourhood_attention(q, k, v, rpb):
    b, s, h, dh = q.shape
    rows = s // GRID_W
    key_idx, bias_idx = neighbourhood_indices(rows)
    rpb_flat = rpb.reshape(h, -1)
    q_rows = q.reshape(b, rows, GRID_W, h, dh).transpose(1, 0, 2, 3, 4)
    scale = dh ** -0.5

    def row_block(args):
        qr, kidx, bidx = args
        kg = jnp.take(k, kidx, axis=1)
        vg = jnp.take(v, kidx, axis=1)
        sc = jnp.einsum('bqhd,bqkhd->bhqk', qr, kg, preferred_element_type=jnp.float32) * scale
        sc = sc + rpb_flat[:, bidx][None].astype(jnp.float32)
        att = jax.nn.softmax(sc, axis=-1).astype(v.dtype)
        return jnp.einsum('bhqk,bqkhd->bqhd', att, vg)

    out = lax.map(row_block, (q_rows, jnp.asarray(key_idx), jnp.asarray(bias_idx)))
    return out.transpose(1, 0, 2, 3, 4).reshape(b, s, h * dh)


def latent_attention(c_q, c_kv, k_rope, q_norm_g, w_uq, kv_norm_g, w_ukv, cos, sin):
    b, s, _ = c_q.shape
    dq = MLA_NOPE_DIM + MLA_ROPE_DIM
    q = (rms_norm(c_q, q_norm_g) @ w_uq).reshape(b, s, MLA_HEADS, dq)
    q_pe = apply_rope(q[..., MLA_NOPE_DIM:], cos[:, None], sin[:, None])
    q = jnp.concatenate([q[..., :MLA_NOPE_DIM], q_pe], -1)
    kv = (rms_norm(c_kv, kv_norm_g) @ w_ukv).reshape(b, s, MLA_HEADS, MLA_NOPE_DIM + MLA_V_DIM)
    v = kv[..., MLA_NOPE_DIM:]
    k_pe = apply_rope(k_rope, cos, sin)
    k = jnp.concatenate([kv[..., :MLA_NOPE_DIM],
                         jnp.broadcast_to(k_pe[:, :, None, :], (b, s, MLA_HEADS, MLA_ROPE_DIM))], -1)
    scale = dq ** -0.5
    nb = s // MLA_BLOCK
    q_blocks = q.reshape(b, nb, MLA_BLOCK, MLA_HEADS, dq).transpose(1, 0, 2, 3, 4)

    def block(qb):
        sc = jnp.einsum('bqhd,bkhd->bhqk', qb, k, preferred_element_type=jnp.float32) * scale
        att = jax.nn.softmax(sc, axis=-1).astype(v.dtype)
        return jnp.einsum('bhqk,bkhd->bqhd', att, v)

    out = lax.map(block, q_blocks)
    return out.transpose(1, 0, 2, 3, 4).reshape(b, s, MLA_HEADS * MLA_V_DIM)


def retention_scan(q, k, v, log_gamma, include_diag):
    b, h, s, dk = q.shape
    dv = v.shape[-1]
    n = s // RET_CHUNK

    def chunks(t):
        return t.reshape(b, h, n, RET_CHUNK, t.shape[-1]).transpose(2, 0, 1, 3, 4)

    idx = jnp.arange(RET_CHUNK, dtype=jnp.float32)
    rel = idx[:, None] - idx[None, :]
    keep = (rel >= 0) if include_diag else (rel > 0)
    d_intra = jnp.where(keep[None], jnp.exp(log_gamma[:, None, None] * jnp.maximum(rel, 0.0)[None]), 0.0)
    q_dec = jnp.exp(log_gamma[:, None] * (idx + 1.0))[None, :, :, None]
    k_dec = jnp.exp(log_gamma[:, None] * (RET_CHUNK - 1.0 - idx))[None, :, :, None]
    c_dec = jnp.exp(log_gamma * RET_CHUNK)[None, :, None, None]

    def step(state, inp):
        qi, ki, vi = inp
        sc = jnp.einsum('bhqd,bhkd->bhqk', qi, ki) * d_intra
        out = jnp.einsum('bhqk,bhkv->bhqv', sc, vi) + jnp.einsum('bhqd,bhdv->bhqv', qi, state) * q_dec
        state = state * c_dec + jnp.einsum('bhkd,bhkv->bhdv', ki * k_dec, vi)
        return state, out

    state0 = jnp.zeros((b, h, dk, dv), jnp.float32)
    _, out = lax.scan(step, state0, (chunks(q), chunks(k), chunks(v)))
    return out.transpose(1, 2, 0, 3, 4).reshape(b, h, s, dv)


def retention_mixer(hc, log_rate_f, log_rate_b, cos, sin):
    b, s, _ = hc.shape
    q = hc[..., :RET_QK].reshape(b, s, RET_HEADS, RET_QK_DIM)
    k = hc[..., RET_QK:2 * RET_QK].reshape(b, s, RET_HEADS, RET_QK_DIM)
    v = hc[..., 2 * RET_QK:2 * RET_QK + RET_V].reshape(b, s, RET_HEADS, RET_V_DIM)
    g = hc[..., 2 * RET_QK + RET_V:]
    q = apply_rope(q, cos[:, None], sin[:, None]).astype(jnp.float32).transpose(0, 2, 1, 3)
    k = (apply_rope(k, cos[:, None], sin[:, None]).astype(jnp.float32) * RET_QK_DIM ** -0.5).transpose(0, 2, 1, 3)
    v = v.astype(jnp.float32).transpose(0, 2, 1, 3)
    lg_f = jnp.log1p(-jnp.exp(log_rate_f.astype(jnp.float32)))
    lg_b = jnp.log1p(-jnp.exp(log_rate_b.astype(jnp.float32)))
    fwd = retention_scan(q, k, v, lg_f, True)
    bwd = jnp.flip(retention_scan(jnp.flip(q, 2), jnp.flip(k, 2), jnp.flip(v, 2), lg_b, False), 2)
    r = fwd + bwd
    r = r - jnp.mean(r, -1, keepdims=True)
    r = r * lax.rsqrt(jnp.mean(r * r, -1, keepdims=True) + LN_EPS)
    r = r.transpose(0, 2, 1, 3).reshape(b, s, RET_V)
    return (jax.nn.silu(g.astype(jnp.float32)) * r).astype(hc.dtype)


def hier_moe(x, w_group, b_group, w_router, b_router, w_gate, w_up, w_down):
    b, s, d = x.shape
    f32 = jnp.float32
    xt = x.reshape(b * s, d)
    n = xt.shape[0]
    group_prob = jax.nn.softmax(jnp.dot(xt, w_group, preferred_element_type=f32) + b_group.astype(f32), -1)
    g_idx = jnp.argmax(group_prob, -1).astype(jnp.int32)
    p_group = jnp.max(group_prob, -1, keepdims=True)
    logits = jnp.einsum('nd,gde->nge', xt, w_router, preferred_element_type=f32) + b_router.astype(f32)
    sel = jnp.einsum('nge,ng->ne', logits, jax.nn.one_hot(g_idx, N_GROUPS, dtype=f32))
    top_p, top_i = lax.top_k(jax.nn.softmax(sel, -1), TOP_K)
    w_tok = p_group * top_p / jnp.sum(top_p, -1, keepdims=True)
    e_tok = g_idx[:, None] * EXPERTS_PER_GROUP + top_i.astype(jnp.int32)
    n_assign = n * TOP_K
    a_e = e_tok.reshape(-1)
    a_t = jnp.repeat(jnp.arange(n, dtype=jnp.int32), TOP_K)
    a_w = w_tok.reshape(-1)
    order = jnp.argsort(a_e)
    se, st, sw = a_e[order], a_t[order], a_w[order]
    counts = jnp.bincount(a_e, length=N_EXPERTS).astype(jnp.int32)
    padded = (counts + MOE_BLOCK - 1) // MOE_BLOCK * MOE_BLOCK
    start = jnp.cumsum(counts) - counts
    p_end = jnp.cumsum(padded)
    p_start = p_end - padded
    dest = p_start[se] + jnp.arange(n_assign, dtype=jnp.int32) - start[se]
    n_blocks = -(-n_assign // MOE_BLOCK) + N_EXPERTS
    cap = n_blocks * MOE_BLOCK
    slot_t = jnp.zeros((cap,), jnp.int32).at[dest].set(st)
    slot_w = jnp.zeros((cap,), f32).at[dest].set(sw)
    blk_start = jnp.arange(n_blocks, dtype=jnp.int32) * MOE_BLOCK
    blk_e = jnp.clip(jnp.searchsorted(p_end, blk_start, side='right'), 0, N_EXPERTS - 1).astype(jnp.int32)

    def expert_block(args):
        tok, e = args
        xb = xt[tok]
        hb = jax.nn.silu(xb @ w_gate[e]) * (xb @ w_up[e])
        return hb @ w_down[e]

    yb = lax.map(expert_block, (slot_t.reshape(n_blocks, MOE_BLOCK), blk_e))
    y = jnp.zeros((n, d), f32).at[slot_t].add(yb.reshape(cap, d).astype(f32) * slot_w[:, None])
    return y.reshape(b, s, d).astype(x.dtype)


def _normal(key, shape, scale):
    return jax.random.normal(key, shape, jnp.float32) * scale


def setup_inputs(seed: int = 0) -> dict:
    key = jax.random.key(seed)
    ks = jax.random.split(key, 32)
    rpb_shape = (N_EVEN, NA_HEADS, 2 * NA_WIN_H - 1, 2 * NA_WIN_W - 1)
    base_rate = -(5.0 + jnp.arange(RET_HEADS, dtype=jnp.float32)) * math.log(2.0)
    return {
        'x': _normal(ks[0], (BATCH, SEQ, D_MODEL), 1.0),
        'p': _normal(ks[1], (DEPTH, BATCH, SEQ, PLE_DIM), 1.0),
        'ab_w_in': _normal(ks[2], (N_EVEN, D_MODEL, AB_IN), D_MODEL ** -0.5),
        'ab_rpb': _normal(ks[3], rpb_shape, 0.1),
        'ab_q_norm': 1.0 + _normal(ks[4], (N_EVEN, MLA_Q_RANK), 0.02),
        'ab_w_uq': _normal(ks[5], (N_EVEN, MLA_Q_RANK, MLA_HEADS * (MLA_NOPE_DIM + MLA_ROPE_DIM)), MLA_Q_RANK ** -0.5),
        'ab_kv_norm': 1.0 + _normal(ks[6], (N_EVEN, MLA_KV_RANK), 0.02),
        'ab_w_ukv': _normal(ks[7], (N_EVEN, MLA_KV_RANK, MLA_HEADS * (MLA_NOPE_DIM + MLA_V_DIM)), MLA_KV_RANK ** -0.5),
        'ab_w_out': _normal(ks[8], (N_EVEN, AB_OUT, D_MODEL), AB_OUT ** -0.5 * DN_BETA),
        'c_w_in': _normal(ks[9], (N_ODD, D_MODEL, C_IN), D_MODEL ** -0.5),
        'c_log_rate_f': base_rate[None, :] + _normal(ks[10], (N_ODD, RET_HEADS), 0.05),
        'c_log_rate_b': base_rate[None, :] + _normal(ks[11], (N_ODD, RET_HEADS), 0.05),
        'c_w_out': _normal(ks[12], (N_ODD, C_OUT, D_MODEL), C_OUT ** -0.5 * DN_BETA),
        'ln1_g': 1.0 + _normal(ks[13], (DEPTH, D_MODEL), 0.02),
        'ln1_b': _normal(ks[14], (DEPTH, D_MODEL), 0.02),
        'moe_w_group': _normal(ks[15], (DEPTH, D_MODEL, N_GROUPS), D_MODEL ** -0.5),
        'moe_b_group': _normal(ks[16], (DEPTH, N_GROUPS), 0.01),
        'moe_w_router': _normal(ks[17], (DEPTH, N_GROUPS, D_MODEL, EXPERTS_PER_GROUP), D_MODEL ** -0.5),
        'moe_b_router': _normal(ks[18], (DEPTH, N_GROUPS, EXPERTS_PER_GROUP), 0.01),
        'moe_w_gate': _normal(ks[19], (DEPTH, N_EXPERTS, D_MODEL, D_EXPERT), D_MODEL ** -0.5),
        'moe_w_up': _normal(ks[20], (DEPTH, N_EXPERTS, D_MODEL, D_EXPERT), D_MODEL ** -0.5),
        'moe_w_down': _normal(ks[21], (DEPTH, N_EXPERTS, D_EXPERT, D_MODEL), D_EXPERT ** -0.5 * DN_BETA),
        'ple_w_proj': _normal(ks[22], (DEPTH, PLE_DIM, D_MODEL), PLE_DIM ** -0.5),
        'ple_w_gate': _normal(ks[23], (DEPTH, D_MODEL, D_MODEL), D_MODEL ** -0.5),
        'ple_b_gate': _normal(ks[24], (DEPTH, D_MODEL), 0.02),
        'ln2_g': 1.0 + _normal(ks[25], (DEPTH, D_MODEL), 0.02),
        'ln2_b': _normal(ks[26], (DEPTH, D_MODEL), 0.02),
    }


def reference(x, p, ab_w_in, ab_rpb, ab_q_norm, ab_w_uq, ab_kv_norm, ab_w_ukv, ab_w_out,
              c_w_in, c_log_rate_f, c_log_rate_b, c_w_out, ln1_g, ln1_b,
              moe_w_group, moe_b_group, moe_w_router, moe_b_router, moe_w_gate, moe_w_up, moe_w_down,
              ple_w_proj, ple_w_gate, ple_b_gate, ln2_g, ln2_b):
    b, s, _ = x.shape
    cos_mla, sin_mla = rope_tables(s, MLA_ROPE_DIM)
    cos_ret, sin_ret = rope_tables(s, RET_QK_DIM)
    o1 = 3 * NA_WIDTH
    o2 = o1 + MLA_Q_RANK
    o3 = o2 + MLA_KV_RANK
    for i in range(DEPTH):
        j = i // 2
        if i % 2 == 0:
            h = x @ ab_w_in[j]
            qa = h[..., :NA_WIDTH].reshape(b, s, NA_HEADS, NA_HEAD_DIM)
            ka = h[..., NA_WIDTH:2 * NA_WIDTH].reshape(b, s, NA_HEADS, NA_HEAD_DIM)
            va = h[..., 2 * NA_WIDTH:o1].reshape(b, s, NA_HEADS, NA_HEAD_DIM)
            a_out = neighbourhood_attention(qa, ka, va, ab_rpb[j])
            b_out = latent_attention(h[..., o1:o2], h[..., o2:o3], h[..., o3:], ab_q_norm[j], ab_w_uq[j],
                                     ab_kv_norm[j], ab_w_ukv[j], cos_mla, sin_mla)
            mix = jnp.concatenate([a_out, b_out], -1) @ ab_w_out[j]
        else:
            mix = retention_mixer(x @ c_w_in[j], c_log_rate_f[j], c_log_rate_b[j], cos_ret, sin_ret) @ c_w_out[j]
        x = layer_norm(DN_ALPHA * x + mix, ln1_g[i], ln1_b[i])
        ffn = hier_moe(x, moe_w_group[i], moe_b_group[i], moe_w_router[i], moe_b_router[i],
                       moe_w_gate[i], moe_w_up[i], moe_w_down[i])
        ple = jax.nn.sigmoid(x @ ple_w_gate[i] + ple_b_gate[i]) * (p[i] @ ple_w_proj[i])
        x = layer_norm(DN_ALPHA * x + ffn + ple, ln2_g[i], ln2_b[i])
    return x
```

```python
import functools
import math

import numpy as np
import jax
import jax.numpy as jnp
from jax import lax
from jax.experimental import pallas as pl
from jax.experimental.pallas import tpu as pltpu

DEPTH = 2
GRID_W = 64
NA_HEADS = 8
NA_HEAD_DIM = 128
NA_WIN_H = 8
NA_WIN_W = 16
MLA_HEADS = 8
MLA_Q_RANK = 512
MLA_KV_RANK = 256
MLA_NOPE_DIM = 128
MLA_ROPE_DIM = 64
MLA_V_DIM = 128
RET_HEADS = 8
RET_QK_DIM = 256
RET_V_DIM = 512
RET_CHUNK = 128
N_GROUPS = 4
EXPERTS_PER_GROUP = 8
N_EXPERTS = N_GROUPS * EXPERTS_PER_GROUP
D_EXPERT = 512
MOE_BLOCK = 128
ROPE_BASE = 10000.0
LN_EPS = 1e-5
RMS_EPS = 1e-6
DN_ALPHA = (2 * DEPTH) ** 0.25
NA_WIDTH = NA_HEADS * NA_HEAD_DIM

LANES = 128
SUBLANES = 8
VMEM_LIMIT_BYTES = 56 * 1024 * 1024
MASK_VALUE = -1e30

F32 = jnp.float32
BF16 = jnp.bfloat16
I32 = jnp.int32
U32 = jnp.uint32


def _cparams(sem):
    return pltpu.CompilerParams(dimension_semantics=sem, vmem_limit_bytes=VMEM_LIMIT_BYTES)


def _dot(a, b):
    return jnp.dot(a, b, preferred_element_type=F32)


def _dot_nt(a, b, precision=None):
    return lax.dot_general(a, b, (((1,), (1,)), ((), ())), preferred_element_type=F32,
                           precision=precision)


def _pack_halves(y):
    c = y.shape[1] // 2
    bits = pltpu.bitcast(y.astype(BF16).astype(F32), U32)
    return (bits[:, :c] >> 16) | (bits[:, c:] & jnp.uint32(0xFFFF0000))


def _unpack_halves(w):
    lo = pltpu.bitcast(w << 16, F32)
    hi = pltpu.bitcast(w & jnp.uint32(0xFFFF0000), F32)
    return lo, hi


def _mm_kernel(x_ref, w_ref, o_ref):
    o_ref[...] = _dot(x_ref[...].astype(BF16), w_ref[...]).astype(o_ref.dtype)


def _matmul(x, w, out_dtype, tm, tn):
    m, k = x.shape
    n = w.shape[1]
    return pl.pallas_call(
        _mm_kernel,
        grid=(m // tm, n // tn),
        in_specs=[pl.BlockSpec((tm, k), lambda i, j: (i, 0)),
                  pl.BlockSpec((k, tn), lambda i, j: (0, j))],
        out_specs=pl.BlockSpec((tm, tn), lambda i, j: (i, j)),
        out_shape=jax.ShapeDtypeStruct((m, n), out_dtype),
        compiler_params=_cparams(("parallel", "arbitrary")),
        name="matmul",
    )(x, w)


def _layer_norm_rows(z, g, b):
    mean = jnp.mean(z, axis=-1, keepdims=True)
    zc = z - mean
    var = jnp.mean(zc * zc, axis=-1, keepdims=True)
    return zc * lax.rsqrt(var + LN_EPS) * g + b


def _proj_ln_kernel(*refs, n_act, nk):
    acts = refs[:n_act]
    ws = refs[n_act:2 * n_act]
    x_ref, g_ref, b_ref, y_ref, yb_ref, acc_ref = refs[2 * n_act:]
    k = pl.program_id(1)
    part = _dot(acts[0][...], ws[0][...])
    for a, w in zip(acts[1:], ws[1:]):
        part = part + _dot(a[...], w[...])

    @pl.when(k == 0)
    def _():
        acc_ref[...] = part

    @pl.when(k > 0)
    def _():
        acc_ref[...] = acc_ref[...] + part

    @pl.when(k == nk - 1)
    def _():
        z = DN_ALPHA * x_ref[...] + acc_ref[...]
        y = _layer_norm_rows(z, g_ref[...], b_ref[...])
        y_ref[...] = y
        yb_ref[...] = y.astype(BF16)


def _proj_ln(acts, ws, x, g, b, tm, nk):
    m, d = x.shape
    n_act = len(acts)
    in_specs = []
    for a in acts:
        kk = a.shape[1] // nk
        in_specs.append(pl.BlockSpec((tm, kk), lambda i, k: (i, k)))
    for w in ws:
        kk = w.shape[0] // nk
        in_specs.append(pl.BlockSpec((kk, d), lambda i, k: (k, 0)))
    in_specs += [pl.BlockSpec((tm, d), lambda i, k: (i, 0)),
                 pl.BlockSpec((1, d), lambda i, k: (0, 0)),
                 pl.BlockSpec((1, d), lambda i, k: (0, 0))]
    return pl.pallas_call(
        functools.partial(_proj_ln_kernel, n_act=n_act, nk=nk),
        grid=(m // tm, nk),
        in_specs=in_specs,
        out_specs=[pl.BlockSpec((tm, d), lambda i, k: (i, 0)),
                   pl.BlockSpec((tm, d), lambda i, k: (i, 0))],
        out_shape=[jax.ShapeDtypeStruct((m, d), F32), jax.ShapeDtypeStruct((m, d), BF16)],
        scratch_shapes=[pltpu.VMEM((tm, d), F32)],
        compiler_params=_cparams(("parallel", "arbitrary")),
        name="proj_ln",
    )(*acts, *ws, x, g.reshape(1, d), b.reshape(1, d))


def _na_bias_tables(rpb):
    c = np.arange(GRID_W)
    cs = np.clip(c - NA_WIN_W // 2, 0, GRID_W - NA_WIN_W)
    kc = np.arange(GRID_W)
    valid = (kc[None, :] >= cs[:, None]) & (kc[None, :] < cs[:, None] + NA_WIN_W)
    dc = np.clip(kc[None, :] - c[:, None] + NA_WIN_W - 1, 0, 2 * NA_WIN_W - 2)
    off = np.arange(NA_WIN_H)
    j = np.arange(NA_WIN_H)
    dr = off[:, None] + j[None, :]
    flat = dr[:, None, :, None] * (2 * NA_WIN_W - 1) + dc[None, :, None, :]
    flat = jnp.asarray(flat.reshape(NA_WIN_H, GRID_W, NA_WIN_H * GRID_W), I32)
    vals = rpb.reshape(rpb.shape[0], -1)[:, flat]
    mask = np.broadcast_to(valid[None, :, None, :], (NA_WIN_H, GRID_W, NA_WIN_H, GRID_W))
    mask = jnp.asarray(mask.reshape(NA_WIN_H, GRID_W, NA_WIN_H * GRID_W))
    return jnp.where(mask[None], vals.astype(F32), MASK_VALUE)


def _na_kernel(q_ref, k_ref, v_ref, bias_ref, o_ref, *, rows):
    scale = NA_HEAD_DIM ** -0.5
    nkeys = NA_WIN_H * GRID_W

    def body(r, carry):
        rs = jnp.clip(r - NA_WIN_H // 2, 0, rows - NA_WIN_H)
        off = rs - r + NA_WIN_H - 1
        q0 = pl.multiple_of(r * GRID_W, GRID_W)
        k0 = pl.multiple_of(rs * GRID_W, GRID_W)
        q = q_ref[pl.ds(q0, GRID_W), :]
        kw = k_ref[pl.ds(k0, nkeys), :]
        vw = v_ref[pl.ds(k0, nkeys), :]
        s = _dot_nt(q, kw) * scale + bias_ref[0, off]
        m = jnp.max(s, axis=-1, keepdims=True)
        p = jnp.exp(s - m)
        l = jnp.sum(p, axis=-1, keepdims=True)
        o = _dot(p.astype(BF16), vw) / l
        o_ref[pl.ds(q0, GRID_W), :] = o.astype(o_ref.dtype)
        return carry

    lax.fori_loop(0, rows, body, 0)


def _na_attention(h, bias_tables, batch, seq):
    rows = seq // GRID_W
    d = NA_HEAD_DIM
    nkeys = NA_WIN_H * GRID_W
    return pl.pallas_call(
        functools.partial(_na_kernel, rows=rows),
        grid=(batch, NA_HEADS),
        in_specs=[pl.BlockSpec((seq, d), lambda b, hh: (b, hh)),
                  pl.BlockSpec((seq, d), lambda b, hh: (b, NA_HEADS + hh)),
                  pl.BlockSpec((seq, d), lambda b, hh: (b, 2 * NA_HEADS + hh)),
                  pl.BlockSpec((1, NA_WIN_H, GRID_W, nkeys), lambda b, hh: (hh, 0, 0, 0))],
        out_specs=pl.BlockSpec((seq, d), lambda b, hh: (b, hh)),
        out_shape=jax.ShapeDtypeStruct((batch * seq, NA_WIDTH), BF16),
        compiler_params=_cparams(("parallel", "arbitrary")),
        name="na_attention",
    )(h, h, h, bias_tables)


def _rms_rows(x, g):
    return x * lax.rsqrt(jnp.mean(x * x, axis=-1, keepdims=True) + RMS_EPS) * g


def _rope_lanes(t, cosf, sinf):
    return t * cosf + pltpu.roll(t, LANES // 2, 1) * sinf


def _mla_prep_kernel(cq_ref, ckv_ref, kr_ref, gq_ref, gkv_ref, wq_ref, wk_ref, wv_ref, cos_ref, sin_ref,
                     q_ref, k_ref, v_ref):
    dq = MLA_NOPE_DIM + MLA_ROPE_DIM
    cosf = cos_ref[...]
    sinf = sin_ref[...]
    cqn = _rms_rows(cq_ref[...].astype(F32), gq_ref[...]).astype(BF16)
    ckvn = _rms_rows(ckv_ref[...].astype(F32), gkv_ref[...]).astype(BF16)
    qf = _dot(cqn, wq_ref[...]) * (dq ** -0.5)
    kf = _dot(ckvn, wk_ref[...])
    v_ref[...] = _dot(ckvn, wv_ref[...]).astype(BF16)
    kpe = _rope_lanes(kr_ref[...].astype(F32), cosf, sinf).astype(BF16)
    for hh in range(MLA_HEADS):
        c0 = hh * 2 * LANES
        q_ref[:, c0:c0 + LANES] = qf[:, c0:c0 + LANES].astype(BF16)
        q_ref[:, c0 + LANES:c0 + 2 * LANES] = _rope_lanes(qf[:, c0 + LANES:c0 + 2 * LANES], cosf, sinf).astype(BF16)
        k_ref[:, c0:c0 + LANES] = kf[:, hh * LANES:(hh + 1) * LANES].astype(BF16)
        k_ref[:, c0 + LANES:c0 + 2 * LANES] = kpe


def _mla_prep(h, gq, gkv, wq_p, wk, wv, cosf, sinf, col_cq, seq, tm):
    n = h.shape[0]
    hw = MLA_HEADS * 2 * LANES
    nsb = seq // tm
    b_cq = col_cq // MLA_Q_RANK
    b_ckv = (col_cq + MLA_Q_RANK) // MLA_KV_RANK
    b_kr = (col_cq + MLA_Q_RANK + MLA_KV_RANK) // LANES
    return pl.pallas_call(
        _mla_prep_kernel,
        grid=(n // tm,),
        in_specs=[pl.BlockSpec((tm, MLA_Q_RANK), lambda i: (i, b_cq)),
                  pl.BlockSpec((tm, MLA_KV_RANK), lambda i: (i, b_ckv)),
                  pl.BlockSpec((tm, LANES), lambda i: (i, b_kr)),
                  pl.BlockSpec((1, MLA_Q_RANK), lambda i: (0, 0)),
                  pl.BlockSpec((1, MLA_KV_RANK), lambda i: (0, 0)),
                  pl.BlockSpec((MLA_Q_RANK, hw), lambda i: (0, 0)),
                  pl.BlockSpec((MLA_KV_RANK, MLA_HEADS * LANES), lambda i: (0, 0)),
                  pl.BlockSpec((MLA_KV_RANK, MLA_HEADS * LANES), lambda i: (0, 0)),
                  pl.BlockSpec((tm, LANES), lambda i: (i % nsb, 0)),
                  pl.BlockSpec((tm, LANES), lambda i: (i % nsb, 0))],
        out_specs=[pl.BlockSpec((tm, hw), lambda i: (i, 0)),
                   pl.BlockSpec((tm, hw), lambda i: (i, 0)),
                   pl.BlockSpec((tm, MLA_HEADS * LANES), lambda i: (i, 0))],
        out_shape=[jax.ShapeDtypeStruct((n, hw), BF16), jax.ShapeDtypeStruct((n, hw), BF16),
                   jax.ShapeDtypeStruct((n, MLA_HEADS * LANES), BF16)],
        compiler_params=_cparams(("parallel",)),
        name="mla_prep",
    )(h, h, h, gq.reshape(1, -1), gkv.reshape(1, -1), wq_p, wk, wv, cosf, sinf)


def _mla_attn_kernel(q_ref, k_ref, v_ref, o_ref, *, tk):
    q = q_ref[...]
    tq = q.shape[0]
    nk = k_ref.shape[0] // tk

    def body(c, carry):
        m, l, acc = carry
        k0 = pl.multiple_of(c * tk, tk)
        s = _dot_nt(q, k_ref[pl.ds(k0, tk), :])
        m_new = jnp.maximum(m, jnp.max(s, axis=-1, keepdims=True))
        a = jnp.exp(m - m_new)
        p = jnp.exp(s - m_new)
        l = a * l + jnp.sum(p, axis=-1, keepdims=True)
        acc = a * acc + _dot(p.astype(BF16), v_ref[pl.ds(k0, tk), :])
        return m_new, l, acc

    m0 = jnp.full((tq, 1), MASK_VALUE, F32)
    l0 = jnp.zeros((tq, 1), F32)
    acc0 = jnp.zeros((tq, MLA_V_DIM), F32)
    m, l, acc = lax.fori_loop(0, nk, body, (m0, l0, acc0))
    o_ref[...] = (acc / l).astype(o_ref.dtype)


def _mla_attention(q_p, k_p, v, batch, seq, tq, tk):
    n = q_p.shape[0]
    nqb = seq // tq
    return pl.pallas_call(
        functools.partial(_mla_attn_kernel, tk=tk),
        grid=(batch, MLA_HEADS, nqb),
        in_specs=[pl.BlockSpec((tq, 2 * LANES), lambda b, hh, i: (b * nqb + i, hh)),
                  pl.BlockSpec((seq, 2 * LANES), lambda b, hh, i: (b, hh)),
                  pl.BlockSpec((seq, MLA_V_DIM), lambda b, hh, i: (b, hh))],
        out_specs=pl.BlockSpec((tq, MLA_V_DIM), lambda b, hh, i: (b * nqb + i, hh)),
        out_shape=jax.ShapeDtypeStruct((n, MLA_HEADS * MLA_V_DIM), BF16),
        compiler_params=_cparams(("parallel", "parallel", "arbitrary")),
        name="mla_attention",
    )(q_p, k_p, v)


def _ret_kernel(lg_ref, q_ref, k_ref, v_ref, g_ref, cos_ref, sin_ref, o_ref, acc_ref, st_ref, *, nchunk):
    c_len = RET_CHUNK
    half = RET_QK_DIM // 2
    hh = pl.program_id(1)
    lgf = lg_ref[0, hh]
    lgb = lg_ref[1, hh]
    ii = lax.broadcasted_iota(I32, (c_len, c_len), 0).astype(F32)
    jj = lax.broadcasted_iota(I32, (c_len, c_len), 1).astype(F32)
    rel = ii - jj
    dmat = jnp.where(rel >= 0, jnp.exp(lgf * jnp.maximum(rel, 0.0)), jnp.exp(lgb * jnp.maximum(-rel, 0.0)))
    pos = lax.broadcasted_iota(I32, (c_len, 1), 0).astype(F32)
    qdec_f = jnp.exp(lgf * (pos + 1.0))
    kdec_f = jnp.exp(lgf * (c_len - 1.0 - pos))
    qdec_b = jnp.exp(lgb * (c_len - pos))
    kdec_b = jnp.exp(lgb * pos)
    full_chunk = jnp.full((1, RET_V_DIM), float(c_len), F32)
    cdec_f = jnp.exp(lgf * full_chunk)
    cdec_b = jnp.exp(lgb * full_chunk)

    def rope(x, t0):
        cos = cos_ref[pl.ds(t0, c_len), :]
        sin = sin_ref[pl.ds(t0, c_len), :]
        x1 = x[:, :half]
        x2 = x[:, half:]
        return jnp.concatenate([x1 * cos - x2 * sin, x2 * cos + x1 * sin], axis=-1)

    def load_qk(t0):
        q = rope(q_ref[pl.ds(t0, c_len), :].astype(F32), t0) * (RET_QK_DIM ** -0.5)
        k = rope(k_ref[pl.ds(t0, c_len), :].astype(F32), t0)
        return q, k

    def state_update(k, kdec, cdec, vc):
        kd_t = (k * kdec).T.astype(BF16)
        st_ref[...] = st_ref[...] * cdec + _dot(kd_t, vc)

    st_ref[...] = jnp.zeros_like(st_ref)

    def bwd_body(i, carry):
        c = nchunk - 1 - i
        t0 = pl.multiple_of(c * c_len, c_len)
        q, k = load_qk(t0)
        vc = v_ref[pl.ds(t0, c_len), :]
        acc_ref[pl.ds(t0, c_len), :] = _dot(q.astype(BF16), st_ref[...].astype(BF16)) * qdec_b
        state_update(k, kdec_b, cdec_b, vc)
        return carry

    lax.fori_loop(0, nchunk, bwd_body, 0)
    st_ref[...] = jnp.zeros_like(st_ref)

    def fwd_body(c, carry):
        t0 = pl.multiple_of(c * c_len, c_len)
        q, k = load_qk(t0)
        vc = v_ref[pl.ds(t0, c_len), :]
        qb = q.astype(BF16)
        sc = _dot_nt(qb, k.astype(BF16)) * dmat
        r = (_dot(sc.astype(BF16), vc) + _dot(qb, st_ref[...].astype(BF16)) * qdec_f
             + acc_ref[pl.ds(t0, c_len), :])
        state_update(k, kdec_f, cdec_f, vc)
        r = r - jnp.mean(r, axis=-1, keepdims=True)
        r = r * lax.rsqrt(jnp.mean(r * r, axis=-1, keepdims=True) + LN_EPS)
        g = g_ref[pl.ds(t0, c_len), :].astype(F32)
        o_ref[pl.ds(t0, c_len), :] = (g * jax.nn.sigmoid(g) * r).astype(o_ref.dtype)
        return carry

    lax.fori_loop(0, nchunk, fwd_body, 0)


def _retention(hc, lg, cosr, sinr, batch, seq):
    dk, dv, nh = RET_QK_DIM, RET_V_DIM, RET_HEADS
    v_blk0 = (2 * nh * dk) // dv
    return pl.pallas_call(
        functools.partial(_ret_kernel, nchunk=seq // RET_CHUNK),
        grid=(batch, nh),
        in_specs=[pl.BlockSpec(memory_space=pltpu.SMEM),
                  pl.BlockSpec((seq, dk), lambda b, hh: (b, hh)),
                  pl.BlockSpec((seq, dk), lambda b, hh: (b, nh + hh)),
                  pl.BlockSpec((seq, dv), lambda b, hh: (b, v_blk0 + hh)),
                  pl.BlockSpec((seq, dv), lambda b, hh: (b, v_blk0 + nh + hh)),
                  pl.BlockSpec((seq, dk // 2), lambda b, hh: (0, 0)),
                  pl.BlockSpec((seq, dk // 2), lambda b, hh: (0, 0))],
        out_specs=pl.BlockSpec((seq, dv), lambda b, hh: (b, hh)),
        scratch_shapes=[pltpu.VMEM((seq, dv), F32), pltpu.VMEM((dk, dv), F32)],
        out_shape=jax.ShapeDtypeStruct((batch * seq, nh * dv), BF16),
        compiler_params=_cparams(("parallel", "arbitrary")),
        name="retention",
    )(lg, hc, hc, hc, hc, cosr, sinr)


ROUTER_ROWS = 40


def _router_kernel(x_ref, wt_ref, b_ref, tri_ref, ids_ref, wts_ref, cnt_ref, carry_ref):
    i = pl.program_id(0)
    tm = x_ref.shape[0]

    @pl.when(i == 0)
    def _():
        carry_ref[...] = jnp.zeros_like(carry_ref)

    logits = _dot_nt(wt_ref[...], x_ref[...], precision=lax.Precision.HIGHEST) + b_ref[...]
    grow = lax.broadcasted_iota(I32, (SUBLANES, tm), 0).astype(F32)
    gl = jnp.where(grow < N_GROUPS, logits[0:SUBLANES], MASK_VALUE)
    gmax = jnp.max(gl, axis=0, keepdims=True)
    gsum = jnp.sum(jnp.exp(gl - gmax), axis=0, keepdims=True)
    p_group = 1.0 / gsum
    g_idx = jnp.min(jnp.where(gl == gmax, grow, float(N_GROUPS)), axis=0, keepdims=True)
    sel = jnp.zeros((EXPERTS_PER_GROUP, tm), F32)
    for g in range(N_GROUPS):
        r0 = SUBLANES + g * EXPERTS_PER_GROUP
        sel = sel + jnp.where(g_idx == float(g), logits[r0:r0 + EXPERTS_PER_GROUP], 0.0)
    erow = lax.broadcasted_iota(I32, (EXPERTS_PER_GROUP, tm), 0).astype(F32)
    smax = jnp.max(sel, axis=0, keepdims=True)
    sexp = jnp.exp(sel - smax)
    probs = sexp / jnp.sum(sexp, axis=0, keepdims=True)
    p1 = jnp.max(probs, axis=0, keepdims=True)
    i1 = jnp.min(jnp.where(probs == p1, erow, float(EXPERTS_PER_GROUP)), axis=0, keepdims=True)
    rest = jnp.where(erow == i1, -1.0, probs)
    p2 = jnp.max(rest, axis=0, keepdims=True)
    i2 = jnp.min(jnp.where(rest == p2, erow, float(EXPERTS_PER_GROUP)), axis=0, keepdims=True)
    denom = p1 + p2
    e0 = g_idx * EXPERTS_PER_GROUP + i1
    e1 = g_idx * EXPERTS_PER_GROUP + i2

    xrow = lax.broadcasted_iota(I32, (N_EXPERTS, tm), 0).astype(F32)
    oh0 = jnp.where(xrow == e0, 1.0, 0.0)
    oh1 = jnp.where(xrow == e1, 1.0, 0.0)
    onehot = oh0 + oh1
    before = _dot(onehot.astype(BF16), tri_ref[...]) + carry_ref[:, 0:1]
    rank0 = jnp.sum(oh0 * before, axis=0, keepdims=True)
    rank1 = jnp.sum(oh1 * before, axis=0, keepdims=True)
    carry_ref[...] = carry_ref[...] + jnp.sum(onehot, axis=1, keepdims=True)

    ids_ref[...] = jnp.zeros_like(ids_ref)
    ids_ref[0:1, :] = e0.astype(I32)
    ids_ref[1:2, :] = e1.astype(I32)
    ids_ref[2:3, :] = rank0.astype(I32)
    ids_ref[3:4, :] = rank1.astype(I32)
    wts_ref[...] = jnp.zeros_like(wts_ref)
    wts_ref[0:1, :] = p_group * p1 / denom
    wts_ref[1:2, :] = p_group * p2 / denom
    cnt_ref[...] = carry_ref[...]


def _router(x, wt, bias, tm):
    n, d = x.shape
    tri = jnp.asarray(np.triu(np.ones((tm, tm), np.float32), 1), BF16)
    return pl.pallas_call(
        _router_kernel,
        grid=(n // tm,),
        in_specs=[pl.BlockSpec((tm, d), lambda i: (i, 0)),
                  pl.BlockSpec((ROUTER_ROWS, d), lambda i: (0, 0)),
                  pl.BlockSpec((ROUTER_ROWS, 1), lambda i: (0, 0)),
                  pl.BlockSpec((tm, tm), lambda i: (0, 0))],
        out_specs=[pl.BlockSpec((SUBLANES, tm), lambda i: (0, i)),
                   pl.BlockSpec((SUBLANES, tm), lambda i: (0, i)),
                   pl.BlockSpec((N_EXPERTS, LANES), lambda i: (0, 0))],
        out_shape=[jax.ShapeDtypeStruct((SUBLANES, n), I32), jax.ShapeDtypeStruct((SUBLANES, n), F32),
                   jax.ShapeDtypeStruct((N_EXPERTS, LANES), F32)],
        scratch_shapes=[pltpu.VMEM((N_EXPERTS, LANES), F32)],
        compiler_params=_cparams(("arbitrary",)),
        name="moe_router",
    )(x, wt, bias, tri)


def _slots_kernel(ids_ref, cnt_ref, slots_ref, blk_ref, *, nblk_pad):
    tm = ids_ref.shape[1]
    cnt = cnt_ref[:, 0:1]
    padded = jnp.floor((cnt + (MOE_BLOCK - 1)) / MOE_BLOCK) * MOE_BLOCK
    er = lax.broadcasted_iota(I32, (N_EXPERTS, N_EXPERTS), 0)
    ec = lax.broadcasted_iota(I32, (N_EXPERTS, N_EXPERTS), 1)
    padded_row = jnp.sum(jnp.where(er == ec, padded, 0.0), axis=0, keepdims=True)
    p_start = jnp.sum(jnp.where(ec < er, padded_row, 0.0), axis=1, keepdims=True)
    p_end = p_start + padded
    xrow = lax.broadcasted_iota(I32, (N_EXPERTS, tm), 0)
    e0 = ids_ref[0:1, :]
    e1 = ids_ref[1:2, :]
    s0 = jnp.sum(jnp.where(xrow == e0, p_start, 0.0), axis=0, keepdims=True).astype(I32) + ids_ref[2:3, :]
    s1 = jnp.sum(jnp.where(xrow == e1, p_start, 0.0), axis=0, keepdims=True).astype(I32) + ids_ref[3:4, :]
    slots_ref[...] = jnp.zeros_like(slots_ref)
    slots_ref[0:1, :] = s0
    slots_ref[1:2, :] = s1
    bstart = (lax.broadcasted_iota(I32, (1, nblk_pad), 1) * MOE_BLOCK).astype(F32)
    blk_e = jnp.sum(jnp.where(p_end <= bstart, 1.0, 0.0), axis=0, keepdims=True)
    blk_e = jnp.minimum(blk_e, N_EXPERTS - 1.0).astype(I32)
    n_used = (jnp.sum(padded, axis=0, keepdims=True) / MOE_BLOCK).astype(I32)
    blk_ref[...] = jnp.zeros_like(blk_ref)
    blk_ref[0:1, :] = blk_e
    blk_ref[1:2, :] = jnp.broadcast_to(n_used, (1, nblk_pad))


def _slots(ids, cnt, tm, nblk_pad):
    n = ids.shape[1]
    return pl.pallas_call(
        functools.partial(_slots_kernel, nblk_pad=nblk_pad),
        grid=(n // tm,),
        in_specs=[pl.BlockSpec((SUBLANES, tm), lambda i: (0, i)),
                  pl.BlockSpec((N_EXPERTS, LANES), lambda i: (0, 0))],
        out_specs=[pl.BlockSpec((SUBLANES, tm), lambda i: (0, i)),
                   pl.BlockSpec((SUBLANES, nblk_pad), lambda i: (0, 0))],
        out_shape=[jax.ShapeDtypeStruct((SUBLANES, n), I32), jax.ShapeDtypeStruct((SUBLANES, nblk_pad), I32)],
        compiler_params=_cparams(("arbitrary",)),
        name="moe_slots",
    )(ids, cnt)


def _dispatch_kernel(slots_ref, x_ref, xs_in_ref, xs_ref, pk_ref, sem):
    del xs_in_ref
    i = pl.program_id(0)
    tm = x_ref.shape[0]
    n = pl.num_programs(0) * tm
    pk_ref[...] = _pack_halves(x_ref[...])

    def row_copy(t, which):
        slot = slots_ref[which * n + i * tm + t]
        return pltpu.make_async_copy(pk_ref.at[pl.ds(t, 1)], xs_ref.at[pl.ds(slot, 1)], sem)

    def start(t, carry):
        row_copy(t, 0).start()
        row_copy(t, 1).start()
        return carry

    def wait(t, carry):
        row_copy(t, 0).wait()
        row_copy(t, 1).wait()
        return carry

    lax.fori_loop(0, tm, start, 0)
    lax.fori_loop(0, tm, wait, 0)


def _dispatch(slots_flat, x, cap, tm):
    n, d = x.shape
    xs0 = jnp.zeros((cap, d // 2), U32)
    return pl.pallas_call(
        _dispatch_kernel,
        grid_spec=pltpu.PrefetchScalarGridSpec(
            num_scalar_prefetch=1,
            grid=(n // tm,),
            in_specs=[pl.BlockSpec((tm, d), lambda i, s: (i, 0)),
                      pl.BlockSpec(memory_space=pl.ANY)],
            out_specs=pl.BlockSpec(memory_space=pl.ANY),
            scratch_shapes=[pltpu.VMEM((tm, d // 2), U32), pltpu.SemaphoreType.DMA(())]),
        out_shape=jax.ShapeDtypeStruct((cap, d // 2), U32),
        input_output_aliases={2: 0},
        compiler_params=_cparams(("arbitrary",)),
        name="moe_dispatch",
    )(slots_flat, x, xs0)


def _expert_kernel(blk_ref, xs_ref, wg_ref, wu_ref, wd_ref, y_ref, *, nblk_pad):
    i = pl.program_id(0)
    n_used = blk_ref[nblk_pad]

    @pl.when(i < n_used)
    def _():
        lo, hi = _unpack_halves(xs_ref[...])
        xb = jnp.concatenate([lo.astype(BF16), hi.astype(BF16)], axis=-1)
        g = _dot(xb, wg_ref[0])
        u = _dot(xb, wu_ref[0])
        hmid = (g * jax.nn.sigmoid(g) * u).astype(BF16)
        y_ref[...] = _pack_halves(_dot(hmid, wd_ref[0]))

    @pl.when(i >= n_used)
    def _():
        y_ref[...] = jnp.zeros_like(y_ref)


def _experts(blk_flat, xs, wg, wu, wd, nblk_pad):
    cap, dh = xs.shape
    d = 2 * dh
    de = wg.shape[2]
    return pl.pallas_call(
        functools.partial(_expert_kernel, nblk_pad=nblk_pad),
        grid_spec=pltpu.PrefetchScalarGridSpec(
            num_scalar_prefetch=1,
            grid=(cap // MOE_BLOCK,),
            in_specs=[pl.BlockSpec((MOE_BLOCK, dh), lambda i, blk: (i, 0)),
                      pl.BlockSpec((1, d, de), lambda i, blk: (blk[i], 0, 0)),
                      pl.BlockSpec((1, d, de), lambda i, blk: (blk[i], 0, 0)),
                      pl.BlockSpec((1, de, d), lambda i, blk: (blk[i], 0, 0))],
            out_specs=pl.BlockSpec((MOE_BLOCK, dh), lambda i, blk: (i, 0))),
        out_shape=jax.ShapeDtypeStruct((cap, dh), U32),
        compiler_params=_cparams(("arbitrary",)),
        name="moe_experts",
    )(blk_flat, xs, wg, wu, wd)


def _tail_kernel(slots_ref, x_ref, xb_ref, p_ref, wts_ref, wgate_ref, bgate_ref, wproj_ref, g_ref, b_ref,
                 yb_hbm, y_ref, ybf_ref, rows_ref, sem):
    i = pl.program_id(0)
    tm = x_ref.shape[0]
    n = pl.num_programs(0) * tm

    def row_copy(t, which):
        slot = slots_ref[which * n + i * tm + t]
        return pltpu.make_async_copy(yb_hbm.at[pl.ds(slot, 1)], rows_ref.at[which, pl.ds(t, 1)], sem)

    def start(t, carry):
        row_copy(t, 0).start()
        row_copy(t, 1).start()
        return carry

    def wait(t, carry):
        row_copy(t, 0).wait()
        row_copy(t, 1).wait()
        return carry

    lax.fori_loop(0, tm, start, 0)
    gate = jax.nn.sigmoid(_dot(xb_ref[...], wgate_ref[...]) + bgate_ref[...])
    ple = gate * _dot(p_ref[...].astype(BF16), wproj_ref[...])
    lax.fori_loop(0, tm, wait, 0)
    w = wts_ref[...]
    lo0, hi0 = _unpack_halves(rows_ref[0])
    lo1, hi1 = _unpack_halves(rows_ref[1])
    w0 = w[:, 0:1]
    w1 = w[:, 1:2]
    ffn = jnp.concatenate([lo0 * w0 + lo1 * w1, hi0 * w0 + hi1 * w1], axis=-1)
    z = DN_ALPHA * x_ref[...] + ffn + ple
    y = _layer_norm_rows(z, g_ref[...], b_ref[...])
    y_ref[...] = y
    ybf_ref[...] = y.astype(BF16)


def _layer_tail(slots_flat, x, xb, p, wts_t, wgate, bgate, wproj, g, b, yb, tm):
    n, d = x.shape
    pd = p.shape[1]
    return pl.pallas_call(
        _tail_kernel,
        grid_spec=pltpu.PrefetchScalarGridSpec(
            num_scalar_prefetch=1,
            grid=(n // tm,),
            in_specs=[pl.BlockSpec((tm, d), lambda i, s: (i, 0)),
                      pl.BlockSpec((tm, d), lambda i, s: (i, 0)),
                      pl.BlockSpec((tm, pd), lambda i, s: (i, 0)),
                      pl.BlockSpec((tm, 2), lambda i, s: (i, 0)),
                      pl.BlockSpec((d, d), lambda i, s: (0, 0)),
                      pl.BlockSpec((1, d), lambda i, s: (0, 0)),
                      pl.BlockSpec((pd, d), lambda i, s: (0, 0)),
                      pl.BlockSpec((1, d), lambda i, s: (0, 0)),
                      pl.BlockSpec((1, d), lambda i, s: (0, 0)),
                      pl.BlockSpec(memory_space=pl.ANY)],
            out_specs=[pl.BlockSpec((tm, d), lambda i, s: (i, 0)),
                       pl.BlockSpec((tm, d), lambda i, s: (i, 0))],
            scratch_shapes=[pltpu.VMEM((2, tm, d // 2), U32), pltpu.SemaphoreType.DMA(())]),
        out_shape=[jax.ShapeDtypeStruct((n, d), F32), jax.ShapeDtypeStruct((n, d), BF16)],
        compiler_params=_cparams(("arbitrary",)),
        name="layer_tail",
    )(slots_flat, x, xb, p, wts_t, wgate, bgate.reshape(1, d), wproj, g.reshape(1, d), b.reshape(1, d), yb)


def _rope_table(seq, dim):
    pos = jnp.arange(seq, dtype=F32)
    inv = jnp.exp(jnp.arange(0, dim, 2, dtype=F32) * (-math.log(ROPE_BASE) / dim))
    ang = pos[:, None] * inv[None, :]
    return jnp.cos(ang), jnp.sin(ang)


def _moe_layer(x, xb, p, w_group, b_group, w_router, b_router, w_gate, w_up, w_down,
               ple_w_proj, ple_w_gate, ple_b_gate, ln_g, ln_b):
    n, d = x.shape
    nblk = -(-(2 * n) // MOE_BLOCK) + N_EXPERTS
    nblk_pad = -(-nblk // LANES) * LANES
    cap = nblk * MOE_BLOCK
    wt = jnp.zeros((ROUTER_ROWS, d), F32)
    wt = wt.at[0:N_GROUPS].set(w_group.T)
    wt = wt.at[SUBLANES:].set(w_router.transpose(0, 2, 1).reshape(N_EXPERTS, d))
    bias = jnp.zeros((ROUTER_ROWS, 1), F32)
    bias = bias.at[0:N_GROUPS, 0].set(b_group)
    bias = bias.at[SUBLANES:, 0].set(b_router.reshape(N_EXPERTS))
    ids, wts, cnt = _router(x, wt, bias, tm=512)
    slots, blk = _slots(ids, cnt, tm=min(2048, n), nblk_pad=nblk_pad)
    slots_flat = slots[0:2].reshape(2 * n)
    blk_flat = blk[0:2].reshape(2 * nblk_pad)
    xs = _dispatch(slots_flat, x, cap, tm=256)
    yb = _experts(blk_flat, xs, w_gate.astype(BF16), w_up.astype(BF16), w_down.astype(BF16), nblk_pad)
    return _layer_tail(slots_flat, x, xb, p, wts[0:2].T, ple_w_gate.astype(BF16), ple_b_gate,
                       ple_w_proj.astype(BF16), ln_g, ln_b, yb, tm=256)


def _mixer_ab(x, xres, batch, seq, w_in, rpb, q_norm, w_uq, kv_norm, w_ukv, w_out, ln_g, ln_b):
    d = x.shape[1]
    o1 = 3 * NA_WIDTH
    o2 = o1 + MLA_Q_RANK
    o3 = o2 + MLA_KV_RANK
    half = MLA_ROPE_DIM // 2
    kr = w_in[:, o3:o3 + MLA_ROPE_DIM]
    kr_sw = jnp.concatenate([kr[:, half:], kr[:, :half]], axis=1)
    width = -(-(o3 + 2 * MLA_ROPE_DIM) // 1024) * 1024
    w_in_p = jnp.concatenate([w_in, kr_sw, jnp.zeros((d, width - o3 - 2 * MLA_ROPE_DIM), F32)], axis=1)
    h = _matmul(x, w_in_p.astype(BF16), BF16, tm=512, tn=1024)
    a_out = _na_attention(h, _na_bias_tables(rpb), batch, seq)
    dq = MLA_NOPE_DIM + MLA_ROPE_DIM
    wq = w_uq.reshape(MLA_Q_RANK, MLA_HEADS, dq)
    wq_pe = wq[:, :, MLA_NOPE_DIM:]
    wq_p = jnp.concatenate([wq, wq_pe[:, :, half:], wq_pe[:, :, :half]], axis=2)
    wq_p = wq_p.reshape(MLA_Q_RANK, MLA_HEADS * 2 * LANES).astype(BF16)
    wkv = w_ukv.reshape(MLA_KV_RANK, MLA_HEADS, MLA_NOPE_DIM + MLA_V_DIM)
    wk = wkv[:, :, :MLA_NOPE_DIM].reshape(MLA_KV_RANK, MLA_HEADS * MLA_NOPE_DIM).astype(BF16)
    wv = wkv[:, :, MLA_NOPE_DIM:].reshape(MLA_KV_RANK, MLA_HEADS * MLA_V_DIM).astype(BF16)
    cos, sin = _rope_table(seq, MLA_ROPE_DIM)
    zpad = jnp.zeros((seq, LANES - MLA_ROPE_DIM), F32)
    cosf = jnp.concatenate([cos, cos, zpad], axis=1)
    sinf = jnp.concatenate([-sin, sin, zpad], axis=1)
    q_p, k_p, v = _mla_prep(h, q_norm, kv_norm, wq_p, wk, wv, cosf, sinf, o1, seq, tm=512)
    b_out = _mla_attention(q_p, k_p, v, batch, seq, tq=512, tk=1024)
    w_out_b = w_out.astype(BF16)
    return _proj_ln([a_out, b_out], [w_out_b[:NA_WIDTH], w_out_b[NA_WIDTH:]], xres, ln_g, ln_b, tm=512, nk=1)


def _mixer_c(xb, xres, batch, seq, w_in, log_rate_f, log_rate_b, w_out, ln_g, ln_b):
    hc = _matmul(xb, w_in.astype(BF16), BF16, tm=min(1024, xb.shape[0]), tn=1024)
    lg = jnp.stack([jnp.log1p(-jnp.exp(log_rate_f.astype(F32))), jnp.log1p(-jnp.exp(log_rate_b.astype(F32)))])
    cosr, sinr = _rope_table(seq, RET_QK_DIM)
    r = _retention(hc, lg, cosr, sinr, batch, seq)
    return _proj_ln([r], [w_out.astype(BF16)], xres, ln_g, ln_b, tm=512, nk=2)


def kernel(x, p, ab_w_in, ab_rpb, ab_q_norm, ab_w_uq, ab_kv_norm, ab_w_ukv, ab_w_out, c_w_in, c_log_rate_f,
           c_log_rate_b, c_w_out, ln1_g, ln1_b, moe_w_group, moe_b_group, moe_w_router, moe_b_router,
           moe_w_gate, moe_w_up, moe_w_down, ple_w_proj, ple_w_gate, ple_b_gate, ln2_g, ln2_b):
    batch, seq, d = x.shape
    n = batch * seq
    xf = x.reshape(n, d)
    xb = None
    for i in range(DEPTH):
        j = i // 2
        if i % 2 == 0:
            src = xf if xb is None else xb
            xf, xb = _mixer_ab(src, xf, batch, seq, ab_w_in[j], ab_rpb[j], ab_q_norm[j], ab_w_uq[j],
                               ab_kv_norm[j], ab_w_ukv[j], ab_w_out[j], ln1_g[i], ln1_b[i])
        else:
            xf, xb = _mixer_c(xb, xf, batch, seq, c_w_in[j], c_log_rate_f[j], c_log_rate_b[j], c_w_out[j],
                              ln1_g[i], ln1_b[i])
        xf, xb = _moe_layer(xf, xb, p[i].reshape(n, -1), moe_w_group[i], moe_b_group[i], moe_w_router[i],
                            moe_b_router[i], moe_w_gate[i], moe_w_up[i], moe_w_down[i], ple_w_proj[i],
                            ple_w_gate[i], ple_b_gate[i], ln2_g[i], ln2_b[i])
    return xf.reshape(batch, seq, d)
```

```python
import functools
import math

import numpy as np
import jax
import jax.numpy as jnp
from jax import lax
from jax.experimental import pallas as pl
from jax.experimental.pallas import tpu as pltpu

DEPTH = 2
GRID_W = 64
NA_HEADS = 8
NA_HEAD_DIM = 128
NA_WIN_H = 8
NA_WIN_W = 16
MLA_HEADS = 8
MLA_Q_RANK = 512
MLA_KV_RANK = 256
MLA_NOPE_DIM = 128
MLA_ROPE_DIM = 64
MLA_V_DIM = 128
RET_HEADS = 8
RET_QK_DIM = 256
RET_V_DIM = 512
RET_CHUNK = 128
N_GROUPS = 4
EXPERTS_PER_GROUP = 8
N_EXPERTS = N_GROUPS * EXPERTS_PER_GROUP
D_EXPERT = 512
MOE_BLOCK = 128
ROPE_BASE = 10000.0
LN_EPS = 1e-5
RMS_EPS = 1e-6
DN_ALPHA = (2 * DEPTH) ** 0.25
NA_WIDTH = NA_HEADS * NA_HEAD_DIM

LANES = 128
SUBLANES = 8
VMEM_LIMIT_BYTES = 56 * 1024 * 1024
MASK_VALUE = -1e30

F32 = jnp.float32
BF16 = jnp.bfloat16
I32 = jnp.int32
U32 = jnp.uint32


def _cparams(sem):
    return pltpu.CompilerParams(dimension_semantics=sem, vmem_limit_bytes=VMEM_LIMIT_BYTES)


def _dot(a, b):
    return jnp.dot(a, b, preferred_element_type=F32)


def _dot_nt(a, b, precision=None):
    return lax.dot_general(a, b, (((1,), (1,)), ((), ())), preferred_element_type=F32,
                           precision=precision)


def _pack_halves(y):
    c = y.shape[1] // 2
    bits = pltpu.bitcast(y.astype(BF16).astype(F32), U32)
    return (bits[:, :c] >> 16) | (bits[:, c:] & jnp.uint32(0xFFFF0000))


def _unpack_halves(w):
    lo = pltpu.bitcast(w << 16, F32)
    hi = pltpu.bitcast(w & jnp.uint32(0xFFFF0000), F32)
    return lo, hi


def _mm_kernel(x_ref, w_ref, o_ref):
    o_ref[...] = _dot(x_ref[...].astype(BF16), w_ref[...]).astype(o_ref.dtype)


def _matmul(x, w, out_dtype, tm, tn):
    m, k = x.shape
    n = w.shape[1]
    return pl.pallas_call(
        _mm_kernel,
        grid=(m // tm, n // tn),
        in_specs=[pl.BlockSpec((tm, k), lambda i, j: (i, 0)),
                  pl.BlockSpec((k, tn), lambda i, j: (0, j))],
        out_specs=pl.BlockSpec((tm, tn), lambda i, j: (i, j)),
        out_shape=jax.ShapeDtypeStruct((m, n), out_dtype),
        compiler_params=_cparams(("parallel", "arbitrary")),
        name="matmul",
    )(x, w)


def _mm_rope_kernel(x_ref, w_ref, cos_ref, sin_ref, o_ref, *, n_q_tiles, n_rope_tiles, first_gate_tile,
                    head_w, q_scale):
    j = pl.program_id(1)
    acc = _dot(x_ref[...].astype(BF16), w_ref[...])

    @pl.when((j >= n_rope_tiles) & (j < first_gate_tile))
    def _():
        o_ref[...] = acc.astype(o_ref.dtype)

    @pl.when(j >= first_gate_tile)
    def _():
        o_ref[...] = (acc * jax.nn.sigmoid(acc)).astype(o_ref.dtype)

    @pl.when(j < n_rope_tiles)
    def _():
        scale = jnp.where(j < n_q_tiles, q_scale, 1.0)
        cos = cos_ref[...] * scale
        sin = sin_ref[...] * scale
        half = head_w // 2
        for c0 in range(0, acc.shape[1], head_w):
            x1 = acc[:, c0:c0 + half]
            x2 = acc[:, c0 + half:c0 + head_w]
            o_ref[:, c0:c0 + half] = (x1 * cos - x2 * sin).astype(o_ref.dtype)
            o_ref[:, c0 + half:c0 + head_w] = (x2 * cos + x1 * sin).astype(o_ref.dtype)


def _matmul_rope(x, w, cos, sin, out_dtype, tm, tn, n_q_cols, n_rope_cols, gate_col0, head_w, q_scale):
    m, k = x.shape
    n = w.shape[1]
    nsb = cos.shape[0] // tm
    return pl.pallas_call(
        functools.partial(_mm_rope_kernel, n_q_tiles=n_q_cols // tn, n_rope_tiles=n_rope_cols // tn,
                          first_gate_tile=gate_col0 // tn, head_w=head_w, q_scale=q_scale),
        grid=(m // tm, n // tn),
        in_specs=[pl.BlockSpec((tm, k), lambda i, j: (i, 0)),
                  pl.BlockSpec((k, tn), lambda i, j: (0, j)),
                  pl.BlockSpec((tm, head_w // 2), lambda i, j: (i % nsb, 0)),
                  pl.BlockSpec((tm, head_w // 2), lambda i, j: (i % nsb, 0))],
        out_specs=pl.BlockSpec((tm, tn), lambda i, j: (i, j)),
        out_shape=jax.ShapeDtypeStruct((m, n), out_dtype),
        compiler_params=_cparams(("parallel", "arbitrary")),
        name="matmul_rope",
    )(x, w, cos, sin)


def _layer_norm_rows(z, g, b):
    mean = jnp.mean(z, axis=-1, keepdims=True)
    zc = z - mean
    var = jnp.mean(zc * zc, axis=-1, keepdims=True)
    return zc * lax.rsqrt(var + LN_EPS) * g + b


def _proj_ln_kernel(*refs, n_act, nk):
    acts = refs[:n_act]
    ws = refs[n_act:2 * n_act]
    x_ref, g_ref, b_ref, y_ref, yb_ref, acc_ref = refs[2 * n_act:]
    k = pl.program_id(1)
    part = _dot(acts[0][...], ws[0][...])
    for a, w in zip(acts[1:], ws[1:]):
        part = part + _dot(a[...], w[...])

    @pl.when(k == 0)
    def _():
        acc_ref[...] = part

    @pl.when(k > 0)
    def _():
        acc_ref[...] = acc_ref[...] + part

    @pl.when(k == nk - 1)
    def _():
        z = DN_ALPHA * x_ref[...] + acc_ref[...]
        y = _layer_norm_rows(z, g_ref[...], b_ref[...])
        y_ref[...] = y
        yb_ref[...] = y.astype(BF16)


def _proj_ln(acts, ws, x, g, b, tm, nk):
    m, d = x.shape
    n_act = len(acts)
    in_specs = []
    for a in acts:
        kk = a.shape[1] // nk
        in_specs.append(pl.BlockSpec((tm, kk), lambda i, k: (i, k)))
    for w in ws:
        kk = w.shape[0] // nk
        in_specs.append(pl.BlockSpec((kk, d), lambda i, k: (k, 0)))
    in_specs += [pl.BlockSpec((tm, d), lambda i, k: (i, 0)),
                 pl.BlockSpec((1, d), lambda i, k: (0, 0)),
                 pl.BlockSpec((1, d), lambda i, k: (0, 0))]
    return pl.pallas_call(
        functools.partial(_proj_ln_kernel, n_act=n_act, nk=nk),
        grid=(m // tm, nk),
        in_specs=in_specs,
        out_specs=[pl.BlockSpec((tm, d), lambda i, k: (i, 0)),
                   pl.BlockSpec((tm, d), lambda i, k: (i, 0))],
        out_shape=[jax.ShapeDtypeStruct((m, d), F32), jax.ShapeDtypeStruct((m, d), BF16)],
        scratch_shapes=[pltpu.VMEM((tm, d), F32)],
        compiler_params=_cparams(("parallel", "arbitrary")),
        name="proj_ln",
    )(*acts, *ws, x, g.reshape(1, d), b.reshape(1, d))


def _na_bias_tables(rpb):
    nh = rpb.shape[0]
    c = np.arange(GRID_W)
    cs = np.clip(c - NA_WIN_W // 2, 0, GRID_W - NA_WIN_W)
    kc = np.arange(GRID_W)
    valid = (kc[None, :] >= cs[:, None]) & (kc[None, :] < cs[:, None] + NA_WIN_W)
    dc = kc[None, :] - c[:, None] + NA_WIN_W - 1
    onehot = (dc[:, :, None] == np.arange(2 * NA_WIN_W - 1)[None, None, :]) & valid[:, :, None]
    cols = jnp.einsum("hrd,ckd->hrck", rpb.astype(F32), jnp.asarray(onehot, F32),
                      precision=lax.Precision.HIGHEST)
    cols = jnp.where(jnp.asarray(valid)[None, None], cols, MASK_VALUE)
    tabs = jnp.stack([cols[:, off:off + NA_WIN_H] for off in range(NA_WIN_H)], axis=1)
    return tabs.transpose(0, 1, 3, 2, 4).reshape(nh, NA_WIN_H, GRID_W, NA_WIN_H * GRID_W)


def _na_kernel(q_ref, k_ref, v_ref, bias_ref, o_ref, *, rows, group):
    scale = NA_HEAD_DIM ** -0.5
    nkeys = NA_WIN_H * GRID_W

    def body(i, carry):
        geom, scores = [], []
        for u in range(group):
            r = i * group + u
            rs = jnp.clip(r - NA_WIN_H // 2, 0, rows - NA_WIN_H)
            off = rs - r + NA_WIN_H - 1
            q0 = pl.multiple_of(r * GRID_W, GRID_W)
            k0 = pl.multiple_of(rs * GRID_W, GRID_W)
            geom.append((q0, k0))
            s = _dot_nt(q_ref[pl.ds(q0, GRID_W), :], k_ref[pl.ds(k0, nkeys), :])
            scores.append(s * scale + bias_ref[0, off])
        for (q0, k0), s in zip(geom, scores):
            m = jnp.max(s, axis=-1, keepdims=True)
            p = jnp.exp(s - m)
            l = jnp.sum(p, axis=-1, keepdims=True)
            o = _dot(p.astype(BF16), v_ref[pl.ds(k0, nkeys), :]) / l
            o_ref[pl.ds(q0, GRID_W), :] = o.astype(o_ref.dtype)
        return carry

    lax.fori_loop(0, rows // group, body, 0)


def _na_attention(h, bias_tables, batch, seq):
    rows = seq // GRID_W
    d = NA_HEAD_DIM
    nkeys = NA_WIN_H * GRID_W
    return pl.pallas_call(
        functools.partial(_na_kernel, rows=rows, group=8),
        grid=(batch, NA_HEADS),
        in_specs=[pl.BlockSpec((seq, d), lambda b, hh: (b, hh)),
                  pl.BlockSpec((seq, d), lambda b, hh: (b, NA_HEADS + hh)),
                  pl.BlockSpec((seq, d), lambda b, hh: (b, 2 * NA_HEADS + hh)),
                  pl.BlockSpec((1, NA_WIN_H, GRID_W, nkeys), lambda b, hh: (hh, 0, 0, 0))],
        out_specs=pl.BlockSpec((seq, d), lambda b, hh: (b, hh)),
        out_shape=jax.ShapeDtypeStruct((batch * seq, NA_WIDTH), BF16),
        compiler_params=_cparams(("parallel", "arbitrary")),
        name="na_attention",
    )(h, h, h, bias_tables)


def _rms_rows(x, g):
    return x * lax.rsqrt(jnp.mean(x * x, axis=-1, keepdims=True) + RMS_EPS) * g


def _rope_lanes(t, cosf, sinf):
    return t * cosf + pltpu.roll(t, LANES // 2, 1) * sinf


def _mla_prep_kernel(cq_ref, ckv_ref, kr_ref, gq_ref, gkv_ref, wq_ref, wk_ref, wvt_ref, cos_ref, sin_ref,
                     q_ref, k_ref, vt_ref):
    dq = MLA_NOPE_DIM + MLA_ROPE_DIM
    cosf = cos_ref[...]
    sinf = sin_ref[...]
    cqn = _rms_rows(cq_ref[...].astype(F32), gq_ref[...]).astype(BF16)
    ckvn = _rms_rows(ckv_ref[...].astype(F32), gkv_ref[...]).astype(BF16)
    qf = _dot(cqn, wq_ref[...]) * (dq ** -0.5 * math.log2(math.e))
    kf = _dot(ckvn, wk_ref[...])
    vt_ref[...] = _dot_nt(wvt_ref[...], ckvn).astype(BF16)
    kpe = _rope_lanes(kr_ref[...].astype(F32), cosf, sinf).astype(BF16)
    for hh in range(MLA_HEADS):
        c0 = hh * 2 * LANES
        q_ref[:, c0:c0 + LANES] = qf[:, c0:c0 + LANES].astype(BF16)
        q_ref[:, c0 + LANES:c0 + 2 * LANES] = _rope_lanes(qf[:, c0 + LANES:c0 + 2 * LANES], cosf, sinf).astype(BF16)
        k_ref[:, c0:c0 + LANES] = kf[:, hh * LANES:(hh + 1) * LANES].astype(BF16)
        k_ref[:, c0 + LANES:c0 + 2 * LANES] = kpe


def _mla_prep(h, gq, gkv, wq_p, wk, wvt, cosf, sinf, col_cq, seq, tm):
    n = h.shape[0]
    hw = MLA_HEADS * 2 * LANES
    nsb = seq // tm
    b_cq = col_cq // MLA_Q_RANK
    b_ckv = (col_cq + MLA_Q_RANK) // MLA_KV_RANK
    b_kr = (col_cq + MLA_Q_RANK + MLA_KV_RANK) // LANES
    return pl.pallas_call(
        _mla_prep_kernel,
        grid=(n // tm,),
        in_specs=[pl.BlockSpec((tm, MLA_Q_RANK), lambda i: (i, b_cq)),
                  pl.BlockSpec((tm, MLA_KV_RANK), lambda i: (i, b_ckv)),
                  pl.BlockSpec((tm, LANES), lambda i: (i, b_kr)),
                  pl.BlockSpec((1, MLA_Q_RANK), lambda i: (0, 0)),
                  pl.BlockSpec((1, MLA_KV_RANK), lambda i: (0, 0)),
                  pl.BlockSpec((MLA_Q_RANK, hw), lambda i: (0, 0)),
                  pl.BlockSpec((MLA_KV_RANK, MLA_HEADS * LANES), lambda i: (0, 0)),
                  pl.BlockSpec((MLA_HEADS * MLA_V_DIM, MLA_KV_RANK), lambda i: (0, 0)),
                  pl.BlockSpec((tm, LANES), lambda i: (i % nsb, 0)),
                  pl.BlockSpec((tm, LANES), lambda i: (i % nsb, 0))],
        out_specs=[pl.BlockSpec((tm, hw), lambda i: (i, 0)),
                   pl.BlockSpec((tm, hw), lambda i: (i, 0)),
                   pl.BlockSpec((MLA_HEADS * MLA_V_DIM, tm), lambda i: (0, i))],
        out_shape=[jax.ShapeDtypeStruct((n, hw), BF16), jax.ShapeDtypeStruct((n, hw), BF16),
                   jax.ShapeDtypeStruct((MLA_HEADS * MLA_V_DIM, n), BF16)],
        compiler_params=_cparams(("parallel",)),
        name="mla_prep",
    )(h, h, h, gq.reshape(1, -1), gkv.reshape(1, -1), wq_p, wk, wvt, cosf, sinf)


def _mla_attn_kernel(q_ref, k_ref, vt_ref, o_ref, *, tk, sub):
    nchunk = k_ref.shape[0] // tk
    tq = q_ref.shape[0]
    nsub = tq // sub
    qs = [q_ref[s * sub:(s + 1) * sub, :] for s in range(nsub)]
    m = [jnp.full((1, sub), MASK_VALUE, F32) for _ in range(nsub)]
    l = [jnp.zeros((1, sub), F32) for _ in range(nsub)]
    acc = [jnp.zeros((MLA_V_DIM, sub), F32) for _ in range(nsub)]

    def scores(s, c):
        return _dot_nt(k_ref[c * tk:(c + 1) * tk, :], qs[s])

    st_next = [scores(s, 0) for s in range(nsub)]
    for c in range(nchunk):
        for s in range(nsub):
            st = st_next[s]
            m_new = jnp.maximum(m[s], jnp.max(st, axis=0, keepdims=True))
            a = jnp.exp2(m[s] - m_new)
            p = jnp.exp2(st - m_new)
            l[s] = a * l[s] + jnp.sum(p, axis=0, keepdims=True)
            if c + 1 < nchunk:
                st_next[s] = scores(s, c + 1)
            acc[s] = a * acc[s] + _dot(vt_ref[:, c * tk:(c + 1) * tk], p.astype(BF16))
            m[s] = m_new
    for s in range(nsub):
        o_ref[s * sub:(s + 1) * sub, :] = (acc[s] / l[s]).T.astype(o_ref.dtype)


def _mla_attention(q_p, k_p, vt, batch, seq, tq, tk, sub):
    n = q_p.shape[0]
    nqb = seq // tq
    return pl.pallas_call(
        functools.partial(_mla_attn_kernel, tk=tk, sub=sub),
        grid=(batch, MLA_HEADS, nqb),
        in_specs=[pl.BlockSpec((tq, 2 * LANES), lambda b, hh, i: (b * nqb + i, hh)),
                  pl.BlockSpec((seq, 2 * LANES), lambda b, hh, i: (b, hh)),
                  pl.BlockSpec((MLA_V_DIM, seq), lambda b, hh, i: (hh, b))],
        out_specs=pl.BlockSpec((tq, MLA_V_DIM), lambda b, hh, i: (b * nqb + i, hh)),
        out_shape=jax.ShapeDtypeStruct((n, MLA_HEADS * MLA_V_DIM), BF16),
        compiler_params=_cparams(("parallel", "parallel", "arbitrary")),
        name="mla_attention",
    )(q_p, k_p, vt)


def _ret_kernel(lg_ref, q_ref, k_ref, v_ref, g_ref, o_ref, acc_ref, st_ref, *, c_len, group):
    nchunk = q_ref.shape[0] // c_len
    hh = pl.program_id(1)
    lgf = lg_ref[0, hh]
    lgb = lg_ref[1, hh]
    ii = lax.broadcasted_iota(I32, (c_len, c_len), 0).astype(F32)
    jj = lax.broadcasted_iota(I32, (c_len, c_len), 1).astype(F32)
    rel = ii - jj
    dmat = jnp.where(rel >= 0, jnp.exp(lgf * jnp.maximum(rel, 0.0)), jnp.exp(lgb * jnp.maximum(-rel, 0.0)))
    pos = lax.broadcasted_iota(I32, (c_len, 1), 0).astype(F32)
    qdec_f = jnp.exp(lgf * (pos + 1.0))
    kdec_f = jnp.exp(lgf * (c_len - 1.0 - pos))
    qdec_b = jnp.exp(lgb * (c_len - pos))
    kdec_b = jnp.exp(lgb * pos)
    full_chunk = jnp.full((1, RET_V_DIM), float(c_len), F32)
    cdec_f = jnp.exp(lgf * full_chunk)
    cdec_b = jnp.exp(lgb * full_chunk)

    def decayed_keys_t(t0, kdec):
        return (k_ref[pl.ds(t0, c_len), :].astype(F32) * kdec).T.astype(BF16)

    st_ref[...] = jnp.zeros_like(st_ref)

    def bwd_body(i, carry):
        t0s = [pl.multiple_of((nchunk - 1 - (i * group + u)) * c_len, c_len) for u in range(group)]
        upd = [_dot(decayed_keys_t(t0, kdec_b), v_ref[pl.ds(t0, c_len), :]) for t0 in t0s]
        for t0, u_c in zip(t0s, upd):
            st = st_ref[...]
            acc_ref[pl.ds(t0, c_len), :] = _dot(q_ref[pl.ds(t0, c_len), :], st.astype(BF16)) * qdec_b
            st_ref[...] = st * cdec_b + u_c
        return carry

    lax.fori_loop(0, nchunk // group, bwd_body, 0)
    st_ref[...] = jnp.zeros_like(st_ref)

    def fwd_body(i, carry):
        t0s = [pl.multiple_of((i * group + u) * c_len, c_len) for u in range(group)]
        scs = [_dot_nt(q_ref[pl.ds(t0, c_len), :], k_ref[pl.ds(t0, c_len), :]) * dmat for t0 in t0s]
        upd = [_dot(decayed_keys_t(t0, kdec_f), v_ref[pl.ds(t0, c_len), :]) for t0 in t0s]
        for t0, sc, u_c in zip(t0s, scs, upd):
            st = st_ref[...]
            r = (_dot(sc.astype(BF16), v_ref[pl.ds(t0, c_len), :])
                 + _dot(q_ref[pl.ds(t0, c_len), :], st.astype(BF16)) * qdec_f
                 + acc_ref[pl.ds(t0, c_len), :])
            st_ref[...] = st * cdec_f + u_c
            r = r - jnp.mean(r, axis=-1, keepdims=True)
            r = r * lax.rsqrt(jnp.mean(r * r, axis=-1, keepdims=True) + LN_EPS)
            o_ref[pl.ds(t0, c_len), :] = (g_ref[pl.ds(t0, c_len), :].astype(F32) * r).astype(o_ref.dtype)
        return carry

    lax.fori_loop(0, nchunk // group, fwd_body, 0)


def _retention(hc, lg, batch, seq, c_len, group):
    dk, dv, nh = RET_QK_DIM, RET_V_DIM, RET_HEADS
    v_blk0 = (2 * nh * dk) // dv
    return pl.pallas_call(
        functools.partial(_ret_kernel, c_len=c_len, group=group),
        grid=(batch, nh),
        in_specs=[pl.BlockSpec(memory_space=pltpu.SMEM),
                  pl.BlockSpec((seq, dk), lambda b, hh: (b, hh)),
                  pl.BlockSpec((seq, dk), lambda b, hh: (b, nh + hh)),
                  pl.BlockSpec((seq, dv), lambda b, hh: (b, v_blk0 + hh)),
                  pl.BlockSpec((seq, dv), lambda b, hh: (b, v_blk0 + nh + hh))],
        out_specs=pl.BlockSpec((seq, dv), lambda b, hh: (b, hh)),
        scratch_shapes=[pltpu.VMEM((seq, dv), F32), pltpu.VMEM((dk, dv), F32)],
        out_shape=jax.ShapeDtypeStruct((batch * seq, nh * dv), BF16),
        compiler_params=_cparams(("parallel", "arbitrary")),
        name="retention",
    )(lg, hc, hc, hc, hc)


ROUTER_ROWS = 40


def _router_kernel(x_ref, wt_ref, b_ref, tri_ref, ids_ref, wts_ref, cnt_ref, carry_ref):
    i = pl.program_id(0)
    tm = x_ref.shape[0]

    @pl.when(i == 0)
    def _():
        carry_ref[...] = jnp.zeros_like(carry_ref)

    logits = _dot_nt(wt_ref[...], x_ref[...], precision=lax.Precision.HIGHEST) + b_ref[...]
    grow = lax.broadcasted_iota(I32, (SUBLANES, tm), 0).astype(F32)
    gl = jnp.where(grow < N_GROUPS, logits[0:SUBLANES], MASK_VALUE)
    gmax = jnp.max(gl, axis=0, keepdims=True)
    gsum = jnp.sum(jnp.exp(gl - gmax), axis=0, keepdims=True)
    p_group = 1.0 / gsum
    g_idx = jnp.min(jnp.where(gl == gmax, grow, float(N_GROUPS)), axis=0, keepdims=True)
    sel = jnp.zeros((EXPERTS_PER_GROUP, tm), F32)
    for g in range(N_GROUPS):
        r0 = SUBLANES + g * EXPERTS_PER_GROUP
        sel = sel + jnp.where(g_idx == float(g), logits[r0:r0 + EXPERTS_PER_GROUP], 0.0)
    erow = lax.broadcasted_iota(I32, (EXPERTS_PER_GROUP, tm), 0).astype(F32)
    smax = jnp.max(sel, axis=0, keepdims=True)
    sexp = jnp.exp(sel - smax)
    probs = sexp / jnp.sum(sexp, axis=0, keepdims=True)
    p1 = jnp.max(probs, axis=0, keepdims=True)
    i1 = jnp.min(jnp.where(probs == p1, erow, float(EXPERTS_PER_GROUP)), axis=0, keepdims=True)
    rest = jnp.where(erow == i1, -1.0, probs)
    p2 = jnp.max(rest, axis=0, keepdims=True)
    i2 = jnp.min(jnp.where(rest == p2, erow, float(EXPERTS_PER_GROUP)), axis=0, keepdims=True)
    denom = p1 + p2
    e0 = g_idx * EXPERTS_PER_GROUP + i1
    e1 = g_idx * EXPERTS_PER_GROUP + i2

    xrow = lax.broadcasted_iota(I32, (N_EXPERTS, tm), 0).astype(F32)
    oh0 = jnp.where(xrow == e0, 1.0, 0.0)
    oh1 = jnp.where(xrow == e1, 1.0, 0.0)
    onehot = oh0 + oh1
    before = _dot(onehot.astype(BF16), tri_ref[...]) + carry_ref[:, 0:1]
    rank0 = jnp.sum(oh0 * before, axis=0, keepdims=True)
    rank1 = jnp.sum(oh1 * before, axis=0, keepdims=True)
    carry_ref[...] = carry_ref[...] + jnp.sum(onehot, axis=1, keepdims=True)

    ids_ref[...] = jnp.zeros_like(ids_ref)
    ids_ref[0:1, :] = e0.astype(I32)
    ids_ref[1:2, :] = e1.astype(I32)
    ids_ref[2:3, :] = rank0.astype(I32)
    ids_ref[3:4, :] = rank1.astype(I32)
    wts_ref[...] = jnp.zeros_like(wts_ref)
    wts_ref[0:1, :] = p_group * p1 / denom
    wts_ref[1:2, :] = p_group * p2 / denom
    cnt_ref[...] = carry_ref[...]


def _router(x, wt, bias, tm):
    n, d = x.shape
    tri = jnp.asarray(np.triu(np.ones((tm, tm), np.float32), 1), BF16)
    return pl.pallas_call(
        _router_kernel,
        grid=(n // tm,),
        in_specs=[pl.BlockSpec((tm, d), lambda i: (i, 0)),
                  pl.BlockSpec((ROUTER_ROWS, d), lambda i: (0, 0)),
                  pl.BlockSpec((ROUTER_ROWS, 1), lambda i: (0, 0)),
                  pl.BlockSpec((tm, tm), lambda i: (0, 0))],
        out_specs=[pl.BlockSpec((SUBLANES, tm), lambda i: (0, i)),
                   pl.BlockSpec((SUBLANES, tm), lambda i: (0, i)),
                   pl.BlockSpec((N_EXPERTS, LANES), lambda i: (0, 0))],
        out_shape=[jax.ShapeDtypeStruct((SUBLANES, n), I32), jax.ShapeDtypeStruct((SUBLANES, n), F32),
                   jax.ShapeDtypeStruct((N_EXPERTS, LANES), F32)],
        scratch_shapes=[pltpu.VMEM((N_EXPERTS, LANES), F32)],
        compiler_params=_cparams(("arbitrary",)),
        name="moe_router",
    )(x, wt, bias, tri)


def _slots_kernel(ids_ref, cnt_ref, slots_ref, blk_ref, *, nblk_pad):
    tm = ids_ref.shape[1]
    cnt = cnt_ref[:, 0:1]
    padded = jnp.floor((cnt + (MOE_BLOCK - 1)) / MOE_BLOCK) * MOE_BLOCK
    er = lax.broadcasted_iota(I32, (N_EXPERTS, N_EXPERTS), 0)
    ec = lax.broadcasted_iota(I32, (N_EXPERTS, N_EXPERTS), 1)
    padded_row = jnp.sum(jnp.where(er == ec, padded, 0.0), axis=0, keepdims=True)
    p_start = jnp.sum(jnp.where(ec < er, padded_row, 0.0), axis=1, keepdims=True)
    p_end = p_start + padded
    xrow = lax.broadcasted_iota(I32, (N_EXPERTS, tm), 0)
    e0 = ids_ref[0:1, :]
    e1 = ids_ref[1:2, :]
    s0 = jnp.sum(jnp.where(xrow == e0, p_start, 0.0), axis=0, keepdims=True).astype(I32) + ids_ref[2:3, :]
    s1 = jnp.sum(jnp.where(xrow == e1, p_start, 0.0), axis=0, keepdims=True).astype(I32) + ids_ref[3:4, :]
    slots_ref[...] = jnp.zeros_like(slots_ref)
    slots_ref[0:1, :] = s0
    slots_ref[1:2, :] = s1
    bstart = (lax.broadcasted_iota(I32, (1, nblk_pad), 1) * MOE_BLOCK).astype(F32)
    blk_e = jnp.sum(jnp.where(p_end <= bstart, 1.0, 0.0), axis=0, keepdims=True)
    blk_e = jnp.minimum(blk_e, N_EXPERTS - 1.0).astype(I32)
    n_used = (jnp.sum(padded, axis=0, keepdims=True) / MOE_BLOCK).astype(I32)
    blk_ref[...] = jnp.zeros_like(blk_ref)
    blk_ref[0:1, :] = blk_e
    blk_ref[1:2, :] = jnp.broadcast_to(n_used, (1, nblk_pad))


def _slots(ids, cnt, tm, nblk_pad):
    n = ids.shape[1]
    return pl.pallas_call(
        functools.partial(_slots_kernel, nblk_pad=nblk_pad),
        grid=(n // tm,),
        in_specs=[pl.BlockSpec((SUBLANES, tm), lambda i: (0, i)),
                  pl.BlockSpec((N_EXPERTS, LANES), lambda i: (0, 0))],
        out_specs=[pl.BlockSpec((SUBLANES, tm), lambda i: (0, i)),
                   pl.BlockSpec((SUBLANES, nblk_pad), lambda i: (0, 0))],
        out_shape=[jax.ShapeDtypeStruct((SUBLANES, n), I32), jax.ShapeDtypeStruct((SUBLANES, nblk_pad), I32)],
        compiler_params=_cparams(("arbitrary",)),
        name="moe_slots",
    )(ids, cnt)


def _dispatch_kernel(slots_ref, x_ref, xs_in_ref, xs_ref, pk_ref, sem):
    del xs_in_ref
    i = pl.program_id(0)
    tm = x_ref.shape[0]
    n = pl.num_programs(0) * tm
    pk_ref[...] = _pack_halves(x_ref[...])

    def row_copy(t, which):
        slot = slots_ref[which * n + i * tm + t]
        return pltpu.make_async_copy(pk_ref.at[pl.ds(t, 1)], xs_ref.at[pl.ds(slot, 1)], sem)

    def start(t, carry):
        row_copy(t, 0).start()
        row_copy(t, 1).start()
        return carry

    def wait(t, carry):
        row_copy(t, 0).wait()
        row_copy(t, 1).wait()
        return carry

    lax.fori_loop(0, tm, start, 0)
    lax.fori_loop(0, tm, wait, 0)


def _dispatch(slots_flat, x, cap, tm):
    n, d = x.shape
    xs0 = jnp.zeros((cap, d // 2), U32)
    return pl.pallas_call(
        _dispatch_kernel,
        grid_spec=pltpu.PrefetchScalarGridSpec(
            num_scalar_prefetch=1,
            grid=(n // tm,),
            in_specs=[pl.BlockSpec((tm, d), lambda i, s: (i, 0)),
                      pl.BlockSpec(memory_space=pl.ANY)],
            out_specs=pl.BlockSpec(memory_space=pl.ANY),
            scratch_shapes=[pltpu.VMEM((tm, d // 2), U32), pltpu.SemaphoreType.DMA(())]),
        out_shape=jax.ShapeDtypeStruct((cap, d // 2), U32),
        input_output_aliases={2: 0},
        compiler_params=_cparams(("arbitrary",)),
        name="moe_dispatch",
    )(slots_flat, x, xs0)


def _expert_kernel(blk_ref, xs_ref, wg_ref, wu_ref, wd_ref, y_ref, *, nblk_pad):
    i = pl.program_id(0)
    n_used = blk_ref[nblk_pad]

    @pl.when(i < n_used)
    def _():
        lo, hi = _unpack_halves(xs_ref[...])
        xb = jnp.concatenate([lo.astype(BF16), hi.astype(BF16)], axis=-1)
        g = _dot(xb, wg_ref[0])
        u = _dot(xb, wu_ref[0])
        hmid = (g * jax.nn.sigmoid(g) * u).astype(BF16)
        y_ref[...] = _pack_halves(_dot(hmid, wd_ref[0]))

    @pl.when(i >= n_used)
    def _():
        y_ref[...] = jnp.zeros_like(y_ref)


def _experts(blk_flat, xs, wg, wu, wd, nblk_pad):
    cap, dh = xs.shape
    d = 2 * dh
    de = wg.shape[2]
    return pl.pallas_call(
        functools.partial(_expert_kernel, nblk_pad=nblk_pad),
        grid_spec=pltpu.PrefetchScalarGridSpec(
            num_scalar_prefetch=1,
            grid=(cap // MOE_BLOCK,),
            in_specs=[pl.BlockSpec((MOE_BLOCK, dh), lambda i, blk: (i, 0)),
                      pl.BlockSpec((1, d, de), lambda i, blk: (blk[i], 0, 0)),
                      pl.BlockSpec((1, d, de), lambda i, blk: (blk[i], 0, 0)),
                      pl.BlockSpec((1, de, d), lambda i, blk: (blk[i], 0, 0))],
            out_specs=pl.BlockSpec((MOE_BLOCK, dh), lambda i, blk: (i, 0))),
        out_shape=jax.ShapeDtypeStruct((cap, dh), U32),
        compiler_params=_cparams(("arbitrary",)),
        name="moe_experts",
    )(blk_flat, xs, wg, wu, wd)


def _tail_kernel(slots_ref, x_ref, xb_ref, p_ref, wts_ref, wgate_ref, bgate_ref, wproj_ref, g_ref, b_ref,
                 yb_hbm, y_ref, ybf_ref, rows_ref, sem):
    i = pl.program_id(0)
    tm = x_ref.shape[0]
    n = pl.num_programs(0) * tm

    def row_copy(t, which):
        slot = slots_ref[which * n + i * tm + t]
        return pltpu.make_async_copy(yb_hbm.at[pl.ds(slot, 1)], rows_ref.at[which, pl.ds(t, 1)], sem)

    def start(t, carry):
        row_copy(t, 0).start()
        row_copy(t, 1).start()
        return carry

    def wait(t, carry):
        row_copy(t, 0).wait()
        row_copy(t, 1).wait()
        return carry

    lax.fori_loop(0, tm, start, 0)
    gate = jax.nn.sigmoid(_dot(xb_ref[...], wgate_ref[...]) + bgate_ref[...])
    ple = gate * _dot(p_ref[...].astype(BF16), wproj_ref[...])
    lax.fori_loop(0, tm, wait, 0)
    w = wts_ref[...]
    lo0, hi0 = _unpack_halves(rows_ref[0])
    lo1, hi1 = _unpack_halves(rows_ref[1])
    w0 = w[:, 0:1]
    w1 = w[:, 1:2]
    ffn = jnp.concatenate([lo0 * w0 + lo1 * w1, hi0 * w0 + hi1 * w1], axis=-1)
    z = DN_ALPHA * x_ref[...] + ffn + ple
    y = _layer_norm_rows(z, g_ref[...], b_ref[...])
    y_ref[...] = y
    ybf_ref[...] = y.astype(BF16)


def _layer_tail(slots_flat, x, xb, p, wts_t, wgate, bgate, wproj, g, b, yb, tm):
    n, d = x.shape
    pd = p.shape[1]
    return pl.pallas_call(
        _tail_kernel,
        grid_spec=pltpu.PrefetchScalarGridSpec(
            num_scalar_prefetch=1,
            grid=(n // tm,),
            in_specs=[pl.BlockSpec((tm, d), lambda i, s: (i, 0)),
                      pl.BlockSpec((tm, d), lambda i, s: (i, 0)),
                      pl.BlockSpec((tm, pd), lambda i, s: (i, 0)),
                      pl.BlockSpec((tm, 2), lambda i, s: (i, 0)),
                      pl.BlockSpec((d, d), lambda i, s: (0, 0)),
                      pl.BlockSpec((1, d), lambda i, s: (0, 0)),
                      pl.BlockSpec((pd, d), lambda i, s: (0, 0)),
                      pl.BlockSpec((1, d), lambda i, s: (0, 0)),
                      pl.BlockSpec((1, d), lambda i, s: (0, 0)),
                      pl.BlockSpec(memory_space=pl.ANY)],
            out_specs=[pl.BlockSpec((tm, d), lambda i, s: (i, 0)),
                       pl.BlockSpec((tm, d), lambda i, s: (i, 0))],
            scratch_shapes=[pltpu.VMEM((2, tm, d // 2), U32), pltpu.SemaphoreType.DMA(())]),
        out_shape=[jax.ShapeDtypeStruct((n, d), F32), jax.ShapeDtypeStruct((n, d), BF16)],
        compiler_params=_cparams(("arbitrary",)),
        name="layer_tail",
    )(slots_flat, x, xb, p, wts_t, wgate, bgate.reshape(1, d), wproj, g.reshape(1, d), b.reshape(1, d), yb)


def _rope_table(seq, dim):
    pos = jnp.arange(seq, dtype=F32)
    inv = jnp.exp(jnp.arange(0, dim, 2, dtype=F32) * (-math.log(ROPE_BASE) / dim))
    ang = pos[:, None] * inv[None, :]
    return jnp.cos(ang), jnp.sin(ang)


def _moe_layer(x, xb, p, w_group, b_group, w_router, b_router, w_gate, w_up, w_down,
               ple_w_proj, ple_w_gate, ple_b_gate, ln_g, ln_b):
    n, d = x.shape
    nblk = -(-(2 * n) // MOE_BLOCK) + N_EXPERTS
    nblk_pad = -(-nblk // LANES) * LANES
    cap = nblk * MOE_BLOCK
    wt = jnp.zeros((ROUTER_ROWS, d), F32)
    wt = wt.at[0:N_GROUPS].set(w_group.T)
    wt = wt.at[SUBLANES:].set(w_router.transpose(0, 2, 1).reshape(N_EXPERTS, d))
    bias = jnp.zeros((ROUTER_ROWS, 1), F32)
    bias = bias.at[0:N_GROUPS, 0].set(b_group)
    bias = bias.at[SUBLANES:, 0].set(b_router.reshape(N_EXPERTS))
    ids, wts, cnt = _router(x, wt, bias, tm=512)
    slots, blk = _slots(ids, cnt, tm=min(2048, n), nblk_pad=nblk_pad)
    slots_flat = slots[0:2].reshape(2 * n)
    blk_flat = blk[0:2].reshape(2 * nblk_pad)
    xs = _dispatch(slots_flat, x, cap, tm=256)
    yb = _experts(blk_flat, xs, w_gate.astype(BF16), w_up.astype(BF16), w_down.astype(BF16), nblk_pad)
    return _layer_tail(slots_flat, x, xb, p, wts[0:2].T, ple_w_gate.astype(BF16), ple_b_gate,
                       ple_w_proj.astype(BF16), ln_g, ln_b, yb, tm=256)


def _mixer_ab(x, xres, batch, seq, w_in, rpb, q_norm, w_uq, kv_norm, w_ukv, w_out, ln_g, ln_b):
    d = x.shape[1]
    o1 = 3 * NA_WIDTH
    o2 = o1 + MLA_Q_RANK
    o3 = o2 + MLA_KV_RANK
    half = MLA_ROPE_DIM // 2
    kr = w_in[:, o3:o3 + MLA_ROPE_DIM]
    kr_sw = jnp.concatenate([kr[:, half:], kr[:, :half]], axis=1)
    width = -(-(o3 + 2 * MLA_ROPE_DIM) // 1024) * 1024
    w_in_p = jnp.concatenate([w_in, kr_sw, jnp.zeros((d, width - o3 - 2 * MLA_ROPE_DIM), F32)], axis=1)
    h = _matmul(x, w_in_p.astype(BF16), BF16, tm=512, tn=1024)
    a_out = _na_attention(h, _na_bias_tables(rpb), batch, seq)
    dq = MLA_NOPE_DIM + MLA_ROPE_DIM
    wq = w_uq.reshape(MLA_Q_RANK, MLA_HEADS, dq)
    wq_pe = wq[:, :, MLA_NOPE_DIM:]
    wq_p = jnp.concatenate([wq, wq_pe[:, :, half:], wq_pe[:, :, :half]], axis=2)
    wq_p = wq_p.reshape(MLA_Q_RANK, MLA_HEADS * 2 * LANES).astype(BF16)
    wkv = w_ukv.reshape(MLA_KV_RANK, MLA_HEADS, MLA_NOPE_DIM + MLA_V_DIM)
    wk = wkv[:, :, :MLA_NOPE_DIM].reshape(MLA_KV_RANK, MLA_HEADS * MLA_NOPE_DIM).astype(BF16)
    wvt = wkv[:, :, MLA_NOPE_DIM:].reshape(MLA_KV_RANK, MLA_HEADS * MLA_V_DIM).T.astype(BF16)
    cos, sin = _rope_table(seq, MLA_ROPE_DIM)
    zpad = jnp.zeros((seq, LANES - MLA_ROPE_DIM), F32)
    cosf = jnp.concatenate([cos, cos, zpad], axis=1)
    sinf = jnp.concatenate([-sin, sin, zpad], axis=1)
    q_p, k_p, vt = _mla_prep(h, q_norm, kv_norm, wq_p, wk, wvt, cosf, sinf, o1, seq, tm=512)
    b_out = _mla_attention(q_p, k_p, vt, batch, seq, tq=1024, tk=1024, sub=256)
    w_out_b = w_out.astype(BF16)
    return _proj_ln([a_out, b_out], [w_out_b[:NA_WIDTH], w_out_b[NA_WIDTH:]], xres, ln_g, ln_b, tm=512, nk=1)


def _mixer_c(xb, xres, batch, seq, w_in, log_rate_f, log_rate_b, w_out, ln_g, ln_b):
    cosr, sinr = _rope_table(seq, RET_QK_DIM)
    n_q = RET_HEADS * RET_QK_DIM
    hc = _matmul_rope(xb, w_in.astype(BF16), cosr, sinr, BF16, tm=min(1024, seq), tn=1024, n_q_cols=n_q,
                      n_rope_cols=2 * n_q, gate_col0=2 * n_q + RET_HEADS * RET_V_DIM, head_w=RET_QK_DIM,
                      q_scale=RET_QK_DIM ** -0.5)
    lg = jnp.stack([jnp.log1p(-jnp.exp(log_rate_f.astype(F32))), jnp.log1p(-jnp.exp(log_rate_b.astype(F32)))])
    r = _retention(hc, lg, batch, seq, c_len=256, group=2)
    return _proj_ln([r], [w_out.astype(BF16)], xres, ln_g, ln_b, tm=512, nk=2)


def kernel(x, p, ab_w_in, ab_rpb, ab_q_norm, ab_w_uq, ab_kv_norm, ab_w_ukv, ab_w_out, c_w_in, c_log_rate_f,
           c_log_rate_b, c_w_out, ln1_g, ln1_b, moe_w_group, moe_b_group, moe_w_router, moe_b_router,
           moe_w_gate, moe_w_up, moe_w_down, ple_w_proj, ple_w_gate, ple_b_gate, ln2_g, ln2_b):
    batch, seq, d = x.shape
    n = batch * seq
    xf = x.reshape(n, d)
    xb = None
    for i in range(DEPTH):
        j = i // 2
        if i % 2 == 0:
            src = xf if xb is None else xb
            xf, xb = _mixer_ab(src, xf, batch, seq, ab_w_in[j], ab_rpb[j], ab_q_norm[j], ab_w_uq[j],
                               ab_kv_norm[j], ab_w_ukv[j], ab_w_out[j], ln1_g[i], ln1_b[i])
        else:
            xf, xb = _mixer_c(xb, xf, batch, seq, c_w_in[j], c_log_rate_f[j], c_log_rate_b[j], c_w_out[j],
                              ln1_g[i], ln1_b[i])
        xf, xb = _moe_layer(xf, xb, p[i].reshape(n, -1), moe_w_group[i], moe_b_group[i], moe_w_router[i],
                            moe_b_router[i], moe_w_gate[i], moe_w_up[i], moe_w_down[i], ple_w_proj[i],
                            ple_w_gate[i], ple_b_gate[i], ln2_g[i], ln2_b[i])
    return xf.reshape(batch, seq, d)
```

```python
import functools
import math

import numpy as np
import jax
import jax.numpy as jnp
from jax import lax
from jax.experimental import pallas as pl
from jax.experimental.pallas import tpu as pltpu

DEPTH = 2
GRID_W = 64
NA_HEADS = 8
NA_HEAD_DIM = 128
NA_WIN_H = 8
NA_WIN_W = 16
MLA_HEADS = 8
MLA_Q_RANK = 512
MLA_KV_RANK = 256
MLA_NOPE_DIM = 128
MLA_ROPE_DIM = 64
MLA_V_DIM = 128
RET_HEADS = 8
RET_QK_DIM = 256
RET_V_DIM = 512
RET_CHUNK = 128
N_GROUPS = 4
EXPERTS_PER_GROUP = 8
N_EXPERTS = N_GROUPS * EXPERTS_PER_GROUP
D_EXPERT = 512
MOE_BLOCK = 128
ROPE_BASE = 10000.0
LN_EPS = 1e-5
RMS_EPS = 1e-6
DN_ALPHA = (2 * DEPTH) ** 0.25
NA_WIDTH = NA_HEADS * NA_HEAD_DIM

LANES = 128
SUBLANES = 8
VMEM_LIMIT_BYTES = 60 * 1024 * 1024
MASK_VALUE = -1e30

F32 = jnp.float32
BF16 = jnp.bfloat16
I32 = jnp.int32
U32 = jnp.uint32


def _cparams(sem):
    return pltpu.CompilerParams(dimension_semantics=sem, vmem_limit_bytes=VMEM_LIMIT_BYTES)


def _dot(a, b):
    return jnp.dot(a, b, preferred_element_type=F32)


def _dot_nt(a, b, precision=None):
    return lax.dot_general(a, b, (((1,), (1,)), ((), ())), preferred_element_type=F32,
                           precision=precision)


def _pack_halves(y):
    c = y.shape[1] // 2
    bits = pltpu.bitcast(y.astype(BF16).astype(F32), U32)
    return (bits[:, :c] >> 16) | (bits[:, c:] & jnp.uint32(0xFFFF0000))


def _unpack_halves(w):
    lo = pltpu.bitcast(w << 16, F32)
    hi = pltpu.bitcast(w & jnp.uint32(0xFFFF0000), F32)
    return lo, hi


def _mm_kernel(x_ref, w_ref, o_ref):
    o_ref[...] = _dot(x_ref[...].astype(BF16), w_ref[...]).astype(o_ref.dtype)


def _matmul(x, w, out_dtype, tm, tn):
    m, k = x.shape
    n = w.shape[1]
    return pl.pallas_call(
        _mm_kernel,
        grid=(m // tm, n // tn),
        in_specs=[pl.BlockSpec((tm, k), lambda i, j: (i, 0)),
                  pl.BlockSpec((k, tn), lambda i, j: (0, j))],
        out_specs=pl.BlockSpec((tm, tn), lambda i, j: (i, j)),
        out_shape=jax.ShapeDtypeStruct((m, n), out_dtype),
        compiler_params=_cparams(("parallel", "arbitrary")),
        name="matmul",
    )(x, w)


def _mm_rope_kernel(x_ref, w_ref, cos_ref, sin_ref, o_ref, *, n_q_tiles, n_rope_tiles, first_gate_tile,
                    head_w, q_scale):
    j = pl.program_id(1)
    acc = _dot(x_ref[...].astype(BF16), w_ref[...])

    @pl.when((j >= n_rope_tiles) & (j < first_gate_tile))
    def _():
        o_ref[...] = acc.astype(o_ref.dtype)

    @pl.when(j >= first_gate_tile)
    def _():
        o_ref[...] = (acc * jax.nn.sigmoid(acc)).astype(o_ref.dtype)

    @pl.when(j < n_rope_tiles)
    def _():
        scale = jnp.where(j < n_q_tiles, q_scale, 1.0)
        cos = cos_ref[...] * scale
        sin = sin_ref[...] * scale
        half = head_w // 2
        for c0 in range(0, acc.shape[1], head_w):
            x1 = acc[:, c0:c0 + half]
            x2 = acc[:, c0 + half:c0 + head_w]
            o_ref[:, c0:c0 + half] = (x1 * cos - x2 * sin).astype(o_ref.dtype)
            o_ref[:, c0 + half:c0 + head_w] = (x2 * cos + x1 * sin).astype(o_ref.dtype)


def _matmul_rope(x, w, cos, sin, out_dtype, tm, tn, n_q_cols, n_rope_cols, gate_col0, head_w, q_scale):
    m, k = x.shape
    n = w.shape[1]
    nsb = cos.shape[0] // tm
    return pl.pallas_call(
        functools.partial(_mm_rope_kernel, n_q_tiles=n_q_cols // tn, n_rope_tiles=n_rope_cols // tn,
                          first_gate_tile=gate_col0 // tn, head_w=head_w, q_scale=q_scale),
        grid=(m // tm, n // tn),
        in_specs=[pl.BlockSpec((tm, k), lambda i, j: (i, 0)),
                  pl.BlockSpec((k, tn), lambda i, j: (0, j)),
                  pl.BlockSpec((tm, head_w // 2), lambda i, j: (i % nsb, 0)),
                  pl.BlockSpec((tm, head_w // 2), lambda i, j: (i % nsb, 0))],
        out_specs=pl.BlockSpec((tm, tn), lambda i, j: (i, j)),
        out_shape=jax.ShapeDtypeStruct((m, n), out_dtype),
        compiler_params=_cparams(("parallel", "arbitrary")),
        name="matmul_rope",
    )(x, w, cos, sin)


def _layer_norm_rows(z, g, b):
    mean = jnp.mean(z, axis=-1, keepdims=True)
    zc = z - mean
    var = jnp.mean(zc * zc, axis=-1, keepdims=True)
    return zc * lax.rsqrt(var + LN_EPS) * g + b


def _proj_ln_kernel(*refs, n_act, nk):
    acts = refs[:n_act]
    ws = refs[n_act:2 * n_act]
    x_ref, g_ref, b_ref, y_ref, yb_ref, yp_ref, acc_ref = refs[2 * n_act:]
    k = pl.program_id(1)
    part = _dot(acts[0][...], ws[0][...])
    for a, w in zip(acts[1:], ws[1:]):
        part = part + _dot(a[...], w[...])

    @pl.when(k == 0)
    def _():
        acc_ref[...] = part

    @pl.when(k > 0)
    def _():
        acc_ref[...] = acc_ref[...] + part

    @pl.when(k == nk - 1)
    def _():
        z = DN_ALPHA * x_ref[...] + acc_ref[...]
        y = _layer_norm_rows(z, g_ref[...], b_ref[...])
        y_ref[...] = y
        yb_ref[...] = y.astype(BF16)
        yp_ref[...] = _pack_halves(y)


def _proj_ln(acts, ws, x, g, b, tm, nk):
    m, d = x.shape
    n_act = len(acts)
    in_specs = []
    for a in acts:
        kk = a.shape[1] // nk
        in_specs.append(pl.BlockSpec((tm, kk), lambda i, k: (i, k)))
    for w in ws:
        kk = w.shape[0] // nk
        in_specs.append(pl.BlockSpec((kk, d), lambda i, k: (k, 0)))
    in_specs += [pl.BlockSpec((tm, d), lambda i, k: (i, 0)),
                 pl.BlockSpec((1, d), lambda i, k: (0, 0)),
                 pl.BlockSpec((1, d), lambda i, k: (0, 0))]
    return pl.pallas_call(
        functools.partial(_proj_ln_kernel, n_act=n_act, nk=nk),
        grid=(m // tm, nk),
        in_specs=in_specs,
        out_specs=[pl.BlockSpec((tm, d), lambda i, k: (i, 0)),
                   pl.BlockSpec((tm, d), lambda i, k: (i, 0)),
                   pl.BlockSpec((tm, d // 2), lambda i, k: (i, 0))],
        out_shape=[jax.ShapeDtypeStruct((m, d), F32), jax.ShapeDtypeStruct((m, d), BF16),
                   jax.ShapeDtypeStruct((m, d // 2), U32)],
        scratch_shapes=[pltpu.VMEM((tm, d), F32)],
        compiler_params=_cparams(("parallel", "arbitrary")),
        name="proj_ln",
    )(*acts, *ws, x, g.reshape(1, d), b.reshape(1, d))


def _na_bias_tables(rpb):
    nh = rpb.shape[0]
    c = np.arange(GRID_W)
    cs = np.clip(c - NA_WIN_W // 2, 0, GRID_W - NA_WIN_W)
    kc = np.arange(GRID_W)
    valid = (kc[None, :] >= cs[:, None]) & (kc[None, :] < cs[:, None] + NA_WIN_W)
    dc = kc[None, :] - c[:, None] + NA_WIN_W - 1
    onehot = (dc[:, :, None] == np.arange(2 * NA_WIN_W - 1)[None, None, :]) & valid[:, :, None]
    cols = jnp.einsum("hrd,ckd->hrck", rpb.astype(F32), jnp.asarray(onehot, F32),
                      precision=lax.Precision.HIGHEST)
    cols = jnp.where(jnp.asarray(valid)[None, None], cols, MASK_VALUE)
    tabs = jnp.stack([cols[:, off:off + NA_WIN_H] for off in range(NA_WIN_H)], axis=1)
    return tabs.transpose(0, 1, 3, 2, 4).reshape(nh, NA_WIN_H, GRID_W, NA_WIN_H * GRID_W)


def _na_kernel(q_ref, k_ref, v_ref, bias_ref, o_ref, *, rows, group):
    scale = NA_HEAD_DIM ** -0.5
    nkeys = NA_WIN_H * GRID_W

    def body(i, carry):
        geom, scores = [], []
        for u in range(group):
            r = i * group + u
            rs = jnp.clip(r - NA_WIN_H // 2, 0, rows - NA_WIN_H)
            off = rs - r + NA_WIN_H - 1
            q0 = pl.multiple_of(r * GRID_W, GRID_W)
            k0 = pl.multiple_of(rs * GRID_W, GRID_W)
            geom.append((q0, k0))
            s = _dot_nt(q_ref[pl.ds(q0, GRID_W), :], k_ref[pl.ds(k0, nkeys), :])
            scores.append(s * scale + bias_ref[0, off])
        for (q0, k0), s in zip(geom, scores):
            m = jnp.max(s, axis=-1, keepdims=True)
            p = jnp.exp(s - m)
            l = jnp.sum(p, axis=-1, keepdims=True)
            o = _dot(p.astype(BF16), v_ref[pl.ds(k0, nkeys), :]) / l
            o_ref[pl.ds(q0, GRID_W), :] = o.astype(o_ref.dtype)
        return carry

    lax.fori_loop(0, rows // group, body, 0)


def _na_attention(h, bias_tables, batch, seq):
    rows = seq // GRID_W
    d = NA_HEAD_DIM
    nkeys = NA_WIN_H * GRID_W
    return pl.pallas_call(
        functools.partial(_na_kernel, rows=rows, group=8),
        grid=(batch, NA_HEADS),
        in_specs=[pl.BlockSpec((seq, d), lambda b, hh: (b, hh)),
                  pl.BlockSpec((seq, d), lambda b, hh: (b, NA_HEADS + hh)),
                  pl.BlockSpec((seq, d), lambda b, hh: (b, 2 * NA_HEADS + hh)),
                  pl.BlockSpec((1, NA_WIN_H, GRID_W, nkeys), lambda b, hh: (hh, 0, 0, 0))],
        out_specs=pl.BlockSpec((seq, d), lambda b, hh: (b, hh)),
        out_shape=jax.ShapeDtypeStruct((batch * seq, NA_WIDTH), BF16),
        compiler_params=_cparams(("parallel", "arbitrary")),
        name="na_attention",
    )(h, h, h, bias_tables)


def _rms_rows(x, g):
    return x * lax.rsqrt(jnp.mean(x * x, axis=-1, keepdims=True) + RMS_EPS) * g


def _rope_lanes(t, cosf, sinf):
    return t * cosf + pltpu.roll(t, LANES // 2, 1) * sinf


def _mla_prep_kernel(cq_ref, ckv_ref, kr_ref, gq_ref, gkv_ref, wq_ref, wk_ref, wvt_ref, cos_ref, sin_ref,
                     q_ref, k_ref, vt_ref):
    dq = MLA_NOPE_DIM + MLA_ROPE_DIM
    cosf = cos_ref[...]
    sinf = sin_ref[...]
    cqn = _rms_rows(cq_ref[...].astype(F32), gq_ref[...]).astype(BF16)
    ckvn = _rms_rows(ckv_ref[...].astype(F32), gkv_ref[...]).astype(BF16)
    qf = _dot(cqn, wq_ref[...]) * (dq ** -0.5 * math.log2(math.e))
    kf = _dot(ckvn, wk_ref[...])
    vt_ref[...] = _dot_nt(wvt_ref[...], ckvn).astype(BF16)
    kpe = _rope_lanes(kr_ref[...].astype(F32), cosf, sinf).astype(BF16)
    for hh in range(MLA_HEADS):
        c0 = hh * 2 * LANES
        q_ref[:, c0:c0 + LANES] = qf[:, c0:c0 + LANES].astype(BF16)
        q_ref[:, c0 + LANES:c0 + 2 * LANES] = _rope_lanes(qf[:, c0 + LANES:c0 + 2 * LANES], cosf, sinf).astype(BF16)
        k_ref[:, c0:c0 + LANES] = kf[:, hh * LANES:(hh + 1) * LANES].astype(BF16)
        k_ref[:, c0 + LANES:c0 + 2 * LANES] = kpe


def _mla_prep(h, gq, gkv, wq_p, wk, wvt, cosf, sinf, col_cq, seq, tm):
    n = h.shape[0]
    hw = MLA_HEADS * 2 * LANES
    nsb = seq // tm
    b_cq = col_cq // MLA_Q_RANK
    b_ckv = (col_cq + MLA_Q_RANK) // MLA_KV_RANK
    b_kr = (col_cq + MLA_Q_RANK + MLA_KV_RANK) // LANES
    return pl.pallas_call(
        _mla_prep_kernel,
        grid=(n // tm,),
        in_specs=[pl.BlockSpec((tm, MLA_Q_RANK), lambda i: (i, b_cq)),
                  pl.BlockSpec((tm, MLA_KV_RANK), lambda i: (i, b_ckv)),
                  pl.BlockSpec((tm, LANES), lambda i: (i, b_kr)),
                  pl.BlockSpec((1, MLA_Q_RANK), lambda i: (0, 0)),
                  pl.BlockSpec((1, MLA_KV_RANK), lambda i: (0, 0)),
                  pl.BlockSpec((MLA_Q_RANK, hw), lambda i: (0, 0)),
                  pl.BlockSpec((MLA_KV_RANK, MLA_HEADS * LANES), lambda i: (0, 0)),
                  pl.BlockSpec((MLA_HEADS * MLA_V_DIM, MLA_KV_RANK), lambda i: (0, 0)),
                  pl.BlockSpec((tm, LANES), lambda i: (i % nsb, 0)),
                  pl.BlockSpec((tm, LANES), lambda i: (i % nsb, 0))],
        out_specs=[pl.BlockSpec((tm, hw), lambda i: (i, 0)),
                   pl.BlockSpec((tm, hw), lambda i: (i, 0)),
                   pl.BlockSpec((MLA_HEADS * MLA_V_DIM, tm), lambda i: (0, i))],
        out_shape=[jax.ShapeDtypeStruct((n, hw), BF16), jax.ShapeDtypeStruct((n, hw), BF16),
                   jax.ShapeDtypeStruct((MLA_HEADS * MLA_V_DIM, n), BF16)],
        compiler_params=_cparams(("parallel",)),
        name="mla_prep",
    )(h, h, h, gq.reshape(1, -1), gkv.reshape(1, -1), wq_p, wk, wvt, cosf, sinf)


def _mla_attn_kernel(q_ref, k_ref, vt_ref, o_ref, *, tk, sub):
    nchunk = k_ref.shape[0] // tk
    tq = q_ref.shape[0]
    nsub = tq // sub
    qs = [q_ref[s * sub:(s + 1) * sub, :] for s in range(nsub)]
    m = [jnp.full((1, sub), MASK_VALUE, F32) for _ in range(nsub)]
    l = [jnp.zeros((1, sub), F32) for _ in range(nsub)]
    acc = [jnp.zeros((MLA_V_DIM, sub), F32) for _ in range(nsub)]

    def scores(s, c):
        return _dot_nt(k_ref[c * tk:(c + 1) * tk, :], qs[s])

    st_next = [scores(s, 0) for s in range(nsub)]
    for c in range(nchunk):
        for s in range(nsub):
            st = st_next[s]
            m_new = jnp.maximum(m[s], jnp.max(st, axis=0, keepdims=True))
            a = jnp.exp2(m[s] - m_new)
            p = jnp.exp2(st - m_new)
            l[s] = a * l[s] + jnp.sum(p, axis=0, keepdims=True)
            if c + 1 < nchunk:
                st_next[s] = scores(s, c + 1)
            acc[s] = a * acc[s] + _dot(vt_ref[:, c * tk:(c + 1) * tk], p.astype(BF16))
            m[s] = m_new
    for s in range(nsub):
        o_ref[s * sub:(s + 1) * sub, :] = (acc[s] / l[s]).T.astype(o_ref.dtype)


def _mla_attention(q_p, k_p, vt, batch, seq, tq, tk, sub):
    n = q_p.shape[0]
    nqb = seq // tq
    return pl.pallas_call(
        functools.partial(_mla_attn_kernel, tk=tk, sub=sub),
        grid=(batch, MLA_HEADS, nqb),
        in_specs=[pl.BlockSpec((tq, 2 * LANES), lambda b, hh, i: (b * nqb + i, hh)),
                  pl.BlockSpec((seq, 2 * LANES), lambda b, hh, i: (b, hh)),
                  pl.BlockSpec((MLA_V_DIM, seq), lambda b, hh, i: (hh, b))],
        out_specs=pl.BlockSpec((tq, MLA_V_DIM), lambda b, hh, i: (b * nqb + i, hh)),
        out_shape=jax.ShapeDtypeStruct((n, MLA_HEADS * MLA_V_DIM), BF16),
        compiler_params=_cparams(("parallel", "parallel", "arbitrary")),
        name="mla_attention",
    )(q_p, k_p, vt)


def _ret_kernel(lg_ref, q_ref, k_ref, v_ref, g_ref, o_ref, acc_ref, st_ref, *, c_len, group):
    nchunk = q_ref.shape[0] // c_len
    hh = pl.program_id(1)
    lgf = lg_ref[0, hh]
    lgb = lg_ref[1, hh]
    ii = lax.broadcasted_iota(I32, (c_len, c_len), 0).astype(F32)
    jj = lax.broadcasted_iota(I32, (c_len, c_len), 1).astype(F32)
    rel = ii - jj
    dmat = jnp.where(rel >= 0, jnp.exp(lgf * jnp.maximum(rel, 0.0)), jnp.exp(lgb * jnp.maximum(-rel, 0.0)))
    pos = lax.broadcasted_iota(I32, (c_len, 1), 0).astype(F32)
    qdec_f = jnp.exp(lgf * (pos + 1.0))
    kdec_f = jnp.exp(lgf * (c_len - 1.0 - pos))
    qdec_b = jnp.exp(lgb * (c_len - pos))
    kdec_b = jnp.exp(lgb * pos)
    full_chunk = jnp.full((1, RET_V_DIM), float(c_len), F32)
    cdec_f = jnp.exp(lgf * full_chunk)
    cdec_b = jnp.exp(lgb * full_chunk)

    def decayed_keys_t(t0, kdec):
        return (k_ref[pl.ds(t0, c_len), :].astype(F32) * kdec).T.astype(BF16)

    st_ref[...] = jnp.zeros_like(st_ref)

    def bwd_body(i, carry):
        t0s = [pl.multiple_of((nchunk - 1 - (i * group + u)) * c_len, c_len) for u in range(group)]
        upd = [_dot(decayed_keys_t(t0, kdec_b), v_ref[pl.ds(t0, c_len), :]) for t0 in t0s]
        for t0, u_c in zip(t0s, upd):
            st = st_ref[...]
            acc_ref[pl.ds(t0, c_len), :] = _dot(q_ref[pl.ds(t0, c_len), :], st.astype(BF16)) * qdec_b
            st_ref[...] = st * cdec_b + u_c
        return carry

    lax.fori_loop(0, nchunk // group, bwd_body, 0)
    st_ref[...] = jnp.zeros_like(st_ref)

    def fwd_body(i, carry):
        t0s = [pl.multiple_of((i * group + u) * c_len, c_len) for u in range(group)]
        scs = [_dot_nt(q_ref[pl.ds(t0, c_len), :], k_ref[pl.ds(t0, c_len), :]) * dmat for t0 in t0s]
        upd = [_dot(decayed_keys_t(t0, kdec_f), v_ref[pl.ds(t0, c_len), :]) for t0 in t0s]
        for t0, sc, u_c in zip(t0s, scs, upd):
            st = st_ref[...]
            r = (_dot(sc.astype(BF16), v_ref[pl.ds(t0, c_len), :])
                 + _dot(q_ref[pl.ds(t0, c_len), :], st.astype(BF16)) * qdec_f
                 + acc_ref[pl.ds(t0, c_len), :])
            st_ref[...] = st * cdec_f + u_c
            r = r - jnp.mean(r, axis=-1, keepdims=True)
            r = r * lax.rsqrt(jnp.mean(r * r, axis=-1, keepdims=True) + LN_EPS)
            o_ref[pl.ds(t0, c_len), :] = (g_ref[pl.ds(t0, c_len), :].astype(F32) * r).astype(o_ref.dtype)
        return carry

    lax.fori_loop(0, nchunk // group, fwd_body, 0)


def _retention(hc, lg, batch, seq, c_len, group):
    dk, dv, nh = RET_QK_DIM, RET_V_DIM, RET_HEADS
    v_blk0 = (2 * nh * dk) // dv
    return pl.pallas_call(
        functools.partial(_ret_kernel, c_len=c_len, group=group),
        grid=(batch, nh),
        in_specs=[pl.BlockSpec(memory_space=pltpu.SMEM),
                  pl.BlockSpec((seq, dk), lambda b, hh: (b, hh)),
                  pl.BlockSpec((seq, dk), lambda b, hh: (b, nh + hh)),
                  pl.BlockSpec((seq, dv), lambda b, hh: (b, v_blk0 + hh)),
                  pl.BlockSpec((seq, dv), lambda b, hh: (b, v_blk0 + nh + hh))],
        out_specs=pl.BlockSpec((seq, dv), lambda b, hh: (b, hh)),
        scratch_shapes=[pltpu.VMEM((seq, dv), F32), pltpu.VMEM((dk, dv), F32)],
        out_shape=jax.ShapeDtypeStruct((batch * seq, nh * dv), BF16),
        compiler_params=_cparams(("parallel", "arbitrary")),
        name="retention",
    )(lg, hc, hc, hc, hc)


ROUTER_ROWS = 40


def _router_kernel(x_ref, wt_ref, b_ref, tri_ref, ids_ref, wts_ref, cnt_ref, carry_ref):
    i = pl.program_id(0)
    tm = x_ref.shape[0]

    @pl.when(i == 0)
    def _():
        carry_ref[...] = jnp.zeros_like(carry_ref)

    logits = _dot_nt(wt_ref[...], x_ref[...]) + b_ref[...]
    grow = lax.broadcasted_iota(I32, (SUBLANES, tm), 0).astype(F32)
    gl = jnp.where(grow < N_GROUPS, logits[0:SUBLANES], MASK_VALUE)
    gmax = jnp.max(gl, axis=0, keepdims=True)
    gsum = jnp.sum(jnp.exp(gl - gmax), axis=0, keepdims=True)
    p_group = 1.0 / gsum
    g_idx = jnp.min(jnp.where(gl == gmax, grow, float(N_GROUPS)), axis=0, keepdims=True)
    sel = jnp.zeros((EXPERTS_PER_GROUP, tm), F32)
    for g in range(N_GROUPS):
        r0 = SUBLANES + g * EXPERTS_PER_GROUP
        sel = sel + jnp.where(g_idx == float(g), logits[r0:r0 + EXPERTS_PER_GROUP], 0.0)
    erow = lax.broadcasted_iota(I32, (EXPERTS_PER_GROUP, tm), 0).astype(F32)
    smax = jnp.max(sel, axis=0, keepdims=True)
    sexp = jnp.exp(sel - smax)
    probs = sexp / jnp.sum(sexp, axis=0, keepdims=True)
    p1 = jnp.max(probs, axis=0, keepdims=True)
    i1 = jnp.min(jnp.where(probs == p1, erow, float(EXPERTS_PER_GROUP)), axis=0, keepdims=True)
    rest = jnp.where(erow == i1, -1.0, probs)
    p2 = jnp.max(rest, axis=0, keepdims=True)
    i2 = jnp.min(jnp.where(rest == p2, erow, float(EXPERTS_PER_GROUP)), axis=0, keepdims=True)
    denom = p1 + p2
    e0 = g_idx * EXPERTS_PER_GROUP + i1
    e1 = g_idx * EXPERTS_PER_GROUP + i2

    xrow = lax.broadcasted_iota(I32, (N_EXPERTS, tm), 0).astype(F32)
    oh0 = jnp.where(xrow == e0, 1.0, 0.0)
    oh1 = jnp.where(xrow == e1, 1.0, 0.0)
    onehot = oh0 + oh1
    before = _dot(onehot.astype(BF16), tri_ref[...]) + carry_ref[:, 0:1]
    rank0 = jnp.sum(oh0 * before, axis=0, keepdims=True)
    rank1 = jnp.sum(oh1 * before, axis=0, keepdims=True)
    carry_ref[...] = carry_ref[...] + jnp.sum(onehot, axis=1, keepdims=True)

    ids_ref[...] = jnp.zeros_like(ids_ref)
    ids_ref[0:1, :] = e0.astype(I32)
    ids_ref[1:2, :] = e1.astype(I32)
    ids_ref[2:3, :] = rank0.astype(I32)
    ids_ref[3:4, :] = rank1.astype(I32)
    wts_ref[...] = jnp.zeros_like(wts_ref)
    wts_ref[0:1, :] = p_group * p1 / denom
    wts_ref[1:2, :] = p_group * p2 / denom
    cnt_ref[...] = carry_ref[...]


def _router(x, wt, bias, tm):
    n, d = x.shape
    tri = jnp.asarray(np.triu(np.ones((tm, tm), np.float32), 1), BF16)
    return pl.pallas_call(
        _router_kernel,
        grid=(n // tm,),
        in_specs=[pl.BlockSpec((tm, d), lambda i: (i, 0)),
                  pl.BlockSpec((ROUTER_ROWS, d), lambda i: (0, 0)),
                  pl.BlockSpec((ROUTER_ROWS, 1), lambda i: (0, 0)),
                  pl.BlockSpec((tm, tm), lambda i: (0, 0))],
        out_specs=[pl.BlockSpec((SUBLANES, tm), lambda i: (0, i)),
                   pl.BlockSpec((SUBLANES, tm), lambda i: (0, i)),
                   pl.BlockSpec((N_EXPERTS, LANES), lambda i: (0, 0))],
        out_shape=[jax.ShapeDtypeStruct((SUBLANES, n), I32), jax.ShapeDtypeStruct((SUBLANES, n), F32),
                   jax.ShapeDtypeStruct((N_EXPERTS, LANES), F32)],
        scratch_shapes=[pltpu.VMEM((N_EXPERTS, LANES), F32)],
        compiler_params=_cparams(("arbitrary",)),
        name="moe_router",
    )(x, wt, bias, tri)


def _slots_kernel(ids_ref, cnt_ref, slots_ref, blk_ref, *, nblk_pad):
    tm = ids_ref.shape[1]
    cnt = cnt_ref[:, 0:1]
    padded = jnp.floor((cnt + (MOE_BLOCK - 1)) / MOE_BLOCK) * MOE_BLOCK
    er = lax.broadcasted_iota(I32, (N_EXPERTS, N_EXPERTS), 0)
    ec = lax.broadcasted_iota(I32, (N_EXPERTS, N_EXPERTS), 1)
    padded_row = jnp.sum(jnp.where(er == ec, padded, 0.0), axis=0, keepdims=True)
    p_start = jnp.sum(jnp.where(ec < er, padded_row, 0.0), axis=1, keepdims=True)
    p_end = p_start + padded
    xrow = lax.broadcasted_iota(I32, (N_EXPERTS, tm), 0)
    e0 = ids_ref[0:1, :]
    e1 = ids_ref[1:2, :]
    s0 = jnp.sum(jnp.where(xrow == e0, p_start, 0.0), axis=0, keepdims=True).astype(I32) + ids_ref[2:3, :]
    s1 = jnp.sum(jnp.where(xrow == e1, p_start, 0.0), axis=0, keepdims=True).astype(I32) + ids_ref[3:4, :]
    slots_ref[...] = jnp.zeros_like(slots_ref)
    slots_ref[0:1, :] = s0
    slots_ref[1:2, :] = s1
    bstart = (lax.broadcasted_iota(I32, (1, nblk_pad), 1) * MOE_BLOCK).astype(F32)
    blk_e = jnp.sum(jnp.where(p_end <= bstart, 1.0, 0.0), axis=0, keepdims=True)
    blk_e = jnp.minimum(blk_e, N_EXPERTS - 1.0).astype(I32)
    n_used = (jnp.sum(padded, axis=0, keepdims=True) / MOE_BLOCK).astype(I32)
    blk_ref[...] = jnp.zeros_like(blk_ref)
    blk_ref[0:1, :] = blk_e
    blk_ref[1:2, :] = jnp.broadcast_to(n_used, (1, nblk_pad))


def _slots(ids, cnt, tm, nblk_pad):
    n = ids.shape[1]
    return pl.pallas_call(
        functools.partial(_slots_kernel, nblk_pad=nblk_pad),
        grid=(n // tm,),
        in_specs=[pl.BlockSpec((SUBLANES, tm), lambda i: (0, i)),
                  pl.BlockSpec((N_EXPERTS, LANES), lambda i: (0, 0))],
        out_specs=[pl.BlockSpec((SUBLANES, tm), lambda i: (0, i)),
                   pl.BlockSpec((SUBLANES, nblk_pad), lambda i: (0, 0))],
        out_shape=[jax.ShapeDtypeStruct((SUBLANES, n), I32), jax.ShapeDtypeStruct((SUBLANES, nblk_pad), I32)],
        compiler_params=_cparams(("arbitrary",)),
        name="moe_slots",
    )(ids, cnt)


def _slot_tokens_kernel(slots_ref, tok_ref, *, n, cap):
    def zero(j, carry):
        tok_ref[j] = 0
        return carry

    def scatter(t, carry):
        tok_ref[slots_ref[t]] = t
        tok_ref[slots_ref[n + t]] = t
        return carry

    lax.fori_loop(0, cap, zero, 0, unroll=8)
    lax.fori_loop(0, n, scatter, 0, unroll=8)


def _slot_tokens(slots_flat, n, cap):
    return pl.pallas_call(
        functools.partial(_slot_tokens_kernel, n=n, cap=cap),
        grid_spec=pltpu.PrefetchScalarGridSpec(
            num_scalar_prefetch=1,
            grid=(1,),
            in_specs=[],
            out_specs=pl.BlockSpec(memory_space=pltpu.SMEM)),
        out_shape=jax.ShapeDtypeStruct((cap,), I32),
        compiler_params=_cparams(("arbitrary",)),
        name="moe_slot_tokens",
    )(slots_flat)


def _expert_kernel(blk_ref, tok_ref, xpk_hbm, wg_ref, wu_ref, wd_ref, y_ref, xbuf, wgb, wub, wdb, sems, *,
                   nblk_pad):
    i = pl.program_id(0)
    n_used = blk_ref[nblk_pad]
    cast_rows = 256

    def start_rows(block, buf):
        for r in range(MOE_BLOCK):
            pltpu.make_async_copy(xpk_hbm.at[pl.ds(tok_ref[block * MOE_BLOCK + r], 1)],
                                  xbuf.at[buf, pl.ds(r, 1)], sems.at[buf]).start()

    def wait_rows(buf):
        pltpu.make_async_copy(xpk_hbm.at[pl.ds(0, MOE_BLOCK)], xbuf.at[buf], sems.at[buf]).wait()

    @pl.when(i == 0)
    def _():
        start_rows(0, 0)

    @pl.when((i < n_used) & ((i == 0) | (blk_ref[i] != blk_ref[jnp.maximum(i - 1, 0)])))
    def _():
        for src, dst in ((wg_ref, wgb), (wu_ref, wub), (wd_ref, wdb)):
            for r0 in range(0, dst.shape[0], cast_rows):
                dst[r0:r0 + cast_rows, :] = src[0, 0, r0:r0 + cast_rows, :].astype(BF16)

    @pl.when(i < n_used)
    def _():
        buf = i % 2
        nxt = jnp.minimum(i + 1, n_used - 1)
        wait_rows(buf)
        start_rows(nxt, 1 - buf)
        lo, hi = _unpack_halves(xbuf[buf])
        xb = jnp.concatenate([lo.astype(BF16), hi.astype(BF16)], axis=-1)
        g = _dot(xb, wgb[...])
        u = _dot(xb, wub[...])
        hmid = (g * jax.nn.sigmoid(g) * u).astype(BF16)
        y_ref[...] = _pack_halves(_dot(hmid, wdb[...]))

    @pl.when(i == n_used - 1)
    def _():
        wait_rows(1 - i % 2)

    @pl.when(i >= n_used)
    def _():
        y_ref[...] = jnp.zeros_like(y_ref)


def _experts(blk_flat, slot_tok, xpk, wg, wu, wd, layer, nblk_pad):
    dh = xpk.shape[1]
    d = 2 * dh
    de = wg.shape[3]
    cap = slot_tok.shape[0]
    return pl.pallas_call(
        functools.partial(_expert_kernel, nblk_pad=nblk_pad),
        grid_spec=pltpu.PrefetchScalarGridSpec(
            num_scalar_prefetch=2,
            grid=(cap // MOE_BLOCK,),
            in_specs=[pl.BlockSpec(memory_space=pl.ANY),
                      pl.BlockSpec((1, 1, d, de), lambda i, blk, tok: (layer, blk[i], 0, 0)),
                      pl.BlockSpec((1, 1, d, de), lambda i, blk, tok: (layer, blk[i], 0, 0)),
                      pl.BlockSpec((1, 1, de, d), lambda i, blk, tok: (layer, blk[i], 0, 0))],
            out_specs=pl.BlockSpec((MOE_BLOCK, dh), lambda i, blk, tok: (i, 0)),
            scratch_shapes=[pltpu.VMEM((2, MOE_BLOCK, dh), U32),
                            pltpu.VMEM((d, de), BF16), pltpu.VMEM((d, de), BF16), pltpu.VMEM((de, d), BF16),
                            pltpu.SemaphoreType.DMA((2,))]),
        out_shape=jax.ShapeDtypeStruct((cap, dh), U32),
        compiler_params=_cparams(("arbitrary",)),
        name="moe_experts",
    )(blk_flat, slot_tok, xpk, wg, wu, wd)


def _tail_kernel(slots_ref, x_ref, xb_ref, p_ref, wts_ref, wgate_ref, bgate_ref, wproj_ref, g_ref, b_ref,
                 yb_hbm, y_ref, ybf_ref, rows_ref, sems):
    i = pl.program_id(0)
    nsteps = pl.num_programs(0)
    tm = x_ref.shape[0]
    n = nsteps * tm

    def start_rows(step, buf):
        for t in range(tm):
            for which in range(2):
                pltpu.make_async_copy(yb_hbm.at[pl.ds(slots_ref[which * n + step * tm + t], 1)],
                                      rows_ref.at[buf, which, pl.ds(t, 1)], sems.at[buf]).start()

    def wait_rows(buf):
        for which in range(2):
            pltpu.make_async_copy(yb_hbm.at[pl.ds(0, tm)], rows_ref.at[buf, which], sems.at[buf]).wait()

    @pl.when(i == 0)
    def _():
        start_rows(0, 0)

    buf = i % 2
    nxt = jnp.minimum(i + 1, nsteps - 1)
    wait_rows(buf)
    start_rows(nxt, 1 - buf)
    gate = jax.nn.sigmoid(_dot(xb_ref[...], wgate_ref[...]) + bgate_ref[...])
    ple = gate * _dot(p_ref[0].astype(BF16), wproj_ref[...])
    w = wts_ref[...]
    lo0, hi0 = _unpack_halves(rows_ref[buf, 0])
    lo1, hi1 = _unpack_halves(rows_ref[buf, 1])
    w0 = w[:, 0:1]
    w1 = w[:, 1:2]
    ffn = jnp.concatenate([lo0 * w0 + lo1 * w1, hi0 * w0 + hi1 * w1], axis=-1)
    z = DN_ALPHA * x_ref[...] + ffn + ple
    y = _layer_norm_rows(z, g_ref[...], b_ref[...])
    y_ref[...] = y
    ybf_ref[...] = y.astype(BF16)

    @pl.when(i == nsteps - 1)
    def _():
        wait_rows(1 - buf)


def _layer_tail(slots_flat, x, xb, p, layer, wts_t, wgate, bgate, wproj, g, b, yb, tm):
    n, d = x.shape
    pd = p.shape[2]
    return pl.pallas_call(
        _tail_kernel,
        grid_spec=pltpu.PrefetchScalarGridSpec(
            num_scalar_prefetch=1,
            grid=(n // tm,),
            in_specs=[pl.BlockSpec((tm, d), lambda i, s: (i, 0)),
                      pl.BlockSpec((tm, d), lambda i, s: (i, 0)),
                      pl.BlockSpec((1, tm, pd), lambda i, s: (layer, i, 0)),
                      pl.BlockSpec((tm, 2), lambda i, s: (i, 0)),
                      pl.BlockSpec((d, d), lambda i, s: (0, 0)),
                      pl.BlockSpec((1, d), lambda i, s: (0, 0)),
                      pl.BlockSpec((pd, d), lambda i, s: (0, 0)),
                      pl.BlockSpec((1, d), lambda i, s: (0, 0)),
                      pl.BlockSpec((1, d), lambda i, s: (0, 0)),
                      pl.BlockSpec(memory_space=pl.ANY)],
            out_specs=[pl.BlockSpec((tm, d), lambda i, s: (i, 0)),
                       pl.BlockSpec((tm, d), lambda i, s: (i, 0))],
            scratch_shapes=[pltpu.VMEM((2, 2, tm, d // 2), U32), pltpu.SemaphoreType.DMA((2,))]),
        out_shape=[jax.ShapeDtypeStruct((n, d), F32), jax.ShapeDtypeStruct((n, d), BF16)],
        compiler_params=_cparams(("arbitrary",)),
        name="layer_tail",
    )(slots_flat, x, xb, p, wts_t, wgate, bgate.reshape(1, d), wproj, g.reshape(1, d), b.reshape(1, d), yb)


def _rope_table(seq, dim):
    pos = jnp.arange(seq, dtype=F32)
    inv = jnp.exp(jnp.arange(0, dim, 2, dtype=F32) * (-math.log(ROPE_BASE) / dim))
    ang = pos[:, None] * inv[None, :]
    return jnp.cos(ang), jnp.sin(ang)


def _moe_layer(x, xb, xpk, p, layer, w_group, b_group, w_router, b_router, w_gate, w_up, w_down,
               ple_w_proj, ple_w_gate, ple_b_gate, ln_g, ln_b):
    n, d = x.shape
    nblk = -(-(2 * n) // MOE_BLOCK) + N_EXPERTS
    nblk_pad = -(-nblk // LANES) * LANES
    cap = nblk * MOE_BLOCK
    wt = jnp.zeros((ROUTER_ROWS, d), F32)
    wt = wt.at[0:N_GROUPS].set(w_group.T)
    wt = wt.at[SUBLANES:].set(w_router.transpose(0, 2, 1).reshape(N_EXPERTS, d))
    bias = jnp.zeros((ROUTER_ROWS, 1), F32)
    bias = bias.at[0:N_GROUPS, 0].set(b_group)
    bias = bias.at[SUBLANES:, 0].set(b_router.reshape(N_EXPERTS))
    ids, wts, cnt = _router(xb, wt.astype(BF16), bias, tm=512)
    slots, blk = _slots(ids, cnt, tm=min(2048, n), nblk_pad=nblk_pad)
    slots_flat = slots[0:2].reshape(2 * n)
    blk_flat = blk[0:2].reshape(2 * nblk_pad)
    slot_tok = _slot_tokens(slots_flat, n, cap)
    yb = _experts(blk_flat, slot_tok, xpk, w_gate, w_up, w_down, layer, nblk_pad)
    return _layer_tail(slots_flat, x, xb, p, layer, wts[0:2].T, ple_w_gate.astype(BF16), ple_b_gate,
                       ple_w_proj.astype(BF16), ln_g, ln_b, yb, tm=256)


def _mixer_ab(x, xres, batch, seq, w_in, rpb, q_norm, w_uq, kv_norm, w_ukv, w_out, ln_g, ln_b):
    d = x.shape[1]
    o1 = 3 * NA_WIDTH
    o2 = o1 + MLA_Q_RANK
    o3 = o2 + MLA_KV_RANK
    half = MLA_ROPE_DIM // 2
    kr = w_in[:, o3:o3 + MLA_ROPE_DIM]
    kr_sw = jnp.concatenate([kr[:, half:], kr[:, :half]], axis=1)
    width = -(-(o3 + 2 * MLA_ROPE_DIM) // 1024) * 1024
    w_in_p = jnp.concatenate([w_in, kr_sw, jnp.zeros((d, width - o3 - 2 * MLA_ROPE_DIM), F32)], axis=1)
    h = _matmul(x, w_in_p.astype(BF16), BF16, tm=512, tn=1024)
    a_out = _na_attention(h, _na_bias_tables(rpb), batch, seq)
    dq = MLA_NOPE_DIM + MLA_ROPE_DIM
    wq = w_uq.reshape(MLA_Q_RANK, MLA_HEADS, dq)
    wq_pe = wq[:, :, MLA_NOPE_DIM:]
    wq_p = jnp.concatenate([wq, wq_pe[:, :, half:], wq_pe[:, :, :half]], axis=2)
    wq_p = wq_p.reshape(MLA_Q_RANK, MLA_HEADS * 2 * LANES).astype(BF16)
    wkv = w_ukv.reshape(MLA_KV_RANK, MLA_HEADS, MLA_NOPE_DIM + MLA_V_DIM)
    wk = wkv[:, :, :MLA_NOPE_DIM].reshape(MLA_KV_RANK, MLA_HEADS * MLA_NOPE_DIM).astype(BF16)
    wvt = wkv[:, :, MLA_NOPE_DIM:].reshape(MLA_KV_RANK, MLA_HEADS * MLA_V_DIM).T.astype(BF16)
    cos, sin = _rope_table(seq, MLA_ROPE_DIM)
    zpad = jnp.zeros((seq, LANES - MLA_ROPE_DIM), F32)
    cosf = jnp.concatenate([cos, cos, zpad], axis=1)
    sinf = jnp.concatenate([-sin, sin, zpad], axis=1)
    q_p, k_p, vt = _mla_prep(h, q_norm, kv_norm, wq_p, wk, wvt, cosf, sinf, o1, seq, tm=512)
    b_out = _mla_attention(q_p, k_p, vt, batch, seq, tq=1024, tk=1024, sub=256)
    w_out_b = w_out.astype(BF16)
    return _proj_ln([a_out, b_out], [w_out_b[:NA_WIDTH], w_out_b[NA_WIDTH:]], xres, ln_g, ln_b, tm=512, nk=1)


def _mixer_c(xb, xres, batch, seq, w_in, log_rate_f, log_rate_b, w_out, ln_g, ln_b):
    cosr, sinr = _rope_table(seq, RET_QK_DIM)
    n_q = RET_HEADS * RET_QK_DIM
    hc = _matmul_rope(xb, w_in.astype(BF16), cosr, sinr, BF16, tm=min(1024, seq), tn=1024, n_q_cols=n_q,
                      n_rope_cols=2 * n_q, gate_col0=2 * n_q + RET_HEADS * RET_V_DIM, head_w=RET_QK_DIM,
                      q_scale=RET_QK_DIM ** -0.5)
    lg = jnp.stack([jnp.log1p(-jnp.exp(log_rate_f.astype(F32))), jnp.log1p(-jnp.exp(log_rate_b.astype(F32)))])
    r = _retention(hc, lg, batch, seq, c_len=256, group=2)
    return _proj_ln([r], [w_out.astype(BF16)], xres, ln_g, ln_b, tm=512, nk=2)


def kernel(x, p, ab_w_in, ab_rpb, ab_q_norm, ab_w_uq, ab_kv_norm, ab_w_ukv, ab_w_out, c_w_in, c_log_rate_f,
           c_log_rate_b, c_w_out, ln1_g, ln1_b, moe_w_group, moe_b_group, moe_w_router, moe_b_router,
           moe_w_gate, moe_w_up, moe_w_down, ple_w_proj, ple_w_gate, ple_b_gate, ln2_g, ln2_b):
    batch, seq, d = x.shape
    n = batch * seq
    xf = x.reshape(n, d)
    p_flat = p.reshape(DEPTH, n, -1)
    xb = None
    for i in range(DEPTH):
        j = i // 2
        if i % 2 == 0:
            src = xf if xb is None else xb
            xf, xb, xpk = _mixer_ab(src, xf, batch, seq, ab_w_in[j], ab_rpb[j], ab_q_norm[j], ab_w_uq[j],
                                    ab_kv_norm[j], ab_w_ukv[j], ab_w_out[j], ln1_g[i], ln1_b[i])
        else:
            xf, xb, xpk = _mixer_c(xb, xf, batch, seq, c_w_in[j], c_log_rate_f[j], c_log_rate_b[j], c_w_out[j],
                                   ln1_g[i], ln1_b[i])
        xf, xb = _moe_layer(xf, xb, xpk, p_flat, i, moe_w_group[i], moe_b_group[i], moe_w_router[i],
                            moe_b_router[i], moe_w_gate, moe_w_up, moe_w_down, ple_w_proj[i],
                            ple_w_gate[i], ple_b_gate[i], ln2_g[i], ln2_b[i])
    return xf.reshape(batch, seq, d)
```

```python
import functools
import math

import numpy as np
import jax
import jax.numpy as jnp
from jax import lax
from jax.experimental import pallas as pl
from jax.experimental.pallas import tpu as pltpu

DEPTH = 2
GRID_W = 64
NA_HEADS = 8
NA_HEAD_DIM = 128
NA_WIN_H = 8
NA_WIN_W = 16
MLA_HEADS = 8
MLA_Q_RANK = 512
MLA_KV_RANK = 256
MLA_NOPE_DIM = 128
MLA_ROPE_DIM = 64
MLA_V_DIM = 128
RET_HEADS = 8
RET_QK_DIM = 256
RET_V_DIM = 512
RET_CHUNK = 128
N_GROUPS = 4
EXPERTS_PER_GROUP = 8
N_EXPERTS = N_GROUPS * EXPERTS_PER_GROUP
D_EXPERT = 512
MOE_BLOCK = 128
ROPE_BASE = 10000.0
LN_EPS = 1e-5
RMS_EPS = 1e-6
DN_ALPHA = (2 * DEPTH) ** 0.25
NA_WIDTH = NA_HEADS * NA_HEAD_DIM

LANES = 128
SUBLANES = 8
VMEM_LIMIT_BYTES = 60 * 1024 * 1024
MASK_VALUE = -1e30

F32 = jnp.float32
BF16 = jnp.bfloat16
I32 = jnp.int32
U32 = jnp.uint32


def _cparams(sem):
    return pltpu.CompilerParams(dimension_semantics=sem, vmem_limit_bytes=VMEM_LIMIT_BYTES)


def _dot(a, b):
    return jnp.dot(a, b, preferred_element_type=F32)


def _dot_nt(a, b, precision=None):
    return lax.dot_general(a, b, (((1,), (1,)), ((), ())), preferred_element_type=F32,
                           precision=precision)


def _pack_halves(y):
    c = y.shape[1] // 2
    bits = pltpu.bitcast(y.astype(BF16).astype(F32), U32)
    return (bits[:, :c] >> 16) | (bits[:, c:] & jnp.uint32(0xFFFF0000))


def _unpack_halves(w):
    lo = pltpu.bitcast(w << 16, F32)
    hi = pltpu.bitcast(w & jnp.uint32(0xFFFF0000), F32)
    return lo, hi


def _mm_kernel(x_ref, w_ref, o_ref):
    o_ref[...] = _dot(x_ref[...].astype(BF16), w_ref[...]).astype(o_ref.dtype)


def _matmul(x, w, out_dtype, tm, tn):
    m, k = x.shape
    n = w.shape[1]
    return pl.pallas_call(
        _mm_kernel,
        grid=(m // tm, n // tn),
        in_specs=[pl.BlockSpec((tm, k), lambda i, j: (i, 0)),
                  pl.BlockSpec((k, tn), lambda i, j: (0, j))],
        out_specs=pl.BlockSpec((tm, tn), lambda i, j: (i, j)),
        out_shape=jax.ShapeDtypeStruct((m, n), out_dtype),
        compiler_params=_cparams(("parallel", "arbitrary")),
        name="matmul",
    )(x, w)


def _mm_rope_kernel(x_ref, w_ref, cos_ref, sin_ref, o_ref, *, n_q_tiles, n_rope_tiles, first_gate_tile,
                    head_w, q_scale):
    j = pl.program_id(1)
    acc = _dot(x_ref[...].astype(BF16), w_ref[...])

    @pl.when((j >= n_rope_tiles) & (j < first_gate_tile))
    def _():
        o_ref[...] = acc.astype(o_ref.dtype)

    @pl.when(j >= first_gate_tile)
    def _():
        o_ref[...] = (acc * jax.nn.sigmoid(acc)).astype(o_ref.dtype)

    @pl.when(j < n_rope_tiles)
    def _():
        scale = jnp.where(j < n_q_tiles, q_scale, 1.0)
        cos = cos_ref[...] * scale
        sin = sin_ref[...] * scale
        half = head_w // 2
        for c0 in range(0, acc.shape[1], head_w):
            x1 = acc[:, c0:c0 + half]
            x2 = acc[:, c0 + half:c0 + head_w]
            o_ref[:, c0:c0 + half] = (x1 * cos - x2 * sin).astype(o_ref.dtype)
            o_ref[:, c0 + half:c0 + head_w] = (x2 * cos + x1 * sin).astype(o_ref.dtype)


def _matmul_rope(x, w, cos, sin, out_dtype, tm, tn, n_q_cols, n_rope_cols, gate_col0, head_w, q_scale):
    m, k = x.shape
    n = w.shape[1]
    nsb = cos.shape[0] // tm
    return pl.pallas_call(
        functools.partial(_mm_rope_kernel, n_q_tiles=n_q_cols // tn, n_rope_tiles=n_rope_cols // tn,
                          first_gate_tile=gate_col0 // tn, head_w=head_w, q_scale=q_scale),
        grid=(m // tm, n // tn),
        in_specs=[pl.BlockSpec((tm, k), lambda i, j: (i, 0)),
                  pl.BlockSpec((k, tn), lambda i, j: (0, j)),
                  pl.BlockSpec((tm, head_w // 2), lambda i, j: (i % nsb, 0)),
                  pl.BlockSpec((tm, head_w // 2), lambda i, j: (i % nsb, 0))],
        out_specs=pl.BlockSpec((tm, tn), lambda i, j: (i, j)),
        out_shape=jax.ShapeDtypeStruct((m, n), out_dtype),
        compiler_params=_cparams(("parallel", "arbitrary")),
        name="matmul_rope",
    )(x, w, cos, sin)


def _layer_norm_rows(z, g, b):
    mean = jnp.mean(z, axis=-1, keepdims=True)
    zc = z - mean
    var = jnp.mean(zc * zc, axis=-1, keepdims=True)
    return zc * lax.rsqrt(var + LN_EPS) * g + b


def _proj_ln_kernel(*refs, n_act, nk):
    acts = refs[:n_act]
    ws = refs[n_act:2 * n_act]
    x_ref, g_ref, b_ref, y_ref, yb_ref, yp_ref, acc_ref = refs[2 * n_act:]
    k = pl.program_id(1)
    part = _dot(acts[0][...], ws[0][...])
    for a, w in zip(acts[1:], ws[1:]):
        part = part + _dot(a[...], w[...])

    @pl.when(k == 0)
    def _():
        acc_ref[...] = part

    @pl.when(k > 0)
    def _():
        acc_ref[...] = acc_ref[...] + part

    @pl.when(k == nk - 1)
    def _():
        z = DN_ALPHA * x_ref[...] + acc_ref[...]
        y = _layer_norm_rows(z, g_ref[...], b_ref[...])
        y_ref[...] = y
        yb_ref[...] = y.astype(BF16)
        yp_ref[...] = _pack_halves(y)


def _proj_ln(acts, ws, x, g, b, tm, nk):
    m, d = x.shape
    n_act = len(acts)
    in_specs = []
    for a in acts:
        kk = a.shape[1] // nk
        in_specs.append(pl.BlockSpec((tm, kk), lambda i, k: (i, k)))
    for w in ws:
        kk = w.shape[0] // nk
        in_specs.append(pl.BlockSpec((kk, d), lambda i, k: (k, 0)))
    in_specs += [pl.BlockSpec((tm, d), lambda i, k: (i, 0)),
                 pl.BlockSpec((1, d), lambda i, k: (0, 0)),
                 pl.BlockSpec((1, d), lambda i, k: (0, 0))]
    return pl.pallas_call(
        functools.partial(_proj_ln_kernel, n_act=n_act, nk=nk),
        grid=(m // tm, nk),
        in_specs=in_specs,
        out_specs=[pl.BlockSpec((tm, d), lambda i, k: (i, 0)),
                   pl.BlockSpec((tm, d), lambda i, k: (i, 0)),
                   pl.BlockSpec((tm, d // 2), lambda i, k: (i, 0))],
        out_shape=[jax.ShapeDtypeStruct((m, d), F32), jax.ShapeDtypeStruct((m, d), BF16),
                   jax.ShapeDtypeStruct((m, d // 2), U32)],
        scratch_shapes=[pltpu.VMEM((tm, d), F32)],
        compiler_params=_cparams(("parallel", "arbitrary")),
        name="proj_ln",
    )(*acts, *ws, x, g.reshape(1, d), b.reshape(1, d))


def _na_bias_tables(rpb):
    nh = rpb.shape[0]
    c = np.arange(GRID_W)
    cs = np.clip(c - NA_WIN_W // 2, 0, GRID_W - NA_WIN_W)
    kc = np.arange(GRID_W)
    valid = (kc[None, :] >= cs[:, None]) & (kc[None, :] < cs[:, None] + NA_WIN_W)
    dc = kc[None, :] - c[:, None] + NA_WIN_W - 1
    onehot = (dc[:, :, None] == np.arange(2 * NA_WIN_W - 1)[None, None, :]) & valid[:, :, None]
    cols = jnp.einsum("hrd,ckd->hrck", rpb.astype(F32), jnp.asarray(onehot, F32),
                      precision=lax.Precision.HIGHEST)
    cols = jnp.where(jnp.asarray(valid)[None, None], cols, MASK_VALUE)
    tabs = jnp.stack([cols[:, off:off + NA_WIN_H] for off in range(NA_WIN_H)], axis=1)
    return tabs.transpose(0, 1, 3, 2, 4).reshape(nh, NA_WIN_H, GRID_W, NA_WIN_H * GRID_W)


def _na_kernel(q_ref, k_ref, v_ref, bias_ref, o_ref, *, rows, group):
    scale = NA_HEAD_DIM ** -0.5
    nkeys = NA_WIN_H * GRID_W

    def body(i, carry):
        geom, scores = [], []
        for u in range(group):
            r = i * group + u
            rs = jnp.clip(r - NA_WIN_H // 2, 0, rows - NA_WIN_H)
            off = rs - r + NA_WIN_H - 1
            q0 = pl.multiple_of(r * GRID_W, GRID_W)
            k0 = pl.multiple_of(rs * GRID_W, GRID_W)
            geom.append((q0, k0))
            s = _dot_nt(q_ref[pl.ds(q0, GRID_W), :], k_ref[pl.ds(k0, nkeys), :])
            scores.append(s * scale + bias_ref[0, off])
        for (q0, k0), s in zip(geom, scores):
            m = jnp.max(s, axis=-1, keepdims=True)
            p = jnp.exp(s - m)
            l = jnp.sum(p, axis=-1, keepdims=True)
            o = _dot(p.astype(BF16), v_ref[pl.ds(k0, nkeys), :]) / l
            o_ref[pl.ds(q0, GRID_W), :] = o.astype(o_ref.dtype)
        return carry

    lax.fori_loop(0, rows // group, body, 0)


def _na_attention(h, bias_tables, batch, seq):
    rows = seq // GRID_W
    d = NA_HEAD_DIM
    nkeys = NA_WIN_H * GRID_W
    return pl.pallas_call(
        functools.partial(_na_kernel, rows=rows, group=8),
        grid=(batch, NA_HEADS),
        in_specs=[pl.BlockSpec((seq, d), lambda b, hh: (b, hh)),
                  pl.BlockSpec((seq, d), lambda b, hh: (b, NA_HEADS + hh)),
                  pl.BlockSpec((seq, d), lambda b, hh: (b, 2 * NA_HEADS + hh)),
                  pl.BlockSpec((1, NA_WIN_H, GRID_W, nkeys), lambda b, hh: (hh, 0, 0, 0))],
        out_specs=pl.BlockSpec((seq, d), lambda b, hh: (b, hh)),
        out_shape=jax.ShapeDtypeStruct((batch * seq, NA_WIDTH), BF16),
        compiler_params=_cparams(("parallel", "arbitrary")),
        name="na_attention",
    )(h, h, h, bias_tables)


def _rms_rows(x, g):
    return x * lax.rsqrt(jnp.mean(x * x, axis=-1, keepdims=True) + RMS_EPS) * g


def _rope_lanes(t, cosf, sinf):
    return t * cosf + pltpu.roll(t, LANES // 2, 1) * sinf


def _mla_prep_kernel(cq_ref, ckv_ref, kr_ref, gq_ref, gkv_ref, wq_ref, wk_ref, wvt_ref, cos_ref, sin_ref,
                     q_ref, k_ref, vt_ref):
    dq = MLA_NOPE_DIM + MLA_ROPE_DIM
    cosf = cos_ref[...]
    sinf = sin_ref[...]
    cqn = _rms_rows(cq_ref[...].astype(F32), gq_ref[...]).astype(BF16)
    ckvn = _rms_rows(ckv_ref[...].astype(F32), gkv_ref[...]).astype(BF16)
    qf = _dot(cqn, wq_ref[...]) * (dq ** -0.5 * math.log2(math.e))
    kf = _dot(ckvn, wk_ref[...])
    vt_ref[...] = _dot_nt(wvt_ref[...], ckvn).astype(BF16)
    kpe = _rope_lanes(kr_ref[...].astype(F32), cosf, sinf).astype(BF16)
    for hh in range(MLA_HEADS):
        c0 = hh * 2 * LANES
        q_ref[:, c0:c0 + LANES] = qf[:, c0:c0 + LANES].astype(BF16)
        q_ref[:, c0 + LANES:c0 + 2 * LANES] = _rope_lanes(qf[:, c0 + LANES:c0 + 2 * LANES], cosf, sinf).astype(BF16)
        k_ref[:, c0:c0 + LANES] = kf[:, hh * LANES:(hh + 1) * LANES].astype(BF16)
        k_ref[:, c0 + LANES:c0 + 2 * LANES] = kpe


def _mla_prep(h, gq, gkv, wq_p, wk, wvt, cosf, sinf, col_cq, seq, tm):
    n = h.shape[0]
    hw = MLA_HEADS * 2 * LANES
    nsb = seq // tm
    b_cq = col_cq // MLA_Q_RANK
    b_ckv = (col_cq + MLA_Q_RANK) // MLA_KV_RANK
    b_kr = (col_cq + MLA_Q_RANK + MLA_KV_RANK) // LANES
    return pl.pallas_call(
        _mla_prep_kernel,
        grid=(n // tm,),
        in_specs=[pl.BlockSpec((tm, MLA_Q_RANK), lambda i: (i, b_cq)),
                  pl.BlockSpec((tm, MLA_KV_RANK), lambda i: (i, b_ckv)),
                  pl.BlockSpec((tm, LANES), lambda i: (i, b_kr)),
                  pl.BlockSpec((1, MLA_Q_RANK), lambda i: (0, 0)),
                  pl.BlockSpec((1, MLA_KV_RANK), lambda i: (0, 0)),
                  pl.BlockSpec((MLA_Q_RANK, hw), lambda i: (0, 0)),
                  pl.BlockSpec((MLA_KV_RANK, MLA_HEADS * LANES), lambda i: (0, 0)),
                  pl.BlockSpec((MLA_HEADS * MLA_V_DIM, MLA_KV_RANK), lambda i: (0, 0)),
                  pl.BlockSpec((tm, LANES), lambda i: (i % nsb, 0)),
                  pl.BlockSpec((tm, LANES), lambda i: (i % nsb, 0))],
        out_specs=[pl.BlockSpec((tm, hw), lambda i: (i, 0)),
                   pl.BlockSpec((tm, hw), lambda i: (i, 0)),
                   pl.BlockSpec((MLA_HEADS * MLA_V_DIM, tm), lambda i: (0, i))],
        out_shape=[jax.ShapeDtypeStruct((n, hw), BF16), jax.ShapeDtypeStruct((n, hw), BF16),
                   jax.ShapeDtypeStruct((MLA_HEADS * MLA_V_DIM, n), BF16)],
        compiler_params=_cparams(("parallel",)),
        name="mla_prep",
    )(h, h, h, gq.reshape(1, -1), gkv.reshape(1, -1), wq_p, wk, wvt, cosf, sinf)


def _mla_attn_kernel(q_ref, k_ref, vt_ref, o_ref, *, tk, sub):
    nchunk = k_ref.shape[0] // tk
    tq = q_ref.shape[0]
    nsub = tq // sub
    qs = [q_ref[s * sub:(s + 1) * sub, :] for s in range(nsub)]
    m = [jnp.full((1, sub), MASK_VALUE, F32) for _ in range(nsub)]
    l = [jnp.zeros((1, sub), F32) for _ in range(nsub)]
    acc = [jnp.zeros((MLA_V_DIM, sub), F32) for _ in range(nsub)]

    def scores(s, c):
        return _dot_nt(k_ref[c * tk:(c + 1) * tk, :], qs[s])

    st_next = [scores(s, 0) for s in range(nsub)]
    for c in range(nchunk):
        for s in range(nsub):
            st = st_next[s]
            m_new = jnp.maximum(m[s], jnp.max(st, axis=0, keepdims=True))
            a = jnp.exp2(m[s] - m_new)
            p = jnp.exp2(st - m_new)
            l[s] = a * l[s] + jnp.sum(p, axis=0, keepdims=True)
            if c + 1 < nchunk:
                st_next[s] = scores(s, c + 1)
            acc[s] = a * acc[s] + _dot(vt_ref[:, c * tk:(c + 1) * tk], p.astype(BF16))
            m[s] = m_new
    for s in range(nsub):
        o_ref[s * sub:(s + 1) * sub, :] = (acc[s] / l[s]).T.astype(o_ref.dtype)


def _mla_attention(q_p, k_p, vt, batch, seq, tq, tk, sub):
    n = q_p.shape[0]
    nqb = seq // tq
    return pl.pallas_call(
        functools.partial(_mla_attn_kernel, tk=tk, sub=sub),
        grid=(batch, MLA_HEADS, nqb),
        in_specs=[pl.BlockSpec((tq, 2 * LANES), lambda b, hh, i: (b * nqb + i, hh)),
                  pl.BlockSpec((seq, 2 * LANES), lambda b, hh, i: (b, hh)),
                  pl.BlockSpec((MLA_V_DIM, seq), lambda b, hh, i: (hh, b))],
        out_specs=pl.BlockSpec((tq, MLA_V_DIM), lambda b, hh, i: (b * nqb + i, hh)),
        out_shape=jax.ShapeDtypeStruct((n, MLA_HEADS * MLA_V_DIM), BF16),
        compiler_params=_cparams(("parallel", "parallel", "arbitrary")),
        name="mla_attention",
    )(q_p, k_p, vt)


def _ret_kernel(lg_ref, q_ref, k_ref, v_ref, g_ref, o_ref, acc_ref, st_ref, *, c_len, group):
    nchunk = q_ref.shape[0] // c_len
    hh = pl.program_id(1)
    lgf = lg_ref[0, hh]
    lgb = lg_ref[1, hh]
    ii = lax.broadcasted_iota(I32, (c_len, c_len), 0).astype(F32)
    jj = lax.broadcasted_iota(I32, (c_len, c_len), 1).astype(F32)
    rel = ii - jj
    dmat = jnp.where(rel >= 0, jnp.exp(lgf * jnp.maximum(rel, 0.0)), jnp.exp(lgb * jnp.maximum(-rel, 0.0)))
    pos = lax.broadcasted_iota(I32, (c_len, 1), 0).astype(F32)
    qdec_f = jnp.exp(lgf * (pos + 1.0))
    kdec_f = jnp.exp(lgf * (c_len - 1.0 - pos))
    qdec_b = jnp.exp(lgb * (c_len - pos))
    kdec_b = jnp.exp(lgb * pos)
    full_chunk = jnp.full((1, RET_V_DIM), float(c_len), F32)
    cdec_f = jnp.exp(lgf * full_chunk)
    cdec_b = jnp.exp(lgb * full_chunk)

    def decayed_keys_t(t0, kdec):
        return (k_ref[pl.ds(t0, c_len), :].astype(F32) * kdec).T.astype(BF16)

    st_ref[...] = jnp.zeros_like(st_ref)

    def bwd_body(i, carry):
        t0s = [pl.multiple_of((nchunk - 1 - (i * group + u)) * c_len, c_len) for u in range(group)]
        upd = [_dot(decayed_keys_t(t0, kdec_b), v_ref[pl.ds(t0, c_len), :]) for t0 in t0s]
        for t0, u_c in zip(t0s, upd):
            st = st_ref[...]
            acc_ref[pl.ds(t0, c_len), :] = _dot(q_ref[pl.ds(t0, c_len), :], st.astype(BF16)) * qdec_b
            st_ref[...] = st * cdec_b + u_c
        return carry

    lax.fori_loop(0, nchunk // group, bwd_body, 0)
    st_ref[...] = jnp.zeros_like(st_ref)

    def fwd_body(i, carry):
        t0s = [pl.multiple_of((i * group + u) * c_len, c_len) for u in range(group)]
        scs = [_dot_nt(q_ref[pl.ds(t0, c_len), :], k_ref[pl.ds(t0, c_len), :]) * dmat for t0 in t0s]
        upd = [_dot(decayed_keys_t(t0, kdec_f), v_ref[pl.ds(t0, c_len), :]) for t0 in t0s]
        for t0, sc, u_c in zip(t0s, scs, upd):
            st = st_ref[...]
            r = (_dot(sc.astype(BF16), v_ref[pl.ds(t0, c_len), :])
                 + _dot(q_ref[pl.ds(t0, c_len), :], st.astype(BF16)) * qdec_f
                 + acc_ref[pl.ds(t0, c_len), :])
            st_ref[...] = st * cdec_f + u_c
            r = r - jnp.mean(r, axis=-1, keepdims=True)
            r = r * lax.rsqrt(jnp.mean(r * r, axis=-1, keepdims=True) + LN_EPS)
            o_ref[pl.ds(t0, c_len), :] = (g_ref[pl.ds(t0, c_len), :].astype(F32) * r).astype(o_ref.dtype)
        return carry

    lax.fori_loop(0, nchunk // group, fwd_body, 0)


def _retention(hc, lg, batch, seq, c_len, group):
    dk, dv, nh = RET_QK_DIM, RET_V_DIM, RET_HEADS
    v_blk0 = (2 * nh * dk) // dv
    return pl.pallas_call(
        functools.partial(_ret_kernel, c_len=c_len, group=group),
        grid=(batch, nh),
        in_specs=[pl.BlockSpec(memory_space=pltpu.SMEM),
                  pl.BlockSpec((seq, dk), lambda b, hh: (b, hh)),
                  pl.BlockSpec((seq, dk), lambda b, hh: (b, nh + hh)),
                  pl.BlockSpec((seq, dv), lambda b, hh: (b, v_blk0 + hh)),
                  pl.BlockSpec((seq, dv), lambda b, hh: (b, v_blk0 + nh + hh))],
        out_specs=pl.BlockSpec((seq, dv), lambda b, hh: (b, hh)),
        scratch_shapes=[pltpu.VMEM((seq, dv), F32), pltpu.VMEM((dk, dv), F32)],
        out_shape=jax.ShapeDtypeStruct((batch * seq, nh * dv), BF16),
        compiler_params=_cparams(("parallel", "arbitrary")),
        name="retention",
    )(lg, hc, hc, hc, hc)


ROUTER_ROWS = 40


def _router_kernel(x_ref, wt_ref, b_ref, tri_ref, ids_ref, wts_ref, cnt_ref, carry_ref):
    i = pl.program_id(0)
    tm = x_ref.shape[0]

    @pl.when(i == 0)
    def _():
        carry_ref[...] = jnp.zeros_like(carry_ref)

    logits = _dot_nt(wt_ref[...], x_ref[...]) + b_ref[...]
    grow = lax.broadcasted_iota(I32, (SUBLANES, tm), 0).astype(F32)
    gl = jnp.where(grow < N_GROUPS, logits[0:SUBLANES], MASK_VALUE)
    gmax = jnp.max(gl, axis=0, keepdims=True)
    gsum = jnp.sum(jnp.exp(gl - gmax), axis=0, keepdims=True)
    p_group = 1.0 / gsum
    g_idx = jnp.min(jnp.where(gl == gmax, grow, float(N_GROUPS)), axis=0, keepdims=True)
    sel = jnp.zeros((EXPERTS_PER_GROUP, tm), F32)
    for g in range(N_GROUPS):
        r0 = SUBLANES + g * EXPERTS_PER_GROUP
        sel = sel + jnp.where(g_idx == float(g), logits[r0:r0 + EXPERTS_PER_GROUP], 0.0)
    erow = lax.broadcasted_iota(I32, (EXPERTS_PER_GROUP, tm), 0).astype(F32)
    smax = jnp.max(sel, axis=0, keepdims=True)
    sexp = jnp.exp(sel - smax)
    probs = sexp / jnp.sum(sexp, axis=0, keepdims=True)
    p1 = jnp.max(probs, axis=0, keepdims=True)
    i1 = jnp.min(jnp.where(probs == p1, erow, float(EXPERTS_PER_GROUP)), axis=0, keepdims=True)
    rest = jnp.where(erow == i1, -1.0, probs)
    p2 = jnp.max(rest, axis=0, keepdims=True)
    i2 = jnp.min(jnp.where(rest == p2, erow, float(EXPERTS_PER_GROUP)), axis=0, keepdims=True)
    denom = p1 + p2
    e0 = g_idx * EXPERTS_PER_GROUP + i1
    e1 = g_idx * EXPERTS_PER_GROUP + i2

    xrow = lax.broadcasted_iota(I32, (N_EXPERTS, tm), 0).astype(F32)
    oh0 = jnp.where(xrow == e0, 1.0, 0.0)
    oh1 = jnp.where(xrow == e1, 1.0, 0.0)
    onehot = oh0 + oh1
    before = _dot(onehot.astype(BF16), tri_ref[...]) + carry_ref[:, 0:1]
    rank0 = jnp.sum(oh0 * before, axis=0, keepdims=True)
    rank1 = jnp.sum(oh1 * before, axis=0, keepdims=True)
    carry_ref[...] = carry_ref[...] + jnp.sum(onehot, axis=1, keepdims=True)

    ids_ref[...] = jnp.zeros_like(ids_ref)
    ids_ref[0:1, :] = e0.astype(I32)
    ids_ref[1:2, :] = e1.astype(I32)
    ids_ref[2:3, :] = rank0.astype(I32)
    ids_ref[3:4, :] = rank1.astype(I32)
    wts_ref[...] = jnp.zeros_like(wts_ref)
    wts_ref[0:1, :] = p_group * p1 / denom
    wts_ref[1:2, :] = p_group * p2 / denom
    cnt_ref[...] = carry_ref[...]


def _router(x, wt, bias, tm):
    n, d = x.shape
    tri = jnp.asarray(np.triu(np.ones((tm, tm), np.float32), 1), BF16)
    return pl.pallas_call(
        _router_kernel,
        grid=(n // tm,),
        in_specs=[pl.BlockSpec((tm, d), lambda i: (i, 0)),
                  pl.BlockSpec((ROUTER_ROWS, d), lambda i: (0, 0)),
                  pl.BlockSpec((ROUTER_ROWS, 1), lambda i: (0, 0)),
                  pl.BlockSpec((tm, tm), lambda i: (0, 0))],
        out_specs=[pl.BlockSpec((SUBLANES, tm), lambda i: (0, i)),
                   pl.BlockSpec((SUBLANES, tm), lambda i: (0, i)),
                   pl.BlockSpec((N_EXPERTS, LANES), lambda i: (0, 0))],
        out_shape=[jax.ShapeDtypeStruct((SUBLANES, n), I32), jax.ShapeDtypeStruct((SUBLANES, n), F32),
                   jax.ShapeDtypeStruct((N_EXPERTS, LANES), F32)],
        scratch_shapes=[pltpu.VMEM((N_EXPERTS, LANES), F32)],
        compiler_params=_cparams(("arbitrary",)),
        name="moe_router",
    )(x, wt, bias, tri)


def _slots_kernel(ids_ref, cnt_ref, slots_ref, blk_ref, *, nblk_pad):
    tm = ids_ref.shape[1]
    cnt = cnt_ref[:, 0:1]
    padded = jnp.floor((cnt + (MOE_BLOCK - 1)) / MOE_BLOCK) * MOE_BLOCK
    er = lax.broadcasted_iota(I32, (N_EXPERTS, N_EXPERTS), 0)
    ec = lax.broadcasted_iota(I32, (N_EXPERTS, N_EXPERTS), 1)
    padded_row = jnp.sum(jnp.where(er == ec, padded, 0.0), axis=0, keepdims=True)
    p_start = jnp.sum(jnp.where(ec < er, padded_row, 0.0), axis=1, keepdims=True)
    p_end = p_start + padded
    xrow = lax.broadcasted_iota(I32, (N_EXPERTS, tm), 0)
    e0 = ids_ref[0:1, :]
    e1 = ids_ref[1:2, :]
    s0 = jnp.sum(jnp.where(xrow == e0, p_start, 0.0), axis=0, keepdims=True).astype(I32) + ids_ref[2:3, :]
    s1 = jnp.sum(jnp.where(xrow == e1, p_start, 0.0), axis=0, keepdims=True).astype(I32) + ids_ref[3:4, :]
    slots_ref[...] = jnp.zeros_like(slots_ref)
    slots_ref[0:1, :] = s0
    slots_ref[1:2, :] = s1
    bstart = (lax.broadcasted_iota(I32, (1, nblk_pad), 1) * MOE_BLOCK).astype(F32)
    blk_e = jnp.minimum(jnp.sum(jnp.where(p_end <= bstart, 1.0, 0.0), axis=0, keepdims=True), N_EXPERTS - 1.0)
    total = jnp.sum(padded, axis=0, keepdims=True)
    erow = lax.broadcasted_iota(I32, (N_EXPERTS, nblk_pad), 0).astype(F32)
    own_end = jnp.sum(jnp.where(erow == blk_e, p_end, 0.0), axis=0, keepdims=True)
    nxt_e = jnp.minimum(jnp.sum(jnp.where(p_end <= own_end, 1.0, 0.0), axis=0, keepdims=True), N_EXPERTS - 1.0)
    nxt_e = jnp.where(own_end < total, nxt_e, -1.0)
    blk_ref[...] = jnp.zeros_like(blk_ref)
    blk_ref[0:1, :] = blk_e.astype(I32)
    blk_ref[1:2, :] = jnp.broadcast_to((total / MOE_BLOCK).astype(I32), (1, nblk_pad))
    blk_ref[2:3, :] = nxt_e.astype(I32)
    lane = lax.broadcasted_iota(I32, (N_EXPERTS, nblk_pad), 1).astype(F32)
    blk_ref[3:4, :] = jnp.sum(jnp.where(erow == lane, p_start + cnt, 0.0), axis=0, keepdims=True).astype(I32)
    blk_ref[4:5, :] = jnp.sum(jnp.where(erow == lane, p_end, 0.0), axis=0, keepdims=True).astype(I32)


def _slots(ids, cnt, tm, nblk_pad):
    n = ids.shape[1]
    return pl.pallas_call(
        functools.partial(_slots_kernel, nblk_pad=nblk_pad),
        grid=(n // tm,),
        in_specs=[pl.BlockSpec((SUBLANES, tm), lambda i: (0, i)),
                  pl.BlockSpec((N_EXPERTS, LANES), lambda i: (0, 0))],
        out_specs=[pl.BlockSpec((SUBLANES, tm), lambda i: (0, i)),
                   pl.BlockSpec((SUBLANES, nblk_pad), lambda i: (0, 0))],
        out_shape=[jax.ShapeDtypeStruct((SUBLANES, n), I32), jax.ShapeDtypeStruct((SUBLANES, nblk_pad), I32)],
        compiler_params=_cparams(("arbitrary",)),
        name="moe_slots",
    )(ids, cnt)


def _slot_tokens_kernel(slots_ref, blk_ref, tok_ref, *, n, cap, nblk_pad):
    def zero(j, carry):
        tok_ref[j] = 0
        return carry

    def scatter(t, carry):
        tok_ref[slots_ref[t]] = t
        tok_ref[slots_ref[n + t]] = t
        return carry

    for e in range(N_EXPERTS):
        lax.fori_loop(blk_ref[3 * nblk_pad + e], blk_ref[4 * nblk_pad + e], zero, 0)
    lax.fori_loop(blk_ref[nblk_pad] * MOE_BLOCK, cap, zero, 0)
    lax.fori_loop(0, n, scatter, 0, unroll=8)


def _slot_tokens(slots_flat, blk_flat, n, cap, nblk_pad):
    return pl.pallas_call(
        functools.partial(_slot_tokens_kernel, n=n, cap=cap, nblk_pad=nblk_pad),
        grid_spec=pltpu.PrefetchScalarGridSpec(
            num_scalar_prefetch=2,
            grid=(1,),
            in_specs=[],
            out_specs=pl.BlockSpec(memory_space=pltpu.SMEM)),
        out_shape=jax.ShapeDtypeStruct((cap,), I32),
        compiler_params=_cparams(("arbitrary",)),
        name="moe_slot_tokens",
    )(slots_flat, blk_flat)


def _expert_kernel(blk_ref, tok_ref, xpk_hbm, wg_hbm, wu_hbm, wd_hbm, y_ref, xbuf, wgf, wuf, wdf, wgb, wub, wdb,
                   sems, wsems, cnt_ref, *, nblk_pad, layer):
    i = pl.program_id(0)
    n_used = blk_ref[nblk_pad]
    cast_rows = 256

    def weight_copies(e, slot):
        return [pltpu.make_async_copy(src.at[layer, e], dst.at[slot], wsems.at[slot])
                for src, dst in ((wg_hbm, wgf), (wu_hbm, wuf), (wd_hbm, wdf))]

    def start_rows(block, buf, r_lo=0, r_hi=MOE_BLOCK):
        for r in range(r_lo, r_hi):
            pltpu.make_async_copy(xpk_hbm.at[pl.ds(tok_ref[block * MOE_BLOCK + r], 1)],
                                  xbuf.at[buf, pl.ds(r, 1)], sems.at[buf]).start()

    def wait_rows(buf):
        pltpu.make_async_copy(xpk_hbm.at[pl.ds(0, MOE_BLOCK)], xbuf.at[buf], sems.at[buf]).wait()

    @pl.when(i == 0)
    def _():
        cnt_ref[0] = 0
        for cp in weight_copies(blk_ref[0], 0):
            cp.start(priority=1)
        start_rows(0, 0)

    @pl.when((i < n_used) & ((i == 0) | (blk_ref[i] != blk_ref[jnp.maximum(i - 1, 0)])))
    def _():
        slot = cnt_ref[0] % 2
        cnt_ref[0] = cnt_ref[0] + 1
        for cp in weight_copies(blk_ref[i], slot):
            cp.wait()
        nxt_e = blk_ref[2 * nblk_pad + i]

        @pl.when(nxt_e >= 0)
        def _():
            for cp in weight_copies(nxt_e, 1 - slot):
                cp.start(priority=1)

        for src, dst in ((wgf, wgb), (wuf, wub), (wdf, wdb)):
            for r0 in range(0, dst.shape[0], cast_rows):
                dst[r0:r0 + cast_rows, :] = src[slot, r0:r0 + cast_rows, :].astype(BF16)

    @pl.when(i < n_used)
    def _():
        buf = i % 2
        nxt = jnp.minimum(i + 1, n_used - 1)
        wait_rows(buf)
        lo, hi = _unpack_halves(xbuf[buf])
        xb = jnp.concatenate([lo.astype(BF16), hi.astype(BF16)], axis=-1)
        g = _dot(xb, wgb[...])
        start_rows(nxt, 1 - buf, 0, MOE_BLOCK // 2)
        u = _dot(xb, wub[...])
        start_rows(nxt, 1 - buf, MOE_BLOCK // 2, MOE_BLOCK)
        hmid = (g * jax.nn.sigmoid(g) * u).astype(BF16)
        y_ref[...] = _pack_halves(_dot(hmid, wdb[...]))

    @pl.when(i == n_used - 1)
    def _():
        wait_rows(1 - i % 2)

    @pl.when(i >= n_used)
    def _():
        y_ref[...] = jnp.zeros_like(y_ref)


def _experts(blk_flat, slot_tok, xpk, wg, wu, wd, layer, nblk_pad):
    dh = xpk.shape[1]
    d = 2 * dh
    de = wg.shape[3]
    cap = slot_tok.shape[0]
    return pl.pallas_call(
        functools.partial(_expert_kernel, nblk_pad=nblk_pad, layer=layer),
        grid_spec=pltpu.PrefetchScalarGridSpec(
            num_scalar_prefetch=2,
            grid=(cap // MOE_BLOCK,),
            in_specs=[pl.BlockSpec(memory_space=pl.ANY), pl.BlockSpec(memory_space=pl.ANY),
                      pl.BlockSpec(memory_space=pl.ANY), pl.BlockSpec(memory_space=pl.ANY)],
            out_specs=pl.BlockSpec((MOE_BLOCK, dh), lambda i, blk, tok: (i, 0)),
            scratch_shapes=[pltpu.VMEM((2, MOE_BLOCK, dh), U32),
                            pltpu.VMEM((2, d, de), F32), pltpu.VMEM((2, d, de), F32), pltpu.VMEM((2, de, d), F32),
                            pltpu.VMEM((d, de), BF16), pltpu.VMEM((d, de), BF16), pltpu.VMEM((de, d), BF16),
                            pltpu.SemaphoreType.DMA((2,)), pltpu.SemaphoreType.DMA((2,)),
                            pltpu.SMEM((1,), I32)]),
        out_shape=jax.ShapeDtypeStruct((cap, dh), U32),
        compiler_params=_cparams(("arbitrary",)),
        name="moe_experts",
    )(blk_flat, slot_tok, xpk, wg, wu, wd)


def _tail_kernel(slots_ref, x_ref, xb_ref, p_ref, wts_ref, wgate_ref, bgate_ref, wproj_ref, g_ref, b_ref,
                 yb_hbm, y_ref, ybf_ref, rows_ref, sems):
    i = pl.program_id(0)
    nsteps = pl.num_programs(0)
    tm = x_ref.shape[0]
    n = nsteps * tm

    def start_rows(step, buf):
        for t in range(tm):
            for which in range(2):
                pltpu.make_async_copy(yb_hbm.at[pl.ds(slots_ref[which * n + step * tm + t], 1)],
                                      rows_ref.at[buf, which, pl.ds(t, 1)], sems.at[buf]).start()

    def wait_rows(buf):
        for which in range(2):
            pltpu.make_async_copy(yb_hbm.at[pl.ds(0, tm)], rows_ref.at[buf, which], sems.at[buf]).wait()

    @pl.when(i == 0)
    def _():
        start_rows(0, 0)

    buf = i % 2
    nxt = jnp.minimum(i + 1, nsteps - 1)
    wait_rows(buf)
    start_rows(nxt, 1 - buf)
    gate = jax.nn.sigmoid(_dot(xb_ref[...], wgate_ref[...]) + bgate_ref[...])
    ple = gate * _dot(p_ref[0].astype(BF16), wproj_ref[...])
    w = wts_ref[...]
    lo0, hi0 = _unpack_halves(rows_ref[buf, 0])
    lo1, hi1 = _unpack_halves(rows_ref[buf, 1])
    w0 = w[:, 0:1]
    w1 = w[:, 1:2]
    ffn = jnp.concatenate([lo0 * w0 + lo1 * w1, hi0 * w0 + hi1 * w1], axis=-1)
    z = DN_ALPHA * x_ref[...] + ffn + ple
    y = _layer_norm_rows(z, g_ref[...], b_ref[...])
    y_ref[...] = y
    ybf_ref[...] = y.astype(BF16)

    @pl.when(i == nsteps - 1)
    def _():
        wait_rows(1 - buf)


def _layer_tail(slots_flat, x, xb, p, layer, wts_t, wgate, bgate, wproj, g, b, yb, tm):
    n, d = x.shape
    pd = p.shape[2]
    return pl.pallas_call(
        _tail_kernel,
        grid_spec=pltpu.PrefetchScalarGridSpec(
            num_scalar_prefetch=1,
            grid=(n // tm,),
            in_specs=[pl.BlockSpec((tm, d), lambda i, s: (i, 0)),
                      pl.BlockSpec((tm, d), lambda i, s: (i, 0)),
                      pl.BlockSpec((1, tm, pd), lambda i, s: (layer, i, 0)),
                      pl.BlockSpec((tm, 2), lambda i, s: (i, 0)),
                      pl.BlockSpec((d, d), lambda i, s: (0, 0)),
                      pl.BlockSpec((1, d), lambda i, s: (0, 0)),
                      pl.BlockSpec((pd, d), lambda i, s: (0, 0)),
                      pl.BlockSpec((1, d), lambda i, s: (0, 0)),
                      pl.BlockSpec((1, d), lambda i, s: (0, 0)),
                      pl.BlockSpec(memory_space=pl.ANY)],
            out_specs=[pl.BlockSpec((tm, d), lambda i, s: (i, 0)),
                       pl.BlockSpec((tm, d), lambda i, s: (i, 0))],
            scratch_shapes=[pltpu.VMEM((2, 2, tm, d // 2), U32), pltpu.SemaphoreType.DMA((2,))]),
        out_shape=[jax.ShapeDtypeStruct((n, d), F32), jax.ShapeDtypeStruct((n, d), BF16)],
        compiler_params=_cparams(("arbitrary",)),
        name="layer_tail",
    )(slots_flat, x, xb, p, wts_t, wgate, bgate.reshape(1, d), wproj, g.reshape(1, d), b.reshape(1, d), yb)


def _rope_table(seq, dim):
    pos = jnp.arange(seq, dtype=F32)
    inv = jnp.exp(jnp.arange(0, dim, 2, dtype=F32) * (-math.log(ROPE_BASE) / dim))
    ang = pos[:, None] * inv[None, :]
    return jnp.cos(ang), jnp.sin(ang)


def _moe_layer(x, xb, xpk, p, layer, w_group, b_group, w_router, b_router, w_gate, w_up, w_down,
               ple_w_proj, ple_w_gate, ple_b_gate, ln_g, ln_b):
    n, d = x.shape
    nblk = -(-(2 * n) // MOE_BLOCK) + N_EXPERTS
    nblk_pad = -(-nblk // LANES) * LANES
    cap = nblk * MOE_BLOCK
    wt = jnp.zeros((ROUTER_ROWS, d), F32)
    wt = wt.at[0:N_GROUPS].set(w_group.T)
    wt = wt.at[SUBLANES:].set(w_router.transpose(0, 2, 1).reshape(N_EXPERTS, d))
    bias = jnp.zeros((ROUTER_ROWS, 1), F32)
    bias = bias.at[0:N_GROUPS, 0].set(b_group)
    bias = bias.at[SUBLANES:, 0].set(b_router.reshape(N_EXPERTS))
    ids, wts, cnt = _router(xb, wt.astype(BF16), bias, tm=512)
    slots, blk = _slots(ids, cnt, tm=min(2048, n), nblk_pad=nblk_pad)
    slots_flat = slots[0:2].reshape(2 * n)
    blk_flat = blk[0:5].reshape(5 * nblk_pad)
    slot_tok = _slot_tokens(slots_flat, blk_flat, n, cap, nblk_pad)
    yb = _experts(blk_flat, slot_tok, xpk, w_gate, w_up, w_down, layer, nblk_pad)
    return _layer_tail(slots_flat, x, xb, p, layer, wts[0:2].T, ple_w_gate.astype(BF16), ple_b_gate,
                       ple_w_proj.astype(BF16), ln_g, ln_b, yb, tm=256)


def _mixer_ab(x, xres, batch, seq, w_in, rpb, q_norm, w_uq, kv_norm, w_ukv, w_out, ln_g, ln_b):
    d = x.shape[1]
    o1 = 3 * NA_WIDTH
    o2 = o1 + MLA_Q_RANK
    o3 = o2 + MLA_KV_RANK
    half = MLA_ROPE_DIM // 2
    kr = w_in[:, o3:o3 + MLA_ROPE_DIM]
    kr_sw = jnp.concatenate([kr[:, half:], kr[:, :half]], axis=1)
    width = -(-(o3 + 2 * MLA_ROPE_DIM) // 1024) * 1024
    w_in_p = jnp.concatenate([w_in, kr_sw, jnp.zeros((d, width - o3 - 2 * MLA_ROPE_DIM), F32)], axis=1)
    h = _matmul(x, w_in_p.astype(BF16), BF16, tm=512, tn=1024)
    a_out = _na_attention(h, _na_bias_tables(rpb), batch, seq)
    dq = MLA_NOPE_DIM + MLA_ROPE_DIM
    wq = w_uq.reshape(MLA_Q_RANK, MLA_HEADS, dq)
    wq_pe = wq[:, :, MLA_NOPE_DIM:]
    wq_p = jnp.concatenate([wq, wq_pe[:, :, half:], wq_pe[:, :, :half]], axis=2)
    wq_p = wq_p.reshape(MLA_Q_RANK, MLA_HEADS * 2 * LANES).astype(BF16)
    wkv = w_ukv.reshape(MLA_KV_RANK, MLA_HEADS, MLA_NOPE_DIM + MLA_V_DIM)
    wk = wkv[:, :, :MLA_NOPE_DIM].reshape(MLA_KV_RANK, MLA_HEADS * MLA_NOPE_DIM).astype(BF16)
    wvt = wkv[:, :, MLA_NOPE_DIM:].reshape(MLA_KV_RANK, MLA_HEADS * MLA_V_DIM).T.astype(BF16)
    cos, sin = _rope_table(seq, MLA_ROPE_DIM)
    zpad = jnp.zeros((seq, LANES - MLA_ROPE_DIM), F32)
    cosf = jnp.concatenate([cos, cos, zpad], axis=1)
    sinf = jnp.concatenate([-sin, sin, zpad], axis=1)
    q_p, k_p, vt = _mla_prep(h, q_norm, kv_norm, wq_p, wk, wvt, cosf, sinf, o1, seq, tm=512)
    b_out = _mla_attention(q_p, k_p, vt, batch, seq, tq=1024, tk=1024, sub=256)
    w_out_b = w_out.astype(BF16)
    return _proj_ln([a_out, b_out], [w_out_b[:NA_WIDTH], w_out_b[NA_WIDTH:]], xres, ln_g, ln_b, tm=512, nk=1)


def _mixer_c(xb, xres, batch, seq, w_in, log_rate_f, log_rate_b, w_out, ln_g, ln_b):
    cosr, sinr = _rope_table(seq, RET_QK_DIM)
    n_q = RET_HEADS * RET_QK_DIM
    hc = _matmul_rope(xb, w_in.astype(BF16), cosr, sinr, BF16, tm=min(1024, seq), tn=1024, n_q_cols=n_q,
                      n_rope_cols=2 * n_q, gate_col0=2 * n_q + RET_HEADS * RET_V_DIM, head_w=RET_QK_DIM,
                      q_scale=RET_QK_DIM ** -0.5)
    lg = jnp.stack([jnp.log1p(-jnp.exp(log_rate_f.astype(F32))), jnp.log1p(-jnp.exp(log_rate_b.astype(F32)))])
    r = _retention(hc, lg, batch, seq, c_len=256, group=2)
    return _proj_ln([r], [w_out.astype(BF16)], xres, ln_g, ln_b, tm=512, nk=2)


def kernel(x, p, ab_w_in, ab_rpb, ab_q_norm, ab_w_uq, ab_kv_norm, ab_w_ukv, ab_w_out, c_w_in, c_log_rate_f,
           c_log_rate_b, c_w_out, ln1_g, ln1_b, moe_w_group, moe_b_group, moe_w_router, moe_b_router,
           moe_w_gate, moe_w_up, moe_w_down, ple_w_proj, ple_w_gate, ple_b_gate, ln2_g, ln2_b):
    batch, seq, d = x.shape
    n = batch * seq
    xf = x.reshape(n, d)
    p_flat = p.reshape(DEPTH, n, -1)
    xb = None
    for i in range(DEPTH):
        j = i // 2
        if i % 2 == 0:
            src = xf if xb is None else xb
            xf, xb, xpk = _mixer_ab(src, xf, batch, seq, ab_w_in[j], ab_rpb[j], ab_q_norm[j], ab_w_uq[j],
                                    ab_kv_norm[j], ab_w_ukv[j], ab_w_out[j], ln1_g[i], ln1_b[i])
        else:
            xf, xb, xpk = _mixer_c(xb, xf, batch, seq, c_w_in[j], c_log_rate_f[j], c_log_rate_b[j], c_w_out[j],
                                   ln1_g[i], ln1_b[i])
        xf, xb = _moe_layer(xf, xb, xpk, p_flat, i, moe_w_group[i], moe_b_group[i], moe_w_router[i],
                            moe_b_router[i], moe_w_gate, moe_w_up, moe_w_down, ple_w_proj[i],
                            ple_w_gate[i], ple_b_gate[i], ln2_g[i], ln2_b[i])
    return xf.reshape(batch, seq, d)
```

```python
import functools
import math

import numpy as np
import jax
import jax.numpy as jnp
from jax import lax
from jax.experimental import pallas as pl
from jax.experimental.pallas import tpu as pltpu

DEPTH = 2
GRID_W = 64
NA_HEADS = 8
NA_HEAD_DIM = 128
NA_WIN_H = 8
NA_WIN_W = 16
MLA_HEADS = 8
MLA_Q_RANK = 512
MLA_KV_RANK = 256
MLA_NOPE_DIM = 128
MLA_ROPE_DIM = 64
MLA_V_DIM = 128
RET_HEADS = 8
RET_QK_DIM = 256
RET_V_DIM = 512
RET_CHUNK = 128
N_GROUPS = 4
EXPERTS_PER_GROUP = 8
N_EXPERTS = N_GROUPS * EXPERTS_PER_GROUP
D_EXPERT = 512
MOE_BLOCK = 128
ROPE_BASE = 10000.0
LN_EPS = 1e-5
RMS_EPS = 1e-6
DN_ALPHA = (2 * DEPTH) ** 0.25
NA_WIDTH = NA_HEADS * NA_HEAD_DIM

LANES = 128
SUBLANES = 8
VMEM_LIMIT_BYTES = 60 * 1024 * 1024
MASK_VALUE = -1e30

F32 = jnp.float32
BF16 = jnp.bfloat16
I32 = jnp.int32
U32 = jnp.uint32


def _cparams(sem):
    return pltpu.CompilerParams(dimension_semantics=sem, vmem_limit_bytes=VMEM_LIMIT_BYTES)


def _dot(a, b):
    return jnp.dot(a, b, preferred_element_type=F32)


def _dot_nt(a, b, precision=None):
    return lax.dot_general(a, b, (((1,), (1,)), ((), ())), preferred_element_type=F32,
                           precision=precision)


def _pack_halves(y):
    c = y.shape[1] // 2
    bits = pltpu.bitcast(y.astype(BF16).astype(F32), U32)
    return (bits[:, :c] >> 16) | (bits[:, c:] & jnp.uint32(0xFFFF0000))


def _unpack_halves(w):
    lo = pltpu.bitcast(w << 16, F32)
    hi = pltpu.bitcast(w & jnp.uint32(0xFFFF0000), F32)
    return lo, hi


def _store_row_tiles(ref, packed):
    m = packed.shape[0]
    for s in range(SUBLANES):
        ref[pl.ds(s, m, stride=SUBLANES), :] = packed[:, s * LANES:(s + 1) * LANES]


def _load_row_tiles(ref, m):
    return jnp.concatenate([ref[pl.ds(s, m, stride=SUBLANES), :] for s in range(SUBLANES)], axis=-1)


def _mm_kernel(x_ref, w_ref, o_ref):
    o_ref[...] = _dot(x_ref[...].astype(BF16), w_ref[...]).astype(o_ref.dtype)


def _matmul(x, w, out_dtype, tm, tn):
    m, k = x.shape
    n = w.shape[1]
    return pl.pallas_call(
        _mm_kernel,
        grid=(m // tm, n // tn),
        in_specs=[pl.BlockSpec((tm, k), lambda i, j: (i, 0)),
                  pl.BlockSpec((k, tn), lambda i, j: (0, j))],
        out_specs=pl.BlockSpec((tm, tn), lambda i, j: (i, j)),
        out_shape=jax.ShapeDtypeStruct((m, n), out_dtype),
        compiler_params=_cparams(("parallel", "arbitrary")),
        name="matmul",
    )(x, w)


def _mm_rope_kernel(x_ref, w_ref, cos_ref, sin_ref, o_ref, *, n_q_tiles, n_rope_tiles, first_gate_tile,
                    head_w, q_scale):
    j = pl.program_id(1)
    acc = _dot(x_ref[...].astype(BF16), w_ref[...])

    @pl.when((j >= n_rope_tiles) & (j < first_gate_tile))
    def _():
        o_ref[...] = acc.astype(o_ref.dtype)

    @pl.when(j >= first_gate_tile)
    def _():
        o_ref[...] = (acc * jax.nn.sigmoid(acc)).astype(o_ref.dtype)

    @pl.when(j < n_rope_tiles)
    def _():
        scale = jnp.where(j < n_q_tiles, q_scale, 1.0)
        cos = cos_ref[...] * scale
        sin = sin_ref[...] * scale
        half = head_w // 2
        for c0 in range(0, acc.shape[1], head_w):
            x1 = acc[:, c0:c0 + half]
            x2 = acc[:, c0 + half:c0 + head_w]
            o_ref[:, c0:c0 + half] = (x1 * cos - x2 * sin).astype(o_ref.dtype)
            o_ref[:, c0 + half:c0 + head_w] = (x2 * cos + x1 * sin).astype(o_ref.dtype)


def _matmul_rope(x, w, cos, sin, out_dtype, tm, tn, n_q_cols, n_rope_cols, gate_col0, head_w, q_scale):
    m, k = x.shape
    n = w.shape[1]
    nsb = cos.shape[0] // tm
    return pl.pallas_call(
        functools.partial(_mm_rope_kernel, n_q_tiles=n_q_cols // tn, n_rope_tiles=n_rope_cols // tn,
                          first_gate_tile=gate_col0 // tn, head_w=head_w, q_scale=q_scale),
        grid=(m // tm, n // tn),
        in_specs=[pl.BlockSpec((tm, k), lambda i, j: (i, 0)),
                  pl.BlockSpec((k, tn), lambda i, j: (0, j)),
                  pl.BlockSpec((tm, head_w // 2), lambda i, j: (i % nsb, 0)),
                  pl.BlockSpec((tm, head_w // 2), lambda i, j: (i % nsb, 0))],
        out_specs=pl.BlockSpec((tm, tn), lambda i, j: (i, j)),
        out_shape=jax.ShapeDtypeStruct((m, n), out_dtype),
        compiler_params=_cparams(("parallel", "arbitrary")),
        name="matmul_rope",
    )(x, w, cos, sin)


def _layer_norm_rows(z, g, b):
    mean = jnp.mean(z, axis=-1, keepdims=True)
    zc = z - mean
    var = jnp.mean(zc * zc, axis=-1, keepdims=True)
    return zc * lax.rsqrt(var + LN_EPS) * g + b


def _proj_ln_kernel(*refs, n_act, nk):
    acts = refs[:n_act]
    ws = refs[n_act:2 * n_act]
    x_ref, g_ref, b_ref, y_ref, yb_ref, yp_ref, acc_ref = refs[2 * n_act:]
    k = pl.program_id(1)
    part = _dot(acts[0][...], ws[0][...])
    for a, w in zip(acts[1:], ws[1:]):
        part = part + _dot(a[...], w[...])

    @pl.when(k == 0)
    def _():
        acc_ref[...] = part

    @pl.when(k > 0)
    def _():
        acc_ref[...] = acc_ref[...] + part

    @pl.when(k == nk - 1)
    def _():
        z = DN_ALPHA * x_ref[...] + acc_ref[...]
        y = _layer_norm_rows(z, g_ref[...], b_ref[...])
        y_ref[...] = y
        yb_ref[...] = y.astype(BF16)
        _store_row_tiles(yp_ref, _pack_halves(y))


def _proj_ln(acts, ws, x, g, b, tm, nk):
    m, d = x.shape
    n_act = len(acts)
    in_specs = []
    for a in acts:
        kk = a.shape[1] // nk
        in_specs.append(pl.BlockSpec((tm, kk), lambda i, k: (i, k)))
    for w in ws:
        kk = w.shape[0] // nk
        in_specs.append(pl.BlockSpec((kk, d), lambda i, k: (k, 0)))
    in_specs += [pl.BlockSpec((tm, d), lambda i, k: (i, 0)),
                 pl.BlockSpec((1, d), lambda i, k: (0, 0)),
                 pl.BlockSpec((1, d), lambda i, k: (0, 0))]
    return pl.pallas_call(
        functools.partial(_proj_ln_kernel, n_act=n_act, nk=nk),
        grid=(m // tm, nk),
        in_specs=in_specs,
        out_specs=[pl.BlockSpec((tm, d), lambda i, k: (i, 0)),
                   pl.BlockSpec((tm, d), lambda i, k: (i, 0)),
                   pl.BlockSpec((tm * SUBLANES, LANES), lambda i, k: (i, 0))],
        out_shape=[jax.ShapeDtypeStruct((m, d), F32), jax.ShapeDtypeStruct((m, d), BF16),
                   jax.ShapeDtypeStruct((m * SUBLANES, LANES), U32)],
        scratch_shapes=[pltpu.VMEM((tm, d), F32)],
        compiler_params=_cparams(("parallel", "arbitrary")),
        name="proj_ln",
    )(*acts, *ws, x, g.reshape(1, d), b.reshape(1, d))


def _na_bias_tables(rpb):
    nh = rpb.shape[0]
    c = np.arange(GRID_W)
    cs = np.clip(c - NA_WIN_W // 2, 0, GRID_W - NA_WIN_W)
    kc = np.arange(GRID_W)
    valid = (kc[None, :] >= cs[:, None]) & (kc[None, :] < cs[:, None] + NA_WIN_W)
    dc = kc[None, :] - c[:, None] + NA_WIN_W - 1
    onehot = (dc[:, :, None] == np.arange(2 * NA_WIN_W - 1)[None, None, :]) & valid[:, :, None]
    cols = jnp.einsum("hrd,ckd->hrck", rpb.astype(F32), jnp.asarray(onehot, F32),
                      precision=lax.Precision.HIGHEST)
    cols = jnp.where(jnp.asarray(valid)[None, None], cols, MASK_VALUE)
    tabs = jnp.stack([cols[:, off:off + NA_WIN_H] for off in range(NA_WIN_H)], axis=1)
    return tabs.transpose(0, 1, 3, 2, 4).reshape(nh, NA_WIN_H, GRID_W, NA_WIN_H * GRID_W)


def _na_kernel(q_ref, k_ref, v_ref, bias_ref, o_ref, *, rows, group):
    scale = NA_HEAD_DIM ** -0.5
    nkeys = NA_WIN_H * GRID_W

    def body(i, carry):
        geom, scores = [], []
        for u in range(group):
            r = i * group + u
            rs = jnp.clip(r - NA_WIN_H // 2, 0, rows - NA_WIN_H)
            off = rs - r + NA_WIN_H - 1
            q0 = pl.multiple_of(r * GRID_W, GRID_W)
            k0 = pl.multiple_of(rs * GRID_W, GRID_W)
            geom.append((q0, k0))
            s = _dot_nt(q_ref[pl.ds(q0, GRID_W), :], k_ref[pl.ds(k0, nkeys), :])
            scores.append(s * scale + bias_ref[0, off])
        for (q0, k0), s in zip(geom, scores):
            m = jnp.max(s, axis=-1, keepdims=True)
            p = jnp.exp(s - m)
            l = jnp.sum(p, axis=-1, keepdims=True)
            o = _dot(p.astype(BF16), v_ref[pl.ds(k0, nkeys), :]) / l
            o_ref[pl.ds(q0, GRID_W), :] = o.astype(o_ref.dtype)
        return carry

    lax.fori_loop(0, rows // group, body, 0)


def _na_attention(h, bias_tables, batch, seq):
    rows = seq // GRID_W
    d = NA_HEAD_DIM
    nkeys = NA_WIN_H * GRID_W
    return pl.pallas_call(
        functools.partial(_na_kernel, rows=rows, group=8),
        grid=(batch, NA_HEADS),
        in_specs=[pl.BlockSpec((seq, d), lambda b, hh: (b, hh)),
                  pl.BlockSpec((seq, d), lambda b, hh: (b, NA_HEADS + hh)),
                  pl.BlockSpec((seq, d), lambda b, hh: (b, 2 * NA_HEADS + hh)),
                  pl.BlockSpec((1, NA_WIN_H, GRID_W, nkeys), lambda b, hh: (hh, 0, 0, 0))],
        out_specs=pl.BlockSpec((seq, d), lambda b, hh: (b, hh)),
        out_shape=jax.ShapeDtypeStruct((batch * seq, NA_WIDTH), BF16),
        compiler_params=_cparams(("parallel", "arbitrary")),
        name="na_attention",
    )(h, h, h, bias_tables)


def _rms_rows(x, g):
    return x * lax.rsqrt(jnp.mean(x * x, axis=-1, keepdims=True) + RMS_EPS) * g


def _rope_lanes(t, cosf, sinf):
    return t * cosf + pltpu.roll(t, LANES // 2, 1) * sinf


def _mla_prep_kernel(cq_ref, ckv_ref, kr_ref, gq_ref, gkv_ref, wq_ref, wk_ref, wvt_ref, cos_ref, sin_ref,
                     q_ref, k_ref, vt_ref):
    dq = MLA_NOPE_DIM + MLA_ROPE_DIM
    cosf = cos_ref[...]
    sinf = sin_ref[...]
    cqn = _rms_rows(cq_ref[...].astype(F32), gq_ref[...]).astype(BF16)
    ckvn = _rms_rows(ckv_ref[...].astype(F32), gkv_ref[...]).astype(BF16)
    qf = _dot(cqn, wq_ref[...]) * (dq ** -0.5 * math.log2(math.e))
    kf = _dot(ckvn, wk_ref[...])
    vt_ref[...] = _dot_nt(wvt_ref[...], ckvn).astype(BF16)
    kpe = _rope_lanes(kr_ref[...].astype(F32), cosf, sinf).astype(BF16)
    for hh in range(MLA_HEADS):
        c0 = hh * 2 * LANES
        q_ref[:, c0:c0 + LANES] = qf[:, c0:c0 + LANES].astype(BF16)
        q_ref[:, c0 + LANES:c0 + 2 * LANES] = _rope_lanes(qf[:, c0 + LANES:c0 + 2 * LANES], cosf, sinf).astype(BF16)
        k_ref[:, c0:c0 + LANES] = kf[:, hh * LANES:(hh + 1) * LANES].astype(BF16)
        k_ref[:, c0 + LANES:c0 + 2 * LANES] = kpe


def _mla_prep(h, gq, gkv, wq_p, wk, wvt, cosf, sinf, col_cq, seq, tm):
    n = h.shape[0]
    hw = MLA_HEADS * 2 * LANES
    nsb = seq // tm
    b_cq = col_cq // MLA_Q_RANK
    b_ckv = (col_cq + MLA_Q_RANK) // MLA_KV_RANK
    b_kr = (col_cq + MLA_Q_RANK + MLA_KV_RANK) // LANES
    return pl.pallas_call(
        _mla_prep_kernel,
        grid=(n // tm,),
        in_specs=[pl.BlockSpec((tm, MLA_Q_RANK), lambda i: (i, b_cq)),
                  pl.BlockSpec((tm, MLA_KV_RANK), lambda i: (i, b_ckv)),
                  pl.BlockSpec((tm, LANES), lambda i: (i, b_kr)),
                  pl.BlockSpec((1, MLA_Q_RANK), lambda i: (0, 0)),
                  pl.BlockSpec((1, MLA_KV_RANK), lambda i: (0, 0)),
                  pl.BlockSpec((MLA_Q_RANK, hw), lambda i: (0, 0)),
                  pl.BlockSpec((MLA_KV_RANK, MLA_HEADS * LANES), lambda i: (0, 0)),
                  pl.BlockSpec((MLA_HEADS * MLA_V_DIM, MLA_KV_RANK), lambda i: (0, 0)),
                  pl.BlockSpec((tm, LANES), lambda i: (i % nsb, 0)),
                  pl.BlockSpec((tm, LANES), lambda i: (i % nsb, 0))],
        out_specs=[pl.BlockSpec((tm, hw), lambda i: (i, 0)),
                   pl.BlockSpec((tm, hw), lambda i: (i, 0)),
                   pl.BlockSpec((MLA_HEADS * MLA_V_DIM, tm), lambda i: (0, i))],
        out_shape=[jax.ShapeDtypeStruct((n, hw), BF16), jax.ShapeDtypeStruct((n, hw), BF16),
                   jax.ShapeDtypeStruct((MLA_HEADS * MLA_V_DIM, n), BF16)],
        compiler_params=_cparams(("parallel",)),
        name="mla_prep",
    )(h, h, h, gq.reshape(1, -1), gkv.reshape(1, -1), wq_p, wk, wvt, cosf, sinf)


def _mla_attn_kernel(q_ref, k_ref, vt_ref, o_ref, *, tk, sub):
    nchunk = k_ref.shape[0] // tk
    tq = q_ref.shape[0]
    nsub = tq // sub
    qs = [q_ref[s * sub:(s + 1) * sub, :] for s in range(nsub)]
    m = [jnp.full((1, sub), MASK_VALUE, F32) for _ in range(nsub)]
    l = [jnp.zeros((1, sub), F32) for _ in range(nsub)]
    acc = [jnp.zeros((MLA_V_DIM, sub), F32) for _ in range(nsub)]

    def scores(s, c):
        return _dot_nt(k_ref[c * tk:(c + 1) * tk, :], qs[s])

    st_next = [scores(s, 0) for s in range(nsub)]
    for c in range(nchunk):
        for s in range(nsub):
            st = st_next[s]
            m_new = jnp.maximum(m[s], jnp.max(st, axis=0, keepdims=True))
            a = jnp.exp2(m[s] - m_new)
            p = jnp.exp2(st - m_new)
            l[s] = a * l[s] + jnp.sum(p, axis=0, keepdims=True)
            if c + 1 < nchunk:
                st_next[s] = scores(s, c + 1)
            acc[s] = a * acc[s] + _dot(vt_ref[:, c * tk:(c + 1) * tk], p.astype(BF16))
            m[s] = m_new
    for s in range(nsub):
        o_ref[s * sub:(s + 1) * sub, :] = (acc[s] / l[s]).T.astype(o_ref.dtype)


def _mla_attention(q_p, k_p, vt, batch, seq, tq, tk, sub):
    n = q_p.shape[0]
    nqb = seq // tq
    return pl.pallas_call(
        functools.partial(_mla_attn_kernel, tk=tk, sub=sub),
        grid=(batch, MLA_HEADS, nqb),
        in_specs=[pl.BlockSpec((tq, 2 * LANES), lambda b, hh, i: (b * nqb + i, hh)),
                  pl.BlockSpec((seq, 2 * LANES), lambda b, hh, i: (b, hh)),
                  pl.BlockSpec((MLA_V_DIM, seq), lambda b, hh, i: (hh, b))],
        out_specs=pl.BlockSpec((tq, MLA_V_DIM), lambda b, hh, i: (b * nqb + i, hh)),
        out_shape=jax.ShapeDtypeStruct((n, MLA_HEADS * MLA_V_DIM), BF16),
        compiler_params=_cparams(("parallel", "parallel", "arbitrary")),
        name="mla_attention",
    )(q_p, k_p, vt)


def _ret_kernel(lg_ref, q_ref, k_ref, v_ref, g_ref, o_ref, acc_ref, st_ref, *, c_len, group):
    nchunk = q_ref.shape[0] // c_len
    hh = pl.program_id(1)
    lgf = lg_ref[0, hh]
    lgb = lg_ref[1, hh]
    ii = lax.broadcasted_iota(I32, (c_len, c_len), 0).astype(F32)
    jj = lax.broadcasted_iota(I32, (c_len, c_len), 1).astype(F32)
    rel = ii - jj
    dmat = jnp.where(rel >= 0, jnp.exp(lgf * jnp.maximum(rel, 0.0)), jnp.exp(lgb * jnp.maximum(-rel, 0.0)))
    pos = lax.broadcasted_iota(I32, (c_len, 1), 0).astype(F32)
    qdec_f = jnp.exp(lgf * (pos + 1.0))
    kdec_f = jnp.exp(lgf * (c_len - 1.0 - pos))
    qdec_b = jnp.exp(lgb * (c_len - pos))
    kdec_b = jnp.exp(lgb * pos)
    full_chunk = jnp.full((1, RET_V_DIM), float(c_len), F32)
    cdec_f = jnp.exp(lgf * full_chunk)
    cdec_b = jnp.exp(lgb * full_chunk)

    def decayed_keys_t(t0, kdec):
        return (k_ref[pl.ds(t0, c_len), :].astype(F32) * kdec).T.astype(BF16)

    st_ref[...] = jnp.zeros_like(st_ref)

    def bwd_body(i, carry):
        t0s = [pl.multiple_of((nchunk - 1 - (i * group + u)) * c_len, c_len) for u in range(group)]
        upd = [_dot(decayed_keys_t(t0, kdec_b), v_ref[pl.ds(t0, c_len), :]) for t0 in t0s]
        for t0, u_c in zip(t0s, upd):
            st = st_ref[...]
            acc_ref[pl.ds(t0, c_len), :] = _dot(q_ref[pl.ds(t0, c_len), :], st.astype(BF16)) * qdec_b
            st_ref[...] = st * cdec_b + u_c
        return carry

    lax.fori_loop(0, nchunk // group, bwd_body, 0)
    st_ref[...] = jnp.zeros_like(st_ref)

    def fwd_body(i, carry):
        t0s = [pl.multiple_of((i * group + u) * c_len, c_len) for u in range(group)]
        scs = [_dot_nt(q_ref[pl.ds(t0, c_len), :], k_ref[pl.ds(t0, c_len), :]) * dmat for t0 in t0s]
        upd = [_dot(decayed_keys_t(t0, kdec_f), v_ref[pl.ds(t0, c_len), :]) for t0 in t0s]
        for t0, sc, u_c in zip(t0s, scs, upd):
            st = st_ref[...]
            r = (_dot(sc.astype(BF16), v_ref[pl.ds(t0, c_len), :])
                 + _dot(q_ref[pl.ds(t0, c_len), :], st.astype(BF16)) * qdec_f
                 + acc_ref[pl.ds(t0, c_len), :])
            st_ref[...] = st * cdec_f + u_c
            r = r - jnp.mean(r, axis=-1, keepdims=True)
            r = r * lax.rsqrt(jnp.mean(r * r, axis=-1, keepdims=True) + LN_EPS)
            o_ref[pl.ds(t0, c_len), :] = (g_ref[pl.ds(t0, c_len), :].astype(F32) * r).astype(o_ref.dtype)
        return carry

    lax.fori_loop(0, nchunk // group, fwd_body, 0)


def _retention(hc, lg, batch, seq, c_len, group):
    dk, dv, nh = RET_QK_DIM, RET_V_DIM, RET_HEADS
    v_blk0 = (2 * nh * dk) // dv
    return pl.pallas_call(
        functools.partial(_ret_kernel, c_len=c_len, group=group),
        grid=(batch, nh),
        in_specs=[pl.BlockSpec(memory_space=pltpu.SMEM),
                  pl.BlockSpec((seq, dk), lambda b, hh: (b, hh)),
                  pl.BlockSpec((seq, dk), lambda b, hh: (b, nh + hh)),
                  pl.BlockSpec((seq, dv), lambda b, hh: (b, v_blk0 + hh)),
                  pl.BlockSpec((seq, dv), lambda b, hh: (b, v_blk0 + nh + hh))],
        out_specs=pl.BlockSpec((seq, dv), lambda b, hh: (b, hh)),
        scratch_shapes=[pltpu.VMEM((seq, dv), F32), pltpu.VMEM((dk, dv), F32)],
        out_shape=jax.ShapeDtypeStruct((batch * seq, nh * dv), BF16),
        compiler_params=_cparams(("parallel", "arbitrary")),
        name="retention",
    )(lg, hc, hc, hc, hc)


ROUTER_ROWS = 40


def _router_kernel(x_ref, wt_ref, b_ref, tri_ref, ids_ref, wts_ref, cnt_ref, carry_ref):
    i = pl.program_id(0)
    tm = x_ref.shape[0]

    @pl.when(i == 0)
    def _():
        carry_ref[...] = jnp.zeros_like(carry_ref)

    logits = _dot_nt(wt_ref[...], x_ref[...]) + b_ref[...]
    grow = lax.broadcasted_iota(I32, (SUBLANES, tm), 0).astype(F32)
    gl = jnp.where(grow < N_GROUPS, logits[0:SUBLANES], MASK_VALUE)
    gmax = jnp.max(gl, axis=0, keepdims=True)
    gsum = jnp.sum(jnp.exp(gl - gmax), axis=0, keepdims=True)
    p_group = 1.0 / gsum
    g_idx = jnp.min(jnp.where(gl == gmax, grow, float(N_GROUPS)), axis=0, keepdims=True)
    sel = jnp.zeros((EXPERTS_PER_GROUP, tm), F32)
    for g in range(N_GROUPS):
        r0 = SUBLANES + g * EXPERTS_PER_GROUP
        sel = sel + jnp.where(g_idx == float(g), logits[r0:r0 + EXPERTS_PER_GROUP], 0.0)
    erow = lax.broadcasted_iota(I32, (EXPERTS_PER_GROUP, tm), 0).astype(F32)
    smax = jnp.max(sel, axis=0, keepdims=True)
    sexp = jnp.exp(sel - smax)
    probs = sexp / jnp.sum(sexp, axis=0, keepdims=True)
    p1 = jnp.max(probs, axis=0, keepdims=True)
    i1 = jnp.min(jnp.where(probs == p1, erow, float(EXPERTS_PER_GROUP)), axis=0, keepdims=True)
    rest = jnp.where(erow == i1, -1.0, probs)
    p2 = jnp.max(rest, axis=0, keepdims=True)
    i2 = jnp.min(jnp.where(rest == p2, erow, float(EXPERTS_PER_GROUP)), axis=0, keepdims=True)
    denom = p1 + p2
    e0 = g_idx * EXPERTS_PER_GROUP + i1
    e1 = g_idx * EXPERTS_PER_GROUP + i2

    xrow = lax.broadcasted_iota(I32, (N_EXPERTS, tm), 0).astype(F32)
    oh0 = jnp.where(xrow == e0, 1.0, 0.0)
    oh1 = jnp.where(xrow == e1, 1.0, 0.0)
    onehot = oh0 + oh1
    before = _dot(onehot.astype(BF16), tri_ref[...]) + carry_ref[:, 0:1]
    rank0 = jnp.sum(oh0 * before, axis=0, keepdims=True)
    rank1 = jnp.sum(oh1 * before, axis=0, keepdims=True)
    carry_ref[...] = carry_ref[...] + jnp.sum(onehot, axis=1, keepdims=True)

    ids_ref[...] = jnp.zeros_like(ids_ref)
    ids_ref[0:1, :] = e0.astype(I32)
    ids_ref[1:2, :] = e1.astype(I32)
    ids_ref[2:3, :] = rank0.astype(I32)
    ids_ref[3:4, :] = rank1.astype(I32)
    wts_ref[...] = jnp.zeros_like(wts_ref)
    wts_ref[0:1, :] = p_group * p1 / denom
    wts_ref[1:2, :] = p_group * p2 / denom
    cnt_ref[...] = carry_ref[...]


def _router(x, wt, bias, tm):
    n, d = x.shape
    tri = jnp.asarray(np.triu(np.ones((tm, tm), np.float32), 1), BF16)
    return pl.pallas_call(
        _router_kernel,
        grid=(n // tm,),
        in_specs=[pl.BlockSpec((tm, d), lambda i: (i, 0)),
                  pl.BlockSpec((ROUTER_ROWS, d), lambda i: (0, 0)),
                  pl.BlockSpec((ROUTER_ROWS, 1), lambda i: (0, 0)),
                  pl.BlockSpec((tm, tm), lambda i: (0, 0))],
        out_specs=[pl.BlockSpec((SUBLANES, tm), lambda i: (0, i)),
                   pl.BlockSpec((SUBLANES, tm), lambda i: (0, i)),
                   pl.BlockSpec((N_EXPERTS, LANES), lambda i: (0, 0))],
        out_shape=[jax.ShapeDtypeStruct((SUBLANES, n), I32), jax.ShapeDtypeStruct((SUBLANES, n), F32),
                   jax.ShapeDtypeStruct((N_EXPERTS, LANES), F32)],
        scratch_shapes=[pltpu.VMEM((N_EXPERTS, LANES), F32)],
        compiler_params=_cparams(("arbitrary",)),
        name="moe_router",
    )(x, wt, bias, tri)


def _slots_kernel(ids_ref, cnt_ref, slots_ref, blk_ref, *, nblk_pad):
    tm = ids_ref.shape[1]
    cnt = cnt_ref[:, 0:1]
    padded = jnp.floor((cnt + (MOE_BLOCK - 1)) / MOE_BLOCK) * MOE_BLOCK
    er = lax.broadcasted_iota(I32, (N_EXPERTS, N_EXPERTS), 0)
    ec = lax.broadcasted_iota(I32, (N_EXPERTS, N_EXPERTS), 1)
    padded_row = jnp.sum(jnp.where(er == ec, padded, 0.0), axis=0, keepdims=True)
    p_start = jnp.sum(jnp.where(ec < er, padded_row, 0.0), axis=1, keepdims=True)
    p_end = p_start + padded
    xrow = lax.broadcasted_iota(I32, (N_EXPERTS, tm), 0)
    e0 = ids_ref[0:1, :]
    e1 = ids_ref[1:2, :]
    s0 = jnp.sum(jnp.where(xrow == e0, p_start, 0.0), axis=0, keepdims=True).astype(I32) + ids_ref[2:3, :]
    s1 = jnp.sum(jnp.where(xrow == e1, p_start, 0.0), axis=0, keepdims=True).astype(I32) + ids_ref[3:4, :]
    slots_ref[...] = jnp.zeros_like(slots_ref)
    slots_ref[0:1, :] = s0
    slots_ref[1:2, :] = s1
    bstart = (lax.broadcasted_iota(I32, (1, nblk_pad), 1) * MOE_BLOCK).astype(F32)
    blk_e = jnp.minimum(jnp.sum(jnp.where(p_end <= bstart, 1.0, 0.0), axis=0, keepdims=True), N_EXPERTS - 1.0)
    total = jnp.sum(padded, axis=0, keepdims=True)
    erow = lax.broadcasted_iota(I32, (N_EXPERTS, nblk_pad), 0).astype(F32)
    own_end = jnp.sum(jnp.where(erow == blk_e, p_end, 0.0), axis=0, keepdims=True)
    nxt_e = jnp.minimum(jnp.sum(jnp.where(p_end <= own_end, 1.0, 0.0), axis=0, keepdims=True), N_EXPERTS - 1.0)
    nxt_e = jnp.where(own_end < total, nxt_e, -1.0)
    blk_ref[...] = jnp.zeros_like(blk_ref)
    blk_ref[0:1, :] = blk_e.astype(I32)
    blk_ref[1:2, :] = jnp.broadcast_to((total / MOE_BLOCK).astype(I32), (1, nblk_pad))
    blk_ref[2:3, :] = nxt_e.astype(I32)
    lane = lax.broadcasted_iota(I32, (N_EXPERTS, nblk_pad), 1).astype(F32)
    blk_ref[3:4, :] = jnp.sum(jnp.where(erow == lane, p_start + cnt, 0.0), axis=0, keepdims=True).astype(I32)
    blk_ref[4:5, :] = jnp.sum(jnp.where(erow == lane, p_end, 0.0), axis=0, keepdims=True).astype(I32)


def _slots(ids, cnt, tm, nblk_pad):
    n = ids.shape[1]
    return pl.pallas_call(
        functools.partial(_slots_kernel, nblk_pad=nblk_pad),
        grid=(n // tm,),
        in_specs=[pl.BlockSpec((SUBLANES, tm), lambda i: (0, i)),
                  pl.BlockSpec((N_EXPERTS, LANES), lambda i: (0, 0))],
        out_specs=[pl.BlockSpec((SUBLANES, tm), lambda i: (0, i)),
                   pl.BlockSpec((SUBLANES, nblk_pad), lambda i: (0, 0))],
        out_shape=[jax.ShapeDtypeStruct((SUBLANES, n), I32), jax.ShapeDtypeStruct((SUBLANES, nblk_pad), I32)],
        compiler_params=_cparams(("arbitrary",)),
        name="moe_slots",
    )(ids, cnt)


def _slot_tokens_kernel(slots_ref, blk_ref, tok_ref, *, n, cap, nblk_pad):
    def zero(j, carry):
        tok_ref[j] = 0
        return carry

    def scatter(t, carry):
        tok_ref[slots_ref[t]] = t
        tok_ref[slots_ref[n + t]] = t
        return carry

    for e in range(N_EXPERTS):
        lax.fori_loop(blk_ref[3 * nblk_pad + e], blk_ref[4 * nblk_pad + e], zero, 0)
    lax.fori_loop(blk_ref[nblk_pad] * MOE_BLOCK, cap, zero, 0)
    lax.fori_loop(0, n, scatter, 0, unroll=8)


def _slot_tokens(slots_flat, blk_flat, n, cap, nblk_pad):
    return pl.pallas_call(
        functools.partial(_slot_tokens_kernel, n=n, cap=cap, nblk_pad=nblk_pad),
        grid_spec=pltpu.PrefetchScalarGridSpec(
            num_scalar_prefetch=2,
            grid=(1,),
            in_specs=[],
            out_specs=pl.BlockSpec(memory_space=pltpu.SMEM)),
        out_shape=jax.ShapeDtypeStruct((cap,), I32),
        compiler_params=_cparams(("arbitrary",)),
        name="moe_slot_tokens",
    )(slots_flat, blk_flat)


def _expert_kernel(blk_ref, tok_ref, xpk_hbm, wg_hbm, wu_hbm, wd_hbm, y_ref, xbuf, wgf, wuf, wdf, wgb, wub, wdb,
                   sems, wsems, cnt_ref, *, nblk_pad, layer):
    i = pl.program_id(0)
    n_used = blk_ref[nblk_pad]
    cast_rows = 256

    def weight_copies(e, slot):
        return [pltpu.make_async_copy(src.at[layer, e], dst.at[slot], wsems.at[slot])
                for src, dst in ((wg_hbm, wgf), (wu_hbm, wuf), (wd_hbm, wdf))]

    def start_rows(block, buf, r_lo=0, r_hi=MOE_BLOCK):
        for r in range(r_lo, r_hi):
            src0 = pl.multiple_of(tok_ref[block * MOE_BLOCK + r] * SUBLANES, SUBLANES)
            pltpu.make_async_copy(xpk_hbm.at[pl.ds(src0, SUBLANES)],
                                  xbuf.at[buf, pl.ds(r * SUBLANES, SUBLANES)], sems.at[buf]).start()

    def wait_rows(buf):
        pltpu.make_async_copy(xpk_hbm.at[pl.ds(0, MOE_BLOCK * SUBLANES)], xbuf.at[buf], sems.at[buf]).wait()

    @pl.when(i == 0)
    def _():
        cnt_ref[0] = 0
        for cp in weight_copies(blk_ref[0], 0):
            cp.start(priority=1)
        start_rows(0, 0)

    @pl.when((i < n_used) & ((i == 0) | (blk_ref[i] != blk_ref[jnp.maximum(i - 1, 0)])))
    def _():
        slot = cnt_ref[0] % 2
        cnt_ref[0] = cnt_ref[0] + 1
        for cp in weight_copies(blk_ref[i], slot):
            cp.wait()
        nxt_e = blk_ref[2 * nblk_pad + i]

        @pl.when(nxt_e >= 0)
        def _():
            for cp in weight_copies(nxt_e, 1 - slot):
                cp.start(priority=1)

        for src, dst in ((wgf, wgb), (wuf, wub), (wdf, wdb)):
            for r0 in range(0, dst.shape[0], cast_rows):
                dst[r0:r0 + cast_rows, :] = src[slot, r0:r0 + cast_rows, :].astype(BF16)

    @pl.when(i < n_used)
    def _():
        buf = i % 2
        nxt = jnp.minimum(i + 1, n_used - 1)
        wait_rows(buf)
        lo, hi = _unpack_halves(_load_row_tiles(xbuf.at[buf], MOE_BLOCK))
        xb = jnp.concatenate([lo.astype(BF16), hi.astype(BF16)], axis=-1)
        g = _dot(xb, wgb[...])
        start_rows(nxt, 1 - buf, 0, MOE_BLOCK // 2)
        u = _dot(xb, wub[...])
        start_rows(nxt, 1 - buf, MOE_BLOCK // 2, MOE_BLOCK)
        hmid = (g * jax.nn.sigmoid(g) * u).astype(BF16)
        _store_row_tiles(y_ref, _pack_halves(_dot(hmid, wdb[...])))

    @pl.when(i == n_used - 1)
    def _():
        wait_rows(1 - i % 2)

    @pl.when(i >= n_used)
    def _():
        y_ref[...] = jnp.zeros_like(y_ref)


def _experts(blk_flat, slot_tok, xpk, wg, wu, wd, layer, nblk_pad):
    d, de = wg.shape[2], wg.shape[3]
    assert d == 2 * SUBLANES * LANES and xpk.shape[1] == LANES
    cap = slot_tok.shape[0]
    tile_rows = MOE_BLOCK * SUBLANES
    return pl.pallas_call(
        functools.partial(_expert_kernel, nblk_pad=nblk_pad, layer=layer),
        grid_spec=pltpu.PrefetchScalarGridSpec(
            num_scalar_prefetch=2,
            grid=(cap // MOE_BLOCK,),
            in_specs=[pl.BlockSpec(memory_space=pl.ANY), pl.BlockSpec(memory_space=pl.ANY),
                      pl.BlockSpec(memory_space=pl.ANY), pl.BlockSpec(memory_space=pl.ANY)],
            out_specs=pl.BlockSpec((tile_rows, LANES), lambda i, blk, tok: (i, 0)),
            scratch_shapes=[pltpu.VMEM((2, tile_rows, LANES), U32),
                            pltpu.VMEM((2, d, de), F32), pltpu.VMEM((2, d, de), F32), pltpu.VMEM((2, de, d), F32),
                            pltpu.VMEM((d, de), BF16), pltpu.VMEM((d, de), BF16), pltpu.VMEM((de, d), BF16),
                            pltpu.SemaphoreType.DMA((2,)), pltpu.SemaphoreType.DMA((2,)),
                            pltpu.SMEM((1,), I32)]),
        out_shape=jax.ShapeDtypeStruct((cap * SUBLANES, LANES), U32),
        compiler_params=_cparams(("arbitrary",)),
        name="moe_experts",
    )(blk_flat, slot_tok, xpk, wg, wu, wd)


def _tail_kernel(slots_ref, x_ref, xb_ref, p_ref, wts_ref, wgate_ref, bgate_ref, wproj_ref, g_ref, b_ref,
                 yb_hbm, y_ref, ybf_ref, rows_ref, sems):
    i = pl.program_id(0)
    nsteps = pl.num_programs(0)
    tm = x_ref.shape[0]
    n = nsteps * tm

    def start_rows(step, buf):
        for t in range(tm):
            for which in range(2):
                src0 = pl.multiple_of(slots_ref[which * n + step * tm + t] * SUBLANES, SUBLANES)
                pltpu.make_async_copy(yb_hbm.at[pl.ds(src0, SUBLANES)],
                                      rows_ref.at[buf, which, pl.ds(t * SUBLANES, SUBLANES)], sems.at[buf]).start()

    def wait_rows(buf):
        for which in range(2):
            pltpu.make_async_copy(yb_hbm.at[pl.ds(0, tm * SUBLANES)], rows_ref.at[buf, which],
                                  sems.at[buf]).wait()

    @pl.when(i == 0)
    def _():
        start_rows(0, 0)

    buf = i % 2
    nxt = jnp.minimum(i + 1, nsteps - 1)
    wait_rows(buf)
    start_rows(nxt, 1 - buf)
    gate = jax.nn.sigmoid(_dot(xb_ref[...], wgate_ref[...]) + bgate_ref[...])
    ple = gate * _dot(p_ref[0].astype(BF16), wproj_ref[...])
    w = wts_ref[...]
    lo0, hi0 = _unpack_halves(_load_row_tiles(rows_ref.at[buf, 0], tm))
    lo1, hi1 = _unpack_halves(_load_row_tiles(rows_ref.at[buf, 1], tm))
    w0 = w[:, 0:1]
    w1 = w[:, 1:2]
    ffn = jnp.concatenate([lo0 * w0 + lo1 * w1, hi0 * w0 + hi1 * w1], axis=-1)
    z = DN_ALPHA * x_ref[...] + ffn + ple
    y = _layer_norm_rows(z, g_ref[...], b_ref[...])
    y_ref[...] = y
    ybf_ref[...] = y.astype(BF16)

    @pl.when(i == nsteps - 1)
    def _():
        wait_rows(1 - buf)


def _layer_tail(slots_flat, x, xb, p, layer, wts_t, wgate, bgate, wproj, g, b, yb, tm):
    n, d = x.shape
    pd = p.shape[2]
    return pl.pallas_call(
        _tail_kernel,
        grid_spec=pltpu.PrefetchScalarGridSpec(
            num_scalar_prefetch=1,
            grid=(n // tm,),
            in_specs=[pl.BlockSpec((tm, d), lambda i, s: (i, 0)),
                      pl.BlockSpec((tm, d), lambda i, s: (i, 0)),
                      pl.BlockSpec((1, tm, pd), lambda i, s: (layer, i, 0)),
                      pl.BlockSpec((tm, 2), lambda i, s: (i, 0)),
                      pl.BlockSpec((d, d), lambda i, s: (0, 0)),
                      pl.BlockSpec((1, d), lambda i, s: (0, 0)),
                      pl.BlockSpec((pd, d), lambda i, s: (0, 0)),
                      pl.BlockSpec((1, d), lambda i, s: (0, 0)),
                      pl.BlockSpec((1, d), lambda i, s: (0, 0)),
                      pl.BlockSpec(memory_space=pl.ANY)],
            out_specs=[pl.BlockSpec((tm, d), lambda i, s: (i, 0)),
                       pl.BlockSpec((tm, d), lambda i, s: (i, 0))],
            scratch_shapes=[pltpu.VMEM((2, 2, tm * SUBLANES, LANES), U32), pltpu.SemaphoreType.DMA((2,))]),
        out_shape=[jax.ShapeDtypeStruct((n, d), F32), jax.ShapeDtypeStruct((n, d), BF16)],
        compiler_params=_cparams(("arbitrary",)),
        name="layer_tail",
    )(slots_flat, x, xb, p, wts_t, wgate, bgate.reshape(1, d), wproj, g.reshape(1, d), b.reshape(1, d), yb)


def _rope_table(seq, dim):
    pos = jnp.arange(seq, dtype=F32)
    inv = jnp.exp(jnp.arange(0, dim, 2, dtype=F32) * (-math.log(ROPE_BASE) / dim))
    ang = pos[:, None] * inv[None, :]
    return jnp.cos(ang), jnp.sin(ang)


def _moe_layer(x, xb, xpk, p, layer, w_group, b_group, w_router, b_router, w_gate, w_up, w_down,
               ple_w_proj, ple_w_gate, ple_b_gate, ln_g, ln_b):
    n, d = x.shape
    nblk = -(-(2 * n) // MOE_BLOCK) + N_EXPERTS
    nblk_pad = -(-nblk // LANES) * LANES
    cap = nblk * MOE_BLOCK
    wt = jnp.zeros((ROUTER_ROWS, d), F32)
    wt = wt.at[0:N_GROUPS].set(w_group.T)
    wt = wt.at[SUBLANES:].set(w_router.transpose(0, 2, 1).reshape(N_EXPERTS, d))
    bias = jnp.zeros((ROUTER_ROWS, 1), F32)
    bias = bias.at[0:N_GROUPS, 0].set(b_group)
    bias = bias.at[SUBLANES:, 0].set(b_router.reshape(N_EXPERTS))
    ids, wts, cnt = _router(xb, wt.astype(BF16), bias, tm=512)
    slots, blk = _slots(ids, cnt, tm=min(2048, n), nblk_pad=nblk_pad)
    slots_flat = slots[0:2].reshape(2 * n)
    blk_flat = blk[0:5].reshape(5 * nblk_pad)
    slot_tok = _slot_tokens(slots_flat, blk_flat, n, cap, nblk_pad)
    yb = _experts(blk_flat, slot_tok, xpk, w_gate, w_up, w_down, layer, nblk_pad)
    return _layer_tail(slots_flat, x, xb, p, layer, wts[0:2].T, ple_w_gate.astype(BF16), ple_b_gate,
                       ple_w_proj.astype(BF16), ln_g, ln_b, yb, tm=256)


def _mixer_ab(x, xres, batch, seq, w_in, rpb, q_norm, w_uq, kv_norm, w_ukv, w_out, ln_g, ln_b):
    d = x.shape[1]
    o1 = 3 * NA_WIDTH
    o2 = o1 + MLA_Q_RANK
    o3 = o2 + MLA_KV_RANK
    half = MLA_ROPE_DIM // 2
    kr = w_in[:, o3:o3 + MLA_ROPE_DIM]
    kr_sw = jnp.concatenate([kr[:, half:], kr[:, :half]], axis=1)
    width = -(-(o3 + 2 * MLA_ROPE_DIM) // 1024) * 1024
    w_in_p = jnp.concatenate([w_in, kr_sw, jnp.zeros((d, width - o3 - 2 * MLA_ROPE_DIM), F32)], axis=1)
    h = _matmul(x, w_in_p.astype(BF16), BF16, tm=512, tn=1024)
    a_out = _na_attention(h, _na_bias_tables(rpb), batch, seq)
    dq = MLA_NOPE_DIM + MLA_ROPE_DIM
    wq = w_uq.reshape(MLA_Q_RANK, MLA_HEADS, dq)
    wq_pe = wq[:, :, MLA_NOPE_DIM:]
    wq_p = jnp.concatenate([wq, wq_pe[:, :, half:], wq_pe[:, :, :half]], axis=2)
    wq_p = wq_p.reshape(MLA_Q_RANK, MLA_HEADS * 2 * LANES).astype(BF16)
    wkv = w_ukv.reshape(MLA_KV_RANK, MLA_HEADS, MLA_NOPE_DIM + MLA_V_DIM)
    wk = wkv[:, :, :MLA_NOPE_DIM].reshape(MLA_KV_RANK, MLA_HEADS * MLA_NOPE_DIM).astype(BF16)
    wvt = wkv[:, :, MLA_NOPE_DIM:].reshape(MLA_KV_RANK, MLA_HEADS * MLA_V_DIM).T.astype(BF16)
    cos, sin = _rope_table(seq, MLA_ROPE_DIM)
    zpad = jnp.zeros((seq, LANES - MLA_ROPE_DIM), F32)
    cosf = jnp.concatenate([cos, cos, zpad], axis=1)
    sinf = jnp.concatenate([-sin, sin, zpad], axis=1)
    q_p, k_p, vt = _mla_prep(h, q_norm, kv_norm, wq_p, wk, wvt, cosf, sinf, o1, seq, tm=512)
    b_out = _mla_attention(q_p, k_p, vt, batch, seq, tq=1024, tk=1024, sub=256)
    w_out_b = w_out.astype(BF16)
    return _proj_ln([a_out, b_out], [w_out_b[:NA_WIDTH], w_out_b[NA_WIDTH:]], xres, ln_g, ln_b, tm=512, nk=1)


def _mixer_c(xb, xres, batch, seq, w_in, log_rate_f, log_rate_b, w_out, ln_g, ln_b):
    cosr, sinr = _rope_table(seq, RET_QK_DIM)
    n_q = RET_HEADS * RET_QK_DIM
    hc = _matmul_rope(xb, w_in.astype(BF16), cosr, sinr, BF16, tm=min(1024, seq), tn=1024, n_q_cols=n_q,
                      n_rope_cols=2 * n_q, gate_col0=2 * n_q + RET_HEADS * RET_V_DIM, head_w=RET_QK_DIM,
                      q_scale=RET_QK_DIM ** -0.5)
    lg = jnp.stack([jnp.log1p(-jnp.exp(log_rate_f.astype(F32))), jnp.log1p(-jnp.exp(log_rate_b.astype(F32)))])
    r = _retention(hc, lg, batch, seq, c_len=256, group=2)
    return _proj_ln([r], [w_out.astype(BF16)], xres, ln_g, ln_b, tm=512, nk=2)


def kernel(x, p, ab_w_in, ab_rpb, ab_q_norm, ab_w_uq, ab_kv_norm, ab_w_ukv, ab_w_out, c_w_in, c_log_rate_f,
           c_log_rate_b, c_w_out, ln1_g, ln1_b, moe_w_group, moe_b_group, moe_w_router, moe_b_router,
           moe_w_gate, moe_w_up, moe_w_down, ple_w_proj, ple_w_gate, ple_b_gate, ln2_g, ln2_b):
    batch, seq, d = x.shape
    n = batch * seq
    xf = x.reshape(n, d)
    p_flat = p.reshape(DEPTH, n, -1)
    xb = None
    for i in range(DEPTH):
        j = i // 2
        if i % 2 == 0:
            src = xf if xb is None else xb
            xf, xb, xpk = _mixer_ab(src, xf, batch, seq, ab_w_in[j], ab_rpb[j], ab_q_norm[j], ab_w_uq[j],
                                    ab_kv_norm[j], ab_w_ukv[j], ab_w_out[j], ln1_g[i], ln1_b[i])
        else:
            xf, xb, xpk = _mixer_c(xb, xf, batch, seq, c_w_in[j], c_log_rate_f[j], c_log_rate_b[j], c_w_out[j],
                                   ln1_g[i], ln1_b[i])
        xf, xb = _moe_layer(xf, xb, xpk, p_flat, i, moe_w_group[i], moe_b_group[i], moe_w_router[i],
                            moe_b_router[i], moe_w_gate, moe_w_up, moe_w_down, ple_w_proj[i],
                            ple_w_gate[i], ple_b_gate[i], ln2_g[i], ln2_b[i])
    return xf.reshape(batch, seq, d)
```

```python
import functools
import math

import numpy as np
import jax
import jax.numpy as jnp
from jax import lax
from jax.experimental import pallas as pl
from jax.experimental.pallas import tpu as pltpu

DEPTH = 2
GRID_W = 64
NA_HEADS = 8
NA_HEAD_DIM = 128
NA_WIN_H = 8
NA_WIN_W = 16
MLA_HEADS = 8
MLA_Q_RANK = 512
MLA_KV_RANK = 256
MLA_NOPE_DIM = 128
MLA_ROPE_DIM = 64
MLA_V_DIM = 128
RET_HEADS = 8
RET_QK_DIM = 256
RET_V_DIM = 512
RET_CHUNK = 128
N_GROUPS = 4
EXPERTS_PER_GROUP = 8
N_EXPERTS = N_GROUPS * EXPERTS_PER_GROUP
D_EXPERT = 512
MOE_BLOCK = 128
ROPE_BASE = 10000.0
LN_EPS = 1e-5
RMS_EPS = 1e-6
DN_ALPHA = (2 * DEPTH) ** 0.25
NA_WIDTH = NA_HEADS * NA_HEAD_DIM

LANES = 128
SUBLANES = 8
VMEM_LIMIT_BYTES = 60 * 1024 * 1024
MASK_VALUE = -1e30

F32 = jnp.float32
BF16 = jnp.bfloat16
I32 = jnp.int32
U32 = jnp.uint32


def _cparams(sem):
    return pltpu.CompilerParams(dimension_semantics=sem, vmem_limit_bytes=VMEM_LIMIT_BYTES)


def _dot(a, b):
    return jnp.dot(a, b, preferred_element_type=F32)


def _dot_nt(a, b, precision=None):
    return lax.dot_general(a, b, (((1,), (1,)), ((), ())), preferred_element_type=F32,
                           precision=precision)


def _pack_halves(y):
    c = y.shape[1] // 2
    bits = pltpu.bitcast(y.astype(BF16).astype(F32), U32)
    return (bits[:, :c] >> 16) | (bits[:, c:] & jnp.uint32(0xFFFF0000))


def _unpack_halves(w):
    lo = pltpu.bitcast(w << 16, F32)
    hi = pltpu.bitcast(w & jnp.uint32(0xFFFF0000), F32)
    return lo, hi


def _store_row_tiles(ref, packed):
    m = packed.shape[0]
    for s in range(SUBLANES):
        ref[pl.ds(s, m, stride=SUBLANES), :] = packed[:, s * LANES:(s + 1) * LANES]


def _load_row_tiles(ref, m):
    return jnp.concatenate([ref[pl.ds(s, m, stride=SUBLANES), :] for s in range(SUBLANES)], axis=-1)


def _mm_kernel(x_ref, w_ref, o_ref):
    o_ref[...] = _dot(x_ref[...].astype(BF16), w_ref[...]).astype(o_ref.dtype)


def _matmul(x, w, out_dtype, tm, tn):
    m, k = x.shape
    n = w.shape[1]
    return pl.pallas_call(
        _mm_kernel,
        grid=(m // tm, n // tn),
        in_specs=[pl.BlockSpec((tm, k), lambda i, j: (i, 0)),
                  pl.BlockSpec((k, tn), lambda i, j: (0, j))],
        out_specs=pl.BlockSpec((tm, tn), lambda i, j: (i, j)),
        out_shape=jax.ShapeDtypeStruct((m, n), out_dtype),
        compiler_params=_cparams(("parallel", "arbitrary")),
        name="matmul",
    )(x, w)


def _mm_rope_kernel(x_ref, w_ref, cos_ref, sin_ref, o_ref, *, n_q_tiles, n_rope_tiles, first_gate_tile,
                    head_w, q_scale):
    j = pl.program_id(1)
    acc = _dot(x_ref[...].astype(BF16), w_ref[...])

    @pl.when((j >= n_rope_tiles) & (j < first_gate_tile))
    def _():
        o_ref[...] = acc.astype(o_ref.dtype)

    @pl.when(j >= first_gate_tile)
    def _():
        o_ref[...] = (acc * jax.nn.sigmoid(acc)).astype(o_ref.dtype)

    @pl.when(j < n_rope_tiles)
    def _():
        scale = jnp.where(j < n_q_tiles, q_scale, 1.0)
        cos = cos_ref[...] * scale
        sin = sin_ref[...] * scale
        half = head_w // 2
        for c0 in range(0, acc.shape[1], head_w):
            x1 = acc[:, c0:c0 + half]
            x2 = acc[:, c0 + half:c0 + head_w]
            o_ref[:, c0:c0 + half] = (x1 * cos - x2 * sin).astype(o_ref.dtype)
            o_ref[:, c0 + half:c0 + head_w] = (x2 * cos + x1 * sin).astype(o_ref.dtype)


def _matmul_rope(x, w, cos, sin, out_dtype, tm, tn, n_q_cols, n_rope_cols, gate_col0, head_w, q_scale):
    m, k = x.shape
    n = w.shape[1]
    nsb = cos.shape[0] // tm
    return pl.pallas_call(
        functools.partial(_mm_rope_kernel, n_q_tiles=n_q_cols // tn, n_rope_tiles=n_rope_cols // tn,
                          first_gate_tile=gate_col0 // tn, head_w=head_w, q_scale=q_scale),
        grid=(m // tm, n // tn),
        in_specs=[pl.BlockSpec((tm, k), lambda i, j: (i, 0)),
                  pl.BlockSpec((k, tn), lambda i, j: (0, j)),
                  pl.BlockSpec((tm, head_w // 2), lambda i, j: (i % nsb, 0)),
                  pl.BlockSpec((tm, head_w // 2), lambda i, j: (i % nsb, 0))],
        out_specs=pl.BlockSpec((tm, tn), lambda i, j: (i, j)),
        out_shape=jax.ShapeDtypeStruct((m, n), out_dtype),
        compiler_params=_cparams(("parallel", "arbitrary")),
        name="matmul_rope",
    )(x, w, cos, sin)


def _layer_norm_rows(z, g, b):
    mean = jnp.mean(z, axis=-1, keepdims=True)
    zc = z - mean
    var = jnp.mean(zc * zc, axis=-1, keepdims=True)
    return zc * lax.rsqrt(var + LN_EPS) * g + b


def _proj_ln_kernel(*refs, n_act, nk):
    acts = refs[:n_act]
    ws = refs[n_act:2 * n_act]
    x_ref, g_ref, b_ref, y_ref, yb_ref, yp_ref = refs[2 * n_act:2 * n_act + 6]
    k = pl.program_id(1)
    tm = x_ref.shape[0]
    n_split = 2
    hm = tm // n_split

    def product(rows):
        part = _dot(acts[0][rows, :], ws[0][...])
        for a, w in zip(acts[1:], ws[1:]):
            part = part + _dot(a[rows, :], w[...])
        return part

    def finish(rows, h, proj):
        z = DN_ALPHA * x_ref[rows, :] + proj
        y = _layer_norm_rows(z, g_ref[...], b_ref[...])
        y_ref[rows, :] = y
        yb_ref[rows, :] = y.astype(BF16)
        _store_row_tiles(yp_ref.at[pl.ds(h * hm * SUBLANES, hm * SUBLANES)], _pack_halves(y))

    if nk == 1:
        for h in range(n_split):
            rows = pl.ds(h * hm, hm)
            finish(rows, h, product(rows))
        return
    acc_ref = refs[2 * n_act + 6]

    @pl.when(k == 0)
    def _():
        acc_ref[...] = product(pl.ds(0, tm))

    @pl.when((k > 0) & (k < nk - 1))
    def _():
        acc_ref[...] = acc_ref[...] + product(pl.ds(0, tm))

    @pl.when(k == nk - 1)
    def _():
        for h in range(n_split):
            rows = pl.ds(h * hm, hm)
            finish(rows, h, acc_ref[rows, :] + product(rows))


def _proj_ln(acts, ws, x, g, b, tm, nk):
    m, d = x.shape
    n_act = len(acts)
    in_specs = []
    for a in acts:
        kk = a.shape[1] // nk
        in_specs.append(pl.BlockSpec((tm, kk), lambda i, k: (i, k)))
    for w in ws:
        kk = w.shape[0] // nk
        in_specs.append(pl.BlockSpec((kk, d), lambda i, k: (k, 0)))
    in_specs += [pl.BlockSpec((tm, d), lambda i, k: (i, 0)),
                 pl.BlockSpec((1, d), lambda i, k: (0, 0)),
                 pl.BlockSpec((1, d), lambda i, k: (0, 0))]
    return pl.pallas_call(
        functools.partial(_proj_ln_kernel, n_act=n_act, nk=nk),
        grid=(m // tm, nk),
        in_specs=in_specs,
        out_specs=[pl.BlockSpec((tm, d), lambda i, k: (i, 0)),
                   pl.BlockSpec((tm, d), lambda i, k: (i, 0)),
                   pl.BlockSpec((tm * SUBLANES, LANES), lambda i, k: (i, 0))],
        out_shape=[jax.ShapeDtypeStruct((m, d), F32), jax.ShapeDtypeStruct((m, d), BF16),
                   jax.ShapeDtypeStruct((m * SUBLANES, LANES), U32)],
        scratch_shapes=[pltpu.VMEM((tm, d), F32)] if nk > 1 else [],
        compiler_params=_cparams(("parallel", "arbitrary")),
        name="proj_ln",
    )(*acts, *ws, x, g.reshape(1, d), b.reshape(1, d))


def _na_bias_tables(rpb):
    nh = rpb.shape[0]
    c = np.arange(GRID_W)
    cs = np.clip(c - NA_WIN_W // 2, 0, GRID_W - NA_WIN_W)
    kc = np.arange(GRID_W)
    valid = (kc[None, :] >= cs[:, None]) & (kc[None, :] < cs[:, None] + NA_WIN_W)
    dc = kc[None, :] - c[:, None] + NA_WIN_W - 1
    onehot = (dc[:, :, None] == np.arange(2 * NA_WIN_W - 1)[None, None, :]) & valid[:, :, None]
    cols = jnp.einsum("hrd,ckd->hrck", rpb.astype(F32), jnp.asarray(onehot, F32),
                      precision=lax.Precision.HIGHEST)
    cols = jnp.where(jnp.asarray(valid)[None, None], cols, MASK_VALUE)
    tabs = jnp.stack([cols[:, off:off + NA_WIN_H] for off in range(NA_WIN_H)], axis=1)
    return tabs.transpose(0, 1, 3, 2, 4).reshape(nh, NA_WIN_H, GRID_W, NA_WIN_H * GRID_W)


def _na_kernel(q_ref, k_ref, v_ref, bias_ref, o_ref, *, rows, group):
    scale = NA_HEAD_DIM ** -0.5
    nkeys = NA_WIN_H * GRID_W

    def body(i, carry):
        geom, scores = [], []
        for u in range(group):
            r = i * group + u
            rs = jnp.clip(r - NA_WIN_H // 2, 0, rows - NA_WIN_H)
            off = rs - r + NA_WIN_H - 1
            q0 = pl.multiple_of(r * GRID_W, GRID_W)
            k0 = pl.multiple_of(rs * GRID_W, GRID_W)
            geom.append((q0, k0))
            s = _dot_nt(q_ref[pl.ds(q0, GRID_W), :], k_ref[pl.ds(k0, nkeys), :])
            scores.append(s * scale + bias_ref[0, off])
        for (q0, k0), s in zip(geom, scores):
            m = jnp.max(s, axis=-1, keepdims=True)
            p = jnp.exp(s - m)
            l = jnp.sum(p, axis=-1, keepdims=True)
            o = _dot(p.astype(BF16), v_ref[pl.ds(k0, nkeys), :]) / l
            o_ref[pl.ds(q0, GRID_W), :] = o.astype(o_ref.dtype)
        return carry

    lax.fori_loop(0, rows // group, body, 0)


def _na_attention(h, bias_tables, batch, seq):
    rows = seq // GRID_W
    d = NA_HEAD_DIM
    nkeys = NA_WIN_H * GRID_W
    return pl.pallas_call(
        functools.partial(_na_kernel, rows=rows, group=8),
        grid=(batch, NA_HEADS),
        in_specs=[pl.BlockSpec((seq, d), lambda b, hh: (b, hh)),
                  pl.BlockSpec((seq, d), lambda b, hh: (b, NA_HEADS + hh)),
                  pl.BlockSpec((seq, d), lambda b, hh: (b, 2 * NA_HEADS + hh)),
                  pl.BlockSpec((1, NA_WIN_H, GRID_W, nkeys), lambda b, hh: (hh, 0, 0, 0))],
        out_specs=pl.BlockSpec((seq, d), lambda b, hh: (b, hh)),
        out_shape=jax.ShapeDtypeStruct((batch * seq, NA_WIDTH), BF16),
        compiler_params=_cparams(("parallel", "arbitrary")),
        name="na_attention",
    )(h, h, h, bias_tables)


def _rms_rows(x, g):
    return x * lax.rsqrt(jnp.mean(x * x, axis=-1, keepdims=True) + RMS_EPS) * g


def _rope_lanes(t, cosf, sinf):
    return t * cosf + pltpu.roll(t, LANES // 2, 1) * sinf


def _mla_prep_kernel(cq_ref, ckv_ref, kr_ref, gq_ref, gkv_ref, wq_ref, wk_ref, wvt_ref, cos_ref, sin_ref,
                     q_ref, k_ref, vt_ref):
    dq = MLA_NOPE_DIM + MLA_ROPE_DIM
    cosf = cos_ref[...]
    sinf = sin_ref[...]
    cqn = _rms_rows(cq_ref[...].astype(F32), gq_ref[...]).astype(BF16)
    ckvn = _rms_rows(ckv_ref[...].astype(F32), gkv_ref[...]).astype(BF16)
    qf = _dot(cqn, wq_ref[...]) * (dq ** -0.5 * math.log2(math.e))
    kf = _dot(ckvn, wk_ref[...])
    vt_ref[...] = _dot_nt(wvt_ref[...], ckvn).astype(BF16)
    kpe = _rope_lanes(kr_ref[...].astype(F32), cosf, sinf).astype(BF16)
    for hh in range(MLA_HEADS):
        c0 = hh * 2 * LANES
        q_ref[:, c0:c0 + LANES] = qf[:, c0:c0 + LANES].astype(BF16)
        q_ref[:, c0 + LANES:c0 + 2 * LANES] = _rope_lanes(qf[:, c0 + LANES:c0 + 2 * LANES], cosf, sinf).astype(BF16)
        k_ref[:, c0:c0 + LANES] = kf[:, hh * LANES:(hh + 1) * LANES].astype(BF16)
        k_ref[:, c0 + LANES:c0 + 2 * LANES] = kpe


def _mla_prep(h, gq, gkv, wq_p, wk, wvt, cosf, sinf, col_cq, seq, tm):
    n = h.shape[0]
    hw = MLA_HEADS * 2 * LANES
    nsb = seq // tm
    b_cq = col_cq // MLA_Q_RANK
    b_ckv = (col_cq + MLA_Q_RANK) // MLA_KV_RANK
    b_kr = (col_cq + MLA_Q_RANK + MLA_KV_RANK) // LANES
    return pl.pallas_call(
        _mla_prep_kernel,
        grid=(n // tm,),
        in_specs=[pl.BlockSpec((tm, MLA_Q_RANK), lambda i: (i, b_cq)),
                  pl.BlockSpec((tm, MLA_KV_RANK), lambda i: (i, b_ckv)),
                  pl.BlockSpec((tm, LANES), lambda i: (i, b_kr)),
                  pl.BlockSpec((1, MLA_Q_RANK), lambda i: (0, 0)),
                  pl.BlockSpec((1, MLA_KV_RANK), lambda i: (0, 0)),
                  pl.BlockSpec((MLA_Q_RANK, hw), lambda i: (0, 0)),
                  pl.BlockSpec((MLA_KV_RANK, MLA_HEADS * LANES), lambda i: (0, 0)),
                  pl.BlockSpec((MLA_HEADS * MLA_V_DIM, MLA_KV_RANK), lambda i: (0, 0)),
                  pl.BlockSpec((tm, LANES), lambda i: (i % nsb, 0)),
                  pl.BlockSpec((tm, LANES), lambda i: (i % nsb, 0))],
        out_specs=[pl.BlockSpec((tm, hw), lambda i: (i, 0)),
                   pl.BlockSpec((tm, hw), lambda i: (i, 0)),
                   pl.BlockSpec((MLA_HEADS * MLA_V_DIM, tm), lambda i: (0, i))],
        out_shape=[jax.ShapeDtypeStruct((n, hw), BF16), jax.ShapeDtypeStruct((n, hw), BF16),
                   jax.ShapeDtypeStruct((MLA_HEADS * MLA_V_DIM, n), BF16)],
        compiler_params=_cparams(("parallel",)),
        name="mla_prep",
    )(h, h, h, gq.reshape(1, -1), gkv.reshape(1, -1), wq_p, wk, wvt, cosf, sinf)


def _mla_attn_kernel(q_ref, k_ref, vt_ref, o_ref, *, tk, sub):
    nchunk = k_ref.shape[0] // tk
    tq = q_ref.shape[0]
    nsub = tq // sub
    qs = [q_ref[s * sub:(s + 1) * sub, :] for s in range(nsub)]
    m = [jnp.full((1, sub), MASK_VALUE, F32) for _ in range(nsub)]
    l = [jnp.zeros((1, sub), F32) for _ in range(nsub)]
    acc = [jnp.zeros((MLA_V_DIM, sub), F32) for _ in range(nsub)]

    def scores(s, c):
        return _dot_nt(k_ref[c * tk:(c + 1) * tk, :], qs[s])

    st_next = [scores(s, 0) for s in range(nsub)]
    for c in range(nchunk):
        for s in range(nsub):
            st = st_next[s]
            m_new = jnp.maximum(m[s], jnp.max(st, axis=0, keepdims=True))
            a = jnp.exp2(m[s] - m_new)
            p = jnp.exp2(st - m_new)
            l[s] = a * l[s] + jnp.sum(p, axis=0, keepdims=True)
            if c + 1 < nchunk:
                st_next[s] = scores(s, c + 1)
            acc[s] = a * acc[s] + _dot(vt_ref[:, c * tk:(c + 1) * tk], p.astype(BF16))
            m[s] = m_new
    for s in range(nsub):
        o_ref[s * sub:(s + 1) * sub, :] = (acc[s] / l[s]).T.astype(o_ref.dtype)


def _mla_attention(q_p, k_p, vt, batch, seq, tq, tk, sub):
    n = q_p.shape[0]
    nqb = seq // tq
    return pl.pallas_call(
        functools.partial(_mla_attn_kernel, tk=tk, sub=sub),
        grid=(batch, MLA_HEADS, nqb),
        in_specs=[pl.BlockSpec((tq, 2 * LANES), lambda b, hh, i: (b * nqb + i, hh)),
                  pl.BlockSpec((seq, 2 * LANES), lambda b, hh, i: (b, hh)),
                  pl.BlockSpec((MLA_V_DIM, seq), lambda b, hh, i: (hh, b))],
        out_specs=pl.BlockSpec((tq, MLA_V_DIM), lambda b, hh, i: (b * nqb + i, hh)),
        out_shape=jax.ShapeDtypeStruct((n, MLA_HEADS * MLA_V_DIM), BF16),
        compiler_params=_cparams(("parallel", "parallel", "arbitrary")),
        name="mla_attention",
    )(q_p, k_p, vt)


def _ret_kernel(lg_ref, q_ref, k_ref, v_ref, g_ref, o_ref, acc_ref, st_ref, *, c_len, group):
    nchunk = q_ref.shape[0] // c_len
    hh = pl.program_id(1)
    lgf = lg_ref[0, hh]
    lgb = lg_ref[1, hh]
    ii = lax.broadcasted_iota(I32, (c_len, c_len), 0).astype(F32)
    jj = lax.broadcasted_iota(I32, (c_len, c_len), 1).astype(F32)
    rel = ii - jj
    dmat = jnp.where(rel >= 0, jnp.exp(lgf * jnp.maximum(rel, 0.0)), jnp.exp(lgb * jnp.maximum(-rel, 0.0)))
    pos = lax.broadcasted_iota(I32, (c_len, 1), 0).astype(F32)
    qdec_f = jnp.exp(lgf * (pos + 1.0))
    kdec_f = jnp.exp(lgf * (c_len - 1.0 - pos))
    qdec_b = jnp.exp(lgb * (c_len - pos))
    kdec_b = jnp.exp(lgb * pos)
    full_chunk = jnp.full((1, RET_V_DIM), float(c_len), F32)
    cdec_f = jnp.exp(lgf * full_chunk)
    cdec_b = jnp.exp(lgb * full_chunk)

    def decayed_keys_t(t0, kdec):
        return (k_ref[pl.ds(t0, c_len), :].astype(F32) * kdec).T.astype(BF16)

    st_ref[...] = jnp.zeros_like(st_ref)

    def bwd_body(i, carry):
        t0s = [pl.multiple_of((nchunk - 1 - (i * group + u)) * c_len, c_len) for u in range(group)]
        upd = [_dot(decayed_keys_t(t0, kdec_b), v_ref[pl.ds(t0, c_len), :]) for t0 in t0s]
        for t0, u_c in zip(t0s, upd):
            st = st_ref[...]
            acc_ref[pl.ds(t0, c_len), :] = _dot(q_ref[pl.ds(t0, c_len), :], st.astype(BF16)) * qdec_b
            st_ref[...] = st * cdec_b + u_c
        return carry

    lax.fori_loop(0, nchunk // group, bwd_body, 0)
    st_ref[...] = jnp.zeros_like(st_ref)

    def fwd_body(i, carry):
        t0s = [pl.multiple_of((i * group + u) * c_len, c_len) for u in range(group)]
        scs = [_dot_nt(q_ref[pl.ds(t0, c_len), :], k_ref[pl.ds(t0, c_len), :]) * dmat for t0 in t0s]
        upd = [_dot(decayed_keys_t(t0, kdec_f), v_ref[pl.ds(t0, c_len), :]) for t0 in t0s]
        for t0, sc, u_c in zip(t0s, scs, upd):
            st = st_ref[...]
            r = (_dot(sc.astype(BF16), v_ref[pl.ds(t0, c_len), :])
                 + _dot(q_ref[pl.ds(t0, c_len), :], st.astype(BF16)) * qdec_f
                 + acc_ref[pl.ds(t0, c_len), :])
            st_ref[...] = st * cdec_f + u_c
            r = r - jnp.mean(r, axis=-1, keepdims=True)
            r = r * lax.rsqrt(jnp.mean(r * r, axis=-1, keepdims=True) + LN_EPS)
            o_ref[pl.ds(t0, c_len), :] = (g_ref[pl.ds(t0, c_len), :].astype(F32) * r).astype(o_ref.dtype)
        return carry

    lax.fori_loop(0, nchunk // group, fwd_body, 0)


def _retention(hc, lg, batch, seq, c_len, group):
    dk, dv, nh = RET_QK_DIM, RET_V_DIM, RET_HEADS
    v_blk0 = (2 * nh * dk) // dv
    return pl.pallas_call(
        functools.partial(_ret_kernel, c_len=c_len, group=group),
        grid=(batch, nh),
        in_specs=[pl.BlockSpec(memory_space=pltpu.SMEM),
                  pl.BlockSpec((seq, dk), lambda b, hh: (b, hh)),
                  pl.BlockSpec((seq, dk), lambda b, hh: (b, nh + hh)),
                  pl.BlockSpec((seq, dv), lambda b, hh: (b, v_blk0 + hh)),
                  pl.BlockSpec((seq, dv), lambda b, hh: (b, v_blk0 + nh + hh))],
        out_specs=pl.BlockSpec((seq, dv), lambda b, hh: (b, hh)),
        scratch_shapes=[pltpu.VMEM((seq, dv), F32), pltpu.VMEM((dk, dv), F32)],
        out_shape=jax.ShapeDtypeStruct((batch * seq, nh * dv), BF16),
        compiler_params=_cparams(("parallel", "arbitrary")),
        name="retention",
    )(lg, hc, hc, hc, hc)


ROUTER_ROWS = 40


def _router_kernel(x_ref, wt_ref, b_ref, tri_ref, ids_ref, wts_ref, cnt_ref, carry_ref):
    i = pl.program_id(0)
    tm = x_ref.shape[0]

    @pl.when(i == 0)
    def _():
        carry_ref[...] = jnp.zeros_like(carry_ref)

    logits = _dot_nt(wt_ref[...], x_ref[...]) + b_ref[...]
    grow = lax.broadcasted_iota(I32, (SUBLANES, tm), 0).astype(F32)
    gl = jnp.where(grow < N_GROUPS, logits[0:SUBLANES], MASK_VALUE)
    gmax = jnp.max(gl, axis=0, keepdims=True)
    gsum = jnp.sum(jnp.exp(gl - gmax), axis=0, keepdims=True)
    p_group = 1.0 / gsum
    g_idx = jnp.min(jnp.where(gl == gmax, grow, float(N_GROUPS)), axis=0, keepdims=True)
    sel = jnp.zeros((EXPERTS_PER_GROUP, tm), F32)
    for g in range(N_GROUPS):
        r0 = SUBLANES + g * EXPERTS_PER_GROUP
        sel = sel + jnp.where(g_idx == float(g), logits[r0:r0 + EXPERTS_PER_GROUP], 0.0)
    erow = lax.broadcasted_iota(I32, (EXPERTS_PER_GROUP, tm), 0).astype(F32)
    smax = jnp.max(sel, axis=0, keepdims=True)
    sexp = jnp.exp(sel - smax)
    probs = sexp / jnp.sum(sexp, axis=0, keepdims=True)
    p1 = jnp.max(probs, axis=0, keepdims=True)
    i1 = jnp.min(jnp.where(probs == p1, erow, float(EXPERTS_PER_GROUP)), axis=0, keepdims=True)
    rest = jnp.where(erow == i1, -1.0, probs)
    p2 = jnp.max(rest, axis=0, keepdims=True)
    i2 = jnp.min(jnp.where(rest == p2, erow, float(EXPERTS_PER_GROUP)), axis=0, keepdims=True)
    denom = p1 + p2
    e0 = g_idx * EXPERTS_PER_GROUP + i1
    e1 = g_idx * EXPERTS_PER_GROUP + i2

    xrow = lax.broadcasted_iota(I32, (N_EXPERTS, tm), 0).astype(F32)
    oh0 = jnp.where(xrow == e0, 1.0, 0.0)
    oh1 = jnp.where(xrow == e1, 1.0, 0.0)
    onehot = oh0 + oh1
    before = _dot(onehot.astype(BF16), tri_ref[...]) + carry_ref[:, 0:1]
    rank0 = jnp.sum(oh0 * before, axis=0, keepdims=True)
    rank1 = jnp.sum(oh1 * before, axis=0, keepdims=True)
    carry_ref[...] = carry_ref[...] + jnp.sum(onehot, axis=1, keepdims=True)

    ids_ref[...] = jnp.zeros_like(ids_ref)
    ids_ref[0:1, :] = e0.astype(I32)
    ids_ref[1:2, :] = e1.astype(I32)
    ids_ref[2:3, :] = rank0.astype(I32)
    ids_ref[3:4, :] = rank1.astype(I32)
    wts_ref[...] = jnp.zeros_like(wts_ref)
    wts_ref[0:1, :] = p_group * p1 / denom
    wts_ref[1:2, :] = p_group * p2 / denom
    cnt_ref[...] = carry_ref[...]


def _router(x, wt, bias, tm):
    n, d = x.shape
    tri = jnp.asarray(np.triu(np.ones((tm, tm), np.float32), 1), BF16)
    return pl.pallas_call(
        _router_kernel,
        grid=(n // tm,),
        in_specs=[pl.BlockSpec((tm, d), lambda i: (i, 0)),
                  pl.BlockSpec((ROUTER_ROWS, d), lambda i: (0, 0)),
                  pl.BlockSpec((ROUTER_ROWS, 1), lambda i: (0, 0)),
                  pl.BlockSpec((tm, tm), lambda i: (0, 0))],
        out_specs=[pl.BlockSpec((SUBLANES, tm), lambda i: (0, i)),
                   pl.BlockSpec((SUBLANES, tm), lambda i: (0, i)),
                   pl.BlockSpec((N_EXPERTS, LANES), lambda i: (0, 0))],
        out_shape=[jax.ShapeDtypeStruct((SUBLANES, n), I32), jax.ShapeDtypeStruct((SUBLANES, n), F32),
                   jax.ShapeDtypeStruct((N_EXPERTS, LANES), F32)],
        scratch_shapes=[pltpu.VMEM((N_EXPERTS, LANES), F32)],
        compiler_params=_cparams(("arbitrary",)),
        name="moe_router",
    )(x, wt, bias, tri)


def _slots_kernel(ids_ref, cnt_ref, slots_ref, blk_ref, *, nblk_pad):
    tm = ids_ref.shape[1]
    cnt = cnt_ref[:, 0:1]
    padded = jnp.floor((cnt + (MOE_BLOCK - 1)) / MOE_BLOCK) * MOE_BLOCK
    er = lax.broadcasted_iota(I32, (N_EXPERTS, N_EXPERTS), 0)
    ec = lax.broadcasted_iota(I32, (N_EXPERTS, N_EXPERTS), 1)
    padded_row = jnp.sum(jnp.where(er == ec, padded, 0.0), axis=0, keepdims=True)
    p_start = jnp.sum(jnp.where(ec < er, padded_row, 0.0), axis=1, keepdims=True)
    p_end = p_start + padded
    xrow = lax.broadcasted_iota(I32, (N_EXPERTS, tm), 0)
    e0 = ids_ref[0:1, :]
    e1 = ids_ref[1:2, :]
    s0 = jnp.sum(jnp.where(xrow == e0, p_start, 0.0), axis=0, keepdims=True).astype(I32) + ids_ref[2:3, :]
    s1 = jnp.sum(jnp.where(xrow == e1, p_start, 0.0), axis=0, keepdims=True).astype(I32) + ids_ref[3:4, :]
    slots_ref[...] = jnp.zeros_like(slots_ref)
    slots_ref[0:1, :] = s0
    slots_ref[1:2, :] = s1
    bstart = (lax.broadcasted_iota(I32, (1, nblk_pad), 1) * MOE_BLOCK).astype(F32)
    blk_e = jnp.minimum(jnp.sum(jnp.where(p_end <= bstart, 1.0, 0.0), axis=0, keepdims=True), N_EXPERTS - 1.0)
    total = jnp.sum(padded, axis=0, keepdims=True)
    erow = lax.broadcasted_iota(I32, (N_EXPERTS, nblk_pad), 0).astype(F32)
    own_end = jnp.sum(jnp.where(erow == blk_e, p_end, 0.0), axis=0, keepdims=True)
    nxt_e = jnp.minimum(jnp.sum(jnp.where(p_end <= own_end, 1.0, 0.0), axis=0, keepdims=True), N_EXPERTS - 1.0)
    nxt_e = jnp.where(own_end < total, nxt_e, -1.0)
    blk_ref[...] = jnp.zeros_like(blk_ref)
    blk_ref[0:1, :] = blk_e.astype(I32)
    blk_ref[1:2, :] = jnp.broadcast_to((total / MOE_BLOCK).astype(I32), (1, nblk_pad))
    blk_ref[2:3, :] = nxt_e.astype(I32)
    lane = lax.broadcasted_iota(I32, (N_EXPERTS, nblk_pad), 1).astype(F32)
    blk_ref[3:4, :] = jnp.sum(jnp.where(erow == lane, p_start + cnt, 0.0), axis=0, keepdims=True).astype(I32)
    blk_ref[4:5, :] = jnp.sum(jnp.where(erow == lane, p_end, 0.0), axis=0, keepdims=True).astype(I32)


def _slots(ids, cnt, tm, nblk_pad):
    n = ids.shape[1]
    return pl.pallas_call(
        functools.partial(_slots_kernel, nblk_pad=nblk_pad),
        grid=(n // tm,),
        in_specs=[pl.BlockSpec((SUBLANES, tm), lambda i: (0, i)),
                  pl.BlockSpec((N_EXPERTS, LANES), lambda i: (0, 0))],
        out_specs=[pl.BlockSpec((SUBLANES, tm), lambda i: (0, i)),
                   pl.BlockSpec((SUBLANES, nblk_pad), lambda i: (0, 0))],
        out_shape=[jax.ShapeDtypeStruct((SUBLANES, n), I32), jax.ShapeDtypeStruct((SUBLANES, nblk_pad), I32)],
        compiler_params=_cparams(("arbitrary",)),
        name="moe_slots",
    )(ids, cnt)


def _slot_tokens_kernel(slots_ref, blk_ref, tok_ref, *, n, cap, nblk_pad):
    def zero(j, carry):
        tok_ref[j] = 0
        return carry

    def scatter(t, carry):
        tok_ref[slots_ref[t]] = t
        tok_ref[slots_ref[n + t]] = t
        return carry

    for e in range(N_EXPERTS):
        lax.fori_loop(blk_ref[3 * nblk_pad + e], blk_ref[4 * nblk_pad + e], zero, 0)
    lax.fori_loop(blk_ref[nblk_pad] * MOE_BLOCK, cap, zero, 0)
    lax.fori_loop(0, n, scatter, 0, unroll=8)


def _slot_tokens(slots_flat, blk_flat, n, cap, nblk_pad):
    return pl.pallas_call(
        functools.partial(_slot_tokens_kernel, n=n, cap=cap, nblk_pad=nblk_pad),
        grid_spec=pltpu.PrefetchScalarGridSpec(
            num_scalar_prefetch=2,
            grid=(1,),
            in_specs=[],
            out_specs=pl.BlockSpec(memory_space=pltpu.SMEM)),
        out_shape=jax.ShapeDtypeStruct((cap,), I32),
        compiler_params=_cparams(("arbitrary",)),
        name="moe_slot_tokens",
    )(slots_flat, blk_flat)


ROW_BUFS = 3


def _expert_kernel(blk_ref, tok_ref, xpk_hbm, wg_hbm, wu_hbm, wd_hbm, y_ref, xbuf, wgf, wuf, wdf, wgb, wub, wdb,
                   sems, wsems, cnt_ref, *, nblk_pad, layer):
    i = pl.program_id(0)
    n_used = blk_ref[nblk_pad]
    cast_rows = 256

    def weight_copies(e, slot):
        return [pltpu.make_async_copy(src.at[layer, e], dst.at[slot], wsems.at[slot])
                for src, dst in ((wg_hbm, wgf), (wu_hbm, wuf), (wd_hbm, wdf))]

    def start_rows(block, buf, r_lo=0, r_hi=MOE_BLOCK):
        for r in range(r_lo, r_hi):
            src0 = pl.multiple_of(tok_ref[block * MOE_BLOCK + r] * SUBLANES, SUBLANES)
            pltpu.make_async_copy(xpk_hbm.at[pl.ds(src0, SUBLANES)],
                                  xbuf.at[buf, pl.ds(r * SUBLANES, SUBLANES)], sems.at[buf]).start(priority=r % 2)

    def wait_rows(buf):
        pltpu.make_async_copy(xpk_hbm.at[pl.ds(0, MOE_BLOCK * SUBLANES)], xbuf.at[buf], sems.at[buf]).wait()

    @pl.when(i == 0)
    def _():
        cnt_ref[0] = 0
        for cp in weight_copies(blk_ref[0], 0):
            cp.start(priority=1)
        for ahead in range(ROW_BUFS - 1):
            start_rows(jnp.minimum(ahead, n_used - 1), ahead)

    @pl.when((i < n_used) & ((i == 0) | (blk_ref[i] != blk_ref[jnp.maximum(i - 1, 0)])))
    def _():
        slot = cnt_ref[0] % 2
        cnt_ref[0] = cnt_ref[0] + 1
        for cp in weight_copies(blk_ref[i], slot):
            cp.wait()
        nxt_e = blk_ref[2 * nblk_pad + i]

        @pl.when(nxt_e >= 0)
        def _():
            for cp in weight_copies(nxt_e, 1 - slot):
                cp.start(priority=1)

        for src, dst in ((wgf, wgb), (wuf, wub), (wdf, wdb)):
            for r0 in range(0, dst.shape[0], cast_rows):
                dst[r0:r0 + cast_rows, :] = src[slot, r0:r0 + cast_rows, :].astype(BF16)

    @pl.when(i < n_used)
    def _():
        buf = i % ROW_BUFS
        nbuf = (i + ROW_BUFS - 1) % ROW_BUFS
        nxt = jnp.minimum(i + ROW_BUFS - 1, n_used - 1)
        wait_rows(buf)
        lo, hi = _unpack_halves(_load_row_tiles(xbuf.at[buf], MOE_BLOCK))
        xb = jnp.concatenate([lo.astype(BF16), hi.astype(BF16)], axis=-1)
        g = _dot(xb, wgb[...])
        start_rows(nxt, nbuf, 0, MOE_BLOCK // 2)
        u = _dot(xb, wub[...])
        start_rows(nxt, nbuf, MOE_BLOCK // 2, MOE_BLOCK)
        hmid = (g * jax.nn.sigmoid(g) * u).astype(BF16)
        _store_row_tiles(y_ref, _pack_halves(_dot(hmid, wdb[...])))

    @pl.when(i == n_used - 1)
    def _():
        for ahead in range(1, ROW_BUFS):
            wait_rows((i + ahead) % ROW_BUFS)

    @pl.when(i >= n_used)
    def _():
        y_ref[...] = jnp.zeros_like(y_ref)


def _experts(blk_flat, slot_tok, xpk, wg, wu, wd, layer, nblk_pad):
    d, de = wg.shape[2], wg.shape[3]
    assert d == 2 * SUBLANES * LANES and xpk.shape[1] == LANES
    cap = slot_tok.shape[0]
    tile_rows = MOE_BLOCK * SUBLANES
    return pl.pallas_call(
        functools.partial(_expert_kernel, nblk_pad=nblk_pad, layer=layer),
        grid_spec=pltpu.PrefetchScalarGridSpec(
            num_scalar_prefetch=2,
            grid=(cap // MOE_BLOCK,),
            in_specs=[pl.BlockSpec(memory_space=pl.ANY), pl.BlockSpec(memory_space=pl.ANY),
                      pl.BlockSpec(memory_space=pl.ANY), pl.BlockSpec(memory_space=pl.ANY)],
            out_specs=pl.BlockSpec((tile_rows, LANES), lambda i, blk, tok: (i, 0)),
            scratch_shapes=[pltpu.VMEM((ROW_BUFS, tile_rows, LANES), U32),
                            pltpu.VMEM((2, d, de), F32), pltpu.VMEM((2, d, de), F32), pltpu.VMEM((2, de, d), F32),
                            pltpu.VMEM((d, de), BF16), pltpu.VMEM((d, de), BF16), pltpu.VMEM((de, d), BF16),
                            pltpu.SemaphoreType.DMA((ROW_BUFS,)), pltpu.SemaphoreType.DMA((2,)),
                            pltpu.SMEM((1,), I32)]),
        out_shape=jax.ShapeDtypeStruct((cap * SUBLANES, LANES), U32),
        compiler_params=_cparams(("arbitrary",)),
        name="moe_experts",
    )(blk_flat, slot_tok, xpk, wg, wu, wd)


def _tail_kernel(slots_ref, x_ref, xb_ref, p_ref, wts_ref, wgate_ref, bgate_ref, wproj_ref, g_ref, b_ref,
                 yb_hbm, y_ref, ybf_ref, rows_ref, sems):
    i = pl.program_id(0)
    nsteps = pl.num_programs(0)
    tm = x_ref.shape[0]
    n = nsteps * tm

    def start_rows(step, buf):
        for t in range(tm):
            for which in range(2):
                src0 = pl.multiple_of(slots_ref[which * n + step * tm + t] * SUBLANES, SUBLANES)
                pltpu.make_async_copy(yb_hbm.at[pl.ds(src0, SUBLANES)],
                                      rows_ref.at[buf, which, pl.ds(t * SUBLANES, SUBLANES)], sems.at[buf]).start()

    def wait_rows(buf):
        for which in range(2):
            pltpu.make_async_copy(yb_hbm.at[pl.ds(0, tm * SUBLANES)], rows_ref.at[buf, which],
                                  sems.at[buf]).wait()

    @pl.when(i == 0)
    def _():
        start_rows(0, 0)

    buf = i % 2
    nxt = jnp.minimum(i + 1, nsteps - 1)
    wait_rows(buf)
    gate_pre = _dot(xb_ref[...], wgate_ref[...])
    proj = _dot(p_ref[0].astype(BF16), wproj_ref[...])
    start_rows(nxt, 1 - buf)
    ple = jax.nn.sigmoid(gate_pre + bgate_ref[...]) * proj
    w = wts_ref[...]
    lo0, hi0 = _unpack_halves(_load_row_tiles(rows_ref.at[buf, 0], tm))
    lo1, hi1 = _unpack_halves(_load_row_tiles(rows_ref.at[buf, 1], tm))
    w0 = w[:, 0:1]
    w1 = w[:, 1:2]
    ffn = jnp.concatenate([lo0 * w0 + lo1 * w1, hi0 * w0 + hi1 * w1], axis=-1)
    z = DN_ALPHA * x_ref[...] + ffn + ple
    y = _layer_norm_rows(z, g_ref[...], b_ref[...])
    y_ref[...] = y
    ybf_ref[...] = y.astype(BF16)

    @pl.when(i == nsteps - 1)
    def _():
        wait_rows(1 - buf)


def _layer_tail(slots_flat, x, xb, p, layer, wts_t, wgate, bgate, wproj, g, b, yb, tm):
    n, d = x.shape
    pd = p.shape[2]
    return pl.pallas_call(
        _tail_kernel,
        grid_spec=pltpu.PrefetchScalarGridSpec(
            num_scalar_prefetch=1,
            grid=(n // tm,),
            in_specs=[pl.BlockSpec((tm, d), lambda i, s: (i, 0)),
                      pl.BlockSpec((tm, d), lambda i, s: (i, 0)),
                      pl.BlockSpec((1, tm, pd), lambda i, s: (layer, i, 0)),
                      pl.BlockSpec((tm, 2), lambda i, s: (i, 0)),
                      pl.BlockSpec((d, d), lambda i, s: (0, 0)),
                      pl.BlockSpec((1, d), lambda i, s: (0, 0)),
                      pl.BlockSpec((pd, d), lambda i, s: (0, 0)),
                      pl.BlockSpec((1, d), lambda i, s: (0, 0)),
                      pl.BlockSpec((1, d), lambda i, s: (0, 0)),
                      pl.BlockSpec(memory_space=pl.ANY)],
            out_specs=[pl.BlockSpec((tm, d), lambda i, s: (i, 0)),
                       pl.BlockSpec((tm, d), lambda i, s: (i, 0))],
            scratch_shapes=[pltpu.VMEM((2, 2, tm * SUBLANES, LANES), U32), pltpu.SemaphoreType.DMA((2,))]),
        out_shape=[jax.ShapeDtypeStruct((n, d), F32), jax.ShapeDtypeStruct((n, d), BF16)],
        compiler_params=_cparams(("arbitrary",)),
        name="layer_tail",
    )(slots_flat, x, xb, p, wts_t, wgate, bgate.reshape(1, d), wproj, g.reshape(1, d), b.reshape(1, d), yb)


def _rope_table(seq, dim):
    pos = jnp.arange(seq, dtype=F32)
    inv = jnp.exp(jnp.arange(0, dim, 2, dtype=F32) * (-math.log(ROPE_BASE) / dim))
    ang = pos[:, None] * inv[None, :]
    return jnp.cos(ang), jnp.sin(ang)


def _moe_layer(x, xb, xpk, p, layer, w_group, b_group, w_router, b_router, w_gate, w_up, w_down,
               ple_w_proj, ple_w_gate, ple_b_gate, ln_g, ln_b):
    n, d = x.shape
    nblk = -(-(2 * n) // MOE_BLOCK) + N_EXPERTS
    nblk_pad = -(-nblk // LANES) * LANES
    cap = nblk * MOE_BLOCK
    wt = jnp.zeros((ROUTER_ROWS, d), F32)
    wt = wt.at[0:N_GROUPS].set(w_group.T)
    wt = wt.at[SUBLANES:].set(w_router.transpose(0, 2, 1).reshape(N_EXPERTS, d))
    bias = jnp.zeros((ROUTER_ROWS, 1), F32)
    bias = bias.at[0:N_GROUPS, 0].set(b_group)
    bias = bias.at[SUBLANES:, 0].set(b_router.reshape(N_EXPERTS))
    ids, wts, cnt = _router(xb, wt.astype(BF16), bias, tm=512)
    slots, blk = _slots(ids, cnt, tm=min(2048, n), nblk_pad=nblk_pad)
    slots_flat = slots[0:2].reshape(2 * n)
    blk_flat = blk[0:5].reshape(5 * nblk_pad)
    slot_tok = _slot_tokens(slots_flat, blk_flat, n, cap, nblk_pad)
    yb = _experts(blk_flat, slot_tok, xpk, w_gate, w_up, w_down, layer, nblk_pad)
    return _layer_tail(slots_flat, x, xb, p, layer, wts[0:2].T, ple_w_gate.astype(BF16), ple_b_gate,
                       ple_w_proj.astype(BF16), ln_g, ln_b, yb, tm=256)


def _mixer_ab(x, xres, batch, seq, w_in, rpb, q_norm, w_uq, kv_norm, w_ukv, w_out, ln_g, ln_b):
    d = x.shape[1]
    o1 = 3 * NA_WIDTH
    o2 = o1 + MLA_Q_RANK
    o3 = o2 + MLA_KV_RANK
    half = MLA_ROPE_DIM // 2
    kr = w_in[:, o3:o3 + MLA_ROPE_DIM]
    kr_sw = jnp.concatenate([kr[:, half:], kr[:, :half]], axis=1)
    width = -(-(o3 + 2 * MLA_ROPE_DIM) // 1024) * 1024
    w_in_p = jnp.concatenate([w_in, kr_sw, jnp.zeros((d, width - o3 - 2 * MLA_ROPE_DIM), F32)], axis=1)
    h = _matmul(x, w_in_p.astype(BF16), BF16, tm=512, tn=1024)
    a_out = _na_attention(h, _na_bias_tables(rpb), batch, seq)
    dq = MLA_NOPE_DIM + MLA_ROPE_DIM
    wq = w_uq.reshape(MLA_Q_RANK, MLA_HEADS, dq)
    wq_pe = wq[:, :, MLA_NOPE_DIM:]
    wq_p = jnp.concatenate([wq, wq_pe[:, :, half:], wq_pe[:, :, :half]], axis=2)
    wq_p = wq_p.reshape(MLA_Q_RANK, MLA_HEADS * 2 * LANES).astype(BF16)
    wkv = w_ukv.reshape(MLA_KV_RANK, MLA_HEADS, MLA_NOPE_DIM + MLA_V_DIM)
    wk = wkv[:, :, :MLA_NOPE_DIM].reshape(MLA_KV_RANK, MLA_HEADS * MLA_NOPE_DIM).astype(BF16)
    wvt = wkv[:, :, MLA_NOPE_DIM:].reshape(MLA_KV_RANK, MLA_HEADS * MLA_V_DIM).T.astype(BF16)
    cos, sin = _rope_table(seq, MLA_ROPE_DIM)
    zpad = jnp.zeros((seq, LANES - MLA_ROPE_DIM), F32)
    cosf = jnp.concatenate([cos, cos, zpad], axis=1)
    sinf = jnp.concatenate([-sin, sin, zpad], axis=1)
    q_p, k_p, vt = _mla_prep(h, q_norm, kv_norm, wq_p, wk, wvt, cosf, sinf, o1, seq, tm=512)
    b_out = _mla_attention(q_p, k_p, vt, batch, seq, tq=1024, tk=1024, sub=256)
    w_out_b = w_out.astype(BF16)
    return _proj_ln([a_out, b_out], [w_out_b[:NA_WIDTH], w_out_b[NA_WIDTH:]], xres, ln_g, ln_b, tm=512, nk=1)


def _mixer_c(xb, xres, batch, seq, w_in, log_rate_f, log_rate_b, w_out, ln_g, ln_b):
    cosr, sinr = _rope_table(seq, RET_QK_DIM)
    n_q = RET_HEADS * RET_QK_DIM
    hc = _matmul_rope(xb, w_in.astype(BF16), cosr, sinr, BF16, tm=min(1024, seq), tn=1024, n_q_cols=n_q,
                      n_rope_cols=2 * n_q, gate_col0=2 * n_q + RET_HEADS * RET_V_DIM, head_w=RET_QK_DIM,
                      q_scale=RET_QK_DIM ** -0.5)
    lg = jnp.stack([jnp.log1p(-jnp.exp(log_rate_f.astype(F32))), jnp.log1p(-jnp.exp(log_rate_b.astype(F32)))])
    r = _retention(hc, lg, batch, seq, c_len=256, group=4)
    return _proj_ln([r], [w_out.astype(BF16)], xres, ln_g, ln_b, tm=512, nk=2)


def kernel(x, p, ab_w_in, ab_rpb, ab_q_norm, ab_w_uq, ab_kv_norm, ab_w_ukv, ab_w_out, c_w_in, c_log_rate_f,
           c_log_rate_b, c_w_out, ln1_g, ln1_b, moe_w_group, moe_b_group, moe_w_router, moe_b_router,
           moe_w_gate, moe_w_up, moe_w_down, ple_w_proj, ple_w_gate, ple_b_gate, ln2_g, ln2_b):
    batch, seq, d = x.shape
    n = batch * seq
    xf = x.reshape(n, d)
    p_flat = p.reshape(DEPTH, n, -1)
    xb = None
    for i in range(DEPTH):
        j = i // 2
        if i % 2 == 0:
            src = xf if xb is None else xb
            xf, xb, xpk = _mixer_ab(src, xf, batch, seq, ab_w_in[j], ab_rpb[j], ab_q_norm[j], ab_w_uq[j],
                                    ab_kv_norm[j], ab_w_ukv[j], ab_w_out[j], ln1_g[i], ln1_b[i])
        else:
            xf, xb, xpk = _mixer_c(xb, xf, batch, seq, c_w_in[j], c_log_rate_f[j], c_log_rate_b[j], c_w_out[j],
                                   ln1_g[i], ln1_b[i])
        xf, xb = _moe_layer(xf, xb, xpk, p_flat, i, moe_w_group[i], moe_b_group[i], moe_w_router[i],
                            moe_b_router[i], moe_w_gate, moe_w_up, moe_w_down, ple_w_proj[i],
                            ple_w_gate[i], ple_b_gate[i], ln2_g[i], ln2_b[i])
    return xf.reshape(batch, seq, d)
```

```python
import functools
import math

import numpy as np
import jax
import jax.numpy as jnp
from jax import lax
from jax.experimental import pallas as pl
from jax.experimental.pallas import tpu as pltpu

DEPTH = 2
GRID_W = 64
NA_HEADS = 8
NA_HEAD_DIM = 128
NA_WIN_H = 8
NA_WIN_W = 16
MLA_HEADS = 8
MLA_Q_RANK = 512
MLA_KV_RANK = 256
MLA_NOPE_DIM = 128
MLA_ROPE_DIM = 64
MLA_V_DIM = 128
RET_HEADS = 8
RET_QK_DIM = 256
RET_V_DIM = 512
RET_CHUNK = 128
N_GROUPS = 4
EXPERTS_PER_GROUP = 8
N_EXPERTS = N_GROUPS * EXPERTS_PER_GROUP
D_EXPERT = 512
MOE_BLOCK = 128
ROPE_BASE = 10000.0
LN_EPS = 1e-5
RMS_EPS = 1e-6
DN_ALPHA = (2 * DEPTH) ** 0.25
NA_WIDTH = NA_HEADS * NA_HEAD_DIM

LANES = 128
SUBLANES = 8
VMEM_LIMIT_BYTES = 60 * 1024 * 1024
MASK_VALUE = -1e30

F32 = jnp.float32
BF16 = jnp.bfloat16
I32 = jnp.int32
U32 = jnp.uint32


def _cparams(sem):
    return pltpu.CompilerParams(dimension_semantics=sem, vmem_limit_bytes=VMEM_LIMIT_BYTES)


def _dot(a, b):
    return jnp.dot(a, b, preferred_element_type=F32)


def _dot_nt(a, b, precision=None):
    return lax.dot_general(a, b, (((1,), (1,)), ((), ())), preferred_element_type=F32,
                           precision=precision)


def _pack_halves(y):
    c = y.shape[1] // 2
    bits = pltpu.bitcast(y.astype(BF16).astype(F32), U32)
    return (bits[:, :c] >> 16) | (bits[:, c:] & jnp.uint32(0xFFFF0000))


def _unpack_halves(w):
    lo = pltpu.bitcast(w << 16, F32)
    hi = pltpu.bitcast(w & jnp.uint32(0xFFFF0000), F32)
    return lo, hi


def _store_row_tiles(ref, packed):
    m = packed.shape[0]
    for s in range(SUBLANES):
        ref[pl.ds(s, m, stride=SUBLANES), :] = packed[:, s * LANES:(s + 1) * LANES]


def _load_row_tiles(ref, m):
    return jnp.concatenate([ref[pl.ds(s, m, stride=SUBLANES), :] for s in range(SUBLANES)], axis=-1)


def _mm_kernel(x_ref, w_ref, o_ref):
    o_ref[...] = _dot(x_ref[...].astype(BF16), w_ref[...]).astype(o_ref.dtype)


def _matmul(x, w, out_dtype, tm, tn):
    m, k = x.shape
    n = w.shape[1]
    return pl.pallas_call(
        _mm_kernel,
        grid=(m // tm, n // tn),
        in_specs=[pl.BlockSpec((tm, k), lambda i, j: (i, 0)),
                  pl.BlockSpec((k, tn), lambda i, j: (0, j))],
        out_specs=pl.BlockSpec((tm, tn), lambda i, j: (i, j)),
        out_shape=jax.ShapeDtypeStruct((m, n), out_dtype),
        compiler_params=_cparams(("parallel", "arbitrary")),
        name="matmul",
    )(x, w)


def _mm_rope_kernel(x_ref, w_ref, cos_ref, sin_ref, *rest, n_q_tiles, n_rope_tiles, first_gate_tile,
                    head_w, q_scale, n_cast, cast_j):
    cast_src, o_ref, cast_dst = rest[:n_cast], rest[n_cast], rest[n_cast + 1:]
    j = pl.program_id(1)

    @pl.when(j < cast_j)
    def _():
        _cast_blocks(cast_src, cast_dst)

    acc = _dot(x_ref[...].astype(BF16), w_ref[...])

    @pl.when((j >= n_rope_tiles) & (j < first_gate_tile))
    def _():
        o_ref[...] = acc.astype(o_ref.dtype)

    @pl.when(j >= first_gate_tile)
    def _():
        o_ref[...] = (acc * jax.nn.sigmoid(acc)).astype(o_ref.dtype)

    @pl.when(j < n_rope_tiles)
    def _():
        scale = jnp.where(j < n_q_tiles, q_scale, 1.0)
        cos = cos_ref[...] * scale
        sin = sin_ref[...] * scale
        half = head_w // 2
        for c0 in range(0, acc.shape[1], head_w):
            x1 = acc[:, c0:c0 + half]
            x2 = acc[:, c0 + half:c0 + head_w]
            o_ref[:, c0:c0 + half] = (x1 * cos - x2 * sin).astype(o_ref.dtype)
            o_ref[:, c0 + half:c0 + head_w] = (x2 * cos + x1 * sin).astype(o_ref.dtype)


def _matmul_rope(x, w, cos, sin, out_dtype, tm, tn, n_q_cols, n_rope_cols, gate_col0, head_w, q_scale,
                 cast_ws, cast_layer, cast_j):
    m, k = x.shape
    n = w.shape[1]
    nsb = cos.shape[0] // tm
    streams = [_cast_stream_specs(cw, cast_layer, (m // tm) * cast_j,
                                  lambda i, j: i * cast_j + jnp.minimum(j, cast_j - 1)) for cw in cast_ws]
    outs = pl.pallas_call(
        functools.partial(_mm_rope_kernel, n_q_tiles=n_q_cols // tn, n_rope_tiles=n_rope_cols // tn,
                          first_gate_tile=gate_col0 // tn, head_w=head_w, q_scale=q_scale,
                          n_cast=len(cast_ws), cast_j=cast_j),
        grid=(m // tm, n // tn),
        in_specs=[pl.BlockSpec((tm, k), lambda i, j: (i, 0)),
                  pl.BlockSpec((k, tn), lambda i, j: (0, j)),
                  pl.BlockSpec((tm, head_w // 2), lambda i, j: (i % nsb, 0)),
                  pl.BlockSpec((tm, head_w // 2), lambda i, j: (i % nsb, 0))] + [s[0] for s in streams],
        out_specs=[pl.BlockSpec((tm, tn), lambda i, j: (i, j))] + [s[1] for s in streams],
        out_shape=[jax.ShapeDtypeStruct((m, n), out_dtype)] + [s[2] for s in streams],
        compiler_params=_cparams(("arbitrary", "arbitrary")),
        name="matmul_rope",
    )(x, w, cos, sin, *cast_ws)
    return outs[0], outs[1:]


def _layer_norm_rows(z, g, b):
    mean = jnp.mean(z, axis=-1, keepdims=True)
    zc = z - mean
    var = jnp.mean(zc * zc, axis=-1, keepdims=True)
    return zc * lax.rsqrt(var + LN_EPS) * g + b


def _proj_ln_kernel(*refs, n_act, nk):
    acts = refs[:n_act]
    ws = refs[n_act:2 * n_act]
    x_ref, g_ref, b_ref, y_ref, yb_ref, yp_ref = refs[2 * n_act:2 * n_act + 6]
    k = pl.program_id(1)
    tm = x_ref.shape[0]
    n_split = 2
    hm = tm // n_split

    def product(rows):
        part = _dot(acts[0][rows, :], ws[0][...])
        for a, w in zip(acts[1:], ws[1:]):
            part = part + _dot(a[rows, :], w[...])
        return part

    def finish(rows, h, proj):
        z = DN_ALPHA * x_ref[rows, :] + proj
        y = _layer_norm_rows(z, g_ref[...], b_ref[...])
        y_ref[rows, :] = y
        yb_ref[rows, :] = y.astype(BF16)
        _store_row_tiles(yp_ref.at[pl.ds(h * hm * SUBLANES, hm * SUBLANES)], _pack_halves(y))

    if nk == 1:
        for h in range(n_split):
            rows = pl.ds(h * hm, hm)
            finish(rows, h, product(rows))
        return
    acc_ref = refs[2 * n_act + 6]

    @pl.when(k == 0)
    def _():
        acc_ref[...] = product(pl.ds(0, tm))

    @pl.when((k > 0) & (k < nk - 1))
    def _():
        acc_ref[...] = acc_ref[...] + product(pl.ds(0, tm))

    @pl.when(k == nk - 1)
    def _():
        for h in range(n_split):
            rows = pl.ds(h * hm, hm)
            finish(rows, h, acc_ref[rows, :] + product(rows))


def _proj_ln(acts, ws, x, g, b, tm, nk):
    m, d = x.shape
    n_act = len(acts)
    in_specs = []
    for a in acts:
        kk = a.shape[1] // nk
        in_specs.append(pl.BlockSpec((tm, kk), lambda i, k: (i, k)))
    for w in ws:
        kk = w.shape[0] // nk
        in_specs.append(pl.BlockSpec((kk, d), lambda i, k: (k, 0)))
    in_specs += [pl.BlockSpec((tm, d), lambda i, k: (i, 0)),
                 pl.BlockSpec((1, d), lambda i, k: (0, 0)),
                 pl.BlockSpec((1, d), lambda i, k: (0, 0))]
    return pl.pallas_call(
        functools.partial(_proj_ln_kernel, n_act=n_act, nk=nk),
        grid=(m // tm, nk),
        in_specs=in_specs,
        out_specs=[pl.BlockSpec((tm, d), lambda i, k: (i, 0)),
                   pl.BlockSpec((tm, d), lambda i, k: (i, 0)),
                   pl.BlockSpec((tm * SUBLANES, LANES), lambda i, k: (i, 0))],
        out_shape=[jax.ShapeDtypeStruct((m, d), F32), jax.ShapeDtypeStruct((m, d), BF16),
                   jax.ShapeDtypeStruct((m * SUBLANES, LANES), U32)],
        scratch_shapes=[pltpu.VMEM((tm, d), F32)] if nk > 1 else [],
        compiler_params=_cparams(("parallel", "arbitrary")),
        name="proj_ln",
    )(*acts, *ws, x, g.reshape(1, d), b.reshape(1, d))


def _na_bias_tables(rpb):
    nh = rpb.shape[0]
    c = np.arange(GRID_W)
    cs = np.clip(c - NA_WIN_W // 2, 0, GRID_W - NA_WIN_W)
    kc = np.arange(GRID_W)
    valid = (kc[None, :] >= cs[:, None]) & (kc[None, :] < cs[:, None] + NA_WIN_W)
    dc = kc[None, :] - c[:, None] + NA_WIN_W - 1
    onehot = (dc[:, :, None] == np.arange(2 * NA_WIN_W - 1)[None, None, :]) & valid[:, :, None]
    cols = jnp.einsum("hrd,ckd->hrck", rpb.astype(F32), jnp.asarray(onehot, F32),
                      precision=lax.Precision.HIGHEST)
    cols = jnp.where(jnp.asarray(valid)[None, None], cols, MASK_VALUE)
    tabs = jnp.stack([cols[:, off:off + NA_WIN_H] for off in range(NA_WIN_H)], axis=1)
    return tabs.transpose(0, 1, 3, 2, 4).reshape(nh, NA_WIN_H, GRID_W, NA_WIN_H * GRID_W)


def _na_kernel(q_ref, k_ref, v_ref, bias_ref, o_ref, *, rows, group):
    scale = NA_HEAD_DIM ** -0.5
    nkeys = NA_WIN_H * GRID_W

    def body(i, carry):
        geom, scores = [], []
        for u in range(group):
            r = i * group + u
            rs = jnp.clip(r - NA_WIN_H // 2, 0, rows - NA_WIN_H)
            off = rs - r + NA_WIN_H - 1
            q0 = pl.multiple_of(r * GRID_W, GRID_W)
            k0 = pl.multiple_of(rs * GRID_W, GRID_W)
            geom.append((q0, k0))
            s = _dot_nt(q_ref[pl.ds(q0, GRID_W), :], k_ref[pl.ds(k0, nkeys), :])
            scores.append(s * scale + bias_ref[0, off])
        for (q0, k0), s in zip(geom, scores):
            m = jnp.max(s, axis=-1, keepdims=True)
            p = jnp.exp(s - m)
            l = jnp.sum(p, axis=-1, keepdims=True)
            o = _dot(p.astype(BF16), v_ref[pl.ds(k0, nkeys), :]) / l
            o_ref[pl.ds(q0, GRID_W), :] = o.astype(o_ref.dtype)
        return carry

    lax.fori_loop(0, rows // group, body, 0)


def _na_attention(h, bias_tables, batch, seq):
    rows = seq // GRID_W
    d = NA_HEAD_DIM
    nkeys = NA_WIN_H * GRID_W
    return pl.pallas_call(
        functools.partial(_na_kernel, rows=rows, group=16),
        grid=(batch, NA_HEADS),
        in_specs=[pl.BlockSpec((seq, d), lambda b, hh: (b, hh)),
                  pl.BlockSpec((seq, d), lambda b, hh: (b, NA_HEADS + hh)),
                  pl.BlockSpec((seq, d), lambda b, hh: (b, 2 * NA_HEADS + hh)),
                  pl.BlockSpec((1, NA_WIN_H, GRID_W, nkeys), lambda b, hh: (hh, 0, 0, 0))],
        out_specs=pl.BlockSpec((seq, d), lambda b, hh: (b, hh)),
        out_shape=jax.ShapeDtypeStruct((batch * seq, NA_WIDTH), BF16),
        compiler_params=_cparams(("parallel", "arbitrary")),
        name="na_attention",
    )(h, h, h, bias_tables)


def _rms_rows(x, g):
    return x * lax.rsqrt(jnp.mean(x * x, axis=-1, keepdims=True) + RMS_EPS) * g


def _rope_lanes(t, cosf, sinf):
    return t * cosf + pltpu.roll(t, LANES // 2, 1) * sinf


def _mla_prep_kernel(cq_ref, ckv_ref, kr_ref, gq_ref, gkv_ref, wq_ref, wk_ref, wvt_ref, cos_ref, sin_ref,
                     q_ref, k_ref, vt_ref):
    dq = MLA_NOPE_DIM + MLA_ROPE_DIM
    cosf = cos_ref[...]
    sinf = sin_ref[...]
    cqn = _rms_rows(cq_ref[...].astype(F32), gq_ref[...]).astype(BF16)
    ckvn = _rms_rows(ckv_ref[...].astype(F32), gkv_ref[...]).astype(BF16)
    qf = _dot(cqn, wq_ref[...]) * (dq ** -0.5 * math.log2(math.e))
    kf = _dot(ckvn, wk_ref[...])
    vt_ref[...] = _dot_nt(wvt_ref[...], ckvn).astype(BF16)
    kpe = _rope_lanes(kr_ref[...].astype(F32), cosf, sinf).astype(BF16)
    for hh in range(MLA_HEADS):
        c0 = hh * 2 * LANES
        q_ref[:, c0:c0 + LANES] = qf[:, c0:c0 + LANES].astype(BF16)
        q_ref[:, c0 + LANES:c0 + 2 * LANES] = _rope_lanes(qf[:, c0 + LANES:c0 + 2 * LANES], cosf, sinf).astype(BF16)
        k_ref[:, c0:c0 + LANES] = kf[:, hh * LANES:(hh + 1) * LANES].astype(BF16)
        k_ref[:, c0 + LANES:c0 + 2 * LANES] = kpe


def _mla_prep(h, gq, gkv, wq_p, wk, wvt, cosf, sinf, col_cq, seq, tm):
    n = h.shape[0]
    hw = MLA_HEADS * 2 * LANES
    nsb = seq // tm
    b_cq = col_cq // MLA_Q_RANK
    b_ckv = (col_cq + MLA_Q_RANK) // MLA_KV_RANK
    b_kr = (col_cq + MLA_Q_RANK + MLA_KV_RANK) // LANES
    return pl.pallas_call(
        _mla_prep_kernel,
        grid=(n // tm,),
        in_specs=[pl.BlockSpec((tm, MLA_Q_RANK), lambda i: (i, b_cq)),
                  pl.BlockSpec((tm, MLA_KV_RANK), lambda i: (i, b_ckv)),
                  pl.BlockSpec((tm, LANES), lambda i: (i, b_kr)),
                  pl.BlockSpec((1, MLA_Q_RANK), lambda i: (0, 0)),
                  pl.BlockSpec((1, MLA_KV_RANK), lambda i: (0, 0)),
                  pl.BlockSpec((MLA_Q_RANK, hw), lambda i: (0, 0)),
                  pl.BlockSpec((MLA_KV_RANK, MLA_HEADS * LANES), lambda i: (0, 0)),
                  pl.BlockSpec((MLA_HEADS * MLA_V_DIM, MLA_KV_RANK), lambda i: (0, 0)),
                  pl.BlockSpec((tm, LANES), lambda i: (i % nsb, 0)),
                  pl.BlockSpec((tm, LANES), lambda i: (i % nsb, 0))],
        out_specs=[pl.BlockSpec((tm, hw), lambda i: (i, 0)),
                   pl.BlockSpec((tm, hw), lambda i: (i, 0)),
                   pl.BlockSpec((MLA_HEADS * MLA_V_DIM, tm), lambda i: (0, i))],
        out_shape=[jax.ShapeDtypeStruct((n, hw), BF16), jax.ShapeDtypeStruct((n, hw), BF16),
                   jax.ShapeDtypeStruct((MLA_HEADS * MLA_V_DIM, n), BF16)],
        compiler_params=_cparams(("parallel",)),
        name="mla_prep",
    )(h, h, h, gq.reshape(1, -1), gkv.reshape(1, -1), wq_p, wk, wvt, cosf, sinf)


def _cast_stream_specs(w, layer, n_slots, slot_of):
    _, ne, rows, cols = w.shape
    if n_slots >= ne:
        e_per, rb = 1, n_slots // ne
        while rows % rb or (rows // rb) % (2 * SUBLANES):
            rb -= 1
    else:
        assert ne % n_slots == 0
        e_per, rb = ne // n_slots, 1
    n_blocks = (ne // e_per) * rb

    def block_of(*g):
        s = jnp.minimum(slot_of(*g), n_blocks - 1)
        return s // rb, s % rb

    src = pl.BlockSpec((1, e_per, rows // rb, cols), lambda *g: (layer, *block_of(*g), 0))
    dst = pl.BlockSpec((e_per, rows // rb, cols), lambda *g: (*block_of(*g), 0))
    return src, dst, jax.ShapeDtypeStruct((ne, rows, cols), BF16), n_blocks


def _cast_blocks(srcs, dsts):
    for src, dst in zip(srcs, dsts):
        dst[...] = src[0].astype(BF16)


def _mla_attn_kernel(q_ref, k_ref, vt_ref, *rest, tk, sub, n_cast):
    cast_src, o_ref, cast_dst = rest[:n_cast], rest[n_cast], rest[n_cast + 1:]
    _cast_blocks(cast_src, cast_dst)
    nchunk = k_ref.shape[0] // tk
    tq = q_ref.shape[0]
    nsub = tq // sub
    qs = [q_ref[s * sub:(s + 1) * sub, :] for s in range(nsub)]
    m = [jnp.full((1, sub), MASK_VALUE, F32) for _ in range(nsub)]
    l = [jnp.zeros((1, sub), F32) for _ in range(nsub)]
    acc = [jnp.zeros((MLA_V_DIM, sub), F32) for _ in range(nsub)]

    def scores(s, c):
        return _dot_nt(k_ref[c * tk:(c + 1) * tk, :], qs[s])

    st_next = [scores(s, 0) for s in range(nsub)]
    for c in range(nchunk):
        for s in range(nsub):
            st = st_next[s]
            m_new = jnp.maximum(m[s], jnp.max(st, axis=0, keepdims=True))
            a = jnp.exp2(m[s] - m_new)
            p = jnp.exp2(st - m_new)
            l[s] = a * l[s] + jnp.sum(p, axis=0, keepdims=True)
            if c + 1 < nchunk:
                st_next[s] = scores(s, c + 1)
            acc[s] = a * acc[s] + _dot(vt_ref[:, c * tk:(c + 1) * tk], p.astype(BF16))
            m[s] = m_new
    for s in range(nsub):
        o_ref[s * sub:(s + 1) * sub, :] = (acc[s] / l[s]).T.astype(o_ref.dtype)


def _mla_attention(q_p, k_p, vt, batch, seq, tq, tk, sub, cast_ws, cast_layer):
    n = q_p.shape[0]
    nqb = seq // tq
    n_slots = batch * MLA_HEADS * nqb
    streams = [_cast_stream_specs(w, cast_layer, n_slots, lambda b, hh, i: (b * MLA_HEADS + hh) * nqb + i)
               for w in cast_ws]
    outs = pl.pallas_call(
        functools.partial(_mla_attn_kernel, tk=tk, sub=sub, n_cast=len(cast_ws)),
        grid=(batch, MLA_HEADS, nqb),
        in_specs=[pl.BlockSpec((tq, 2 * LANES), lambda b, hh, i: (b * nqb + i, hh)),
                  pl.BlockSpec((seq, 2 * LANES), lambda b, hh, i: (b, hh)),
                  pl.BlockSpec((MLA_V_DIM, seq), lambda b, hh, i: (hh, b))] + [s[0] for s in streams],
        out_specs=[pl.BlockSpec((tq, MLA_V_DIM), lambda b, hh, i: (b * nqb + i, hh))] + [s[1] for s in streams],
        out_shape=[jax.ShapeDtypeStruct((n, MLA_HEADS * MLA_V_DIM), BF16)] + [s[2] for s in streams],
        compiler_params=_cparams(("arbitrary", "arbitrary", "arbitrary")),
        name="mla_attention",
    )(q_p, k_p, vt, *cast_ws)
    return outs[0], outs[1:]


def _ret_kernel(lg_ref, q_ref, k_ref, v_ref, g_ref, o_ref, acc_ref, st_ref, *, c_len, group):
    nchunk = q_ref.shape[0] // c_len
    hh = pl.program_id(1)
    lgf = lg_ref[0, hh]
    lgb = lg_ref[1, hh]
    ii = lax.broadcasted_iota(I32, (c_len, c_len), 0).astype(F32)
    jj = lax.broadcasted_iota(I32, (c_len, c_len), 1).astype(F32)
    rel = ii - jj
    dmat = jnp.where(rel >= 0, jnp.exp(lgf * jnp.maximum(rel, 0.0)), jnp.exp(lgb * jnp.maximum(-rel, 0.0)))
    pos = lax.broadcasted_iota(I32, (c_len, 1), 0).astype(F32)
    qdec_f = jnp.exp(lgf * (pos + 1.0))
    kdec_f = jnp.exp(lgf * (c_len - 1.0 - pos))
    qdec_b = jnp.exp(lgb * (c_len - pos))
    kdec_b = jnp.exp(lgb * pos)
    full_chunk = jnp.full((1, RET_V_DIM), float(c_len), F32)
    cdec_f = jnp.exp(lgf * full_chunk)
    cdec_b = jnp.exp(lgb * full_chunk)

    def decayed_keys_t(t0, kdec):
        return (k_ref[pl.ds(t0, c_len), :].astype(F32) * kdec).T.astype(BF16)

    st_ref[...] = jnp.zeros_like(st_ref)

    def bwd_body(i, carry):
        t0s = [pl.multiple_of((nchunk - 1 - (i * group + u)) * c_len, c_len) for u in range(group)]
        upd = [_dot(decayed_keys_t(t0, kdec_b), v_ref[pl.ds(t0, c_len), :]) for t0 in t0s]
        for t0, u_c in zip(t0s, upd):
            st = st_ref[...]
            acc_ref[pl.ds(t0, c_len), :] = _dot(q_ref[pl.ds(t0, c_len), :], st.astype(BF16)) * qdec_b
            st_ref[...] = st * cdec_b + u_c
        return carry

    lax.fori_loop(0, nchunk // group, bwd_body, 0)
    st_ref[...] = jnp.zeros_like(st_ref)

    def fwd_body(i, carry):
        t0s = [pl.multiple_of((i * group + u) * c_len, c_len) for u in range(group)]
        scs = [_dot_nt(q_ref[pl.ds(t0, c_len), :], k_ref[pl.ds(t0, c_len), :]) * dmat for t0 in t0s]
        upd = [_dot(decayed_keys_t(t0, kdec_f), v_ref[pl.ds(t0, c_len), :]) for t0 in t0s]
        for t0, sc, u_c in zip(t0s, scs, upd):
            st = st_ref[...]
            r = (_dot(sc.astype(BF16), v_ref[pl.ds(t0, c_len), :])
                 + _dot(q_ref[pl.ds(t0, c_len), :], st.astype(BF16)) * qdec_f
                 + acc_ref[pl.ds(t0, c_len), :])
            st_ref[...] = st * cdec_f + u_c
            r = r - jnp.mean(r, axis=-1, keepdims=True)
            r = r * lax.rsqrt(jnp.mean(r * r, axis=-1, keepdims=True) + LN_EPS)
            o_ref[pl.ds(t0, c_len), :] = (g_ref[pl.ds(t0, c_len), :].astype(F32) * r).astype(o_ref.dtype)
        return carry

    lax.fori_loop(0, nchunk // group, fwd_body, 0)


def _retention(hc, lg, batch, seq, c_len, group):
    dk, dv, nh = RET_QK_DIM, RET_V_DIM, RET_HEADS
    v_blk0 = (2 * nh * dk) // dv
    return pl.pallas_call(
        functools.partial(_ret_kernel, c_len=c_len, group=group),
        grid=(batch, nh),
        in_specs=[pl.BlockSpec(memory_space=pltpu.SMEM),
                  pl.BlockSpec((seq, dk), lambda b, hh: (b, hh)),
                  pl.BlockSpec((seq, dk), lambda b, hh: (b, nh + hh)),
                  pl.BlockSpec((seq, dv), lambda b, hh: (b, v_blk0 + hh)),
                  pl.BlockSpec((seq, dv), lambda b, hh: (b, v_blk0 + nh + hh))],
        out_specs=pl.BlockSpec((seq, dv), lambda b, hh: (b, hh)),
        scratch_shapes=[pltpu.VMEM((seq, dv), F32), pltpu.VMEM((dk, dv), F32)],
        out_shape=jax.ShapeDtypeStruct((batch * seq, nh * dv), BF16),
        compiler_params=_cparams(("parallel", "arbitrary")),
        name="retention",
    )(lg, hc, hc, hc, hc)


ROUTER_ROWS = 40


def _router_kernel(x_ref, wt_ref, b_ref, tri_ref, ids_ref, wts_ref, cnt_ref, carry_ref):
    i = pl.program_id(0)
    tm = x_ref.shape[0]

    @pl.when(i == 0)
    def _():
        carry_ref[...] = jnp.zeros_like(carry_ref)

    logits = _dot_nt(wt_ref[...], x_ref[...]) + b_ref[...]
    grow = lax.broadcasted_iota(I32, (SUBLANES, tm), 0).astype(F32)
    gl = jnp.where(grow < N_GROUPS, logits[0:SUBLANES], MASK_VALUE)
    gmax = jnp.max(gl, axis=0, keepdims=True)
    gsum = jnp.sum(jnp.exp(gl - gmax), axis=0, keepdims=True)
    p_group = 1.0 / gsum
    g_idx = jnp.min(jnp.where(gl == gmax, grow, float(N_GROUPS)), axis=0, keepdims=True)
    sel = jnp.zeros((EXPERTS_PER_GROUP, tm), F32)
    for g in range(N_GROUPS):
        r0 = SUBLANES + g * EXPERTS_PER_GROUP
        sel = sel + jnp.where(g_idx == float(g), logits[r0:r0 + EXPERTS_PER_GROUP], 0.0)
    erow = lax.broadcasted_iota(I32, (EXPERTS_PER_GROUP, tm), 0).astype(F32)
    smax = jnp.max(sel, axis=0, keepdims=True)
    sexp = jnp.exp(sel - smax)
    probs = sexp / jnp.sum(sexp, axis=0, keepdims=True)
    p1 = jnp.max(probs, axis=0, keepdims=True)
    i1 = jnp.min(jnp.where(probs == p1, erow, float(EXPERTS_PER_GROUP)), axis=0, keepdims=True)
    rest = jnp.where(erow == i1, -1.0, probs)
    p2 = jnp.max(rest, axis=0, keepdims=True)
    i2 = jnp.min(jnp.where(rest == p2, erow, float(EXPERTS_PER_GROUP)), axis=0, keepdims=True)
    denom = p1 + p2
    e0 = g_idx * EXPERTS_PER_GROUP + i1
    e1 = g_idx * EXPERTS_PER_GROUP + i2

    xrow = lax.broadcasted_iota(I32, (N_EXPERTS, tm), 0).astype(F32)
    oh0 = jnp.where(xrow == e0, 1.0, 0.0)
    oh1 = jnp.where(xrow == e1, 1.0, 0.0)
    onehot = oh0 + oh1
    before = _dot(onehot.astype(BF16), tri_ref[...]) + carry_ref[:, 0:1]
    rank0 = jnp.sum(oh0 * before, axis=0, keepdims=True)
    rank1 = jnp.sum(oh1 * before, axis=0, keepdims=True)
    carry_ref[...] = carry_ref[...] + jnp.sum(onehot, axis=1, keepdims=True)

    ids_ref[...] = jnp.zeros_like(ids_ref)
    ids_ref[0:1, :] = e0.astype(I32)
    ids_ref[1:2, :] = e1.astype(I32)
    ids_ref[2:3, :] = rank0.astype(I32)
    ids_ref[3:4, :] = rank1.astype(I32)
    wts_ref[...] = jnp.zeros_like(wts_ref)
    wts_ref[0:1, :] = p_group * p1 / denom
    wts_ref[1:2, :] = p_group * p2 / denom
    cnt_ref[...] = carry_ref[...]


def _router(x, wt, bias, tm):
    n, d = x.shape
    tri = jnp.asarray(np.triu(np.ones((tm, tm), np.float32), 1), BF16)
    return pl.pallas_call(
        _router_kernel,
        grid=(n // tm,),
        in_specs=[pl.BlockSpec((tm, d), lambda i: (i, 0)),
                  pl.BlockSpec((ROUTER_ROWS, d), lambda i: (0, 0)),
                  pl.BlockSpec((ROUTER_ROWS, 1), lambda i: (0, 0)),
                  pl.BlockSpec((tm, tm), lambda i: (0, 0))],
        out_specs=[pl.BlockSpec((SUBLANES, tm), lambda i: (0, i)),
                   pl.BlockSpec((SUBLANES, tm), lambda i: (0, i)),
                   pl.BlockSpec((N_EXPERTS, LANES), lambda i: (0, 0))],
        out_shape=[jax.ShapeDtypeStruct((SUBLANES, n), I32), jax.ShapeDtypeStruct((SUBLANES, n), F32),
                   jax.ShapeDtypeStruct((N_EXPERTS, LANES), F32)],
        scratch_shapes=[pltpu.VMEM((N_EXPERTS, LANES), F32)],
        compiler_params=_cparams(("arbitrary",)),
        name="moe_router",
    )(x, wt, bias, tri)


def _slots_kernel(ids_ref, cnt_ref, slots_ref, blk_ref, *, nblk_pad):
    tm = ids_ref.shape[1]
    cnt = cnt_ref[:, 0:1]
    padded = jnp.floor((cnt + (MOE_BLOCK - 1)) / MOE_BLOCK) * MOE_BLOCK
    er = lax.broadcasted_iota(I32, (N_EXPERTS, N_EXPERTS), 0)
    ec = lax.broadcasted_iota(I32, (N_EXPERTS, N_EXPERTS), 1)
    padded_row = jnp.sum(jnp.where(er == ec, padded, 0.0), axis=0, keepdims=True)
    p_start = jnp.sum(jnp.where(ec < er, padded_row, 0.0), axis=1, keepdims=True)
    p_end = p_start + padded
    xrow = lax.broadcasted_iota(I32, (N_EXPERTS, tm), 0)
    e0 = ids_ref[0:1, :]
    e1 = ids_ref[1:2, :]
    s0 = jnp.sum(jnp.where(xrow == e0, p_start, 0.0), axis=0, keepdims=True).astype(I32) + ids_ref[2:3, :]
    s1 = jnp.sum(jnp.where(xrow == e1, p_start, 0.0), axis=0, keepdims=True).astype(I32) + ids_ref[3:4, :]
    slots_ref[...] = jnp.zeros_like(slots_ref)
    slots_ref[0:1, :] = s0
    slots_ref[1:2, :] = s1
    bstart = (lax.broadcasted_iota(I32, (1, nblk_pad), 1) * MOE_BLOCK).astype(F32)
    blk_e = jnp.minimum(jnp.sum(jnp.where(p_end <= bstart, 1.0, 0.0), axis=0, keepdims=True), N_EXPERTS - 1.0)
    total = jnp.sum(padded, axis=0, keepdims=True)
    erow = lax.broadcasted_iota(I32, (N_EXPERTS, nblk_pad), 0).astype(F32)
    own_end = jnp.sum(jnp.where(erow == blk_e, p_end, 0.0), axis=0, keepdims=True)
    nxt_e = jnp.minimum(jnp.sum(jnp.where(p_end <= own_end, 1.0, 0.0), axis=0, keepdims=True), N_EXPERTS - 1.0)
    nxt_e = jnp.where(own_end < total, nxt_e, -1.0)
    blk_ref[...] = jnp.zeros_like(blk_ref)
    blk_ref[0:1, :] = blk_e.astype(I32)
    blk_ref[1:2, :] = jnp.broadcast_to((total / MOE_BLOCK).astype(I32), (1, nblk_pad))
    blk_ref[2:3, :] = nxt_e.astype(I32)
    lane = lax.broadcasted_iota(I32, (N_EXPERTS, nblk_pad), 1).astype(F32)
    blk_ref[3:4, :] = jnp.sum(jnp.where(erow == lane, p_start + cnt, 0.0), axis=0, keepdims=True).astype(I32)
    blk_ref[4:5, :] = jnp.sum(jnp.where(erow == lane, p_end, 0.0), axis=0, keepdims=True).astype(I32)


def _slots(ids, cnt, tm, nblk_pad):
    n = ids.shape[1]
    return pl.pallas_call(
        functools.partial(_slots_kernel, nblk_pad=nblk_pad),
        grid=(n // tm,),
        in_specs=[pl.BlockSpec((SUBLANES, tm), lambda i: (0, i)),
                  pl.BlockSpec((N_EXPERTS, LANES), lambda i: (0, 0))],
        out_specs=[pl.BlockSpec((SUBLANES, tm), lambda i: (0, i)),
                   pl.BlockSpec((SUBLANES, nblk_pad), lambda i: (0, 0))],
        out_shape=[jax.ShapeDtypeStruct((SUBLANES, n), I32), jax.ShapeDtypeStruct((SUBLANES, nblk_pad), I32)],
        compiler_params=_cparams(("arbitrary",)),
        name="moe_slots",
    )(ids, cnt)


def _slot_tokens_kernel(slots_ref, blk_ref, tok_ref, *, n, cap, nblk_pad):
    def zero(j, carry):
        tok_ref[j] = 0
        return carry

    def scatter(t, carry):
        tok_ref[slots_ref[t]] = t
        tok_ref[slots_ref[n + t]] = t
        return carry

    for e in range(N_EXPERTS):
        lax.fori_loop(blk_ref[3 * nblk_pad + e], blk_ref[4 * nblk_pad + e], zero, 0)
    lax.fori_loop(blk_ref[nblk_pad] * MOE_BLOCK, cap, zero, 0)
    lax.fori_loop(0, n, scatter, 0, unroll=8)


def _slot_tokens(slots_flat, blk_flat, n, cap, nblk_pad):
    return pl.pallas_call(
        functools.partial(_slot_tokens_kernel, n=n, cap=cap, nblk_pad=nblk_pad),
        grid_spec=pltpu.PrefetchScalarGridSpec(
            num_scalar_prefetch=2,
            grid=(1,),
            in_specs=[],
            out_specs=pl.BlockSpec(memory_space=pltpu.SMEM)),
        out_shape=jax.ShapeDtypeStruct((cap,), I32),
        compiler_params=_cparams(("arbitrary",)),
        name="moe_slot_tokens",
    )(slots_flat, blk_flat)


ROW_BUFS = 3


def _expert_kernel(blk_ref, tok_ref, xpk_hbm, wg_hbm, wu_hbm, wd_hbm, y_ref, xbuf, wgb, wub, wdb,
                   sems, wsems, cnt_ref, *, nblk_pad):
    i = pl.program_id(0)
    n_used = blk_ref[nblk_pad]

    def weight_copies(e, slot):
        return [pltpu.make_async_copy(src.at[e], dst.at[slot], wsems.at[slot])
                for src, dst in ((wg_hbm, wgb), (wu_hbm, wub), (wd_hbm, wdb))]

    def start_rows(block, buf, r_lo=0, r_hi=MOE_BLOCK):
        for r in range(r_lo, r_hi):
            src0 = pl.multiple_of(tok_ref[block * MOE_BLOCK + r] * SUBLANES, SUBLANES)
            pltpu.make_async_copy(xpk_hbm.at[pl.ds(src0, SUBLANES)],
                                  xbuf.at[buf, pl.ds(r * SUBLANES, SUBLANES)], sems.at[buf]).start(priority=r % 2)

    def wait_rows(buf):
        pltpu.make_async_copy(xpk_hbm.at[pl.ds(0, MOE_BLOCK * SUBLANES)], xbuf.at[buf], sems.at[buf]).wait()

    @pl.when(i == 0)
    def _():
        cnt_ref[0] = 0
        for cp in weight_copies(blk_ref[0], 0):
            cp.start(priority=1)
        for ahead in range(ROW_BUFS - 1):
            start_rows(jnp.minimum(ahead, n_used - 1), ahead)

    @pl.when((i < n_used) & ((i == 0) | (blk_ref[i] != blk_ref[jnp.maximum(i - 1, 0)])))
    def _():
        slot = cnt_ref[0] % 2
        cnt_ref[0] = cnt_ref[0] + 1
        for cp in weight_copies(blk_ref[i], slot):
            cp.wait()
        nxt_e = blk_ref[2 * nblk_pad + i]

        @pl.when(nxt_e >= 0)
        def _():
            for cp in weight_copies(nxt_e, 1 - slot):
                cp.start(priority=1)

    @pl.when(i < n_used)
    def _():
        wslot = (cnt_ref[0] + 1) % 2
        buf = i % ROW_BUFS
        nbuf = (i + ROW_BUFS - 1) % ROW_BUFS
        nxt = jnp.minimum(i + ROW_BUFS - 1, n_used - 1)
        wait_rows(buf)
        lo, hi = _unpack_halves(_load_row_tiles(xbuf.at[buf], MOE_BLOCK))
        xb = jnp.concatenate([lo.astype(BF16), hi.astype(BF16)], axis=-1)
        g = _dot(xb, wgb[wslot])
        start_rows(nxt, nbuf, 0, MOE_BLOCK // 2)
        u = _dot(xb, wub[wslot])
        start_rows(nxt, nbuf, MOE_BLOCK // 2, MOE_BLOCK)
        hmid = (g * jax.nn.sigmoid(g) * u).astype(BF16)
        _store_row_tiles(y_ref, _pack_halves(_dot(hmid, wdb[wslot])))

    @pl.when(i == n_used - 1)
    def _():
        for ahead in range(1, ROW_BUFS):
            wait_rows((i + ahead) % ROW_BUFS)

    @pl.when(i >= n_used)
    def _():
        y_ref[...] = jnp.zeros_like(y_ref)


def _experts(blk_flat, slot_tok, xpk, wg, wu, wd, nblk_pad):
    d, de = wg.shape[1], wg.shape[2]
    assert d == 2 * SUBLANES * LANES and xpk.shape[1] == LANES
    cap = slot_tok.shape[0]
    tile_rows = MOE_BLOCK * SUBLANES
    return pl.pallas_call(
        functools.partial(_expert_kernel, nblk_pad=nblk_pad),
        grid_spec=pltpu.PrefetchScalarGridSpec(
            num_scalar_prefetch=2,
            grid=(cap // MOE_BLOCK,),
            in_specs=[pl.BlockSpec(memory_space=pl.ANY), pl.BlockSpec(memory_space=pl.ANY),
                      pl.BlockSpec(memory_space=pl.ANY), pl.BlockSpec(memory_space=pl.ANY)],
            out_specs=pl.BlockSpec((tile_rows, LANES), lambda i, blk, tok: (i, 0)),
            scratch_shapes=[pltpu.VMEM((ROW_BUFS, tile_rows, LANES), U32),
                            pltpu.VMEM((2, d, de), BF16), pltpu.VMEM((2, d, de), BF16), pltpu.VMEM((2, de, d), BF16),
                            pltpu.SemaphoreType.DMA((ROW_BUFS,)), pltpu.SemaphoreType.DMA((2,)),
                            pltpu.SMEM((1,), I32)]),
        out_shape=jax.ShapeDtypeStruct((cap * SUBLANES, LANES), U32),
        compiler_params=_cparams(("arbitrary",)),
        name="moe_experts",
    )(blk_flat, slot_tok, xpk, wg, wu, wd)


def _tail_kernel(slots_ref, x_ref, xb_ref, p_ref, wts_ref, wgate_ref, bgate_ref, wproj_ref, g_ref, b_ref,
                 yb_hbm, y_ref, ybf_ref, rows_ref, sems):
    i = pl.program_id(0)
    nsteps = pl.num_programs(0)
    tm = x_ref.shape[0]
    n = nsteps * tm

    def start_rows(step, buf):
        for t in range(tm):
            for which in range(2):
                src0 = pl.multiple_of(slots_ref[which * n + step * tm + t] * SUBLANES, SUBLANES)
                pltpu.make_async_copy(yb_hbm.at[pl.ds(src0, SUBLANES)],
                                      rows_ref.at[buf, which, pl.ds(t * SUBLANES, SUBLANES)], sems.at[buf]).start()

    def wait_rows(buf):
        for which in range(2):
            pltpu.make_async_copy(yb_hbm.at[pl.ds(0, tm * SUBLANES)], rows_ref.at[buf, which],
                                  sems.at[buf]).wait()

    @pl.when(i == 0)
    def _():
        start_rows(0, 0)

    buf = i % 2
    nxt = jnp.minimum(i + 1, nsteps - 1)
    wait_rows(buf)
    gate_pre = _dot(xb_ref[...], wgate_ref[...])
    proj = _dot(p_ref[0].astype(BF16), wproj_ref[...])
    start_rows(nxt, 1 - buf)
    ple = jax.nn.sigmoid(gate_pre + bgate_ref[...]) * proj
    w = wts_ref[...]
    lo0, hi0 = _unpack_halves(_load_row_tiles(rows_ref.at[buf, 0], tm))
    lo1, hi1 = _unpack_halves(_load_row_tiles(rows_ref.at[buf, 1], tm))
    w0 = w[:, 0:1]
    w1 = w[:, 1:2]
    ffn = jnp.concatenate([lo0 * w0 + lo1 * w1, hi0 * w0 + hi1 * w1], axis=-1)
    z = DN_ALPHA * x_ref[...] + ffn + ple
    y = _layer_norm_rows(z, g_ref[...], b_ref[...])
    y_ref[...] = y
    ybf_ref[...] = y.astype(BF16)

    @pl.when(i == nsteps - 1)
    def _():
        wait_rows(1 - buf)


def _layer_tail(slots_flat, x, xb, p, layer, wts_t, wgate, bgate, wproj, g, b, yb, tm):
    n, d = x.shape
    pd = p.shape[2]
    return pl.pallas_call(
        _tail_kernel,
        grid_spec=pltpu.PrefetchScalarGridSpec(
            num_scalar_prefetch=1,
            grid=(n // tm,),
            in_specs=[pl.BlockSpec((tm, d), lambda i, s: (i, 0)),
                      pl.BlockSpec((tm, d), lambda i, s: (i, 0)),
                      pl.BlockSpec((1, tm, pd), lambda i, s: (layer, i, 0)),
                      pl.BlockSpec((tm, 2), lambda i, s: (i, 0)),
                      pl.BlockSpec((d, d), lambda i, s: (0, 0)),
                      pl.BlockSpec((1, d), lambda i, s: (0, 0)),
                      pl.BlockSpec((pd, d), lambda i, s: (0, 0)),
                      pl.BlockSpec((1, d), lambda i, s: (0, 0)),
                      pl.BlockSpec((1, d), lambda i, s: (0, 0)),
                      pl.BlockSpec(memory_space=pl.ANY)],
            out_specs=[pl.BlockSpec((tm, d), lambda i, s: (i, 0)),
                       pl.BlockSpec((tm, d), lambda i, s: (i, 0))],
            scratch_shapes=[pltpu.VMEM((2, 2, tm * SUBLANES, LANES), U32), pltpu.SemaphoreType.DMA((2,))]),
        out_shape=[jax.ShapeDtypeStruct((n, d), F32), jax.ShapeDtypeStruct((n, d), BF16)],
        compiler_params=_cparams(("arbitrary",)),
        name="layer_tail",
    )(slots_flat, x, xb, p, wts_t, wgate, bgate.reshape(1, d), wproj, g.reshape(1, d), b.reshape(1, d), yb)


def _rope_table(seq, dim):
    pos = jnp.arange(seq, dtype=F32)
    inv = jnp.exp(jnp.arange(0, dim, 2, dtype=F32) * (-math.log(ROPE_BASE) / dim))
    ang = pos[:, None] * inv[None, :]
    return jnp.cos(ang), jnp.sin(ang)


def _moe_layer(x, xb, xpk, p, layer, w_group, b_group, w_router, b_router, w_gate, w_up, w_down,
               ple_w_proj, ple_w_gate, ple_b_gate, ln_g, ln_b):
    n, d = x.shape
    nblk = -(-(2 * n) // MOE_BLOCK) + N_EXPERTS
    nblk_pad = -(-nblk // LANES) * LANES
    cap = nblk * MOE_BLOCK
    wt = jnp.zeros((ROUTER_ROWS, d), F32)
    wt = wt.at[0:N_GROUPS].set(w_group.T)
    wt = wt.at[SUBLANES:].set(w_router.transpose(0, 2, 1).reshape(N_EXPERTS, d))
    bias = jnp.zeros((ROUTER_ROWS, 1), F32)
    bias = bias.at[0:N_GROUPS, 0].set(b_group)
    bias = bias.at[SUBLANES:, 0].set(b_router.reshape(N_EXPERTS))
    ids, wts, cnt = _router(xb, wt.astype(BF16), bias, tm=512)
    slots, blk = _slots(ids, cnt, tm=min(2048, n), nblk_pad=nblk_pad)
    slots_flat = slots[0:2].reshape(2 * n)
    blk_flat = blk[0:5].reshape(5 * nblk_pad)
    slot_tok = _slot_tokens(slots_flat, blk_flat, n, cap, nblk_pad)
    yb = _experts(blk_flat, slot_tok, xpk, w_gate, w_up, w_down, nblk_pad)
    return _layer_tail(slots_flat, x, xb, p, layer, wts[0:2].T, ple_w_gate.astype(BF16), ple_b_gate,
                       ple_w_proj.astype(BF16), ln_g, ln_b, yb, tm=256)


def _mixer_ab(x, xres, batch, seq, w_in, rpb, q_norm, w_uq, kv_norm, w_ukv, w_out, ln_g, ln_b, expert_ws, layer):
    d = x.shape[1]
    o1 = 3 * NA_WIDTH
    o2 = o1 + MLA_Q_RANK
    o3 = o2 + MLA_KV_RANK
    half = MLA_ROPE_DIM // 2
    kr = w_in[:, o3:o3 + MLA_ROPE_DIM]
    kr_sw = jnp.concatenate([kr[:, half:], kr[:, :half]], axis=1)
    width = -(-(o3 + 2 * MLA_ROPE_DIM) // 1024) * 1024
    w_in_p = jnp.concatenate([w_in, kr_sw, jnp.zeros((d, width - o3 - 2 * MLA_ROPE_DIM), F32)], axis=1)
    h = _matmul(x, w_in_p.astype(BF16), BF16, tm=512, tn=1024)
    a_out = _na_attention(h, _na_bias_tables(rpb), batch, seq)
    dq = MLA_NOPE_DIM + MLA_ROPE_DIM
    wq = w_uq.reshape(MLA_Q_RANK, MLA_HEADS, dq)
    wq_pe = wq[:, :, MLA_NOPE_DIM:]
    wq_p = jnp.concatenate([wq, wq_pe[:, :, half:], wq_pe[:, :, :half]], axis=2)
    wq_p = wq_p.reshape(MLA_Q_RANK, MLA_HEADS * 2 * LANES).astype(BF16)
    wkv = w_ukv.reshape(MLA_KV_RANK, MLA_HEADS, MLA_NOPE_DIM + MLA_V_DIM)
    wk = wkv[:, :, :MLA_NOPE_DIM].reshape(MLA_KV_RANK, MLA_HEADS * MLA_NOPE_DIM).astype(BF16)
    wvt = wkv[:, :, MLA_NOPE_DIM:].reshape(MLA_KV_RANK, MLA_HEADS * MLA_V_DIM).T.astype(BF16)
    cos, sin = _rope_table(seq, MLA_ROPE_DIM)
    zpad = jnp.zeros((seq, LANES - MLA_ROPE_DIM), F32)
    cosf = jnp.concatenate([cos, cos, zpad], axis=1)
    sinf = jnp.concatenate([-sin, sin, zpad], axis=1)
    q_p, k_p, vt = _mla_prep(h, q_norm, kv_norm, wq_p, wk, wvt, cosf, sinf, o1, seq, tm=512)
    b_out, expert_wb = _mla_attention(q_p, k_p, vt, batch, seq, tq=min(1024, seq), tk=1024, sub=256,
                                      cast_ws=expert_ws, cast_layer=layer)
    w_out_b = w_out.astype(BF16)
    outs = _proj_ln([a_out, b_out], [w_out_b[:NA_WIDTH], w_out_b[NA_WIDTH:]], xres, ln_g, ln_b, tm=512, nk=1)
    return outs, expert_wb


def _mixer_c(xb, xres, batch, seq, w_in, log_rate_f, log_rate_b, w_out, ln_g, ln_b, expert_ws, layer):
    cosr, sinr = _rope_table(seq, RET_QK_DIM)
    n_q = RET_HEADS * RET_QK_DIM
    hc, expert_wb = _matmul_rope(xb, w_in.astype(BF16), cosr, sinr, BF16, tm=min(1024, seq), tn=1024,
                                 n_q_cols=n_q, n_rope_cols=2 * n_q, gate_col0=2 * n_q + RET_HEADS * RET_V_DIM,
                                 head_w=RET_QK_DIM, q_scale=RET_QK_DIM ** -0.5,
                                 cast_ws=expert_ws, cast_layer=layer, cast_j=8)
    lg = jnp.stack([jnp.log1p(-jnp.exp(log_rate_f.astype(F32))), jnp.log1p(-jnp.exp(log_rate_b.astype(F32)))])
    r = _retention(hc, lg, batch, seq, c_len=256, group=4)
    return _proj_ln([r], [w_out.astype(BF16)], xres, ln_g, ln_b, tm=512, nk=2), expert_wb


def kernel(x, p, ab_w_in, ab_rpb, ab_q_norm, ab_w_uq, ab_kv_norm, ab_w_ukv, ab_w_out, c_w_in, c_log_rate_f,
           c_log_rate_b, c_w_out, ln1_g, ln1_b, moe_w_group, moe_b_group, moe_w_router, moe_b_router,
           moe_w_gate, moe_w_up, moe_w_down, ple_w_proj, ple_w_gate, ple_b_gate, ln2_g, ln2_b):
    batch, seq, d = x.shape
    n = batch * seq
    xf = x.reshape(n, d)
    p_flat = p.reshape(DEPTH, n, -1)
    xb = None
    expert_ws = (moe_w_gate, moe_w_up, moe_w_down)
    for i in range(DEPTH):
        j = i // 2
        if i % 2 == 0:
            src = xf if xb is None else xb
            (xf, xb, xpk), (wg, wu, wd) = _mixer_ab(
                src, xf, batch, seq, ab_w_in[j], ab_rpb[j], ab_q_norm[j], ab_w_uq[j], ab_kv_norm[j], ab_w_ukv[j],
                ab_w_out[j], ln1_g[i], ln1_b[i], expert_ws, i)
        else:
            (xf, xb, xpk), (wg, wu, wd) = _mixer_c(
                xb, xf, batch, seq, c_w_in[j], c_log_rate_f[j], c_log_rate_b[j], c_w_out[j], ln1_g[i], ln1_b[i],
                expert_ws, i)
        xf, xb = _moe_layer(xf, xb, xpk, p_flat, i, moe_w_group[i], moe_b_group[i], moe_w_router[i],
                            moe_b_router[i], wg, wu, wd, ple_w_proj[i], ple_w_gate[i], ple_b_gate[i],
                            ln2_g[i], ln2_b[i])
    return xf.reshape(batch, seq, d)
```

```python
import functools
import math

import numpy as np
import jax
import jax.numpy as jnp
from jax import lax
from jax.experimental import pallas as pl
from jax.experimental.pallas import tpu as pltpu

DEPTH = 2
GRID_W = 64
NA_HEADS = 8
NA_HEAD_DIM = 128
NA_WIN_H = 8
NA_WIN_W = 16
MLA_HEADS = 8
MLA_Q_RANK = 512
MLA_KV_RANK = 256
MLA_NOPE_DIM = 128
MLA_ROPE_DIM = 64
MLA_V_DIM = 128
RET_HEADS = 8
RET_QK_DIM = 256
RET_V_DIM = 512
RET_CHUNK = 128
N_GROUPS = 4
EXPERTS_PER_GROUP = 8
N_EXPERTS = N_GROUPS * EXPERTS_PER_GROUP
D_EXPERT = 512
MOE_BLOCK = 128
ROPE_BASE = 10000.0
LN_EPS = 1e-5
RMS_EPS = 1e-6
DN_ALPHA = (2 * DEPTH) ** 0.25
NA_WIDTH = NA_HEADS * NA_HEAD_DIM

LANES = 128
SUBLANES = 8
VMEM_LIMIT_BYTES = 60 * 1024 * 1024
MASK_VALUE = -1e30

F32 = jnp.float32
BF16 = jnp.bfloat16
I32 = jnp.int32
U32 = jnp.uint32


def _cparams(sem):
    return pltpu.CompilerParams(dimension_semantics=sem, vmem_limit_bytes=VMEM_LIMIT_BYTES)


def _dot(a, b):
    return jnp.dot(a, b, preferred_element_type=F32)


def _dot_nt(a, b, precision=None):
    return lax.dot_general(a, b, (((1,), (1,)), ((), ())), preferred_element_type=F32,
                           precision=precision)


def _pack_halves(y):
    c = y.shape[1] // 2
    bits = pltpu.bitcast(y.astype(BF16).astype(F32), U32)
    return (bits[:, :c] >> 16) | (bits[:, c:] & jnp.uint32(0xFFFF0000))


def _unpack_halves(w):
    lo = pltpu.bitcast(w << 16, F32)
    hi = pltpu.bitcast(w & jnp.uint32(0xFFFF0000), F32)
    return lo, hi


def _store_row_tiles(ref, packed):
    m = packed.shape[0]
    for s in range(SUBLANES):
        ref[pl.ds(s, m, stride=SUBLANES), :] = packed[:, s * LANES:(s + 1) * LANES]


def _load_row_tiles(ref, m):
    return jnp.concatenate([ref[pl.ds(s, m, stride=SUBLANES), :] for s in range(SUBLANES)], axis=-1)


def _mm_kernel(x_ref, w_ref, *rest, n_cast):
    cast_src, o_ref, cast_dst = rest[:n_cast], rest[n_cast], rest[n_cast + 1:]
    _cast_blocks(cast_src, cast_dst)
    o_ref[...] = _dot(x_ref[...].astype(BF16), w_ref[...]).astype(o_ref.dtype)


def _matmul(x, w, out_dtype, tm, tn, cast_ws):
    m, k = x.shape
    n = w.shape[1]
    nj = n // tn
    streams = [_cast_rows_specs(cw, lead, (m // tm) * nj, lambda i, j: i * nj + j) for cw, lead in cast_ws]
    outs = pl.pallas_call(
        functools.partial(_mm_kernel, n_cast=len(cast_ws)),
        grid=(m // tm, nj),
        in_specs=[pl.BlockSpec((tm, k), lambda i, j: (i, 0)),
                  pl.BlockSpec((k, tn), lambda i, j: (0, j))] + [s[0] for s in streams],
        out_specs=[pl.BlockSpec((tm, tn), lambda i, j: (i, j))] + [s[1] for s in streams],
        out_shape=[jax.ShapeDtypeStruct((m, n), out_dtype)] + [s[2] for s in streams],
        compiler_params=_cparams(("arbitrary", "arbitrary")),
        name="matmul",
    )(x, w, *[cw for cw, _ in cast_ws])
    return outs[0], outs[1:]


def _mm_rope_kernel(x_ref, w_ref, cos_ref, sin_ref, *rest, n_q_tiles, n_rope_tiles, first_gate_tile,
                    head_w, q_scale, n_cast):
    cast_src, o_ref, cast_dst = rest[:n_cast], rest[n_cast], rest[n_cast + 1:]
    j = pl.program_id(1)
    _cast_blocks(cast_src, cast_dst)
    acc = _dot(x_ref[...].astype(BF16), w_ref[...])

    @pl.when((j >= n_rope_tiles) & (j < first_gate_tile))
    def _():
        o_ref[...] = acc.astype(o_ref.dtype)

    @pl.when(j >= first_gate_tile)
    def _():
        o_ref[...] = (acc * jax.nn.sigmoid(acc)).astype(o_ref.dtype)

    @pl.when(j < n_rope_tiles)
    def _():
        scale = jnp.where(j < n_q_tiles, q_scale, 1.0)
        cos = cos_ref[...] * scale
        sin = sin_ref[...] * scale
        half = head_w // 2
        for c0 in range(0, acc.shape[1], head_w):
            x1 = acc[:, c0:c0 + half]
            x2 = acc[:, c0 + half:c0 + head_w]
            o_ref[:, c0:c0 + half] = (x1 * cos - x2 * sin).astype(o_ref.dtype)
            o_ref[:, c0 + half:c0 + head_w] = (x2 * cos + x1 * sin).astype(o_ref.dtype)


def _matmul_rope(x, w, cos, sin, out_dtype, tm, tn, n_q_cols, n_rope_cols, gate_col0, head_w, q_scale,
                 cast_ws, cast_layer, cast_j):
    m, k = x.shape
    n = w.shape[1]
    nsb = cos.shape[0] // tm
    streams = [_cast_stream_specs(cw, cast_layer, (m // tm) * cast_j,
                                  lambda i, j: i * cast_j + jnp.minimum(j, cast_j - 1)) for cw in cast_ws]
    outs = pl.pallas_call(
        functools.partial(_mm_rope_kernel, n_q_tiles=n_q_cols // tn, n_rope_tiles=n_rope_cols // tn,
                          first_gate_tile=gate_col0 // tn, head_w=head_w, q_scale=q_scale,
                          n_cast=len(cast_ws)),
        grid=(m // tm, n // tn),
        in_specs=[pl.BlockSpec((tm, k), lambda i, j: (i, 0)),
                  pl.BlockSpec((k, tn), lambda i, j: (0, j)),
                  pl.BlockSpec((tm, head_w // 2), lambda i, j: (i % nsb, 0)),
                  pl.BlockSpec((tm, head_w // 2), lambda i, j: (i % nsb, 0))] + [s[0] for s in streams],
        out_specs=[pl.BlockSpec((tm, tn), lambda i, j: (i, j))] + [s[1] for s in streams],
        out_shape=[jax.ShapeDtypeStruct((m, n), out_dtype)] + [s[2] for s in streams],
        compiler_params=_cparams(("arbitrary", "arbitrary")),
        name="matmul_rope",
    )(x, w, cos, sin, *cast_ws)
    return outs[0], outs[1:]


def _layer_norm_rows(z, g, b):
    mean = jnp.mean(z, axis=-1, keepdims=True)
    zc = z - mean
    var = jnp.mean(zc * zc, axis=-1, keepdims=True)
    return zc * lax.rsqrt(var + LN_EPS) * g + b


def _proj_ln_kernel(*refs, n_act, nk):
    acts = refs[:n_act]
    ws = refs[n_act:2 * n_act]
    x_ref, g_ref, b_ref, y_ref, yb_ref, yp_ref = refs[2 * n_act:2 * n_act + 6]
    k = pl.program_id(1)
    tm = x_ref.shape[0]
    n_split = 2
    hm = tm // n_split

    def product(rows):
        part = _dot(acts[0][rows, :], ws[0][...])
        for a, w in zip(acts[1:], ws[1:]):
            part = part + _dot(a[rows, :], w[...])
        return part

    def finish(rows, h, proj):
        z = DN_ALPHA * x_ref[rows, :] + proj
        y = _layer_norm_rows(z, g_ref[...], b_ref[...])
        y_ref[rows, :] = y
        yb_ref[rows, :] = y.astype(BF16)
        _store_row_tiles(yp_ref.at[pl.ds(h * hm * SUBLANES, hm * SUBLANES)], _pack_halves(y))

    if nk == 1:
        for h in range(n_split):
            rows = pl.ds(h * hm, hm)
            finish(rows, h, product(rows))
        return
    acc_ref = refs[2 * n_act + 6]

    @pl.when(k == 0)
    def _():
        acc_ref[...] = product(pl.ds(0, tm))

    @pl.when((k > 0) & (k < nk - 1))
    def _():
        acc_ref[...] = acc_ref[...] + product(pl.ds(0, tm))

    @pl.when(k == nk - 1)
    def _():
        for h in range(n_split):
            rows = pl.ds(h * hm, hm)
            finish(rows, h, acc_ref[rows, :] + product(rows))


def _proj_ln(acts, ws, x, g, b, tm, nk):
    m, d = x.shape
    n_act = len(acts)
    in_specs = []
    for a in acts:
        kk = a.shape[1] // nk
        in_specs.append(pl.BlockSpec((tm, kk), lambda i, k: (i, k)))
    for w in ws:
        kk = w.shape[0] // nk
        mode = pl.Buffered(1) if nk == 1 else None
        in_specs.append(pl.BlockSpec((kk, d), lambda i, k: (k, 0), pipeline_mode=mode))
    in_specs += [pl.BlockSpec((tm, d), lambda i, k: (i, 0)),
                 pl.BlockSpec((1, d), lambda i, k: (0, 0)),
                 pl.BlockSpec((1, d), lambda i, k: (0, 0))]
    return pl.pallas_call(
        functools.partial(_proj_ln_kernel, n_act=n_act, nk=nk),
        grid=(m // tm, nk),
        in_specs=in_specs,
        out_specs=[pl.BlockSpec((tm, d), lambda i, k: (i, 0)),
                   pl.BlockSpec((tm, d), lambda i, k: (i, 0)),
                   pl.BlockSpec((tm * SUBLANES, LANES), lambda i, k: (i, 0))],
        out_shape=[jax.ShapeDtypeStruct((m, d), F32), jax.ShapeDtypeStruct((m, d), BF16),
                   jax.ShapeDtypeStruct((m * SUBLANES, LANES), U32)],
        scratch_shapes=[pltpu.VMEM((tm, d), F32)] if nk > 1 else [],
        compiler_params=_cparams(("parallel", "arbitrary")),
        name="proj_ln",
    )(*acts, *ws, x, g.reshape(1, d), b.reshape(1, d))


def _na_bias_tables(rpb):
    nh = rpb.shape[0]
    c = np.arange(GRID_W)
    cs = np.clip(c - NA_WIN_W // 2, 0, GRID_W - NA_WIN_W)
    kc = np.arange(GRID_W)
    valid = (kc[None, :] >= cs[:, None]) & (kc[None, :] < cs[:, None] + NA_WIN_W)
    dc = kc[None, :] - c[:, None] + NA_WIN_W - 1
    onehot = (dc[:, :, None] == np.arange(2 * NA_WIN_W - 1)[None, None, :]) & valid[:, :, None]
    cols = jnp.einsum("hrd,ckd->hrck", rpb.astype(F32), jnp.asarray(onehot, F32),
                      precision=lax.Precision.HIGHEST)
    cols = jnp.where(jnp.asarray(valid)[None, None], cols, MASK_VALUE)
    tabs = jnp.stack([cols[:, off:off + NA_WIN_H] for off in range(NA_WIN_H)], axis=1)
    return tabs.transpose(0, 1, 3, 2, 4).reshape(nh, NA_WIN_H, GRID_W, NA_WIN_H * GRID_W)


def _na_kernel(q_ref, k_ref, v_ref, bias_ref, o_ref, *, rows, group):
    scale = NA_HEAD_DIM ** -0.5
    nkeys = NA_WIN_H * GRID_W

    def body(i, carry):
        geom, scores = [], []
        for u in range(group):
            r = i * group + u
            rs = jnp.clip(r - NA_WIN_H // 2, 0, rows - NA_WIN_H)
            off = rs - r + NA_WIN_H - 1
            q0 = pl.multiple_of(r * GRID_W, GRID_W)
            k0 = pl.multiple_of(rs * GRID_W, GRID_W)
            geom.append((q0, k0))
            s = _dot_nt(q_ref[pl.ds(q0, GRID_W), :], k_ref[pl.ds(k0, nkeys), :])
            scores.append(s * scale + bias_ref[0, off])
        for (q0, k0), s in zip(geom, scores):
            m = jnp.max(s, axis=-1, keepdims=True)
            p = jnp.exp(s - m)
            l = jnp.sum(p, axis=-1, keepdims=True)
            o = _dot(p.astype(BF16), v_ref[pl.ds(k0, nkeys), :]) / l
            o_ref[pl.ds(q0, GRID_W), :] = o.astype(o_ref.dtype)
        return carry

    lax.fori_loop(0, rows // group, body, 0)


def _na_attention(h, bias_tables, batch, seq):
    rows = seq // GRID_W
    d = NA_HEAD_DIM
    nkeys = NA_WIN_H * GRID_W
    return pl.pallas_call(
        functools.partial(_na_kernel, rows=rows, group=16),
        grid=(batch, NA_HEADS),
        in_specs=[pl.BlockSpec((seq, d), lambda b, hh: (b, hh)),
                  pl.BlockSpec((seq, d), lambda b, hh: (b, NA_HEADS + hh)),
                  pl.BlockSpec((seq, d), lambda b, hh: (b, 2 * NA_HEADS + hh)),
                  pl.BlockSpec((1, NA_WIN_H, GRID_W, nkeys), lambda b, hh: (hh, 0, 0, 0))],
        out_specs=pl.BlockSpec((seq, d), lambda b, hh: (b, hh)),
        out_shape=jax.ShapeDtypeStruct((batch * seq, NA_WIDTH), BF16),
        compiler_params=_cparams(("parallel", "arbitrary")),
        name="na_attention",
    )(h, h, h, bias_tables)


def _rms_rows(x, g):
    return x * lax.rsqrt(jnp.mean(x * x, axis=-1, keepdims=True) + RMS_EPS) * g


def _rope_lanes(t, cosf, sinf):
    return t * cosf + pltpu.roll(t, LANES // 2, 1) * sinf


def _mla_prep_kernel(cq_ref, ckv_ref, kr_ref, gq_ref, gkv_ref, wq_ref, wk_ref, wvt_ref, cos_ref, sin_ref,
                     q_ref, k_ref, vt_ref):
    dq = MLA_NOPE_DIM + MLA_ROPE_DIM
    cosf = cos_ref[...]
    sinf = sin_ref[...]
    cqn = _rms_rows(cq_ref[...].astype(F32), gq_ref[...]).astype(BF16)
    ckvn = _rms_rows(ckv_ref[...].astype(F32), gkv_ref[...]).astype(BF16)
    qf = _dot(cqn, wq_ref[...]) * (dq ** -0.5 * math.log2(math.e))
    kf = _dot(ckvn, wk_ref[...])
    vt_ref[...] = _dot_nt(wvt_ref[...], ckvn).astype(BF16)
    kpe = _rope_lanes(kr_ref[...].astype(F32), cosf, sinf).astype(BF16)
    for hh in range(MLA_HEADS):
        c0 = hh * 2 * LANES
        q_ref[:, c0:c0 + LANES] = qf[:, c0:c0 + LANES].astype(BF16)
        q_ref[:, c0 + LANES:c0 + 2 * LANES] = _rope_lanes(qf[:, c0 + LANES:c0 + 2 * LANES], cosf, sinf).astype(BF16)
        k_ref[:, c0:c0 + LANES] = kf[:, hh * LANES:(hh + 1) * LANES].astype(BF16)
        k_ref[:, c0 + LANES:c0 + 2 * LANES] = kpe


def _mla_prep(h, gq, gkv, wq_p, wk, wvt, cosf, sinf, col_cq, seq, tm):
    n = h.shape[0]
    hw = MLA_HEADS * 2 * LANES
    nsb = seq // tm
    b_cq = col_cq // MLA_Q_RANK
    b_ckv = (col_cq + MLA_Q_RANK) // MLA_KV_RANK
    b_kr = (col_cq + MLA_Q_RANK + MLA_KV_RANK) // LANES
    return pl.pallas_call(
        _mla_prep_kernel,
        grid=(n // tm,),
        in_specs=[pl.BlockSpec((tm, MLA_Q_RANK), lambda i: (i, b_cq)),
                  pl.BlockSpec((tm, MLA_KV_RANK), lambda i: (i, b_ckv)),
                  pl.BlockSpec((tm, LANES), lambda i: (i, b_kr)),
                  pl.BlockSpec((1, MLA_Q_RANK), lambda i: (0, 0)),
                  pl.BlockSpec((1, MLA_KV_RANK), lambda i: (0, 0)),
                  pl.BlockSpec((MLA_Q_RANK, hw), lambda i: (0, 0)),
                  pl.BlockSpec((MLA_KV_RANK, MLA_HEADS * LANES), lambda i: (0, 0)),
                  pl.BlockSpec((MLA_HEADS * MLA_V_DIM, MLA_KV_RANK), lambda i: (0, 0)),
                  pl.BlockSpec((tm, LANES), lambda i: (i % nsb, 0)),
                  pl.BlockSpec((tm, LANES), lambda i: (i % nsb, 0))],
        out_specs=[pl.BlockSpec((tm, hw), lambda i: (i, 0)),
                   pl.BlockSpec((tm, hw), lambda i: (i, 0)),
                   pl.BlockSpec((MLA_HEADS * MLA_V_DIM, tm), lambda i: (0, i))],
        out_shape=[jax.ShapeDtypeStruct((n, hw), BF16), jax.ShapeDtypeStruct((n, hw), BF16),
                   jax.ShapeDtypeStruct((MLA_HEADS * MLA_V_DIM, n), BF16)],
        compiler_params=_cparams(("parallel",)),
        name="mla_prep",
    )(h, h, h, gq.reshape(1, -1), gkv.reshape(1, -1), wq_p, wk, wvt, cosf, sinf)


def _cast_stream_specs(w, layer, n_slots, slot_of):
    _, ne, rows, cols = w.shape
    if n_slots >= ne:
        e_per, rb = 1, n_slots // ne
        while rows % rb or (rows // rb) % (2 * SUBLANES):
            rb -= 1
    else:
        assert ne % n_slots == 0
        e_per, rb = ne // n_slots, 1
    n_blocks = (ne // e_per) * rb

    def block_of(*g):
        s = jnp.minimum(slot_of(*g), n_blocks - 1)
        return s // rb, s % rb

    src = pl.BlockSpec((1, e_per, rows // rb, cols), lambda *g: (layer, *block_of(*g), 0))
    dst = pl.BlockSpec((e_per, rows // rb, cols), lambda *g: (*block_of(*g), 0))
    return src, dst, jax.ShapeDtypeStruct((ne, rows, cols), BF16), n_blocks


def _cast_rows_specs(w, lead, n_slots, slot_of):
    rows, cols = w.shape[-2:]
    nb = min(n_slots, rows // (2 * SUBLANES))
    while rows % nb or (rows // nb) % (2 * SUBLANES):
        nb -= 1
    src = pl.BlockSpec((1, rows // nb, cols), lambda *g: (lead, jnp.minimum(slot_of(*g), nb - 1), 0))
    dst = pl.BlockSpec((rows // nb, cols), lambda *g: (jnp.minimum(slot_of(*g), nb - 1), 0))
    return src, dst, jax.ShapeDtypeStruct((rows, cols), BF16), nb


def _cast_blocks(srcs, dsts):
    for src, dst in zip(srcs, dsts):
        dst[...] = src[0].astype(BF16)


def _mla_attn_kernel(q_ref, k_ref, vt_ref, *rest, tk, sub, n_cast):
    cast_src, o_ref, cast_dst = rest[:n_cast], rest[n_cast], rest[n_cast + 1:]
    _cast_blocks(cast_src, cast_dst)
    nchunk = k_ref.shape[0] // tk
    tq = q_ref.shape[0]
    nsub = tq // sub
    qs = [q_ref[s * sub:(s + 1) * sub, :] for s in range(nsub)]
    m = [jnp.full((1, sub), MASK_VALUE, F32) for _ in range(nsub)]
    l = [jnp.zeros((1, sub), F32) for _ in range(nsub)]
    acc = [jnp.zeros((MLA_V_DIM, sub), F32) for _ in range(nsub)]

    def scores(s, c):
        return _dot_nt(k_ref[c * tk:(c + 1) * tk, :], qs[s])

    st_next = [scores(s, 0) for s in range(nsub)]
    for c in range(nchunk):
        for s in range(nsub):
            st = st_next[s]
            m_new = jnp.maximum(m[s], jnp.max(st, axis=0, keepdims=True))
            a = jnp.exp2(m[s] - m_new)
            p = jnp.exp2(st - m_new)
            l[s] = a * l[s] + jnp.sum(p, axis=0, keepdims=True)
            if c + 1 < nchunk:
                st_next[s] = scores(s, c + 1)
            acc[s] = a * acc[s] + _dot(vt_ref[:, c * tk:(c + 1) * tk], p.astype(BF16))
            m[s] = m_new
    for s in range(nsub):
        o_ref[s * sub:(s + 1) * sub, :] = (acc[s] / l[s]).T.astype(o_ref.dtype)


def _mla_attention(q_p, k_p, vt, batch, seq, tq, tk, sub, cast_ws, cast_layer, cast_dense):
    n = q_p.shape[0]
    nqb = seq // tq
    n_slots = batch * MLA_HEADS * nqb

    def slot_of(b, hh, i):
        return (b * MLA_HEADS + hh) * nqb + i

    streams = [_cast_stream_specs(w, cast_layer, n_slots, slot_of) for w in cast_ws]
    streams += [_cast_rows_specs(w, lead, n_slots, slot_of) for w, lead in cast_dense]
    cast_ws = list(cast_ws) + [w for w, _ in cast_dense]
    outs = pl.pallas_call(
        functools.partial(_mla_attn_kernel, tk=tk, sub=sub, n_cast=len(cast_ws)),
        grid=(batch, MLA_HEADS, nqb),
        in_specs=[pl.BlockSpec((tq, 2 * LANES), lambda b, hh, i: (b * nqb + i, hh)),
                  pl.BlockSpec((seq, 2 * LANES), lambda b, hh, i: (b, hh)),
                  pl.BlockSpec((MLA_V_DIM, seq), lambda b, hh, i: (hh, b))] + [s[0] for s in streams],
        out_specs=[pl.BlockSpec((tq, MLA_V_DIM), lambda b, hh, i: (b * nqb + i, hh))] + [s[1] for s in streams],
        out_shape=[jax.ShapeDtypeStruct((n, MLA_HEADS * MLA_V_DIM), BF16)] + [s[2] for s in streams],
        compiler_params=_cparams(("arbitrary", "arbitrary", "arbitrary")),
        name="mla_attention",
    )(q_p, k_p, vt, *cast_ws)
    return outs[0], outs[1:]


def _ret_kernel(lg_ref, q_ref, k_ref, v_ref, g_ref, o_ref, acc_ref, st_ref, *, c_len, group):
    nchunk = q_ref.shape[0] // c_len
    hh = pl.program_id(1)
    lgf = lg_ref[0, hh]
    lgb = lg_ref[1, hh]
    ii = lax.broadcasted_iota(I32, (c_len, c_len), 0).astype(F32)
    jj = lax.broadcasted_iota(I32, (c_len, c_len), 1).astype(F32)
    rel = ii - jj
    dmat = jnp.where(rel >= 0, jnp.exp(lgf * jnp.maximum(rel, 0.0)), jnp.exp(lgb * jnp.maximum(-rel, 0.0)))
    pos = lax.broadcasted_iota(I32, (c_len, 1), 0).astype(F32)
    qdec_f = jnp.exp(lgf * (pos + 1.0))
    kdec_f = jnp.exp(lgf * (c_len - 1.0 - pos))
    qdec_b = jnp.exp(lgb * (c_len - pos))
    kdec_b = jnp.exp(lgb * pos)
    full_chunk = jnp.full((1, RET_V_DIM), float(c_len), F32)
    cdec_f = jnp.exp(lgf * full_chunk)
    cdec_b = jnp.exp(lgb * full_chunk)

    def decayed_keys_t(t0, kdec):
        return (k_ref[pl.ds(t0, c_len), :].astype(F32) * kdec).T.astype(BF16)

    st_ref[...] = jnp.zeros_like(st_ref)

    def bwd_body(i, carry):
        t0s = [pl.multiple_of((nchunk - 1 - (i * group + u)) * c_len, c_len) for u in range(group)]
        upd = [_dot(decayed_keys_t(t0, kdec_b), v_ref[pl.ds(t0, c_len), :]) for t0 in t0s]
        for t0, u_c in zip(t0s, upd):
            st = st_ref[...]
            acc_ref[pl.ds(t0, c_len), :] = _dot(q_ref[pl.ds(t0, c_len), :], st.astype(BF16)) * qdec_b
            st_ref[...] = st * cdec_b + u_c
        return carry

    lax.fori_loop(0, nchunk // group, bwd_body, 0)
    st_ref[...] = jnp.zeros_like(st_ref)

    def fwd_body(i, carry):
        t0s = [pl.multiple_of((i * group + u) * c_len, c_len) for u in range(group)]
        scs = [_dot_nt(q_ref[pl.ds(t0, c_len), :], k_ref[pl.ds(t0, c_len), :]) * dmat for t0 in t0s]
        upd = [_dot(decayed_keys_t(t0, kdec_f), v_ref[pl.ds(t0, c_len), :]) for t0 in t0s]
        for t0, sc, u_c in zip(t0s, scs, upd):
            st = st_ref[...]
            r = (_dot(sc.astype(BF16), v_ref[pl.ds(t0, c_len), :])
                 + _dot(q_ref[pl.ds(t0, c_len), :], st.astype(BF16)) * qdec_f
                 + acc_ref[pl.ds(t0, c_len), :])
            st_ref[...] = st * cdec_f + u_c
            r = r - jnp.mean(r, axis=-1, keepdims=True)
            r = r * lax.rsqrt(jnp.mean(r * r, axis=-1, keepdims=True) + LN_EPS)
            o_ref[pl.ds(t0, c_len), :] = (g_ref[pl.ds(t0, c_len), :].astype(F32) * r).astype(o_ref.dtype)
        return carry

    lax.fori_loop(0, nchunk // group, fwd_body, 0)


def _retention(hc, lg, batch, seq, c_len, group):
    dk, dv, nh = RET_QK_DIM, RET_V_DIM, RET_HEADS
    v_blk0 = (2 * nh * dk) // dv
    return pl.pallas_call(
        functools.partial(_ret_kernel, c_len=c_len, group=group),
        grid=(batch, nh),
        in_specs=[pl.BlockSpec(memory_space=pltpu.SMEM),
                  pl.BlockSpec((seq, dk), lambda b, hh: (b, hh)),
                  pl.BlockSpec((seq, dk), lambda b, hh: (b, nh + hh)),
                  pl.BlockSpec((seq, dv), lambda b, hh: (b, v_blk0 + hh)),
                  pl.BlockSpec((seq, dv), lambda b, hh: (b, v_blk0 + nh + hh))],
        out_specs=pl.BlockSpec((seq, dv), lambda b, hh: (b, hh)),
        scratch_shapes=[pltpu.VMEM((seq, dv), F32), pltpu.VMEM((dk, dv), F32)],
        out_shape=jax.ShapeDtypeStruct((batch * seq, nh * dv), BF16),
        compiler_params=_cparams(("parallel", "arbitrary")),
        name="retention",
    )(lg, hc, hc, hc, hc)


ROUTER_ROWS = 40


def _router_kernel(x_ref, wt_ref, b_ref, tri_ref, ids_ref, wts_ref, cnt_ref, carry_ref):
    i = pl.program_id(0)
    tm = x_ref.shape[0]

    @pl.when(i == 0)
    def _():
        carry_ref[...] = jnp.zeros_like(carry_ref)

    logits = _dot_nt(wt_ref[...], x_ref[...]) + b_ref[...]
    grow = lax.broadcasted_iota(I32, (SUBLANES, tm), 0).astype(F32)
    gl = jnp.where(grow < N_GROUPS, logits[0:SUBLANES], MASK_VALUE)
    gmax = jnp.max(gl, axis=0, keepdims=True)
    gsum = jnp.sum(jnp.exp(gl - gmax), axis=0, keepdims=True)
    p_group = 1.0 / gsum
    g_idx = jnp.min(jnp.where(gl == gmax, grow, float(N_GROUPS)), axis=0, keepdims=True)
    sel = jnp.zeros((EXPERTS_PER_GROUP, tm), F32)
    for g in range(N_GROUPS):
        r0 = SUBLANES + g * EXPERTS_PER_GROUP
        sel = sel + jnp.where(g_idx == float(g), logits[r0:r0 + EXPERTS_PER_GROUP], 0.0)
    erow = lax.broadcasted_iota(I32, (EXPERTS_PER_GROUP, tm), 0).astype(F32)
    smax = jnp.max(sel, axis=0, keepdims=True)
    sexp = jnp.exp(sel - smax)
    probs = sexp / jnp.sum(sexp, axis=0, keepdims=True)
    p1 = jnp.max(probs, axis=0, keepdims=True)
    i1 = jnp.min(jnp.where(probs == p1, erow, float(EXPERTS_PER_GROUP)), axis=0, keepdims=True)
    rest = jnp.where(erow == i1, -1.0, probs)
    p2 = jnp.max(rest, axis=0, keepdims=True)
    i2 = jnp.min(jnp.where(rest == p2, erow, float(EXPERTS_PER_GROUP)), axis=0, keepdims=True)
    denom = p1 + p2
    e0 = g_idx * EXPERTS_PER_GROUP + i1
    e1 = g_idx * EXPERTS_PER_GROUP + i2

    xrow = lax.broadcasted_iota(I32, (N_EXPERTS, tm), 0).astype(F32)
    oh0 = jnp.where(xrow == e0, 1.0, 0.0)
    oh1 = jnp.where(xrow == e1, 1.0, 0.0)
    onehot = oh0 + oh1
    before = _dot(onehot.astype(BF16), tri_ref[...]) + carry_ref[:, 0:1]
    rank0 = jnp.sum(oh0 * before, axis=0, keepdims=True)
    rank1 = jnp.sum(oh1 * before, axis=0, keepdims=True)
    carry_ref[...] = carry_ref[...] + jnp.sum(onehot, axis=1, keepdims=True)

    ids_ref[...] = jnp.zeros_like(ids_ref)
    ids_ref[0:1, :] = e0.astype(I32)
    ids_ref[1:2, :] = e1.astype(I32)
    ids_ref[2:3, :] = rank0.astype(I32)
    ids_ref[3:4, :] = rank1.astype(I32)
    wts_ref[...] = jnp.zeros_like(wts_ref)
    wts_ref[0:1, :] = p_group * p1 / denom
    wts_ref[1:2, :] = p_group * p2 / denom
    cnt_ref[...] = carry_ref[...]


def _router(x, wt, bias, tm):
    n, d = x.shape
    tri = jnp.asarray(np.triu(np.ones((tm, tm), np.float32), 1), BF16)
    return pl.pallas_call(
        _router_kernel,
        grid=(n // tm,),
        in_specs=[pl.BlockSpec((tm, d), lambda i: (i, 0)),
                  pl.BlockSpec((ROUTER_ROWS, d), lambda i: (0, 0)),
                  pl.BlockSpec((ROUTER_ROWS, 1), lambda i: (0, 0)),
                  pl.BlockSpec((tm, tm), lambda i: (0, 0))],
        out_specs=[pl.BlockSpec((SUBLANES, tm), lambda i: (0, i)),
                   pl.BlockSpec((SUBLANES, tm), lambda i: (0, i)),
                   pl.BlockSpec((N_EXPERTS, LANES), lambda i: (0, 0))],
        out_shape=[jax.ShapeDtypeStruct((SUBLANES, n), I32), jax.ShapeDtypeStruct((SUBLANES, n), F32),
                   jax.ShapeDtypeStruct((N_EXPERTS, LANES), F32)],
        scratch_shapes=[pltpu.VMEM((N_EXPERTS, LANES), F32)],
        compiler_params=_cparams(("arbitrary",)),
        name="moe_router",
    )(x, wt, bias, tri)


def _slots_kernel(ids_ref, cnt_ref, slots_ref, blk_ref, *, nblk_pad):
    tm = ids_ref.shape[1]
    cnt = cnt_ref[:, 0:1]
    padded = jnp.floor((cnt + (MOE_BLOCK - 1)) / MOE_BLOCK) * MOE_BLOCK
    er = lax.broadcasted_iota(I32, (N_EXPERTS, N_EXPERTS), 0)
    ec = lax.broadcasted_iota(I32, (N_EXPERTS, N_EXPERTS), 1)
    padded_row = jnp.sum(jnp.where(er == ec, padded, 0.0), axis=0, keepdims=True)
    p_start = jnp.sum(jnp.where(ec < er, padded_row, 0.0), axis=1, keepdims=True)
    p_end = p_start + padded
    xrow = lax.broadcasted_iota(I32, (N_EXPERTS, tm), 0)
    e0 = ids_ref[0:1, :]
    e1 = ids_ref[1:2, :]
    s0 = jnp.sum(jnp.where(xrow == e0, p_start, 0.0), axis=0, keepdims=True).astype(I32) + ids_ref[2:3, :]
    s1 = jnp.sum(jnp.where(xrow == e1, p_start, 0.0), axis=0, keepdims=True).astype(I32) + ids_ref[3:4, :]
    slots_ref[...] = jnp.zeros_like(slots_ref)
    slots_ref[0:1, :] = s0
    slots_ref[1:2, :] = s1
    bstart = (lax.broadcasted_iota(I32, (1, nblk_pad), 1) * MOE_BLOCK).astype(F32)
    blk_e = jnp.minimum(jnp.sum(jnp.where(p_end <= bstart, 1.0, 0.0), axis=0, keepdims=True), N_EXPERTS - 1.0)
    total = jnp.sum(padded, axis=0, keepdims=True)
    erow = lax.broadcasted_iota(I32, (N_EXPERTS, nblk_pad), 0).astype(F32)
    own_end = jnp.sum(jnp.where(erow == blk_e, p_end, 0.0), axis=0, keepdims=True)
    nxt_e = jnp.minimum(jnp.sum(jnp.where(p_end <= own_end, 1.0, 0.0), axis=0, keepdims=True), N_EXPERTS - 1.0)
    nxt_e = jnp.where(own_end < total, nxt_e, -1.0)
    blk_ref[...] = jnp.zeros_like(blk_ref)
    blk_ref[0:1, :] = blk_e.astype(I32)
    blk_ref[1:2, :] = jnp.broadcast_to((total / MOE_BLOCK).astype(I32), (1, nblk_pad))
    blk_ref[2:3, :] = nxt_e.astype(I32)
    lane = lax.broadcasted_iota(I32, (N_EXPERTS, nblk_pad), 1).astype(F32)
    blk_ref[3:4, :] = jnp.sum(jnp.where(erow == lane, p_start + cnt, 0.0), axis=0, keepdims=True).astype(I32)
    blk_ref[4:5, :] = jnp.sum(jnp.where(erow == lane, p_end, 0.0), axis=0, keepdims=True).astype(I32)


def _slots(ids, cnt, tm, nblk_pad):
    n = ids.shape[1]
    return pl.pallas_call(
        functools.partial(_slots_kernel, nblk_pad=nblk_pad),
        grid=(n // tm,),
        in_specs=[pl.BlockSpec((SUBLANES, tm), lambda i: (0, i)),
                  pl.BlockSpec((N_EXPERTS, LANES), lambda i: (0, 0))],
        out_specs=[pl.BlockSpec((SUBLANES, tm), lambda i: (0, i)),
                   pl.BlockSpec((SUBLANES, nblk_pad), lambda i: (0, 0))],
        out_shape=[jax.ShapeDtypeStruct((SUBLANES, n), I32), jax.ShapeDtypeStruct((SUBLANES, nblk_pad), I32)],
        compiler_params=_cparams(("arbitrary",)),
        name="moe_slots",
    )(ids, cnt)


def _slot_tokens_kernel(slots_ref, blk_ref, tok_ref, *, n, cap, nblk_pad):
    def zero(j, carry):
        tok_ref[j] = 0
        return carry

    def scatter(t, carry):
        tok_ref[slots_ref[t]] = t
        tok_ref[slots_ref[n + t]] = t
        return carry

    for e in range(N_EXPERTS):
        lax.fori_loop(blk_ref[3 * nblk_pad + e], blk_ref[4 * nblk_pad + e], zero, 0)
    lax.fori_loop(blk_ref[nblk_pad] * MOE_BLOCK, cap, zero, 0)
    lax.fori_loop(0, n, scatter, 0, unroll=8)


def _slot_tokens(slots_flat, blk_flat, n, cap, nblk_pad):
    return pl.pallas_call(
        functools.partial(_slot_tokens_kernel, n=n, cap=cap, nblk_pad=nblk_pad),
        grid_spec=pltpu.PrefetchScalarGridSpec(
            num_scalar_prefetch=2,
            grid=(1,),
            in_specs=[],
            out_specs=pl.BlockSpec(memory_space=pltpu.SMEM)),
        out_shape=jax.ShapeDtypeStruct((cap,), I32),
        compiler_params=_cparams(("arbitrary",)),
        name="moe_slot_tokens",
    )(slots_flat, blk_flat)


ROW_BUFS = 3


def _expert_kernel(blk_ref, tok_ref, xpk_hbm, wg_hbm, wu_hbm, wd_hbm, y_ref, xbuf, wgb, wub, wdb,
                   sems, wsems, cnt_ref, *, nblk_pad):
    i = pl.program_id(0)
    n_used = blk_ref[nblk_pad]

    def weight_copies(e, slot):
        return [pltpu.make_async_copy(src.at[e], dst.at[slot], wsems.at[slot])
                for src, dst in ((wg_hbm, wgb), (wu_hbm, wub), (wd_hbm, wdb))]

    def start_rows(block, buf, r_lo=0, r_hi=MOE_BLOCK):
        for r in range(r_lo, r_hi):
            src0 = pl.multiple_of(tok_ref[block * MOE_BLOCK + r] * SUBLANES, SUBLANES)
            pltpu.make_async_copy(xpk_hbm.at[pl.ds(src0, SUBLANES)],
                                  xbuf.at[buf, pl.ds(r * SUBLANES, SUBLANES)], sems.at[buf]).start(priority=r % 2)

    def wait_rows(buf):
        pltpu.make_async_copy(xpk_hbm.at[pl.ds(0, MOE_BLOCK * SUBLANES)], xbuf.at[buf], sems.at[buf]).wait()

    @pl.when(i == 0)
    def _():
        cnt_ref[0] = 0
        for cp in weight_copies(blk_ref[0], 0):
            cp.start(priority=1)
        for ahead in range(ROW_BUFS - 1):
            start_rows(jnp.minimum(ahead, n_used - 1), ahead)

    @pl.when((i < n_used) & ((i == 0) | (blk_ref[i] != blk_ref[jnp.maximum(i - 1, 0)])))
    def _():
        slot = cnt_ref[0] % 2
        cnt_ref[0] = cnt_ref[0] + 1
        for cp in weight_copies(blk_ref[i], slot):
            cp.wait()
        nxt_e = blk_ref[2 * nblk_pad + i]

        @pl.when(nxt_e >= 0)
        def _():
            for cp in weight_copies(nxt_e, 1 - slot):
                cp.start(priority=1)

    @pl.when(i < n_used)
    def _():
        wslot = (cnt_ref[0] + 1) % 2
        buf = i % ROW_BUFS
        nbuf = (i + ROW_BUFS - 1) % ROW_BUFS
        nxt = jnp.minimum(i + ROW_BUFS - 1, n_used - 1)
        wait_rows(buf)
        lo, hi = _unpack_halves(_load_row_tiles(xbuf.at[buf], MOE_BLOCK))
        xb = jnp.concatenate([lo.astype(BF16), hi.astype(BF16)], axis=-1)
        g = _dot(xb, wgb[wslot])
        start_rows(nxt, nbuf, 0, MOE_BLOCK // 2)
        u = _dot(xb, wub[wslot])
        start_rows(nxt, nbuf, MOE_BLOCK // 2, MOE_BLOCK)
        hmid = (g * jax.nn.sigmoid(g) * u).astype(BF16)
        _store_row_tiles(y_ref, _pack_halves(_dot(hmid, wdb[wslot])))

    @pl.when(i == n_used - 1)
    def _():
        for ahead in range(1, ROW_BUFS):
            wait_rows((i + ahead) % ROW_BUFS)

    @pl.when(i >= n_used)
    def _():
        y_ref[...] = jnp.zeros_like(y_ref)


def _experts(blk_flat, slot_tok, xpk, wg, wu, wd, nblk_pad):
    d, de = wg.shape[1], wg.shape[2]
    assert d == 2 * SUBLANES * LANES and xpk.shape[1] == LANES
    cap = slot_tok.shape[0]
    tile_rows = MOE_BLOCK * SUBLANES
    return pl.pallas_call(
        functools.partial(_expert_kernel, nblk_pad=nblk_pad),
        grid_spec=pltpu.PrefetchScalarGridSpec(
            num_scalar_prefetch=2,
            grid=(cap // MOE_BLOCK,),
            in_specs=[pl.BlockSpec(memory_space=pl.ANY), pl.BlockSpec(memory_space=pl.ANY),
                      pl.BlockSpec(memory_space=pl.ANY), pl.BlockSpec(memory_space=pl.ANY)],
            out_specs=pl.BlockSpec((tile_rows, LANES), lambda i, blk, tok: (i, 0)),
            scratch_shapes=[pltpu.VMEM((ROW_BUFS, tile_rows, LANES), U32),
                            pltpu.VMEM((2, d, de), BF16), pltpu.VMEM((2, d, de), BF16), pltpu.VMEM((2, de, d), BF16),
                            pltpu.SemaphoreType.DMA((ROW_BUFS,)), pltpu.SemaphoreType.DMA((2,)),
                            pltpu.SMEM((1,), I32)]),
        out_shape=jax.ShapeDtypeStruct((cap * SUBLANES, LANES), U32),
        compiler_params=_cparams(("arbitrary",)),
        name="moe_experts",
    )(blk_flat, slot_tok, xpk, wg, wu, wd)


def _tail_kernel(slots_ref, x_ref, xb_ref, p_ref, wts_ref, wgate_ref, bgate_ref, wproj_ref, g_ref, b_ref,
                 yb_hbm, y_ref, ybf_ref, rows_ref, sems):
    i = pl.program_id(0)
    nsteps = pl.num_programs(0)
    tm = x_ref.shape[0]
    n = nsteps * tm

    def start_rows(step, buf):
        for t in range(tm):
            for which in range(2):
                src0 = pl.multiple_of(slots_ref[which * n + step * tm + t] * SUBLANES, SUBLANES)
                pltpu.make_async_copy(yb_hbm.at[pl.ds(src0, SUBLANES)],
                                      rows_ref.at[buf, which, pl.ds(t * SUBLANES, SUBLANES)], sems.at[buf]).start()

    def wait_rows(buf):
        for which in range(2):
            pltpu.make_async_copy(yb_hbm.at[pl.ds(0, tm * SUBLANES)], rows_ref.at[buf, which],
                                  sems.at[buf]).wait()

    @pl.when(i == 0)
    def _():
        start_rows(0, 0)

    buf = i % 2
    nxt = jnp.minimum(i + 1, nsteps - 1)
    wait_rows(buf)
    gate_pre = _dot(xb_ref[...], wgate_ref[...])
    proj = _dot(p_ref[0].astype(BF16), wproj_ref[...])
    start_rows(nxt, 1 - buf)
    ple = jax.nn.sigmoid(gate_pre + bgate_ref[...]) * proj
    w = wts_ref[...]
    lo0, hi0 = _unpack_halves(_load_row_tiles(rows_ref.at[buf, 0], tm))
    lo1, hi1 = _unpack_halves(_load_row_tiles(rows_ref.at[buf, 1], tm))
    w0 = w[:, 0:1]
    w1 = w[:, 1:2]
    ffn = jnp.concatenate([lo0 * w0 + lo1 * w1, hi0 * w0 + hi1 * w1], axis=-1)
    z = DN_ALPHA * x_ref[...] + ffn + ple
    y = _layer_norm_rows(z, g_ref[...], b_ref[...])
    y_ref[...] = y
    ybf_ref[...] = y.astype(BF16)

    @pl.when(i == nsteps - 1)
    def _():
        wait_rows(1 - buf)


def _layer_tail(slots_flat, x, xb, p, layer, wts_t, wgate, bgate, wproj, g, b, yb, tm):
    n, d = x.shape
    pd = p.shape[2]
    return pl.pallas_call(
        _tail_kernel,
        grid_spec=pltpu.PrefetchScalarGridSpec(
            num_scalar_prefetch=1,
            grid=(n // tm,),
            in_specs=[pl.BlockSpec((tm, d), lambda i, s: (i, 0)),
                      pl.BlockSpec((tm, d), lambda i, s: (i, 0)),
                      pl.BlockSpec((1, tm, pd), lambda i, s: (layer, i, 0)),
                      pl.BlockSpec((tm, 2), lambda i, s: (i, 0)),
                      pl.BlockSpec((d, d), lambda i, s: (0, 0)),
                      pl.BlockSpec((1, d), lambda i, s: (0, 0)),
                      pl.BlockSpec((pd, d), lambda i, s: (0, 0)),
                      pl.BlockSpec((1, d), lambda i, s: (0, 0)),
                      pl.BlockSpec((1, d), lambda i, s: (0, 0)),
                      pl.BlockSpec(memory_space=pl.ANY)],
            out_specs=[pl.BlockSpec((tm, d), lambda i, s: (i, 0)),
                       pl.BlockSpec((tm, d), lambda i, s: (i, 0))],
            scratch_shapes=[pltpu.VMEM((2, 2, tm * SUBLANES, LANES), U32), pltpu.SemaphoreType.DMA((2,))]),
        out_shape=[jax.ShapeDtypeStruct((n, d), F32), jax.ShapeDtypeStruct((n, d), BF16)],
        compiler_params=_cparams(("arbitrary",)),
        name="layer_tail",
    )(slots_flat, x, xb, p, wts_t, wgate, bgate.reshape(1, d), wproj, g.reshape(1, d), b.reshape(1, d), yb)


def _rope_table(seq, dim):
    pos = jnp.arange(seq, dtype=F32)
    inv = jnp.exp(jnp.arange(0, dim, 2, dtype=F32) * (-math.log(ROPE_BASE) / dim))
    ang = pos[:, None] * inv[None, :]
    return jnp.cos(ang), jnp.sin(ang)


def _moe_layer(x, xb, xpk, p, layer, w_group, b_group, w_router, b_router, w_gate, w_up, w_down,
               ple_w_proj, ple_w_gate, ple_b_gate, ln_g, ln_b):
    n, d = x.shape
    nblk = -(-(2 * n) // MOE_BLOCK) + N_EXPERTS
    nblk_pad = -(-nblk // LANES) * LANES
    cap = nblk * MOE_BLOCK
    wt = jnp.zeros((ROUTER_ROWS, d), F32)
    wt = wt.at[0:N_GROUPS].set(w_group.T)
    wt = wt.at[SUBLANES:].set(w_router.transpose(0, 2, 1).reshape(N_EXPERTS, d))
    bias = jnp.zeros((ROUTER_ROWS, 1), F32)
    bias = bias.at[0:N_GROUPS, 0].set(b_group)
    bias = bias.at[SUBLANES:, 0].set(b_router.reshape(N_EXPERTS))
    ids, wts, cnt = _router(xb, wt.astype(BF16), bias, tm=512)
    slots, blk = _slots(ids, cnt, tm=min(2048, n), nblk_pad=nblk_pad)
    slots_flat = slots[0:2].reshape(2 * n)
    blk_flat = blk[0:5].reshape(5 * nblk_pad)
    slot_tok = _slot_tokens(slots_flat, blk_flat, n, cap, nblk_pad)
    yb = _experts(blk_flat, slot_tok, xpk, w_gate, w_up, w_down, nblk_pad)
    return _layer_tail(slots_flat, x, xb, p, layer, wts[0:2].T, ple_w_gate, ple_b_gate, ple_w_proj, ln_g, ln_b,
                       yb, tm=256)


def _mixer_ab(x, xres, batch, seq, w_in, rpb, q_norm, w_uq, kv_norm, w_ukv, w_out_all, j, ln_g, ln_b,
              expert_ws, layer, dense_early, dense_late):
    d = x.shape[1]
    o1 = 3 * NA_WIDTH
    o2 = o1 + MLA_Q_RANK
    o3 = o2 + MLA_KV_RANK
    half = MLA_ROPE_DIM // 2
    kr = w_in[:, o3:o3 + MLA_ROPE_DIM]
    kr_sw = jnp.concatenate([kr[:, half:], kr[:, :half]], axis=1)
    width = -(-(o3 + 2 * MLA_ROPE_DIM) // 1024) * 1024
    w_in_p = jnp.concatenate([w_in, kr_sw, jnp.zeros((d, width - o3 - 2 * MLA_ROPE_DIM), F32)], axis=1)
    h, early_b = _matmul(x, w_in_p.astype(BF16), BF16, tm=512, tn=1024, cast_ws=dense_early)
    a_out = _na_attention(h, _na_bias_tables(rpb), batch, seq)
    dq = MLA_NOPE_DIM + MLA_ROPE_DIM
    wq = w_uq.reshape(MLA_Q_RANK, MLA_HEADS, dq)
    wq_pe = wq[:, :, MLA_NOPE_DIM:]
    wq_p = jnp.concatenate([wq, wq_pe[:, :, half:], wq_pe[:, :, :half]], axis=2)
    wq_p = wq_p.reshape(MLA_Q_RANK, MLA_HEADS * 2 * LANES).astype(BF16)
    wkv = w_ukv.reshape(MLA_KV_RANK, MLA_HEADS, MLA_NOPE_DIM + MLA_V_DIM)
    wk = wkv[:, :, :MLA_NOPE_DIM].reshape(MLA_KV_RANK, MLA_HEADS * MLA_NOPE_DIM).astype(BF16)
    wvt = wkv[:, :, MLA_NOPE_DIM:].reshape(MLA_KV_RANK, MLA_HEADS * MLA_V_DIM).T.astype(BF16)
    cos, sin = _rope_table(seq, MLA_ROPE_DIM)
    zpad = jnp.zeros((seq, LANES - MLA_ROPE_DIM), F32)
    cosf = jnp.concatenate([cos, cos, zpad], axis=1)
    sinf = jnp.concatenate([-sin, sin, zpad], axis=1)
    q_p, k_p, vt = _mla_prep(h, q_norm, kv_norm, wq_p, wk, wvt, cosf, sinf, o1, seq, tm=512)
    b_out, cast_out = _mla_attention(q_p, k_p, vt, batch, seq, tq=min(1024, seq), tk=1024, sub=256,
                                     cast_ws=expert_ws, cast_layer=layer,
                                     cast_dense=[(w_out_all, j)] + list(dense_late))
    expert_wb, w_out_b, late_b = cast_out[:len(expert_ws)], cast_out[len(expert_ws)], cast_out[len(expert_ws) + 1:]
    outs = _proj_ln([a_out, b_out], [w_out_b[:NA_WIDTH], w_out_b[NA_WIDTH:]], xres, ln_g, ln_b, tm=512, nk=1)
    return outs, expert_wb, early_b, late_b


def _mixer_c(xb, xres, batch, seq, w_in, log_rate_f, log_rate_b, w_out, ln_g, ln_b, expert_ws, layer):
    cosr, sinr = _rope_table(seq, RET_QK_DIM)
    n_q = RET_HEADS * RET_QK_DIM
    hc, expert_wb = _matmul_rope(xb, w_in, cosr, sinr, BF16, tm=min(1024, seq), tn=1024,
                                 n_q_cols=n_q, n_rope_cols=2 * n_q, gate_col0=2 * n_q + RET_HEADS * RET_V_DIM,
                                 head_w=RET_QK_DIM, q_scale=RET_QK_DIM ** -0.5,
                                 cast_ws=expert_ws, cast_layer=layer, cast_j=8)
    lg = jnp.stack([jnp.log1p(-jnp.exp(log_rate_f.astype(F32))), jnp.log1p(-jnp.exp(log_rate_b.astype(F32)))])
    r = _retention(hc, lg, batch, seq, c_len=256, group=4)
    return _proj_ln([r], [w_out], xres, ln_g, ln_b, tm=512, nk=1), expert_wb


def kernel(x, p, ab_w_in, ab_rpb, ab_q_norm, ab_w_uq, ab_kv_norm, ab_w_ukv, ab_w_out, c_w_in, c_log_rate_f,
           c_log_rate_b, c_w_out, ln1_g, ln1_b, moe_w_group, moe_b_group, moe_w_router, moe_b_router,
           moe_w_gate, moe_w_up, moe_w_down, ple_w_proj, ple_w_gate, ple_b_gate, ln2_g, ln2_b):
    batch, seq, d = x.shape
    n = batch * seq
    xf = x.reshape(n, d)
    p_flat = p.reshape(DEPTH, n, -1)
    expert_ws = (moe_w_gate, moe_w_up, moe_w_down)
    assert DEPTH == 2
    dense_late = [(c_w_out, 0)] + [(w, i) for i in range(DEPTH) for w in (ple_w_gate, ple_w_proj)]
    (xf, xb, xpk), (wg, wu, wd), (c_w_in_b,), late_b = _mixer_ab(
        xf, xf, batch, seq, ab_w_in[0], ab_rpb[0], ab_q_norm[0], ab_w_uq[0], ab_kv_norm[0], ab_w_ukv[0],
        ab_w_out, 0, ln1_g[0], ln1_b[0], expert_ws, 0, dense_early=[(c_w_in, 0)], dense_late=dense_late)
    c_w_out_b, ple_b = late_b[0], late_b[1:]
    xf, xb = _moe_layer(xf, xb, xpk, p_flat, 0, moe_w_group[0], moe_b_group[0], moe_w_router[0],
                        moe_b_router[0], wg, wu, wd, ple_b[1], ple_b[0], ple_b_gate[0], ln2_g[0], ln2_b[0])
    (xf, xb, xpk), (wg, wu, wd) = _mixer_c(
        xb, xf, batch, seq, c_w_in_b, c_log_rate_f[0], c_log_rate_b[0], c_w_out_b, ln1_g[1], ln1_b[1],
        expert_ws, 1)
    xf, xb = _moe_layer(xf, xb, xpk, p_flat, 1, moe_w_group[1], moe_b_group[1], moe_w_router[1],
                        moe_b_router[1], wg, wu, wd, ple_b[3], ple_b[2], ple_b_gate[1], ln2_g[1], ln2_b[1])
    return xf.reshape(batch, seq, d)
```

```python
import functools
import math

import numpy as np
import jax
import jax.numpy as jnp
from jax import lax
from jax.experimental import pallas as pl
from jax.experimental.pallas import tpu as pltpu

DEPTH = 2
GRID_W = 64
NA_HEADS = 8
NA_HEAD_DIM = 128
NA_WIN_H = 8
NA_WIN_W = 16
MLA_HEADS = 8
MLA_Q_RANK = 512
MLA_KV_RANK = 256
MLA_NOPE_DIM = 128
MLA_ROPE_DIM = 64
MLA_V_DIM = 128
RET_HEADS = 8
RET_QK_DIM = 256
RET_V_DIM = 512
RET_CHUNK = 128
N_GROUPS = 4
EXPERTS_PER_GROUP = 8
N_EXPERTS = N_GROUPS * EXPERTS_PER_GROUP
D_EXPERT = 512
MOE_BLOCK = 128
ROPE_BASE = 10000.0
LN_EPS = 1e-5
RMS_EPS = 1e-6
DN_ALPHA = (2 * DEPTH) ** 0.25
NA_WIDTH = NA_HEADS * NA_HEAD_DIM

LANES = 128
SUBLANES = 8
VMEM_LIMIT_BYTES = 60 * 1024 * 1024
MASK_VALUE = -1e30

F32 = jnp.float32
BF16 = jnp.bfloat16
I32 = jnp.int32
U32 = jnp.uint32


def _cparams(sem):
    return pltpu.CompilerParams(dimension_semantics=sem, vmem_limit_bytes=VMEM_LIMIT_BYTES)


def _dot(a, b):
    return jnp.dot(a, b, preferred_element_type=F32)


def _dot_nt(a, b, precision=None):
    return lax.dot_general(a, b, (((1,), (1,)), ((), ())), preferred_element_type=F32,
                           precision=precision)


def _pack_halves(y):
    c = y.shape[1] // 2
    bits = pltpu.bitcast(y.astype(BF16).astype(F32), U32)
    return (bits[:, :c] >> 16) | (bits[:, c:] & jnp.uint32(0xFFFF0000))


def _unpack_halves(w):
    lo = pltpu.bitcast(w << 16, F32)
    hi = pltpu.bitcast(w & jnp.uint32(0xFFFF0000), F32)
    return lo, hi


def _store_row_tiles(ref, packed):
    m = packed.shape[0]
    for s in range(SUBLANES):
        ref[pl.ds(s, m, stride=SUBLANES), :] = packed[:, s * LANES:(s + 1) * LANES]


def _load_row_tiles(ref, m):
    return jnp.concatenate([ref[pl.ds(s, m, stride=SUBLANES), :] for s in range(SUBLANES)], axis=-1)


def _mm_kernel(x_ref, w_ref, *rest, n_cast):
    cast_src, o_ref, cast_dst = rest[:n_cast], rest[n_cast], rest[n_cast + 1:]
    _cast_blocks(cast_src, cast_dst)
    o_ref[...] = _dot(x_ref[...].astype(BF16), w_ref[...]).astype(o_ref.dtype)


def _matmul(x, w, out_dtype, tm, tn, cast_ws):
    m, k = x.shape
    n = w.shape[1]
    nj = n // tn
    streams = [_cast_rows_specs(cw, lead, (m // tm) * nj, lambda i, j: i * nj + j) for cw, lead in cast_ws]
    outs = pl.pallas_call(
        functools.partial(_mm_kernel, n_cast=len(cast_ws)),
        grid=(m // tm, nj),
        in_specs=[pl.BlockSpec((tm, k), lambda i, j: (i, 0)),
                  pl.BlockSpec((k, tn), lambda i, j: (0, j))] + [s[0] for s in streams],
        out_specs=[pl.BlockSpec((tm, tn), lambda i, j: (i, j))] + [s[1] for s in streams],
        out_shape=[jax.ShapeDtypeStruct((m, n), out_dtype)] + [s[2] for s in streams],
        compiler_params=_cparams(("arbitrary", "arbitrary")),
        name="matmul",
    )(x, w, *[cw for cw, _ in cast_ws])
    return outs[0], outs[1:]


def _mm_rope_kernel(x_ref, w_ref, cos_ref, sin_ref, *rest, n_q_tiles, n_rope_tiles, first_gate_tile,
                    head_w, q_scale, n_cast):
    cast_src, o_ref, cast_dst = rest[:n_cast], rest[n_cast], rest[n_cast + 1:]
    j = pl.program_id(1)
    _cast_blocks(cast_src, cast_dst)
    acc = _dot(x_ref[...].astype(BF16), w_ref[...])

    @pl.when((j >= n_rope_tiles) & (j < first_gate_tile))
    def _():
        o_ref[...] = acc.astype(o_ref.dtype)

    @pl.when(j >= first_gate_tile)
    def _():
        o_ref[...] = (acc * jax.nn.sigmoid(acc)).astype(o_ref.dtype)

    @pl.when(j < n_rope_tiles)
    def _():
        scale = jnp.where(j < n_q_tiles, q_scale, 1.0)
        cos = cos_ref[...] * scale
        sin = sin_ref[...] * scale
        half = head_w // 2
        for c0 in range(0, acc.shape[1], head_w):
            x1 = acc[:, c0:c0 + half]
            x2 = acc[:, c0 + half:c0 + head_w]
            o_ref[:, c0:c0 + half] = (x1 * cos - x2 * sin).astype(o_ref.dtype)
            o_ref[:, c0 + half:c0 + head_w] = (x2 * cos + x1 * sin).astype(o_ref.dtype)


def _matmul_rope(x, w, cos, sin, out_dtype, tm, tn, n_q_cols, n_rope_cols, gate_col0, head_w, q_scale,
                 cast_ws, cast_layer, cast_j):
    m, k = x.shape
    n = w.shape[1]
    nsb = cos.shape[0] // tm
    streams = [_cast_stream_specs(cw, cast_layer, (m // tm) * cast_j,
                                  lambda i, j: i * cast_j + jnp.minimum(j, cast_j - 1)) for cw in cast_ws]
    outs = pl.pallas_call(
        functools.partial(_mm_rope_kernel, n_q_tiles=n_q_cols // tn, n_rope_tiles=n_rope_cols // tn,
                          first_gate_tile=gate_col0 // tn, head_w=head_w, q_scale=q_scale,
                          n_cast=len(cast_ws)),
        grid=(m // tm, n // tn),
        in_specs=[pl.BlockSpec((tm, k), lambda i, j: (i, 0)),
                  pl.BlockSpec((k, tn), lambda i, j: (0, j)),
                  pl.BlockSpec((tm, head_w // 2), lambda i, j: (i % nsb, 0)),
                  pl.BlockSpec((tm, head_w // 2), lambda i, j: (i % nsb, 0))] + [s[0] for s in streams],
        out_specs=[pl.BlockSpec((tm, tn), lambda i, j: (i, j))] + [s[1] for s in streams],
        out_shape=[jax.ShapeDtypeStruct((m, n), out_dtype)] + [s[2] for s in streams],
        compiler_params=_cparams(("arbitrary", "arbitrary")),
        name="matmul_rope",
    )(x, w, cos, sin, *cast_ws)
    return outs[0], outs[1:]


def _layer_norm_rows(z, g, b):
    mean = jnp.mean(z, axis=-1, keepdims=True)
    zc = z - mean
    var = jnp.mean(zc * zc, axis=-1, keepdims=True)
    return zc * lax.rsqrt(var + LN_EPS) * g + b


def _proj_ln_kernel(*refs, n_act, nk):
    acts = refs[:n_act]
    ws = refs[n_act:2 * n_act]
    x_ref, g_ref, b_ref, y_ref, yb_ref, yp_ref = refs[2 * n_act:2 * n_act + 6]
    k = pl.program_id(1)
    tm = x_ref.shape[0]
    n_split = 2
    hm = tm // n_split

    def product(rows):
        part = _dot(acts[0][rows, :], ws[0][...])
        for a, w in zip(acts[1:], ws[1:]):
            part = part + _dot(a[rows, :], w[...])
        return part

    def finish(rows, h, proj):
        z = DN_ALPHA * x_ref[rows, :] + proj
        y = _layer_norm_rows(z, g_ref[...], b_ref[...])
        y_ref[rows, :] = y
        yb_ref[rows, :] = y.astype(BF16)
        _store_row_tiles(yp_ref.at[pl.ds(h * hm * SUBLANES, hm * SUBLANES)], _pack_halves(y))

    if nk == 1:
        for h in range(n_split):
            rows = pl.ds(h * hm, hm)
            finish(rows, h, product(rows))
        return
    acc_ref = refs[2 * n_act + 6]

    @pl.when(k == 0)
    def _():
        acc_ref[...] = product(pl.ds(0, tm))

    @pl.when((k > 0) & (k < nk - 1))
    def _():
        acc_ref[...] = acc_ref[...] + product(pl.ds(0, tm))

    @pl.when(k == nk - 1)
    def _():
        for h in range(n_split):
            rows = pl.ds(h * hm, hm)
            finish(rows, h, acc_ref[rows, :] + product(rows))


def _proj_ln(acts, ws, x, g, b, tm, nk):
    m, d = x.shape
    n_act = len(acts)
    in_specs = []
    for a in acts:
        kk = a.shape[1] // nk
        in_specs.append(pl.BlockSpec((tm, kk), lambda i, k: (i, k)))
    for w in ws:
        kk = w.shape[0] // nk
        mode = pl.Buffered(1) if nk == 1 else None
        in_specs.append(pl.BlockSpec((kk, d), lambda i, k: (k, 0), pipeline_mode=mode))
    in_specs += [pl.BlockSpec((tm, d), lambda i, k: (i, 0)),
                 pl.BlockSpec((1, d), lambda i, k: (0, 0)),
                 pl.BlockSpec((1, d), lambda i, k: (0, 0))]
    return pl.pallas_call(
        functools.partial(_proj_ln_kernel, n_act=n_act, nk=nk),
        grid=(m // tm, nk),
        in_specs=in_specs,
        out_specs=[pl.BlockSpec((tm, d), lambda i, k: (i, 0)),
                   pl.BlockSpec((tm, d), lambda i, k: (i, 0)),
                   pl.BlockSpec((tm * SUBLANES, LANES), lambda i, k: (i, 0))],
        out_shape=[jax.ShapeDtypeStruct((m, d), F32), jax.ShapeDtypeStruct((m, d), BF16),
                   jax.ShapeDtypeStruct((m * SUBLANES, LANES), U32)],
        scratch_shapes=[pltpu.VMEM((tm, d), F32)] if nk > 1 else [],
        compiler_params=_cparams(("parallel", "arbitrary")),
        name="proj_ln",
    )(*acts, *ws, x, g.reshape(1, d), b.reshape(1, d))


def _na_bias_tables(rpb):
    nh = rpb.shape[0]
    c = np.arange(GRID_W)
    cs = np.clip(c - NA_WIN_W // 2, 0, GRID_W - NA_WIN_W)
    kc = np.arange(GRID_W)
    valid = (kc[None, :] >= cs[:, None]) & (kc[None, :] < cs[:, None] + NA_WIN_W)
    dc = kc[None, :] - c[:, None] + NA_WIN_W - 1
    onehot = (dc[:, :, None] == np.arange(2 * NA_WIN_W - 1)[None, None, :]) & valid[:, :, None]
    cols = jnp.einsum("hrd,ckd->hrck", rpb.astype(F32), jnp.asarray(onehot, F32),
                      precision=lax.Precision.HIGHEST)
    cols = jnp.where(jnp.asarray(valid)[None, None], cols, MASK_VALUE)
    tabs = jnp.stack([cols[:, off:off + NA_WIN_H] for off in range(NA_WIN_H)], axis=1)
    return tabs.transpose(0, 1, 3, 2, 4).reshape(nh, NA_WIN_H, GRID_W, NA_WIN_H * GRID_W)


def _na_kernel(q_ref, k_ref, v_ref, bias_ref, o_ref, *, rows, group):
    scale = NA_HEAD_DIM ** -0.5
    nkeys = NA_WIN_H * GRID_W

    def body(i, carry):
        geom, scores = [], []
        for u in range(group):
            r = i * group + u
            rs = jnp.clip(r - NA_WIN_H // 2, 0, rows - NA_WIN_H)
            off = rs - r + NA_WIN_H - 1
            q0 = pl.multiple_of(r * GRID_W, GRID_W)
            k0 = pl.multiple_of(rs * GRID_W, GRID_W)
            geom.append((q0, k0))
            s = _dot_nt(q_ref[pl.ds(q0, GRID_W), :], k_ref[pl.ds(k0, nkeys), :])
            scores.append(s * scale + bias_ref[0, off])
        for (q0, k0), s in zip(geom, scores):
            m = jnp.max(s, axis=-1, keepdims=True)
            p = jnp.exp(s - m)
            l = jnp.sum(p, axis=-1, keepdims=True)
            o = _dot(p.astype(BF16), v_ref[pl.ds(k0, nkeys), :]) / l
            o_ref[pl.ds(q0, GRID_W), :] = o.astype(o_ref.dtype)
        return carry

    lax.fori_loop(0, rows // group, body, 0)


def _na_attention(h, bias_tables, batch, seq):
    rows = seq // GRID_W
    d = NA_HEAD_DIM
    nkeys = NA_WIN_H * GRID_W
    return pl.pallas_call(
        functools.partial(_na_kernel, rows=rows, group=16),
        grid=(batch, NA_HEADS),
        in_specs=[pl.BlockSpec((seq, d), lambda b, hh: (b, hh)),
                  pl.BlockSpec((seq, d), lambda b, hh: (b, NA_HEADS + hh)),
                  pl.BlockSpec((seq, d), lambda b, hh: (b, 2 * NA_HEADS + hh)),
                  pl.BlockSpec((1, NA_WIN_H, GRID_W, nkeys), lambda b, hh: (hh, 0, 0, 0))],
        out_specs=pl.BlockSpec((seq, d), lambda b, hh: (b, hh)),
        out_shape=jax.ShapeDtypeStruct((batch * seq, NA_WIDTH), BF16),
        compiler_params=_cparams(("parallel", "arbitrary")),
        name="na_attention",
    )(h, h, h, bias_tables)


def _rms_rows(x, g):
    return x * lax.rsqrt(jnp.mean(x * x, axis=-1, keepdims=True) + RMS_EPS) * g


def _rope_lanes(t, cosf, sinf):
    return t * cosf + pltpu.roll(t, LANES // 2, 1) * sinf


def _mla_prep_kernel(cq_ref, ckv_ref, kr_ref, gq_ref, gkv_ref, wq_ref, wk_ref, wvt_ref, cos_ref, sin_ref,
                     q_ref, k_ref, vt_ref):
    dq = MLA_NOPE_DIM + MLA_ROPE_DIM
    cosf = cos_ref[...]
    sinf = sin_ref[...]
    cqn = _rms_rows(cq_ref[...].astype(F32), gq_ref[...]).astype(BF16)
    ckvn = _rms_rows(ckv_ref[...].astype(F32), gkv_ref[...]).astype(BF16)
    qf = _dot(cqn, wq_ref[...]) * (dq ** -0.5 * math.log2(math.e))
    kf = _dot(ckvn, wk_ref[...])
    vt_ref[...] = _dot_nt(wvt_ref[...], ckvn).astype(BF16)
    kpe = _rope_lanes(kr_ref[...].astype(F32), cosf, sinf).astype(BF16)
    for hh in range(MLA_HEADS):
        c0 = hh * 2 * LANES
        q_ref[:, c0:c0 + LANES] = qf[:, c0:c0 + LANES].astype(BF16)
        q_ref[:, c0 + LANES:c0 + 2 * LANES] = _rope_lanes(qf[:, c0 + LANES:c0 + 2 * LANES], cosf, sinf).astype(BF16)
        k_ref[:, c0:c0 + LANES] = kf[:, hh * LANES:(hh + 1) * LANES].astype(BF16)
        k_ref[:, c0 + LANES:c0 + 2 * LANES] = kpe


def _mla_prep(h, gq, gkv, wq_p, wk, wvt, cosf, sinf, col_cq, seq, tm):
    n = h.shape[0]
    hw = MLA_HEADS * 2 * LANES
    nsb = seq // tm
    b_cq = col_cq // MLA_Q_RANK
    b_ckv = (col_cq + MLA_Q_RANK) // MLA_KV_RANK
    b_kr = (col_cq + MLA_Q_RANK + MLA_KV_RANK) // LANES
    return pl.pallas_call(
        _mla_prep_kernel,
        grid=(n // tm,),
        in_specs=[pl.BlockSpec((tm, MLA_Q_RANK), lambda i: (i, b_cq)),
                  pl.BlockSpec((tm, MLA_KV_RANK), lambda i: (i, b_ckv)),
                  pl.BlockSpec((tm, LANES), lambda i: (i, b_kr)),
                  pl.BlockSpec((1, MLA_Q_RANK), lambda i: (0, 0)),
                  pl.BlockSpec((1, MLA_KV_RANK), lambda i: (0, 0)),
                  pl.BlockSpec((MLA_Q_RANK, hw), lambda i: (0, 0)),
                  pl.BlockSpec((MLA_KV_RANK, MLA_HEADS * LANES), lambda i: (0, 0)),
                  pl.BlockSpec((MLA_HEADS * MLA_V_DIM, MLA_KV_RANK), lambda i: (0, 0)),
                  pl.BlockSpec((tm, LANES), lambda i: (i % nsb, 0)),
                  pl.BlockSpec((tm, LANES), lambda i: (i % nsb, 0))],
        out_specs=[pl.BlockSpec((tm, hw), lambda i: (i, 0)),
                   pl.BlockSpec((tm, hw), lambda i: (i, 0)),
                   pl.BlockSpec((MLA_HEADS * MLA_V_DIM, tm), lambda i: (0, i))],
        out_shape=[jax.ShapeDtypeStruct((n, hw), BF16), jax.ShapeDtypeStruct((n, hw), BF16),
                   jax.ShapeDtypeStruct((MLA_HEADS * MLA_V_DIM, n), BF16)],
        compiler_params=_cparams(("parallel",)),
        name="mla_prep",
    )(h, h, h, gq.reshape(1, -1), gkv.reshape(1, -1), wq_p, wk, wvt, cosf, sinf)


def _cast_stream_specs(w, layer, n_slots, slot_of):
    _, ne, rows, cols = w.shape
    if n_slots >= ne:
        e_per, rb = 1, n_slots // ne
        while rows % rb or (rows // rb) % (2 * SUBLANES):
            rb -= 1
    else:
        assert ne % n_slots == 0
        e_per, rb = ne // n_slots, 1
    n_blocks = (ne // e_per) * rb

    def block_of(*g):
        s = jnp.minimum(slot_of(*g), n_blocks - 1)
        return s // rb, s % rb

    src = pl.BlockSpec((1, e_per, rows // rb, cols), lambda *g: (layer, *block_of(*g), 0))
    dst = pl.BlockSpec((e_per, rows // rb, cols), lambda *g: (*block_of(*g), 0))
    return src, dst, jax.ShapeDtypeStruct((ne, rows, cols), BF16), n_blocks


def _cast_rows_specs(w, lead, n_slots, slot_of):
    rows, cols = w.shape[-2:]
    nb = min(n_slots, rows // (2 * SUBLANES))
    while rows % nb or (rows // nb) % (2 * SUBLANES):
        nb -= 1
    src = pl.BlockSpec((1, rows // nb, cols), lambda *g: (lead, jnp.minimum(slot_of(*g), nb - 1), 0))
    dst = pl.BlockSpec((rows // nb, cols), lambda *g: (jnp.minimum(slot_of(*g), nb - 1), 0))
    return src, dst, jax.ShapeDtypeStruct((rows, cols), BF16), nb


def _cast_blocks(srcs, dsts):
    for src, dst in zip(srcs, dsts):
        dst[...] = src[0].astype(BF16)


def _mla_attn_kernel(q_ref, k_ref, vt_ref, *rest, tk, sub, n_cast):
    cast_src, o_ref, cast_dst = rest[:n_cast], rest[n_cast], rest[n_cast + 1:]
    _cast_blocks(cast_src, cast_dst)
    nchunk = k_ref.shape[0] // tk
    tq = q_ref.shape[0]
    nsub = tq // sub
    qs = [q_ref[s * sub:(s + 1) * sub, :] for s in range(nsub)]
    m = [jnp.full((1, sub), MASK_VALUE, F32) for _ in range(nsub)]
    l = [jnp.zeros((1, sub), F32) for _ in range(nsub)]
    acc = [jnp.zeros((MLA_V_DIM, sub), F32) for _ in range(nsub)]

    def scores(s, c):
        return _dot_nt(k_ref[c * tk:(c + 1) * tk, :], qs[s])

    st_next = [scores(s, 0) for s in range(nsub)]
    for c in range(nchunk):
        for s in range(nsub):
            st = st_next[s]
            m_new = jnp.maximum(m[s], jnp.max(st, axis=0, keepdims=True))
            a = jnp.exp2(m[s] - m_new)
            p = jnp.exp2(st - m_new)
            l[s] = a * l[s] + jnp.sum(p, axis=0, keepdims=True)
            if c + 1 < nchunk:
                st_next[s] = scores(s, c + 1)
            acc[s] = a * acc[s] + _dot(vt_ref[:, c * tk:(c + 1) * tk], p.astype(BF16))
            m[s] = m_new
    for s in range(nsub):
        o_ref[s * sub:(s + 1) * sub, :] = (acc[s] / l[s]).T.astype(o_ref.dtype)


def _mla_attention(q_p, k_p, vt, batch, seq, tq, tk, sub, cast_ws, cast_layer, cast_dense):
    n = q_p.shape[0]
    nqb = seq // tq
    n_slots = batch * MLA_HEADS * nqb

    def slot_of(b, hh, i):
        return (b * MLA_HEADS + hh) * nqb + i

    streams = [_cast_stream_specs(w, cast_layer, n_slots, slot_of) for w in cast_ws]
    streams += [_cast_rows_specs(w, lead, n_slots, slot_of) for w, lead in cast_dense]
    cast_ws = list(cast_ws) + [w for w, _ in cast_dense]
    outs = pl.pallas_call(
        functools.partial(_mla_attn_kernel, tk=tk, sub=sub, n_cast=len(cast_ws)),
        grid=(batch, MLA_HEADS, nqb),
        in_specs=[pl.BlockSpec((tq, 2 * LANES), lambda b, hh, i: (b * nqb + i, hh)),
                  pl.BlockSpec((seq, 2 * LANES), lambda b, hh, i: (b, hh)),
                  pl.BlockSpec((MLA_V_DIM, seq), lambda b, hh, i: (hh, b))] + [s[0] for s in streams],
        out_specs=[pl.BlockSpec((tq, MLA_V_DIM), lambda b, hh, i: (b * nqb + i, hh))] + [s[1] for s in streams],
        out_shape=[jax.ShapeDtypeStruct((n, MLA_HEADS * MLA_V_DIM), BF16)] + [s[2] for s in streams],
        compiler_params=_cparams(("arbitrary", "arbitrary", "arbitrary")),
        name="mla_attention",
    )(q_p, k_p, vt, *cast_ws)
    return outs[0], outs[1:]


def _ret_kernel(lg_ref, q_ref, k_ref, v_ref, g_ref, o_ref, acc_ref, st_ref, *, c_len, group):
    nchunk = q_ref.shape[0] // c_len
    hh = pl.program_id(1)
    lgf = lg_ref[0, hh]
    lgb = lg_ref[1, hh]
    ii = lax.broadcasted_iota(I32, (c_len, c_len), 0).astype(F32)
    jj = lax.broadcasted_iota(I32, (c_len, c_len), 1).astype(F32)
    rel = ii - jj
    dmat = jnp.where(rel >= 0, jnp.exp(lgf * jnp.maximum(rel, 0.0)), jnp.exp(lgb * jnp.maximum(-rel, 0.0)))
    pos = lax.broadcasted_iota(I32, (c_len, 1), 0).astype(F32)
    qdec_f = jnp.exp(lgf * (pos + 1.0))
    kdec_f = jnp.exp(lgf * (c_len - 1.0 - pos))
    qdec_b = jnp.exp(lgb * (c_len - pos))
    kdec_b = jnp.exp(lgb * pos)
    full_chunk = jnp.full((1, RET_V_DIM), float(c_len), F32)
    cdec_f = jnp.exp(lgf * full_chunk)
    cdec_b = jnp.exp(lgb * full_chunk)

    def decayed_keys_t(t0, kdec):
        return (k_ref[pl.ds(t0, c_len), :].astype(F32) * kdec).T.astype(BF16)

    st_ref[...] = jnp.zeros_like(st_ref)

    def bwd_body(i, carry):
        t0s = [pl.multiple_of((nchunk - 1 - (i * group + u)) * c_len, c_len) for u in range(group)]
        upd = [_dot(decayed_keys_t(t0, kdec_b), v_ref[pl.ds(t0, c_len), :]) for t0 in t0s]
        for t0, u_c in zip(t0s, upd):
            st = st_ref[...]
            acc_ref[pl.ds(t0, c_len), :] = _dot(q_ref[pl.ds(t0, c_len), :], st.astype(BF16)) * qdec_b
            st_ref[...] = st * cdec_b + u_c
        return carry

    lax.fori_loop(0, nchunk // group, bwd_body, 0)
    st_ref[...] = jnp.zeros_like(st_ref)

    def fwd_body(i, carry):
        t0s = [pl.multiple_of((i * group + u) * c_len, c_len) for u in range(group)]
        scs = [_dot_nt(q_ref[pl.ds(t0, c_len), :], k_ref[pl.ds(t0, c_len), :]) * dmat for t0 in t0s]
        upd = [_dot(decayed_keys_t(t0, kdec_f), v_ref[pl.ds(t0, c_len), :]) for t0 in t0s]
        for t0, sc, u_c in zip(t0s, scs, upd):
            st = st_ref[...]
            r = (_dot(sc.astype(BF16), v_ref[pl.ds(t0, c_len), :])
                 + _dot(q_ref[pl.ds(t0, c_len), :], st.astype(BF16)) * qdec_f
                 + acc_ref[pl.ds(t0, c_len), :])
            st_ref[...] = st * cdec_f + u_c
            r = r - jnp.mean(r, axis=-1, keepdims=True)
            r = r * lax.rsqrt(jnp.mean(r * r, axis=-1, keepdims=True) + LN_EPS)
            o_ref[pl.ds(t0, c_len), :] = (g_ref[pl.ds(t0, c_len), :].astype(F32) * r).astype(o_ref.dtype)
        return carry

    lax.fori_loop(0, nchunk // group, fwd_body, 0)


def _retention(hc, lg, batch, seq, c_len, group):
    dk, dv, nh = RET_QK_DIM, RET_V_DIM, RET_HEADS
    v_blk0 = (2 * nh * dk) // dv
    return pl.pallas_call(
        functools.partial(_ret_kernel, c_len=c_len, group=group),
        grid=(batch, nh),
        in_specs=[pl.BlockSpec(memory_space=pltpu.SMEM),
                  pl.BlockSpec((seq, dk), lambda b, hh: (b, hh)),
                  pl.BlockSpec((seq, dk), lambda b, hh: (b, nh + hh)),
                  pl.BlockSpec((seq, dv), lambda b, hh: (b, v_blk0 + hh)),
                  pl.BlockSpec((seq, dv), lambda b, hh: (b, v_blk0 + nh + hh))],
        out_specs=pl.BlockSpec((seq, dv), lambda b, hh: (b, hh)),
        scratch_shapes=[pltpu.VMEM((seq, dv), F32), pltpu.VMEM((dk, dv), F32)],
        out_shape=jax.ShapeDtypeStruct((batch * seq, nh * dv), BF16),
        compiler_params=_cparams(("parallel", "arbitrary")),
        name="retention",
    )(lg, hc, hc, hc, hc)


ROUTER_ROWS = 40


def _router_kernel(x_ref, wt_ref, b_ref, tri_ref, ids_ref, wts_ref, cnt_ref, carry_ref):
    i = pl.program_id(0)
    tm = x_ref.shape[0]

    @pl.when(i == 0)
    def _():
        carry_ref[...] = jnp.zeros_like(carry_ref)

    logits = _dot_nt(wt_ref[...], x_ref[...]) + b_ref[...]
    grow = lax.broadcasted_iota(I32, (SUBLANES, tm), 0).astype(F32)
    gl = jnp.where(grow < N_GROUPS, logits[0:SUBLANES], MASK_VALUE)
    gmax = jnp.max(gl, axis=0, keepdims=True)
    gsum = jnp.sum(jnp.exp(gl - gmax), axis=0, keepdims=True)
    p_group = 1.0 / gsum
    g_idx = jnp.min(jnp.where(gl == gmax, grow, float(N_GROUPS)), axis=0, keepdims=True)
    sel = jnp.zeros((EXPERTS_PER_GROUP, tm), F32)
    for g in range(N_GROUPS):
        r0 = SUBLANES + g * EXPERTS_PER_GROUP
        sel = sel + jnp.where(g_idx == float(g), logits[r0:r0 + EXPERTS_PER_GROUP], 0.0)
    erow = lax.broadcasted_iota(I32, (EXPERTS_PER_GROUP, tm), 0).astype(F32)
    smax = jnp.max(sel, axis=0, keepdims=True)
    sexp = jnp.exp(sel - smax)
    probs = sexp / jnp.sum(sexp, axis=0, keepdims=True)
    p1 = jnp.max(probs, axis=0, keepdims=True)
    i1 = jnp.min(jnp.where(probs == p1, erow, float(EXPERTS_PER_GROUP)), axis=0, keepdims=True)
    rest = jnp.where(erow == i1, -1.0, probs)
    p2 = jnp.max(rest, axis=0, keepdims=True)
    i2 = jnp.min(jnp.where(rest == p2, erow, float(EXPERTS_PER_GROUP)), axis=0, keepdims=True)
    denom = p1 + p2
    e0 = g_idx * EXPERTS_PER_GROUP + i1
    e1 = g_idx * EXPERTS_PER_GROUP + i2

    xrow = lax.broadcasted_iota(I32, (N_EXPERTS, tm), 0).astype(F32)
    oh0 = jnp.where(xrow == e0, 1.0, 0.0)
    oh1 = jnp.where(xrow == e1, 1.0, 0.0)
    onehot = oh0 + oh1
    before = _dot(onehot.astype(BF16), tri_ref[...]) + carry_ref[:, 0:1]
    rank0 = jnp.sum(oh0 * before, axis=0, keepdims=True)
    rank1 = jnp.sum(oh1 * before, axis=0, keepdims=True)
    carry_ref[...] = carry_ref[...] + jnp.sum(onehot, axis=1, keepdims=True)

    ids_ref[...] = jnp.zeros_like(ids_ref)
    ids_ref[0:1, :] = e0.astype(I32)
    ids_ref[1:2, :] = e1.astype(I32)
    ids_ref[2:3, :] = rank0.astype(I32)
    ids_ref[3:4, :] = rank1.astype(I32)
    wts_ref[...] = jnp.zeros_like(wts_ref)
    wts_ref[0:1, :] = p_group * p1 / denom
    wts_ref[1:2, :] = p_group * p2 / denom
    cnt_ref[...] = carry_ref[...]


def _router(x, wt, bias, tm):
    n, d = x.shape
    tri = jnp.asarray(np.triu(np.ones((tm, tm), np.float32), 1), BF16)
    return pl.pallas_call(
        _router_kernel,
        grid=(n // tm,),
        in_specs=[pl.BlockSpec((tm, d), lambda i: (i, 0)),
                  pl.BlockSpec((ROUTER_ROWS, d), lambda i: (0, 0)),
                  pl.BlockSpec((ROUTER_ROWS, 1), lambda i: (0, 0)),
                  pl.BlockSpec((tm, tm), lambda i: (0, 0))],
        out_specs=[pl.BlockSpec((SUBLANES, tm), lambda i: (0, i)),
                   pl.BlockSpec((SUBLANES, tm), lambda i: (0, i)),
                   pl.BlockSpec((N_EXPERTS, LANES), lambda i: (0, 0))],
        out_shape=[jax.ShapeDtypeStruct((SUBLANES, n), I32), jax.ShapeDtypeStruct((SUBLANES, n), F32),
                   jax.ShapeDtypeStruct((N_EXPERTS, LANES), F32)],
        scratch_shapes=[pltpu.VMEM((N_EXPERTS, LANES), F32)],
        compiler_params=_cparams(("arbitrary",)),
        name="moe_router",
    )(x, wt, bias, tri)


def _slots_kernel(ids_ref, cnt_ref, slots_ref, blk_ref, *, nblk_pad):
    tm = ids_ref.shape[1]
    cnt = cnt_ref[:, 0:1]
    padded = jnp.floor((cnt + (MOE_BLOCK - 1)) / MOE_BLOCK) * MOE_BLOCK
    er = lax.broadcasted_iota(I32, (N_EXPERTS, N_EXPERTS), 0)
    ec = lax.broadcasted_iota(I32, (N_EXPERTS, N_EXPERTS), 1)
    padded_row = jnp.sum(jnp.where(er == ec, padded, 0.0), axis=0, keepdims=True)
    p_start = jnp.sum(jnp.where(ec < er, padded_row, 0.0), axis=1, keepdims=True)
    p_end = p_start + padded
    xrow = lax.broadcasted_iota(I32, (N_EXPERTS, tm), 0)
    e0 = ids_ref[0:1, :]
    e1 = ids_ref[1:2, :]
    s0 = jnp.sum(jnp.where(xrow == e0, p_start, 0.0), axis=0, keepdims=True).astype(I32) + ids_ref[2:3, :]
    s1 = jnp.sum(jnp.where(xrow == e1, p_start, 0.0), axis=0, keepdims=True).astype(I32) + ids_ref[3:4, :]
    slots_ref[...] = jnp.zeros_like(slots_ref)
    slots_ref[0:1, :] = s0
    slots_ref[1:2, :] = s1
    slots_ref[2:3, :] = s0 | (s1 << 16)
    bstart = (lax.broadcasted_iota(I32, (1, nblk_pad), 1) * MOE_BLOCK).astype(F32)
    blk_e = jnp.minimum(jnp.sum(jnp.where(p_end <= bstart, 1.0, 0.0), axis=0, keepdims=True), N_EXPERTS - 1.0)
    total = jnp.sum(padded, axis=0, keepdims=True)
    erow = lax.broadcasted_iota(I32, (N_EXPERTS, nblk_pad), 0).astype(F32)
    own_end = jnp.sum(jnp.where(erow == blk_e, p_end, 0.0), axis=0, keepdims=True)
    nxt_e = jnp.minimum(jnp.sum(jnp.where(p_end <= own_end, 1.0, 0.0), axis=0, keepdims=True), N_EXPERTS - 1.0)
    nxt_e = jnp.where(own_end < total, nxt_e, -1.0)
    blk_ref[...] = jnp.zeros_like(blk_ref)
    blk_ref[0:1, :] = blk_e.astype(I32)
    blk_ref[1:2, :] = jnp.broadcast_to((total / MOE_BLOCK).astype(I32), (1, nblk_pad))
    blk_ref[2:3, :] = nxt_e.astype(I32)
    lane = lax.broadcasted_iota(I32, (N_EXPERTS, nblk_pad), 1).astype(F32)
    blk_ref[3:4, :] = jnp.sum(jnp.where(erow == lane, p_start + cnt, 0.0), axis=0, keepdims=True).astype(I32)
    blk_ref[4:5, :] = jnp.sum(jnp.where(erow == lane, p_end, 0.0), axis=0, keepdims=True).astype(I32)


def _slots(ids, cnt, tm, nblk_pad):
    n = ids.shape[1]
    return pl.pallas_call(
        functools.partial(_slots_kernel, nblk_pad=nblk_pad),
        grid=(n // tm,),
        in_specs=[pl.BlockSpec((SUBLANES, tm), lambda i: (0, i)),
                  pl.BlockSpec((N_EXPERTS, LANES), lambda i: (0, 0))],
        out_specs=[pl.BlockSpec((SUBLANES, tm), lambda i: (0, i)),
                   pl.BlockSpec((SUBLANES, nblk_pad), lambda i: (0, 0))],
        out_shape=[jax.ShapeDtypeStruct((SUBLANES, n), I32), jax.ShapeDtypeStruct((SUBLANES, nblk_pad), I32)],
        compiler_params=_cparams(("arbitrary",)),
        name="moe_slots",
    )(ids, cnt)


def _slot_tokens_kernel(slots_ref, blk_ref, tok_ref, *, n, cap, nblk_pad):
    def zero(j, carry):
        tok_ref[j] = 0
        return carry

    def scatter(t, carry):
        both = slots_ref[t]
        tok_ref[both & 0xFFFF] = t
        tok_ref[lax.shift_right_logical(both, 16)] = t
        return carry

    for e in range(N_EXPERTS):
        lax.fori_loop(blk_ref[3 * nblk_pad + e], blk_ref[4 * nblk_pad + e], zero, 0)
    lax.fori_loop(blk_ref[nblk_pad] * MOE_BLOCK, cap, zero, 0)
    lax.fori_loop(0, n, scatter, 0, unroll=8)


def _slot_tokens(slots_packed, blk_flat, n, cap, nblk_pad):
    return pl.pallas_call(
        functools.partial(_slot_tokens_kernel, n=n, cap=cap, nblk_pad=nblk_pad),
        grid_spec=pltpu.PrefetchScalarGridSpec(
            num_scalar_prefetch=2,
            grid=(1,),
            in_specs=[],
            out_specs=pl.BlockSpec(memory_space=pltpu.SMEM)),
        out_shape=jax.ShapeDtypeStruct((cap,), I32),
        compiler_params=_cparams(("arbitrary",)),
        name="moe_slot_tokens",
    )(slots_packed, blk_flat)


ROW_BUFS = 3


def _expert_kernel(blk_ref, tok_ref, xpk_hbm, wg_hbm, wu_hbm, wd_hbm, y_ref, xbuf, wgb, wub, wdb,
                   sems, wsems, cnt_ref, *, nblk_pad):
    i = pl.program_id(0)
    n_used = blk_ref[nblk_pad]

    def weight_copies(e, slot):
        return [pltpu.make_async_copy(src.at[e], dst.at[slot], wsems.at[slot])
                for src, dst in ((wg_hbm, wgb), (wu_hbm, wub), (wd_hbm, wdb))]

    def start_rows(block, buf, r_lo=0, r_hi=MOE_BLOCK):
        for r in range(r_lo, r_hi):
            src0 = pl.multiple_of(tok_ref[block * MOE_BLOCK + r] * SUBLANES, SUBLANES)
            pltpu.make_async_copy(xpk_hbm.at[pl.ds(src0, SUBLANES)],
                                  xbuf.at[buf, pl.ds(r * SUBLANES, SUBLANES)], sems.at[buf]).start(priority=r % 2)

    def wait_rows(buf):
        pltpu.make_async_copy(xpk_hbm.at[pl.ds(0, MOE_BLOCK * SUBLANES)], xbuf.at[buf], sems.at[buf]).wait()

    @pl.when(i == 0)
    def _():
        cnt_ref[0] = 0
        for cp in weight_copies(blk_ref[0], 0):
            cp.start(priority=1)
        for ahead in range(ROW_BUFS - 1):
            start_rows(jnp.minimum(ahead, n_used - 1), ahead)

    @pl.when((i < n_used) & ((i == 0) | (blk_ref[i] != blk_ref[jnp.maximum(i - 1, 0)])))
    def _():
        slot = cnt_ref[0] % 2
        cnt_ref[0] = cnt_ref[0] + 1
        for cp in weight_copies(blk_ref[i], slot):
            cp.wait()
        nxt_e = blk_ref[2 * nblk_pad + i]

        @pl.when(nxt_e >= 0)
        def _():
            for cp in weight_copies(nxt_e, 1 - slot):
                cp.start(priority=1)

    @pl.when(i < n_used)
    def _():
        wslot = (cnt_ref[0] + 1) % 2
        buf = i % ROW_BUFS
        nbuf = (i + ROW_BUFS - 1) % ROW_BUFS
        nxt = jnp.minimum(i + ROW_BUFS - 1, n_used - 1)
        wait_rows(buf)
        lo, hi = _unpack_halves(_load_row_tiles(xbuf.at[buf], MOE_BLOCK))
        xb = jnp.concatenate([lo.astype(BF16), hi.astype(BF16)], axis=-1)
        g = _dot(xb, wgb[wslot])
        start_rows(nxt, nbuf, 0, MOE_BLOCK // 2)
        u = _dot(xb, wub[wslot])
        start_rows(nxt, nbuf, MOE_BLOCK // 2, MOE_BLOCK)
        hmid = (g * jax.nn.sigmoid(g) * u).astype(BF16)
        _store_row_tiles(y_ref, _pack_halves(_dot(hmid, wdb[wslot])))

    @pl.when(i == n_used - 1)
    def _():
        for ahead in range(1, ROW_BUFS):
            wait_rows((i + ahead) % ROW_BUFS)

    @pl.when(i >= n_used)
    def _():
        y_ref[...] = jnp.zeros_like(y_ref)


def _experts(blk_flat, slot_tok, xpk, wg, wu, wd, nblk_pad):
    d, de = wg.shape[1], wg.shape[2]
    assert d == 2 * SUBLANES * LANES and xpk.shape[1] == LANES
    cap = slot_tok.shape[0]
    tile_rows = MOE_BLOCK * SUBLANES
    return pl.pallas_call(
        functools.partial(_expert_kernel, nblk_pad=nblk_pad),
        grid_spec=pltpu.PrefetchScalarGridSpec(
            num_scalar_prefetch=2,
            grid=(cap // MOE_BLOCK,),
            in_specs=[pl.BlockSpec(memory_space=pl.ANY), pl.BlockSpec(memory_space=pl.ANY),
                      pl.BlockSpec(memory_space=pl.ANY), pl.BlockSpec(memory_space=pl.ANY)],
            out_specs=pl.BlockSpec((tile_rows, LANES), lambda i, blk, tok: (i, 0)),
            scratch_shapes=[pltpu.VMEM((ROW_BUFS, tile_rows, LANES), U32),
                            pltpu.VMEM((2, d, de), BF16), pltpu.VMEM((2, d, de), BF16), pltpu.VMEM((2, de, d), BF16),
                            pltpu.SemaphoreType.DMA((ROW_BUFS,)), pltpu.SemaphoreType.DMA((2,)),
                            pltpu.SMEM((1,), I32)]),
        out_shape=jax.ShapeDtypeStruct((cap * SUBLANES, LANES), U32),
        compiler_params=_cparams(("arbitrary",)),
        name="moe_experts",
    )(blk_flat, slot_tok, xpk, wg, wu, wd)


def _tail_kernel(slots_ref, x_ref, xb_ref, p_ref, wts_ref, wgate_ref, bgate_ref, wproj_ref, g_ref, b_ref,
                 yb_hbm, y_ref, ybf_ref, rows_ref, sems):
    i = pl.program_id(0)
    nsteps = pl.num_programs(0)
    tm = x_ref.shape[0]
    n = nsteps * tm

    def start_rows(step, buf):
        for t in range(tm):
            for which in range(2):
                src0 = pl.multiple_of(slots_ref[which * n + step * tm + t] * SUBLANES, SUBLANES)
                pltpu.make_async_copy(yb_hbm.at[pl.ds(src0, SUBLANES)],
                                      rows_ref.at[buf, which, pl.ds(t * SUBLANES, SUBLANES)], sems.at[buf]).start()

    def wait_rows(buf):
        for which in range(2):
            pltpu.make_async_copy(yb_hbm.at[pl.ds(0, tm * SUBLANES)], rows_ref.at[buf, which],
                                  sems.at[buf]).wait()

    @pl.when(i == 0)
    def _():
        start_rows(0, 0)

    buf = i % 2
    nxt = jnp.minimum(i + 1, nsteps - 1)
    wait_rows(buf)
    gate_pre = _dot(xb_ref[...], wgate_ref[...])
    proj = _dot(p_ref[0].astype(BF16), wproj_ref[...])
    start_rows(nxt, 1 - buf)
    ple = jax.nn.sigmoid(gate_pre + bgate_ref[...]) * proj
    w = wts_ref[...]
    lo0, hi0 = _unpack_halves(_load_row_tiles(rows_ref.at[buf, 0], tm))
    lo1, hi1 = _unpack_halves(_load_row_tiles(rows_ref.at[buf, 1], tm))
    w0 = w[:, 0:1]
    w1 = w[:, 1:2]
    ffn = jnp.concatenate([lo0 * w0 + lo1 * w1, hi0 * w0 + hi1 * w1], axis=-1)
    z = DN_ALPHA * x_ref[...] + ffn + ple
    y = _layer_norm_rows(z, g_ref[...], b_ref[...])
    y_ref[...] = y
    ybf_ref[...] = y.astype(BF16)

    @pl.when(i == nsteps - 1)
    def _():
        wait_rows(1 - buf)


def _layer_tail(slots_flat, x, xb, p, layer, wts_t, wgate, bgate, wproj, g, b, yb, tm):
    n, d = x.shape
    pd = p.shape[2]
    return pl.pallas_call(
        _tail_kernel,
        grid_spec=pltpu.PrefetchScalarGridSpec(
            num_scalar_prefetch=1,
            grid=(n // tm,),
            in_specs=[pl.BlockSpec((tm, d), lambda i, s: (i, 0)),
                      pl.BlockSpec((tm, d), lambda i, s: (i, 0)),
                      pl.BlockSpec((1, tm, pd), lambda i, s: (layer, i, 0)),
                      pl.BlockSpec((tm, 2), lambda i, s: (i, 0)),
                      pl.BlockSpec((d, d), lambda i, s: (0, 0)),
                      pl.BlockSpec((1, d), lambda i, s: (0, 0)),
                      pl.BlockSpec((pd, d), lambda i, s: (0, 0)),
                      pl.BlockSpec((1, d), lambda i, s: (0, 0)),
                      pl.BlockSpec((1, d), lambda i, s: (0, 0)),
                      pl.BlockSpec(memory_space=pl.ANY)],
            out_specs=[pl.BlockSpec((tm, d), lambda i, s: (i, 0)),
                       pl.BlockSpec((tm, d), lambda i, s: (i, 0))],
            scratch_shapes=[pltpu.VMEM((2, 2, tm * SUBLANES, LANES), U32), pltpu.SemaphoreType.DMA((2,))]),
        out_shape=[jax.ShapeDtypeStruct((n, d), F32), jax.ShapeDtypeStruct((n, d), BF16)],
        compiler_params=_cparams(("arbitrary",)),
        name="layer_tail",
    )(slots_flat, x, xb, p, wts_t, wgate, bgate.reshape(1, d), wproj, g.reshape(1, d), b.reshape(1, d), yb)


def _rope_table(seq, dim):
    pos = jnp.arange(seq, dtype=F32)
    inv = jnp.exp(jnp.arange(0, dim, 2, dtype=F32) * (-math.log(ROPE_BASE) / dim))
    ang = pos[:, None] * inv[None, :]
    return jnp.cos(ang), jnp.sin(ang)


def _moe_layer(x, xb, xpk, p, layer, w_group, b_group, w_router, b_router, w_gate, w_up, w_down,
               ple_w_proj, ple_w_gate, ple_b_gate, ln_g, ln_b):
    n, d = x.shape
    nblk = -(-(2 * n) // MOE_BLOCK) + N_EXPERTS
    nblk_pad = -(-nblk // LANES) * LANES
    cap = nblk * MOE_BLOCK
    wt = jnp.zeros((ROUTER_ROWS, d), F32)
    wt = wt.at[0:N_GROUPS].set(w_group.T)
    wt = wt.at[SUBLANES:].set(w_router.transpose(0, 2, 1).reshape(N_EXPERTS, d))
    bias = jnp.zeros((ROUTER_ROWS, 1), F32)
    bias = bias.at[0:N_GROUPS, 0].set(b_group)
    bias = bias.at[SUBLANES:, 0].set(b_router.reshape(N_EXPERTS))
    ids, wts, cnt = _router(xb, wt.astype(BF16), bias, tm=512)
    slots, blk = _slots(ids, cnt, tm=min(2048, n), nblk_pad=nblk_pad)
    slots_flat = slots[0:2].reshape(2 * n)
    blk_flat = blk[0:5].reshape(5 * nblk_pad)
    assert cap <= 1 << 16
    slot_tok = _slot_tokens(slots[2], blk_flat, n, cap, nblk_pad)
    yb = _experts(blk_flat, slot_tok, xpk, w_gate, w_up, w_down, nblk_pad)
    return _layer_tail(slots_flat, x, xb, p, layer, wts[0:2].T, ple_w_gate, ple_b_gate, ple_w_proj, ln_g, ln_b,
                       yb, tm=256)


def _mixer_ab(x, xres, batch, seq, w_in, rpb, q_norm, w_uq, kv_norm, w_ukv, w_out_all, j, ln_g, ln_b,
              expert_ws, layer, dense_early, dense_late):
    d = x.shape[1]
    o1 = 3 * NA_WIDTH
    o2 = o1 + MLA_Q_RANK
    o3 = o2 + MLA_KV_RANK
    half = MLA_ROPE_DIM // 2
    kr = w_in[:, o3:o3 + MLA_ROPE_DIM]
    kr_sw = jnp.concatenate([kr[:, half:], kr[:, :half]], axis=1)
    width = -(-(o3 + 2 * MLA_ROPE_DIM) // 1024) * 1024
    w_in_p = jnp.concatenate([w_in, kr_sw, jnp.zeros((d, width - o3 - 2 * MLA_ROPE_DIM), F32)], axis=1)
    h, early_b = _matmul(x, w_in_p.astype(BF16), BF16, tm=1024, tn=1024, cast_ws=dense_early)
    a_out = _na_attention(h, _na_bias_tables(rpb), batch, seq)
    dq = MLA_NOPE_DIM + MLA_ROPE_DIM
    wq = w_uq.reshape(MLA_Q_RANK, MLA_HEADS, dq)
    wq_pe = wq[:, :, MLA_NOPE_DIM:]
    wq_p = jnp.concatenate([wq, wq_pe[:, :, half:], wq_pe[:, :, :half]], axis=2)
    wq_p = wq_p.reshape(MLA_Q_RANK, MLA_HEADS * 2 * LANES).astype(BF16)
    wkv = w_ukv.reshape(MLA_KV_RANK, MLA_HEADS, MLA_NOPE_DIM + MLA_V_DIM)
    wk = wkv[:, :, :MLA_NOPE_DIM].reshape(MLA_KV_RANK, MLA_HEADS * MLA_NOPE_DIM).astype(BF16)
    wvt = wkv[:, :, MLA_NOPE_DIM:].reshape(MLA_KV_RANK, MLA_HEADS * MLA_V_DIM).T.astype(BF16)
    cos, sin = _rope_table(seq, MLA_ROPE_DIM)
    zpad = jnp.zeros((seq, LANES - MLA_ROPE_DIM), F32)
    cosf = jnp.concatenate([cos, cos, zpad], axis=1)
    sinf = jnp.concatenate([-sin, sin, zpad], axis=1)
    q_p, k_p, vt = _mla_prep(h, q_norm, kv_norm, wq_p, wk, wvt, cosf, sinf, o1, seq, tm=512)
    b_out, cast_out = _mla_attention(q_p, k_p, vt, batch, seq, tq=min(1024, seq), tk=1024, sub=256,
                                     cast_ws=expert_ws, cast_layer=layer,
                                     cast_dense=[(w_out_all, j)] + list(dense_late))
    expert_wb, w_out_b, late_b = cast_out[:len(expert_ws)], cast_out[len(expert_ws)], cast_out[len(expert_ws) + 1:]
    outs = _proj_ln([a_out, b_out], [w_out_b[:NA_WIDTH], w_out_b[NA_WIDTH:]], xres, ln_g, ln_b, tm=512, nk=1)
    return outs, expert_wb, early_b, late_b


def _mixer_c(xb, xres, batch, seq, w_in, log_rate_f, log_rate_b, w_out, ln_g, ln_b, expert_ws, layer):
    cosr, sinr = _rope_table(seq, RET_QK_DIM)
    n_q = RET_HEADS * RET_QK_DIM
    hc, expert_wb = _matmul_rope(xb, w_in, cosr, sinr, BF16, tm=min(1024, seq), tn=1024,
                                 n_q_cols=n_q, n_rope_cols=2 * n_q, gate_col0=2 * n_q + RET_HEADS * RET_V_DIM,
                                 head_w=RET_QK_DIM, q_scale=RET_QK_DIM ** -0.5,
                                 cast_ws=expert_ws, cast_layer=layer, cast_j=8)
    lg = jnp.stack([jnp.log1p(-jnp.exp(log_rate_f.astype(F32))), jnp.log1p(-jnp.exp(log_rate_b.astype(F32)))])
    r = _retention(hc, lg, batch, seq, c_len=256, group=4)
    return _proj_ln([r], [w_out], xres, ln_g, ln_b, tm=512, nk=1), expert_wb


def kernel(x, p, ab_w_in, ab_rpb, ab_q_norm, ab_w_uq, ab_kv_norm, ab_w_ukv, ab_w_out, c_w_in, c_log_rate_f,
           c_log_rate_b, c_w_out, ln1_g, ln1_b, moe_w_group, moe_b_group, moe_w_router, moe_b_router,
           moe_w_gate, moe_w_up, moe_w_down, ple_w_proj, ple_w_gate, ple_b_gate, ln2_g, ln2_b):
    batch, seq, d = x.shape
    n = batch * seq
    xf = x.reshape(n, d)
    p_flat = p.reshape(DEPTH, n, -1)
    expert_ws = (moe_w_gate, moe_w_up, moe_w_down)
    assert DEPTH == 2
    dense_late = [(c_w_out, 0)] + [(w, i) for i in range(DEPTH) for w in (ple_w_gate, ple_w_proj)]
    (xf, xb, xpk), (wg, wu, wd), (c_w_in_b,), late_b = _mixer_ab(
        xf, xf, batch, seq, ab_w_in[0], ab_rpb[0], ab_q_norm[0], ab_w_uq[0], ab_kv_norm[0], ab_w_ukv[0],
        ab_w_out, 0, ln1_g[0], ln1_b[0], expert_ws, 0, dense_early=[(c_w_in, 0)], dense_late=dense_late)
    c_w_out_b, ple_b = late_b[0], late_b[1:]
    xf, xb = _moe_layer(xf, xb, xpk, p_flat, 0, moe_w_group[0], moe_b_group[0], moe_w_router[0],
                        moe_b_router[0], wg, wu, wd, ple_b[1], ple_b[0], ple_b_gate[0], ln2_g[0], ln2_b[0])
    (xf, xb, xpk), (wg, wu, wd) = _mixer_c(
        xb, xf, batch, seq, c_w_in_b, c_log_rate_f[0], c_log_rate_b[0], c_w_out_b, ln1_g[1], ln1_b[1],
        expert_ws, 1)
    xf, xb = _moe_layer(xf, xb, xpk, p_flat, 1, moe_w_group[1], moe_b_group[1], moe_w_router[1],
                        moe_b_router[1], wg, wu, wd, ple_b[3], ple_b[2], ple_b_gate[1], ln2_g[1], ln2_b[1])
    return xf.reshape(batch, seq, d)
```

```python
import functools
import math

import numpy as np
import jax
import jax.numpy as jnp
from jax import lax
from jax.experimental import pallas as pl
from jax.experimental.pallas import tpu as pltpu

DEPTH = 2
GRID_W = 64
NA_HEADS = 8
NA_HEAD_DIM = 128
NA_WIN_H = 8
NA_WIN_W = 16
MLA_HEADS = 8
MLA_Q_RANK = 512
MLA_KV_RANK = 256
MLA_NOPE_DIM = 128
MLA_ROPE_DIM = 64
MLA_V_DIM = 128
RET_HEADS = 8
RET_QK_DIM = 256
RET_V_DIM = 512
RET_CHUNK = 128
N_GROUPS = 4
EXPERTS_PER_GROUP = 8
N_EXPERTS = N_GROUPS * EXPERTS_PER_GROUP
D_EXPERT = 512
MOE_BLOCK = 128
ROPE_BASE = 10000.0
LN_EPS = 1e-5
RMS_EPS = 1e-6
DN_ALPHA = (2 * DEPTH) ** 0.25
NA_WIDTH = NA_HEADS * NA_HEAD_DIM

LANES = 128
SUBLANES = 8
VMEM_LIMIT_BYTES = 60 * 1024 * 1024
MASK_VALUE = -1e30

F32 = jnp.float32
BF16 = jnp.bfloat16
I32 = jnp.int32
U32 = jnp.uint32


def _cparams(sem):
    return pltpu.CompilerParams(dimension_semantics=sem, vmem_limit_bytes=VMEM_LIMIT_BYTES)


def _dot(a, b):
    return jnp.dot(a, b, preferred_element_type=F32)


def _dot_nt(a, b, precision=None):
    return lax.dot_general(a, b, (((1,), (1,)), ((), ())), preferred_element_type=F32,
                           precision=precision)


def _pack_halves(y):
    c = y.shape[1] // 2
    bits = pltpu.bitcast(y.astype(BF16).astype(F32), U32)
    return (bits[:, :c] >> 16) | (bits[:, c:] & jnp.uint32(0xFFFF0000))


def _unpack_halves(w):
    lo = pltpu.bitcast(w << 16, F32)
    hi = pltpu.bitcast(w & jnp.uint32(0xFFFF0000), F32)
    return lo, hi


def _store_row_tiles(ref, packed):
    m = packed.shape[0]
    for s in range(SUBLANES):
        ref[pl.ds(s, m, stride=SUBLANES), :] = packed[:, s * LANES:(s + 1) * LANES]


def _load_row_tiles(ref, m):
    return jnp.concatenate([ref[pl.ds(s, m, stride=SUBLANES), :] for s in range(SUBLANES)], axis=-1)


def _mm_kernel(x_ref, w_ref, *rest, n_cast):
    cast_src, o_ref, cast_dst = rest[:n_cast], rest[n_cast], rest[n_cast + 1:]
    _cast_blocks(cast_src, cast_dst)
    o_ref[...] = _dot(x_ref[...].astype(BF16), w_ref[...]).astype(o_ref.dtype)


def _matmul(x, w, out_dtype, tm, tn, cast_ws):
    m, k = x.shape
    n = w.shape[1]
    nj = n // tn
    streams = [_cast_rows_specs(cw, lead, (m // tm) * nj, lambda i, j: i * nj + j) for cw, lead in cast_ws]
    outs = pl.pallas_call(
        functools.partial(_mm_kernel, n_cast=len(cast_ws)),
        grid=(m // tm, nj),
        in_specs=[pl.BlockSpec((tm, k), lambda i, j: (i, 0)),
                  pl.BlockSpec((k, tn), lambda i, j: (0, j))] + [s[0] for s in streams],
        out_specs=[pl.BlockSpec((tm, tn), lambda i, j: (i, j))] + [s[1] for s in streams],
        out_shape=[jax.ShapeDtypeStruct((m, n), out_dtype)] + [s[2] for s in streams],
        compiler_params=_cparams(("arbitrary", "arbitrary")),
        name="matmul",
    )(x, w, *[cw for cw, _ in cast_ws])
    return outs[0], outs[1:]


def _mm_rope_kernel(x_ref, w_ref, cos_ref, sin_ref, *rest, n_q_tiles, n_rope_tiles, first_gate_tile,
                    head_w, q_scale, n_cast):
    cast_src, o_ref, cast_dst = rest[:n_cast], rest[n_cast], rest[n_cast + 1:]
    j = pl.program_id(1)
    _cast_blocks(cast_src, cast_dst)
    acc = _dot(x_ref[...].astype(BF16), w_ref[...])

    @pl.when((j >= n_rope_tiles) & (j < first_gate_tile))
    def _():
        o_ref[...] = acc.astype(o_ref.dtype)

    @pl.when(j >= first_gate_tile)
    def _():
        o_ref[...] = (acc * jax.nn.sigmoid(acc)).astype(o_ref.dtype)

    @pl.when(j < n_rope_tiles)
    def _():
        scale = jnp.where(j < n_q_tiles, q_scale, 1.0)
        cos = cos_ref[...] * scale
        sin = sin_ref[...] * scale
        half = head_w // 2
        for c0 in range(0, acc.shape[1], head_w):
            x1 = acc[:, c0:c0 + half]
            x2 = acc[:, c0 + half:c0 + head_w]
            o_ref[:, c0:c0 + half] = (x1 * cos - x2 * sin).astype(o_ref.dtype)
            o_ref[:, c0 + half:c0 + head_w] = (x2 * cos + x1 * sin).astype(o_ref.dtype)


def _matmul_rope(x, w, cos, sin, out_dtype, tm, tn, n_q_cols, n_rope_cols, gate_col0, head_w, q_scale,
                 cast_ws, cast_layer, cast_j):
    m, k = x.shape
    n = w.shape[1]
    nsb = cos.shape[0] // tm
    streams = [_cast_stream_specs(cw, cast_layer, (m // tm) * cast_j,
                                  lambda i, j: i * cast_j + jnp.minimum(j, cast_j - 1)) for cw in cast_ws]
    outs = pl.pallas_call(
        functools.partial(_mm_rope_kernel, n_q_tiles=n_q_cols // tn, n_rope_tiles=n_rope_cols // tn,
                          first_gate_tile=gate_col0 // tn, head_w=head_w, q_scale=q_scale,
                          n_cast=len(cast_ws)),
        grid=(m // tm, n // tn),
        in_specs=[pl.BlockSpec((tm, k), lambda i, j: (i, 0)),
                  pl.BlockSpec((k, tn), lambda i, j: (0, j)),
                  pl.BlockSpec((tm, head_w // 2), lambda i, j: (i % nsb, 0)),
                  pl.BlockSpec((tm, head_w // 2), lambda i, j: (i % nsb, 0))] + [s[0] for s in streams],
        out_specs=[pl.BlockSpec((tm, tn), lambda i, j: (i, j))] + [s[1] for s in streams],
        out_shape=[jax.ShapeDtypeStruct((m, n), out_dtype)] + [s[2] for s in streams],
        compiler_params=_cparams(("arbitrary", "arbitrary")),
        name="matmul_rope",
    )(x, w, cos, sin, *cast_ws)
    return outs[0], outs[1:]


def _layer_norm_rows(z, g, b):
    mean = jnp.mean(z, axis=-1, keepdims=True)
    zc = z - mean
    var = jnp.mean(zc * zc, axis=-1, keepdims=True)
    return zc * lax.rsqrt(var + LN_EPS) * g + b


def _proj_ln_kernel(*refs, n_act, nk):
    acts = refs[:n_act]
    ws = refs[n_act:2 * n_act]
    x_ref, g_ref, b_ref, y_ref, yb_ref, yp_ref = refs[2 * n_act:2 * n_act + 6]
    k = pl.program_id(1)
    tm = x_ref.shape[0]
    n_split = 2
    hm = tm // n_split

    def product(rows):
        part = _dot(acts[0][rows, :], ws[0][...])
        for a, w in zip(acts[1:], ws[1:]):
            part = part + _dot(a[rows, :], w[...])
        return part

    def finish(rows, h, proj):
        z = DN_ALPHA * x_ref[rows, :] + proj
        y = _layer_norm_rows(z, g_ref[...], b_ref[...])
        y_ref[rows, :] = y
        yb_ref[rows, :] = y.astype(BF16)
        _store_row_tiles(yp_ref.at[pl.ds(h * hm * SUBLANES, hm * SUBLANES)], _pack_halves(y))

    if nk == 1:
        for h in range(n_split):
            rows = pl.ds(h * hm, hm)
            finish(rows, h, product(rows))
        return
    acc_ref = refs[2 * n_act + 6]

    @pl.when(k == 0)
    def _():
        acc_ref[...] = product(pl.ds(0, tm))

    @pl.when((k > 0) & (k < nk - 1))
    def _():
        acc_ref[...] = acc_ref[...] + product(pl.ds(0, tm))

    @pl.when(k == nk - 1)
    def _():
        for h in range(n_split):
            rows = pl.ds(h * hm, hm)
            finish(rows, h, acc_ref[rows, :] + product(rows))


def _proj_ln(acts, ws, x, g, b, tm, nk):
    m, d = x.shape
    n_act = len(acts)
    in_specs = []
    for a in acts:
        kk = a.shape[1] // nk
        in_specs.append(pl.BlockSpec((tm, kk), lambda i, k: (i, k)))
    for w in ws:
        kk = w.shape[0] // nk
        mode = pl.Buffered(1) if nk == 1 else None
        in_specs.append(pl.BlockSpec((kk, d), lambda i, k: (k, 0), pipeline_mode=mode))
    in_specs += [pl.BlockSpec((tm, d), lambda i, k: (i, 0)),
                 pl.BlockSpec((1, d), lambda i, k: (0, 0)),
                 pl.BlockSpec((1, d), lambda i, k: (0, 0))]
    return pl.pallas_call(
        functools.partial(_proj_ln_kernel, n_act=n_act, nk=nk),
        grid=(m // tm, nk),
        in_specs=in_specs,
        out_specs=[pl.BlockSpec((tm, d), lambda i, k: (i, 0)),
                   pl.BlockSpec((tm, d), lambda i, k: (i, 0)),
                   pl.BlockSpec((tm * SUBLANES, LANES), lambda i, k: (i, 0))],
        out_shape=[jax.ShapeDtypeStruct((m, d), F32), jax.ShapeDtypeStruct((m, d), BF16),
                   jax.ShapeDtypeStruct((m * SUBLANES, LANES), U32)],
        scratch_shapes=[pltpu.VMEM((tm, d), F32)] if nk > 1 else [],
        compiler_params=_cparams(("parallel", "arbitrary")),
        name="proj_ln",
    )(*acts, *ws, x, g.reshape(1, d), b.reshape(1, d))


def _na_bias_tables(rpb):
    nh = rpb.shape[0]
    c = np.arange(GRID_W)
    cs = np.clip(c - NA_WIN_W // 2, 0, GRID_W - NA_WIN_W)
    kc = np.arange(GRID_W)
    valid = (kc[None, :] >= cs[:, None]) & (kc[None, :] < cs[:, None] + NA_WIN_W)
    dc = kc[None, :] - c[:, None] + NA_WIN_W - 1
    onehot = (dc[:, :, None] == np.arange(2 * NA_WIN_W - 1)[None, None, :]) & valid[:, :, None]
    cols = jnp.einsum("hrd,ckd->hrck", rpb.astype(F32), jnp.asarray(onehot, F32),
                      precision=lax.Precision.HIGHEST)
    cols = jnp.where(jnp.asarray(valid)[None, None], cols, MASK_VALUE)
    tabs = jnp.stack([cols[:, off:off + NA_WIN_H] for off in range(NA_WIN_H)], axis=1)
    return tabs.transpose(0, 1, 3, 2, 4).reshape(nh, NA_WIN_H, GRID_W, NA_WIN_H * GRID_W)


def _na_kernel(q_ref, k_ref, v_ref, bias_ref, o_ref, *, rows, group):
    scale = NA_HEAD_DIM ** -0.5
    nkeys = NA_WIN_H * GRID_W

    def body(i, carry):
        geom, scores = [], []
        for u in range(group):
            r = i * group + u
            rs = jnp.clip(r - NA_WIN_H // 2, 0, rows - NA_WIN_H)
            off = rs - r + NA_WIN_H - 1
            q0 = pl.multiple_of(r * GRID_W, GRID_W)
            k0 = pl.multiple_of(rs * GRID_W, GRID_W)
            geom.append((q0, k0))
            s = _dot_nt(q_ref[pl.ds(q0, GRID_W), :], k_ref[pl.ds(k0, nkeys), :])
            scores.append(s * scale + bias_ref[0, off])
        for (q0, k0), s in zip(geom, scores):
            m = jnp.max(s, axis=-1, keepdims=True)
            p = jnp.exp(s - m)
            l = jnp.sum(p, axis=-1, keepdims=True)
            o = _dot(p.astype(BF16), v_ref[pl.ds(k0, nkeys), :]) / l
            o_ref[pl.ds(q0, GRID_W), :] = o.astype(o_ref.dtype)
        return carry

    lax.fori_loop(0, rows // group, body, 0)


def _na_attention(h, bias_tables, batch, seq):
    rows = seq // GRID_W
    d = NA_HEAD_DIM
    nkeys = NA_WIN_H * GRID_W
    return pl.pallas_call(
        functools.partial(_na_kernel, rows=rows, group=min(32, rows)),
        grid=(batch, NA_HEADS),
        in_specs=[pl.BlockSpec((seq, d), lambda b, hh: (b, hh)),
                  pl.BlockSpec((seq, d), lambda b, hh: (b, NA_HEADS + hh)),
                  pl.BlockSpec((seq, d), lambda b, hh: (b, 2 * NA_HEADS + hh)),
                  pl.BlockSpec((1, NA_WIN_H, GRID_W, nkeys), lambda b, hh: (hh, 0, 0, 0))],
        out_specs=pl.BlockSpec((seq, d), lambda b, hh: (b, hh)),
        out_shape=jax.ShapeDtypeStruct((batch * seq, NA_WIDTH), BF16),
        compiler_params=_cparams(("parallel", "arbitrary")),
        name="na_attention",
    )(h, h, h, bias_tables)


def _rms_rows(x, g):
    return x * lax.rsqrt(jnp.mean(x * x, axis=-1, keepdims=True) + RMS_EPS) * g


def _rope_lanes(t, cosf, sinf):
    return t * cosf + pltpu.roll(t, LANES // 2, 1) * sinf


def _mla_prep_kernel(cq_ref, ckv_ref, kr_ref, gq_ref, gkv_ref, wq_ref, wk_ref, wvt_ref, cos_ref, sin_ref,
                     q_ref, k_ref, vt_ref):
    dq = MLA_NOPE_DIM + MLA_ROPE_DIM
    cosf = cos_ref[...]
    sinf = sin_ref[...]
    cqn = _rms_rows(cq_ref[...].astype(F32), gq_ref[...]).astype(BF16)
    ckvn = _rms_rows(ckv_ref[...].astype(F32), gkv_ref[...]).astype(BF16)
    qf = _dot(cqn, wq_ref[...]) * (dq ** -0.5 * math.log2(math.e))
    kf = _dot(ckvn, wk_ref[...])
    vt_ref[...] = _dot_nt(wvt_ref[...], ckvn).astype(BF16)
    kpe = _rope_lanes(kr_ref[...].astype(F32), cosf, sinf).astype(BF16)
    for hh in range(MLA_HEADS):
        c0 = hh * 2 * LANES
        q_ref[:, c0:c0 + LANES] = qf[:, c0:c0 + LANES].astype(BF16)
        q_ref[:, c0 + LANES:c0 + 2 * LANES] = _rope_lanes(qf[:, c0 + LANES:c0 + 2 * LANES], cosf, sinf).astype(BF16)
        k_ref[:, c0:c0 + LANES] = kf[:, hh * LANES:(hh + 1) * LANES].astype(BF16)
        k_ref[:, c0 + LANES:c0 + 2 * LANES] = kpe


def _mla_prep(h, gq, gkv, wq_p, wk, wvt, cosf, sinf, col_cq, seq, tm):
    n = h.shape[0]
    hw = MLA_HEADS * 2 * LANES
    nsb = seq // tm
    b_cq = col_cq // MLA_Q_RANK
    b_ckv = (col_cq + MLA_Q_RANK) // MLA_KV_RANK
    b_kr = (col_cq + MLA_Q_RANK + MLA_KV_RANK) // LANES
    return pl.pallas_call(
        _mla_prep_kernel,
        grid=(n // tm,),
        in_specs=[pl.BlockSpec((tm, MLA_Q_RANK), lambda i: (i, b_cq)),
                  pl.BlockSpec((tm, MLA_KV_RANK), lambda i: (i, b_ckv)),
                  pl.BlockSpec((tm, LANES), lambda i: (i, b_kr)),
                  pl.BlockSpec((1, MLA_Q_RANK), lambda i: (0, 0)),
                  pl.BlockSpec((1, MLA_KV_RANK), lambda i: (0, 0)),
                  pl.BlockSpec((MLA_Q_RANK, hw), lambda i: (0, 0)),
                  pl.BlockSpec((MLA_KV_RANK, MLA_HEADS * LANES), lambda i: (0, 0)),
                  pl.BlockSpec((MLA_HEADS * MLA_V_DIM, MLA_KV_RANK), lambda i: (0, 0)),
                  pl.BlockSpec((tm, LANES), lambda i: (i % nsb, 0)),
                  pl.BlockSpec((tm, LANES), lambda i: (i % nsb, 0))],
        out_specs=[pl.BlockSpec((tm, hw), lambda i: (i, 0)),
                   pl.BlockSpec((tm, hw), lambda i: (i, 0)),
                   pl.BlockSpec((MLA_HEADS * MLA_V_DIM, tm), lambda i: (0, i))],
        out_shape=[jax.ShapeDtypeStruct((n, hw), BF16), jax.ShapeDtypeStruct((n, hw), BF16),
                   jax.ShapeDtypeStruct((MLA_HEADS * MLA_V_DIM, n), BF16)],
        compiler_params=_cparams(("parallel",)),
        name="mla_prep",
    )(h, h, h, gq.reshape(1, -1), gkv.reshape(1, -1), wq_p, wk, wvt, cosf, sinf)


def _cast_stream_specs(w, layer, n_slots, slot_of):
    _, ne, rows, cols = w.shape
    if n_slots >= ne:
        e_per, rb = 1, n_slots // ne
        while rows % rb or (rows // rb) % (2 * SUBLANES):
            rb -= 1
    else:
        assert ne % n_slots == 0
        e_per, rb = ne // n_slots, 1
    n_blocks = (ne // e_per) * rb

    def block_of(*g):
        s = jnp.minimum(slot_of(*g), n_blocks - 1)
        return s // rb, s % rb

    src = pl.BlockSpec((1, e_per, rows // rb, cols), lambda *g: (layer, *block_of(*g), 0))
    dst = pl.BlockSpec((e_per, rows // rb, cols), lambda *g: (*block_of(*g), 0))
    return src, dst, jax.ShapeDtypeStruct((ne, rows, cols), BF16), n_blocks


def _cast_rows_specs(w, lead, n_slots, slot_of):
    rows, cols = w.shape[-2:]
    nb = min(n_slots, rows // (2 * SUBLANES))
    while rows % nb or (rows // nb) % (2 * SUBLANES):
        nb -= 1
    src = pl.BlockSpec((1, rows // nb, cols), lambda *g: (lead, jnp.minimum(slot_of(*g), nb - 1), 0))
    dst = pl.BlockSpec((rows // nb, cols), lambda *g: (jnp.minimum(slot_of(*g), nb - 1), 0))
    return src, dst, jax.ShapeDtypeStruct((rows, cols), BF16), nb


def _cast_blocks(srcs, dsts):
    for src, dst in zip(srcs, dsts):
        dst[...] = src[0].astype(BF16)


def _mla_attn_kernel(q_ref, k_ref, vt_ref, *rest, tk, sub, n_cast):
    cast_src, o_ref, cast_dst = rest[:n_cast], rest[n_cast], rest[n_cast + 1:]
    _cast_blocks(cast_src, cast_dst)
    nchunk = k_ref.shape[0] // tk
    tq = q_ref.shape[0]
    nsub = tq // sub
    qs = [q_ref[s * sub:(s + 1) * sub, :] for s in range(nsub)]
    m = [jnp.full((1, sub), MASK_VALUE, F32) for _ in range(nsub)]
    l = [jnp.zeros((1, sub), F32) for _ in range(nsub)]
    acc = [jnp.zeros((MLA_V_DIM, sub), F32) for _ in range(nsub)]

    def scores(s, c):
        return _dot_nt(k_ref[c * tk:(c + 1) * tk, :], qs[s])

    st_next = [scores(s, 0) for s in range(nsub)]
    for c in range(nchunk):
        for s in range(nsub):
            st = st_next[s]
            m_new = jnp.maximum(m[s], jnp.max(st, axis=0, keepdims=True))
            a = jnp.exp2(m[s] - m_new)
            p = jnp.exp2(st - m_new)
            l[s] = a * l[s] + jnp.sum(p, axis=0, keepdims=True)
            if c + 1 < nchunk:
                st_next[s] = scores(s, c + 1)
            acc[s] = a * acc[s] + _dot(vt_ref[:, c * tk:(c + 1) * tk], p.astype(BF16))
            m[s] = m_new
    for s in range(nsub):
        o_ref[s * sub:(s + 1) * sub, :] = (acc[s] / l[s]).T.astype(o_ref.dtype)


def _mla_attention(q_p, k_p, vt, batch, seq, tq, tk, sub, cast_ws, cast_layer, cast_dense):
    n = q_p.shape[0]
    nqb = seq // tq
    n_slots = batch * MLA_HEADS * nqb

    def slot_of(b, hh, i):
        return (b * MLA_HEADS + hh) * nqb + i

    streams = [_cast_stream_specs(w, cast_layer, n_slots, slot_of) for w in cast_ws]
    streams += [_cast_rows_specs(w, lead, n_slots, slot_of) for w, lead in cast_dense]
    cast_ws = list(cast_ws) + [w for w, _ in cast_dense]
    outs = pl.pallas_call(
        functools.partial(_mla_attn_kernel, tk=tk, sub=sub, n_cast=len(cast_ws)),
        grid=(batch, MLA_HEADS, nqb),
        in_specs=[pl.BlockSpec((tq, 2 * LANES), lambda b, hh, i: (b * nqb + i, hh)),
                  pl.BlockSpec((seq, 2 * LANES), lambda b, hh, i: (b, hh)),
                  pl.BlockSpec((MLA_V_DIM, seq), lambda b, hh, i: (hh, b))] + [s[0] for s in streams],
        out_specs=[pl.BlockSpec((tq, MLA_V_DIM), lambda b, hh, i: (b * nqb + i, hh))] + [s[1] for s in streams],
        out_shape=[jax.ShapeDtypeStruct((n, MLA_HEADS * MLA_V_DIM), BF16)] + [s[2] for s in streams],
        compiler_params=_cparams(("arbitrary", "arbitrary", "arbitrary")),
        name="mla_attention",
    )(q_p, k_p, vt, *cast_ws)
    return outs[0], outs[1:]


def _ret_kernel(lg_ref, q_ref, k_ref, v_ref, g_ref, o_ref, acc_ref, st_ref, *, c_len, group):
    nchunk = q_ref.shape[0] // c_len
    hh = pl.program_id(1)
    lgf = lg_ref[0, hh]
    lgb = lg_ref[1, hh]
    ii = lax.broadcasted_iota(I32, (c_len, c_len), 0).astype(F32)
    jj = lax.broadcasted_iota(I32, (c_len, c_len), 1).astype(F32)
    rel = ii - jj
    dmat = jnp.where(rel >= 0, jnp.exp(lgf * jnp.maximum(rel, 0.0)), jnp.exp(lgb * jnp.maximum(-rel, 0.0)))
    pos = lax.broadcasted_iota(I32, (c_len, 1), 0).astype(F32)
    qdec_f = jnp.exp(lgf * (pos + 1.0))
    kdec_f = jnp.exp(lgf * (c_len - 1.0 - pos))
    qdec_b = jnp.exp(lgb * (c_len - pos))
    kdec_b = jnp.exp(lgb * pos)
    full_chunk = jnp.full((1, RET_V_DIM), float(c_len), F32)
    cdec_f = jnp.exp(lgf * full_chunk)
    cdec_b = jnp.exp(lgb * full_chunk)

    def decayed_keys_t(t0, kdec):
        return (k_ref[pl.ds(t0, c_len), :].astype(F32) * kdec).T.astype(BF16)

    st_ref[...] = jnp.zeros_like(st_ref)

    def bwd_body(i, carry):
        t0s = [pl.multiple_of((nchunk - 1 - (i * group + u)) * c_len, c_len) for u in range(group)]
        upd = [_dot(decayed_keys_t(t0, kdec_b), v_ref[pl.ds(t0, c_len), :]) for t0 in t0s]
        for t0, u_c in zip(t0s, upd):
            st = st_ref[...]
            acc_ref[pl.ds(t0, c_len), :] = _dot(q_ref[pl.ds(t0, c_len), :], st.astype(BF16)) * qdec_b
            st_ref[...] = st * cdec_b + u_c
        return carry

    lax.fori_loop(0, nchunk // group, bwd_body, 0)
    st_ref[...] = jnp.zeros_like(st_ref)

    def fwd_body(i, carry):
        t0s = [pl.multiple_of((i * group + u) * c_len, c_len) for u in range(group)]
        scs = [_dot_nt(q_ref[pl.ds(t0, c_len), :], k_ref[pl.ds(t0, c_len), :]) * dmat for t0 in t0s]
        upd = [_dot(decayed_keys_t(t0, kdec_f), v_ref[pl.ds(t0, c_len), :]) for t0 in t0s]
        for t0, sc, u_c in zip(t0s, scs, upd):
            st = st_ref[...]
            r = (_dot(sc.astype(BF16), v_ref[pl.ds(t0, c_len), :])
                 + _dot(q_ref[pl.ds(t0, c_len), :], st.astype(BF16)) * qdec_f
                 + acc_ref[pl.ds(t0, c_len), :])
            st_ref[...] = st * cdec_f + u_c
            r = r - jnp.mean(r, axis=-1, keepdims=True)
            r = r * lax.rsqrt(jnp.mean(r * r, axis=-1, keepdims=True) + LN_EPS)
            o_ref[pl.ds(t0, c_len), :] = (g_ref[pl.ds(t0, c_len), :].astype(F32) * r).astype(o_ref.dtype)
        return carry

    lax.fori_loop(0, nchunk // group, fwd_body, 0)


def _retention(hc, lg, batch, seq, c_len, group):
    dk, dv, nh = RET_QK_DIM, RET_V_DIM, RET_HEADS
    v_blk0 = (2 * nh * dk) // dv
    return pl.pallas_call(
        functools.partial(_ret_kernel, c_len=c_len, group=group),
        grid=(batch, nh),
        in_specs=[pl.BlockSpec(memory_space=pltpu.SMEM),
                  pl.BlockSpec((seq, dk), lambda b, hh: (b, hh)),
                  pl.BlockSpec((seq, dk), lambda b, hh: (b, nh + hh)),
                  pl.BlockSpec((seq, dv), lambda b, hh: (b, v_blk0 + hh)),
                  pl.BlockSpec((seq, dv), lambda b, hh: (b, v_blk0 + nh + hh))],
        out_specs=pl.BlockSpec((seq, dv), lambda b, hh: (b, hh)),
        scratch_shapes=[pltpu.VMEM((seq, dv), F32), pltpu.VMEM((dk, dv), F32)],
        out_shape=jax.ShapeDtypeStruct((batch * seq, nh * dv), BF16),
        compiler_params=_cparams(("parallel", "arbitrary")),
        name="retention",
    )(lg, hc, hc, hc, hc)


ROUTER_ROWS = 40


def _router_kernel(x_ref, wt_ref, b_ref, tri_ref, ids_ref, wts_ref, cnt_ref, carry_ref):
    i = pl.program_id(0)
    tm = x_ref.shape[0]

    @pl.when(i == 0)
    def _():
        carry_ref[...] = jnp.zeros_like(carry_ref)

    logits = _dot_nt(wt_ref[...], x_ref[...]) + b_ref[...]
    grow = lax.broadcasted_iota(I32, (SUBLANES, tm), 0).astype(F32)
    gl = jnp.where(grow < N_GROUPS, logits[0:SUBLANES], MASK_VALUE)
    gmax = jnp.max(gl, axis=0, keepdims=True)
    gsum = jnp.sum(jnp.exp(gl - gmax), axis=0, keepdims=True)
    p_group = 1.0 / gsum
    g_idx = jnp.min(jnp.where(gl == gmax, grow, float(N_GROUPS)), axis=0, keepdims=True)
    sel = jnp.zeros((EXPERTS_PER_GROUP, tm), F32)
    for g in range(N_GROUPS):
        r0 = SUBLANES + g * EXPERTS_PER_GROUP
        sel = sel + jnp.where(g_idx == float(g), logits[r0:r0 + EXPERTS_PER_GROUP], 0.0)
    erow = lax.broadcasted_iota(I32, (EXPERTS_PER_GROUP, tm), 0).astype(F32)
    smax = jnp.max(sel, axis=0, keepdims=True)
    sexp = jnp.exp(sel - smax)
    probs = sexp / jnp.sum(sexp, axis=0, keepdims=True)
    p1 = jnp.max(probs, axis=0, keepdims=True)
    i1 = jnp.min(jnp.where(probs == p1, erow, float(EXPERTS_PER_GROUP)), axis=0, keepdims=True)
    rest = jnp.where(erow == i1, -1.0, probs)
    p2 = jnp.max(rest, axis=0, keepdims=True)
    i2 = jnp.min(jnp.where(rest == p2, erow, float(EXPERTS_PER_GROUP)), axis=0, keepdims=True)
    denom = p1 + p2
    e0 = g_idx * EXPERTS_PER_GROUP + i1
    e1 = g_idx * EXPERTS_PER_GROUP + i2

    xrow = lax.broadcasted_iota(I32, (N_EXPERTS, tm), 0).astype(F32)
    oh0 = jnp.where(xrow == e0, 1.0, 0.0)
    oh1 = jnp.where(xrow == e1, 1.0, 0.0)
    onehot = oh0 + oh1
    before = _dot(onehot.astype(BF16), tri_ref[...]) + carry_ref[:, 0:1]
    rank0 = jnp.sum(oh0 * before, axis=0, keepdims=True)
    rank1 = jnp.sum(oh1 * before, axis=0, keepdims=True)
    carry_ref[...] = carry_ref[...] + jnp.sum(onehot, axis=1, keepdims=True)

    ids_ref[...] = jnp.zeros_like(ids_ref)
    ids_ref[0:1, :] = e0.astype(I32)
    ids_ref[1:2, :] = e1.astype(I32)
    ids_ref[2:3, :] = rank0.astype(I32)
    ids_ref[3:4, :] = rank1.astype(I32)
    wts_ref[...] = jnp.zeros_like(wts_ref)
    wts_ref[0:1, :] = p_group * p1 / denom
    wts_ref[1:2, :] = p_group * p2 / denom
    cnt_ref[...] = carry_ref[...]


def _router(x, wt, bias, tm):
    n, d = x.shape
    tri = jnp.asarray(np.triu(np.ones((tm, tm), np.float32), 1), BF16)
    return pl.pallas_call(
        _router_kernel,
        grid=(n // tm,),
        in_specs=[pl.BlockSpec((tm, d), lambda i: (i, 0)),
                  pl.BlockSpec((ROUTER_ROWS, d), lambda i: (0, 0)),
                  pl.BlockSpec((ROUTER_ROWS, 1), lambda i: (0, 0)),
                  pl.BlockSpec((tm, tm), lambda i: (0, 0))],
        out_specs=[pl.BlockSpec((SUBLANES, tm), lambda i: (0, i)),
                   pl.BlockSpec((SUBLANES, tm), lambda i: (0, i)),
                   pl.BlockSpec((N_EXPERTS, LANES), lambda i: (0, 0))],
        out_shape=[jax.ShapeDtypeStruct((SUBLANES, n), I32), jax.ShapeDtypeStruct((SUBLANES, n), F32),
                   jax.ShapeDtypeStruct((N_EXPERTS, LANES), F32)],
        scratch_shapes=[pltpu.VMEM((N_EXPERTS, LANES), F32)],
        compiler_params=_cparams(("arbitrary",)),
        name="moe_router",
    )(x, wt, bias, tri)


def _slots_kernel(ids_ref, cnt_ref, slots_ref, blk_ref, *, nblk_pad):
    tm = ids_ref.shape[1]
    cnt = cnt_ref[:, 0:1]
    padded = jnp.floor((cnt + (MOE_BLOCK - 1)) / MOE_BLOCK) * MOE_BLOCK
    er = lax.broadcasted_iota(I32, (N_EXPERTS, N_EXPERTS), 0)
    ec = lax.broadcasted_iota(I32, (N_EXPERTS, N_EXPERTS), 1)
    padded_row = jnp.sum(jnp.where(er == ec, padded, 0.0), axis=0, keepdims=True)
    p_start = jnp.sum(jnp.where(ec < er, padded_row, 0.0), axis=1, keepdims=True)
    p_end = p_start + padded
    xrow = lax.broadcasted_iota(I32, (N_EXPERTS, tm), 0)
    e0 = ids_ref[0:1, :]
    e1 = ids_ref[1:2, :]
    s0 = jnp.sum(jnp.where(xrow == e0, p_start, 0.0), axis=0, keepdims=True).astype(I32) + ids_ref[2:3, :]
    s1 = jnp.sum(jnp.where(xrow == e1, p_start, 0.0), axis=0, keepdims=True).astype(I32) + ids_ref[3:4, :]
    slots_ref[...] = jnp.zeros_like(slots_ref)
    slots_ref[0:1, :] = s0
    slots_ref[1:2, :] = s1
    slots_ref[2:3, :] = s0 | (s1 << 16)
    bstart = (lax.broadcasted_iota(I32, (1, nblk_pad), 1) * MOE_BLOCK).astype(F32)
    blk_e = jnp.minimum(jnp.sum(jnp.where(p_end <= bstart, 1.0, 0.0), axis=0, keepdims=True), N_EXPERTS - 1.0)
    total = jnp.sum(padded, axis=0, keepdims=True)
    erow = lax.broadcasted_iota(I32, (N_EXPERTS, nblk_pad), 0).astype(F32)
    own_end = jnp.sum(jnp.where(erow == blk_e, p_end, 0.0), axis=0, keepdims=True)
    nxt_e = jnp.minimum(jnp.sum(jnp.where(p_end <= own_end, 1.0, 0.0), axis=0, keepdims=True), N_EXPERTS - 1.0)
    nxt_e = jnp.where(own_end < total, nxt_e, -1.0)
    blk_ref[...] = jnp.zeros_like(blk_ref)
    blk_ref[0:1, :] = blk_e.astype(I32)
    blk_ref[1:2, :] = jnp.broadcast_to((total / MOE_BLOCK).astype(I32), (1, nblk_pad))
    blk_ref[2:3, :] = nxt_e.astype(I32)
    lane = lax.broadcasted_iota(I32, (N_EXPERTS, nblk_pad), 1).astype(F32)
    blk_ref[3:4, :] = jnp.sum(jnp.where(erow == lane, p_start + cnt, 0.0), axis=0, keepdims=True).astype(I32)
    blk_ref[4:5, :] = jnp.sum(jnp.where(erow == lane, p_end, 0.0), axis=0, keepdims=True).astype(I32)


def _slots(ids, cnt, tm, nblk_pad):
    n = ids.shape[1]
    return pl.pallas_call(
        functools.partial(_slots_kernel, nblk_pad=nblk_pad),
        grid=(n // tm,),
        in_specs=[pl.BlockSpec((SUBLANES, tm), lambda i: (0, i)),
                  pl.BlockSpec((N_EXPERTS, LANES), lambda i: (0, 0))],
        out_specs=[pl.BlockSpec((SUBLANES, tm), lambda i: (0, i)),
                   pl.BlockSpec((SUBLANES, nblk_pad), lambda i: (0, 0))],
        out_shape=[jax.ShapeDtypeStruct((SUBLANES, n), I32), jax.ShapeDtypeStruct((SUBLANES, nblk_pad), I32)],
        compiler_params=_cparams(("arbitrary",)),
        name="moe_slots",
    )(ids, cnt)


def _slot_tokens_kernel(slots_ref, blk_ref, tok_ref, *, n, cap, nblk_pad):
    def zero(j, carry):
        tok_ref[j] = 0
        return carry

    def scatter(t, carry):
        both = slots_ref[t]
        tok_ref[both & 0xFFFF] = t
        tok_ref[lax.shift_right_logical(both, 16)] = t
        return carry

    for e in range(N_EXPERTS):
        lax.fori_loop(blk_ref[3 * nblk_pad + e], blk_ref[4 * nblk_pad + e], zero, 0)
    lax.fori_loop(blk_ref[nblk_pad] * MOE_BLOCK, cap, zero, 0)
    lax.fori_loop(0, n, scatter, 0, unroll=8)


def _slot_tokens(slots_packed, blk_flat, n, cap, nblk_pad):
    return pl.pallas_call(
        functools.partial(_slot_tokens_kernel, n=n, cap=cap, nblk_pad=nblk_pad),
        grid_spec=pltpu.PrefetchScalarGridSpec(
            num_scalar_prefetch=2,
            grid=(1,),
            in_specs=[],
            out_specs=pl.BlockSpec(memory_space=pltpu.SMEM)),
        out_shape=jax.ShapeDtypeStruct((cap,), I32),
        compiler_params=_cparams(("arbitrary",)),
        name="moe_slot_tokens",
    )(slots_packed, blk_flat)


ROW_BUFS = 3


def _expert_kernel(blk_ref, tok_ref, xpk_hbm, wg_hbm, wu_hbm, wd_hbm, y_ref, xbuf, wgb, wub, wdb,
                   sems, wsems, cnt_ref, *, nblk_pad):
    i = pl.program_id(0)
    n_used = blk_ref[nblk_pad]

    def weight_copies(e, slot):
        return [pltpu.make_async_copy(src.at[e], dst.at[slot], wsems.at[slot])
                for src, dst in ((wg_hbm, wgb), (wu_hbm, wub), (wd_hbm, wdb))]

    def start_rows(block, buf, r_lo=0, r_hi=MOE_BLOCK):
        for r in range(r_lo, r_hi):
            src0 = pl.multiple_of(tok_ref[block * MOE_BLOCK + r] * SUBLANES, SUBLANES)
            pltpu.make_async_copy(xpk_hbm.at[pl.ds(src0, SUBLANES)],
                                  xbuf.at[buf, pl.ds(r * SUBLANES, SUBLANES)], sems.at[buf]).start(priority=r % 2)

    def wait_rows(buf):
        pltpu.make_async_copy(xpk_hbm.at[pl.ds(0, MOE_BLOCK * SUBLANES)], xbuf.at[buf], sems.at[buf]).wait()

    @pl.when(i == 0)
    def _():
        cnt_ref[0] = 0
        for cp in weight_copies(blk_ref[0], 0):
            cp.start(priority=1)
        for ahead in range(ROW_BUFS - 1):
            start_rows(jnp.minimum(ahead, n_used - 1), ahead)

    @pl.when((i < n_used) & ((i == 0) | (blk_ref[i] != blk_ref[jnp.maximum(i - 1, 0)])))
    def _():
        slot = cnt_ref[0] % 2
        cnt_ref[0] = cnt_ref[0] + 1
        for cp in weight_copies(blk_ref[i], slot):
            cp.wait()
        nxt_e = blk_ref[2 * nblk_pad + i]

        @pl.when(nxt_e >= 0)
        def _():
            for cp in weight_copies(nxt_e, 1 - slot):
                cp.start(priority=1)

    @pl.when(i < n_used)
    def _():
        wslot = (cnt_ref[0] + 1) % 2
        buf = i % ROW_BUFS
        nbuf = (i + ROW_BUFS - 1) % ROW_BUFS
        nxt = jnp.minimum(i + ROW_BUFS - 1, n_used - 1)
        wait_rows(buf)
        lo, hi = _unpack_halves(_load_row_tiles(xbuf.at[buf], MOE_BLOCK))
        xb = jnp.concatenate([lo.astype(BF16), hi.astype(BF16)], axis=-1)
        g = _dot(xb, wgb[wslot])
        start_rows(nxt, nbuf, 0, MOE_BLOCK // 2)
        u = _dot(xb, wub[wslot])
        start_rows(nxt, nbuf, MOE_BLOCK // 2, MOE_BLOCK)
        hmid = (g * jax.nn.sigmoid(g) * u).astype(BF16)
        _store_row_tiles(y_ref, _pack_halves(_dot(hmid, wdb[wslot])))

    @pl.when(i == n_used - 1)
    def _():
        for ahead in range(1, ROW_BUFS):
            wait_rows((i + ahead) % ROW_BUFS)

    @pl.when(i >= n_used)
    def _():
        y_ref[...] = jnp.zeros_like(y_ref)


def _experts(blk_flat, slot_tok, xpk, wg, wu, wd, nblk_pad):
    d, de = wg.shape[1], wg.shape[2]
    assert d == 2 * SUBLANES * LANES and xpk.shape[1] == LANES
    cap = slot_tok.shape[0]
    tile_rows = MOE_BLOCK * SUBLANES
    return pl.pallas_call(
        functools.partial(_expert_kernel, nblk_pad=nblk_pad),
        grid_spec=pltpu.PrefetchScalarGridSpec(
            num_scalar_prefetch=2,
            grid=(cap // MOE_BLOCK,),
            in_specs=[pl.BlockSpec(memory_space=pl.ANY), pl.BlockSpec(memory_space=pl.ANY),
                      pl.BlockSpec(memory_space=pl.ANY), pl.BlockSpec(memory_space=pl.ANY)],
            out_specs=pl.BlockSpec((tile_rows, LANES), lambda i, blk, tok: (i, 0)),
            scratch_shapes=[pltpu.VMEM((ROW_BUFS, tile_rows, LANES), U32),
                            pltpu.VMEM((2, d, de), BF16), pltpu.VMEM((2, d, de), BF16), pltpu.VMEM((2, de, d), BF16),
                            pltpu.SemaphoreType.DMA((ROW_BUFS,)), pltpu.SemaphoreType.DMA((2,)),
                            pltpu.SMEM((1,), I32)]),
        out_shape=jax.ShapeDtypeStruct((cap * SUBLANES, LANES), U32),
        compiler_params=_cparams(("arbitrary",)),
        name="moe_experts",
    )(blk_flat, slot_tok, xpk, wg, wu, wd)


def _tail_kernel(slots_ref, x_ref, xb_ref, p_ref, wts_ref, wgate_ref, bgate_ref, wproj_ref, g_ref, b_ref,
                 yb_hbm, y_ref, ybf_ref, rows_ref, sems):
    i = pl.program_id(0)
    nsteps = pl.num_programs(0)
    tm = x_ref.shape[0]
    n = nsteps * tm

    def start_rows(step, buf):
        for t in range(tm):
            for which in range(2):
                src0 = pl.multiple_of(slots_ref[which * n + step * tm + t] * SUBLANES, SUBLANES)
                pltpu.make_async_copy(yb_hbm.at[pl.ds(src0, SUBLANES)],
                                      rows_ref.at[buf, which, pl.ds(t * SUBLANES, SUBLANES)], sems.at[buf]).start()

    def wait_rows(buf):
        for which in range(2):
            pltpu.make_async_copy(yb_hbm.at[pl.ds(0, tm * SUBLANES)], rows_ref.at[buf, which],
                                  sems.at[buf]).wait()

    @pl.when(i == 0)
    def _():
        start_rows(0, 0)

    buf = i % 2
    nxt = jnp.minimum(i + 1, nsteps - 1)
    wait_rows(buf)
    gate_pre = _dot(xb_ref[...], wgate_ref[...])
    proj = _dot(p_ref[0].astype(BF16), wproj_ref[...])
    start_rows(nxt, 1 - buf)
    ple = jax.nn.sigmoid(gate_pre + bgate_ref[...]) * proj
    w = wts_ref[...]
    lo0, hi0 = _unpack_halves(_load_row_tiles(rows_ref.at[buf, 0], tm))
    lo1, hi1 = _unpack_halves(_load_row_tiles(rows_ref.at[buf, 1], tm))
    w0 = w[:, 0:1]
    w1 = w[:, 1:2]
    ffn = jnp.concatenate([lo0 * w0 + lo1 * w1, hi0 * w0 + hi1 * w1], axis=-1)
    z = DN_ALPHA * x_ref[...] + ffn + ple
    y = _layer_norm_rows(z, g_ref[...], b_ref[...])
    y_ref[...] = y
    ybf_ref[...] = y.astype(BF16)

    @pl.when(i == nsteps - 1)
    def _():
        wait_rows(1 - buf)


def _layer_tail(slots_flat, x, xb, p, layer, wts_t, wgate, bgate, wproj, g, b, yb, tm):
    n, d = x.shape
    pd = p.shape[2]
    return pl.pallas_call(
        _tail_kernel,
        grid_spec=pltpu.PrefetchScalarGridSpec(
            num_scalar_prefetch=1,
            grid=(n // tm,),
            in_specs=[pl.BlockSpec((tm, d), lambda i, s: (i, 0)),
                      pl.BlockSpec((tm, d), lambda i, s: (i, 0)),
                      pl.BlockSpec((1, tm, pd), lambda i, s: (layer, i, 0)),
                      pl.BlockSpec((tm, 2), lambda i, s: (i, 0)),
                      pl.BlockSpec((d, d), lambda i, s: (0, 0)),
                      pl.BlockSpec((1, d), lambda i, s: (0, 0)),
                      pl.BlockSpec((pd, d), lambda i, s: (0, 0)),
                      pl.BlockSpec((1, d), lambda i, s: (0, 0)),
                      pl.BlockSpec((1, d), lambda i, s: (0, 0)),
                      pl.BlockSpec(memory_space=pl.ANY)],
            out_specs=[pl.BlockSpec((tm, d), lambda i, s: (i, 0)),
                       pl.BlockSpec((tm, d), lambda i, s: (i, 0))],
            scratch_shapes=[pltpu.VMEM((2, 2, tm * SUBLANES, LANES), U32), pltpu.SemaphoreType.DMA((2,))]),
        out_shape=[jax.ShapeDtypeStruct((n, d), F32), jax.ShapeDtypeStruct((n, d), BF16)],
        compiler_params=_cparams(("arbitrary",)),
        name="layer_tail",
    )(slots_flat, x, xb, p, wts_t, wgate, bgate.reshape(1, d), wproj, g.reshape(1, d), b.reshape(1, d), yb)


def _rope_table(seq, dim):
    pos = jnp.arange(seq, dtype=F32)
    inv = jnp.exp(jnp.arange(0, dim, 2, dtype=F32) * (-math.log(ROPE_BASE) / dim))
    ang = pos[:, None] * inv[None, :]
    return jnp.cos(ang), jnp.sin(ang)


def _moe_layer(x, xb, xpk, p, layer, w_group, b_group, w_router, b_router, w_gate, w_up, w_down,
               ple_w_proj, ple_w_gate, ple_b_gate, ln_g, ln_b):
    n, d = x.shape
    nblk = -(-(2 * n) // MOE_BLOCK) + N_EXPERTS
    nblk_pad = -(-nblk // LANES) * LANES
    cap = nblk * MOE_BLOCK
    wt = jnp.zeros((ROUTER_ROWS, d), F32)
    wt = wt.at[0:N_GROUPS].set(w_group.T)
    wt = wt.at[SUBLANES:].set(w_router.transpose(0, 2, 1).reshape(N_EXPERTS, d))
    bias = jnp.zeros((ROUTER_ROWS, 1), F32)
    bias = bias.at[0:N_GROUPS, 0].set(b_group)
    bias = bias.at[SUBLANES:, 0].set(b_router.reshape(N_EXPERTS))
    ids, wts, cnt = _router(xb, wt.astype(BF16), bias, tm=512)
    slots, blk = _slots(ids, cnt, tm=min(2048, n), nblk_pad=nblk_pad)
    slots_flat = slots[0:2].reshape(2 * n)
    blk_flat = blk[0:5].reshape(5 * nblk_pad)
    assert cap <= 1 << 16
    slot_tok = _slot_tokens(slots[2], blk_flat, n, cap, nblk_pad)
    yb = _experts(blk_flat, slot_tok, xpk, w_gate, w_up, w_down, nblk_pad)
    return _layer_tail(slots_flat, x, xb, p, layer, wts[0:2].T, ple_w_gate, ple_b_gate, ple_w_proj, ln_g, ln_b,
                       yb, tm=512)


def _mixer_ab(x, xres, batch, seq, w_in, rpb, q_norm, w_uq, kv_norm, w_ukv, w_out_all, j, ln_g, ln_b,
              expert_ws, layer, dense_early, dense_late):
    d = x.shape[1]
    o1 = 3 * NA_WIDTH
    o2 = o1 + MLA_Q_RANK
    o3 = o2 + MLA_KV_RANK
    half = MLA_ROPE_DIM // 2
    kr = w_in[:, o3:o3 + MLA_ROPE_DIM]
    kr_sw = jnp.concatenate([kr[:, half:], kr[:, :half]], axis=1)
    width = -(-(o3 + 2 * MLA_ROPE_DIM) // 1024) * 1024
    w_in_p = jnp.concatenate([w_in, kr_sw, jnp.zeros((d, width - o3 - 2 * MLA_ROPE_DIM), F32)], axis=1)
    h, early_b = _matmul(x, w_in_p.astype(BF16), BF16, tm=1024, tn=1024, cast_ws=dense_early)
    a_out = _na_attention(h, _na_bias_tables(rpb), batch, seq)
    dq = MLA_NOPE_DIM + MLA_ROPE_DIM
    wq = w_uq.reshape(MLA_Q_RANK, MLA_HEADS, dq)
    wq_pe = wq[:, :, MLA_NOPE_DIM:]
    wq_p = jnp.concatenate([wq, wq_pe[:, :, half:], wq_pe[:, :, :half]], axis=2)
    wq_p = wq_p.reshape(MLA_Q_RANK, MLA_HEADS * 2 * LANES).astype(BF16)
    wkv = w_ukv.reshape(MLA_KV_RANK, MLA_HEADS, MLA_NOPE_DIM + MLA_V_DIM)
    wk = wkv[:, :, :MLA_NOPE_DIM].reshape(MLA_KV_RANK, MLA_HEADS * MLA_NOPE_DIM).astype(BF16)
    wvt = wkv[:, :, MLA_NOPE_DIM:].reshape(MLA_KV_RANK, MLA_HEADS * MLA_V_DIM).T.astype(BF16)
    cos, sin = _rope_table(seq, MLA_ROPE_DIM)
    zpad = jnp.zeros((seq, LANES - MLA_ROPE_DIM), F32)
    cosf = jnp.concatenate([cos, cos, zpad], axis=1)
    sinf = jnp.concatenate([-sin, sin, zpad], axis=1)
    q_p, k_p, vt = _mla_prep(h, q_norm, kv_norm, wq_p, wk, wvt, cosf, sinf, o1, seq, tm=512)
    b_out, cast_out = _mla_attention(q_p, k_p, vt, batch, seq, tq=min(1024, seq), tk=1024, sub=256,
                                     cast_ws=expert_ws, cast_layer=layer,
                                     cast_dense=[(w_out_all, j)] + list(dense_late))
    expert_wb, w_out_b, late_b = cast_out[:len(expert_ws)], cast_out[len(expert_ws)], cast_out[len(expert_ws) + 1:]
    outs = _proj_ln([a_out, b_out], [w_out_b[:NA_WIDTH], w_out_b[NA_WIDTH:]], xres, ln_g, ln_b, tm=512, nk=1)
    return outs, expert_wb, early_b, late_b


def _mixer_c(xb, xres, batch, seq, w_in, log_rate_f, log_rate_b, w_out, ln_g, ln_b, expert_ws, layer):
    cosr, sinr = _rope_table(seq, RET_QK_DIM)
    n_q = RET_HEADS * RET_QK_DIM
    hc, expert_wb = _matmul_rope(xb, w_in, cosr, sinr, BF16, tm=min(1024, seq), tn=1024,
                                 n_q_cols=n_q, n_rope_cols=2 * n_q, gate_col0=2 * n_q + RET_HEADS * RET_V_DIM,
                                 head_w=RET_QK_DIM, q_scale=RET_QK_DIM ** -0.5,
                                 cast_ws=expert_ws, cast_layer=layer, cast_j=8)
    lg = jnp.stack([jnp.log1p(-jnp.exp(log_rate_f.astype(F32))), jnp.log1p(-jnp.exp(log_rate_b.astype(F32)))])
    r = _retention(hc, lg, batch, seq, c_len=256, group=min(8, seq // 256))
    return _proj_ln([r], [w_out], xres, ln_g, ln_b, tm=512, nk=1), expert_wb


def kernel(x, p, ab_w_in, ab_rpb, ab_q_norm, ab_w_uq, ab_kv_norm, ab_w_ukv, ab_w_out, c_w_in, c_log_rate_f,
           c_log_rate_b, c_w_out, ln1_g, ln1_b, moe_w_group, moe_b_group, moe_w_router, moe_b_router,
           moe_w_gate, moe_w_up, moe_w_down, ple_w_proj, ple_w_gate, ple_b_gate, ln2_g, ln2_b):
    batch, seq, d = x.shape
    n = batch * seq
    xf = x.reshape(n, d)
    p_flat = p.reshape(DEPTH, n, -1)
    expert_ws = (moe_w_gate, moe_w_up, moe_w_down)
    assert DEPTH == 2
    dense_late = [(c_w_out, 0)] + [(w, i) for i in range(DEPTH) for w in (ple_w_gate, ple_w_proj)]
    (xf, xb, xpk), (wg, wu, wd), (c_w_in_b,), late_b = _mixer_ab(
        xf, xf, batch, seq, ab_w_in[0], ab_rpb[0], ab_q_norm[0], ab_w_uq[0], ab_kv_norm[0], ab_w_ukv[0],
        ab_w_out, 0, ln1_g[0], ln1_b[0], expert_ws, 0, dense_early=[(c_w_in, 0)], dense_late=dense_late)
    c_w_out_b, ple_b = late_b[0], late_b[1:]
    xf, xb = _moe_layer(xf, xb, xpk, p_flat, 0, moe_w_group[0], moe_b_group[0], moe_w_router[0],
                        moe_b_router[0], wg, wu, wd, ple_b[1], ple_b[0], ple_b_gate[0], ln2_g[0], ln2_b[0])
    (xf, xb, xpk), (wg, wu, wd) = _mixer_c(
        xb, xf, batch, seq, c_w_in_b, c_log_rate_f[0], c_log_rate_b[0], c_w_out_b, ln1_g[1], ln1_b[1],
        expert_ws, 1)
    xf, xb = _moe_layer(xf, xb, xpk, p_flat, 1, moe_w_group[1], moe_b_group[1], moe_w_router[1],
                        moe_b_router[1], wg, wu, wd, ple_b[3], ple_b[2], ple_b_gate[1], ln2_g[1], ln2_b[1])
    return xf.reshape(batch, seq, d)
```

```python
import functools
import math

import numpy as np
import jax
import jax.numpy as jnp
from jax import lax
from jax.experimental import pallas as pl
from jax.experimental.pallas import tpu as pltpu

DEPTH = 2
GRID_W = 64
NA_HEADS = 8
NA_HEAD_DIM = 128
NA_WIN_H = 8
NA_WIN_W = 16
MLA_HEADS = 8
MLA_Q_RANK = 512
MLA_KV_RANK = 256
MLA_NOPE_DIM = 128
MLA_ROPE_DIM = 64
MLA_V_DIM = 128
RET_HEADS = 8
RET_QK_DIM = 256
RET_V_DIM = 512
RET_CHUNK = 128
N_GROUPS = 4
EXPERTS_PER_GROUP = 8
N_EXPERTS = N_GROUPS * EXPERTS_PER_GROUP
D_EXPERT = 512
MOE_BLOCK = 128
ROPE_BASE = 10000.0
LN_EPS = 1e-5
RMS_EPS = 1e-6
DN_ALPHA = (2 * DEPTH) ** 0.25
NA_WIDTH = NA_HEADS * NA_HEAD_DIM

LANES = 128
SUBLANES = 8
VMEM_LIMIT_BYTES = 60 * 1024 * 1024
MASK_VALUE = -1e30

F32 = jnp.float32
BF16 = jnp.bfloat16
I32 = jnp.int32
U32 = jnp.uint32


def _cparams(sem):
    return pltpu.CompilerParams(dimension_semantics=sem, vmem_limit_bytes=VMEM_LIMIT_BYTES)


def _dot(a, b):
    return jnp.dot(a, b, preferred_element_type=F32)


def _dot_nt(a, b, precision=None):
    return lax.dot_general(a, b, (((1,), (1,)), ((), ())), preferred_element_type=F32,
                           precision=precision)


def _pack_halves(y):
    c = y.shape[1] // 2
    bits = pltpu.bitcast(y.astype(BF16).astype(F32), U32)
    return (bits[:, :c] >> 16) | (bits[:, c:] & jnp.uint32(0xFFFF0000))


def _unpack_halves(w):
    lo = pltpu.bitcast(w << 16, F32)
    hi = pltpu.bitcast(w & jnp.uint32(0xFFFF0000), F32)
    return lo, hi


def _store_row_tiles(ref, packed):
    m = packed.shape[0]
    for s in range(SUBLANES):
        ref[pl.ds(s, m, stride=SUBLANES), :] = packed[:, s * LANES:(s + 1) * LANES]


def _load_row_tiles(ref, m):
    return jnp.concatenate([ref[pl.ds(s, m, stride=SUBLANES), :] for s in range(SUBLANES)], axis=-1)


def _mm_kernel(x_ref, w_ref, *rest, n_cast):
    cast_src, o_ref, cast_dst = rest[:n_cast], rest[n_cast], rest[n_cast + 1:]
    _cast_blocks(cast_src, cast_dst)
    o_ref[...] = _dot(x_ref[...].astype(BF16), w_ref[...]).astype(o_ref.dtype)


def _matmul(x, w, out_dtype, tm, tn, cast_ws):
    m, k = x.shape
    n = w.shape[1]
    nj = n // tn
    streams = [_cast_rows_specs(cw, lead, (m // tm) * nj, lambda i, j: i * nj + j) for cw, lead in cast_ws]
    outs = pl.pallas_call(
        functools.partial(_mm_kernel, n_cast=len(cast_ws)),
        grid=(m // tm, nj),
        in_specs=[pl.BlockSpec((tm, k), lambda i, j: (i, 0)),
                  pl.BlockSpec((k, tn), lambda i, j: (0, j))] + [s[0] for s in streams],
        out_specs=[pl.BlockSpec((tm, tn), lambda i, j: (i, j))] + [s[1] for s in streams],
        out_shape=[jax.ShapeDtypeStruct((m, n), out_dtype)] + [s[2] for s in streams],
        compiler_params=_cparams(("arbitrary", "arbitrary")),
        name="matmul",
    )(x, w, *[cw for cw, _ in cast_ws])
    return outs[0], outs[1:]


def _mm_rope_kernel(x_ref, w_ref, cos_ref, sin_ref, *rest, n_q_tiles, n_rope_tiles, first_gate_tile,
                    head_w, q_scale, n_cast):
    cast_src, o_ref, cast_dst = rest[:n_cast], rest[n_cast], rest[n_cast + 1:]
    j = pl.program_id(1)
    _cast_blocks(cast_src, cast_dst)
    acc = _dot(x_ref[...].astype(BF16), w_ref[...])

    @pl.when((j >= n_rope_tiles) & (j < first_gate_tile))
    def _():
        o_ref[...] = acc.astype(o_ref.dtype)

    @pl.when(j >= first_gate_tile)
    def _():
        o_ref[...] = (acc * jax.nn.sigmoid(acc)).astype(o_ref.dtype)

    @pl.when(j < n_rope_tiles)
    def _():
        scale = jnp.where(j < n_q_tiles, q_scale, 1.0)
        cos = cos_ref[...] * scale
        sin = sin_ref[...] * scale
        half = head_w // 2
        for c0 in range(0, acc.shape[1], head_w):
            x1 = acc[:, c0:c0 + half]
            x2 = acc[:, c0 + half:c0 + head_w]
            o_ref[:, c0:c0 + half] = (x1 * cos - x2 * sin).astype(o_ref.dtype)
            o_ref[:, c0 + half:c0 + head_w] = (x2 * cos + x1 * sin).astype(o_ref.dtype)


def _matmul_rope(x, w, cos, sin, out_dtype, tm, tn, n_q_cols, n_rope_cols, gate_col0, head_w, q_scale,
                 cast_ws, cast_layer, cast_j):
    m, k = x.shape
    n = w.shape[1]
    nsb = cos.shape[0] // tm
    streams = [_cast_stream_specs(cw, cast_layer, (m // tm) * cast_j,
                                  lambda i, j: i * cast_j + jnp.minimum(j, cast_j - 1)) for cw in cast_ws]
    outs = pl.pallas_call(
        functools.partial(_mm_rope_kernel, n_q_tiles=n_q_cols // tn, n_rope_tiles=n_rope_cols // tn,
                          first_gate_tile=gate_col0 // tn, head_w=head_w, q_scale=q_scale,
                          n_cast=len(cast_ws)),
        grid=(m // tm, n // tn),
        in_specs=[pl.BlockSpec((tm, k), lambda i, j: (i, 0)),
                  pl.BlockSpec((k, tn), lambda i, j: (0, j)),
                  pl.BlockSpec((tm, head_w // 2), lambda i, j: (i % nsb, 0)),
                  pl.BlockSpec((tm, head_w // 2), lambda i, j: (i % nsb, 0))] + [s[0] for s in streams],
        out_specs=[pl.BlockSpec((tm, tn), lambda i, j: (i, j))] + [s[1] for s in streams],
        out_shape=[jax.ShapeDtypeStruct((m, n), out_dtype)] + [s[2] for s in streams],
        compiler_params=_cparams(("arbitrary", "arbitrary")),
        name="matmul_rope",
    )(x, w, cos, sin, *cast_ws)
    return outs[0], outs[1:]


def _layer_norm_rows(z, g, b):
    mean = jnp.mean(z, axis=-1, keepdims=True)
    zc = z - mean
    var = jnp.mean(zc * zc, axis=-1, keepdims=True)
    return zc * lax.rsqrt(var + LN_EPS) * g + b


def _proj_ln_kernel(*refs, n_act, nk):
    acts = refs[:n_act]
    ws = refs[n_act:2 * n_act]
    x_ref, g_ref, b_ref, y_ref, yb_ref, yp_ref = refs[2 * n_act:2 * n_act + 6]
    k = pl.program_id(1)
    tm = x_ref.shape[0]
    n_split = 2
    hm = tm // n_split

    def product(rows):
        part = _dot(acts[0][rows, :], ws[0][...])
        for a, w in zip(acts[1:], ws[1:]):
            part = part + _dot(a[rows, :], w[...])
        return part

    def finish(rows, h, proj):
        z = DN_ALPHA * x_ref[rows, :] + proj
        y = _layer_norm_rows(z, g_ref[...], b_ref[...])
        y_ref[rows, :] = y
        yb_ref[rows, :] = y.astype(BF16)
        _store_row_tiles(yp_ref.at[pl.ds(h * hm * SUBLANES, hm * SUBLANES)], _pack_halves(y))

    if nk == 1:
        for h in range(n_split):
            rows = pl.ds(h * hm, hm)
            finish(rows, h, product(rows))
        return
    acc_ref = refs[2 * n_act + 6]

    @pl.when(k == 0)
    def _():
        acc_ref[...] = product(pl.ds(0, tm))

    @pl.when((k > 0) & (k < nk - 1))
    def _():
        acc_ref[...] = acc_ref[...] + product(pl.ds(0, tm))

    @pl.when(k == nk - 1)
    def _():
        for h in range(n_split):
            rows = pl.ds(h * hm, hm)
            finish(rows, h, acc_ref[rows, :] + product(rows))


def _proj_ln(acts, ws, x, g, b, tm, nk):
    m, d = x.shape
    n_act = len(acts)
    in_specs = []
    for a in acts:
        kk = a.shape[1] // nk
        in_specs.append(pl.BlockSpec((tm, kk), lambda i, k: (i, k)))
    for w in ws:
        kk = w.shape[0] // nk
        mode = pl.Buffered(1) if nk == 1 else None
        in_specs.append(pl.BlockSpec((kk, d), lambda i, k: (k, 0), pipeline_mode=mode))
    in_specs += [pl.BlockSpec((tm, d), lambda i, k: (i, 0)),
                 pl.BlockSpec((1, d), lambda i, k: (0, 0)),
                 pl.BlockSpec((1, d), lambda i, k: (0, 0))]
    return pl.pallas_call(
        functools.partial(_proj_ln_kernel, n_act=n_act, nk=nk),
        grid=(m // tm, nk),
        in_specs=in_specs,
        out_specs=[pl.BlockSpec((tm, d), lambda i, k: (i, 0)),
                   pl.BlockSpec((tm, d), lambda i, k: (i, 0)),
                   pl.BlockSpec((tm * SUBLANES, LANES), lambda i, k: (i, 0))],
        out_shape=[jax.ShapeDtypeStruct((m, d), F32), jax.ShapeDtypeStruct((m, d), BF16),
                   jax.ShapeDtypeStruct((m * SUBLANES, LANES), U32)],
        scratch_shapes=[pltpu.VMEM((tm, d), F32)] if nk > 1 else [],
        compiler_params=_cparams(("parallel", "arbitrary")),
        name="proj_ln",
    )(*acts, *ws, x, g.reshape(1, d), b.reshape(1, d))


def _na_bias_tables(rpb):
    nh = rpb.shape[0]
    c = np.arange(GRID_W)
    cs = np.clip(c - NA_WIN_W // 2, 0, GRID_W - NA_WIN_W)
    kc = np.arange(GRID_W)
    valid = (kc[None, :] >= cs[:, None]) & (kc[None, :] < cs[:, None] + NA_WIN_W)
    dc = kc[None, :] - c[:, None] + NA_WIN_W - 1
    onehot = (dc[:, :, None] == np.arange(2 * NA_WIN_W - 1)[None, None, :]) & valid[:, :, None]
    cols = jnp.einsum("hrd,ckd->hrck", rpb.astype(F32), jnp.asarray(onehot, F32),
                      precision=lax.Precision.HIGHEST)
    cols = jnp.where(jnp.asarray(valid)[None, None], cols, MASK_VALUE)
    tabs = jnp.stack([cols[:, off:off + NA_WIN_H] for off in range(NA_WIN_H)], axis=1)
    return tabs.transpose(0, 1, 3, 2, 4).reshape(nh, NA_WIN_H, GRID_W, NA_WIN_H * GRID_W)


def _na_kernel(q_ref, k_ref, v_ref, bias_ref, o_ref, *, rows, group):
    scale = NA_HEAD_DIM ** -0.5
    nkeys = NA_WIN_H * GRID_W

    def body(i, carry):
        geom, scores = [], []
        for u in range(group):
            r = i * group + u
            rs = jnp.clip(r - NA_WIN_H // 2, 0, rows - NA_WIN_H)
            off = rs - r + NA_WIN_H - 1
            q0 = pl.multiple_of(r * GRID_W, GRID_W)
            k0 = pl.multiple_of(rs * GRID_W, GRID_W)
            geom.append((q0, k0))
            s = _dot_nt(q_ref[pl.ds(q0, GRID_W), :], k_ref[pl.ds(k0, nkeys), :])
            scores.append(s * scale + bias_ref[0, off])
        for (q0, k0), s in zip(geom, scores):
            m = jnp.max(s, axis=-1, keepdims=True)
            p = jnp.exp(s - m)
            l = jnp.sum(p, axis=-1, keepdims=True)
            o = _dot(p.astype(BF16), v_ref[pl.ds(k0, nkeys), :]) / l
            o_ref[pl.ds(q0, GRID_W), :] = o.astype(o_ref.dtype)
        return carry

    lax.fori_loop(0, rows // group, body, 0)


def _na_attention(h, bias_tables, batch, seq):
    rows = seq // GRID_W
    d = NA_HEAD_DIM
    nkeys = NA_WIN_H * GRID_W
    return pl.pallas_call(
        functools.partial(_na_kernel, rows=rows, group=min(32, rows)),
        grid=(batch, NA_HEADS),
        in_specs=[pl.BlockSpec((seq, d), lambda b, hh: (b, hh)),
                  pl.BlockSpec((seq, d), lambda b, hh: (b, NA_HEADS + hh)),
                  pl.BlockSpec((seq, d), lambda b, hh: (b, 2 * NA_HEADS + hh)),
                  pl.BlockSpec((1, NA_WIN_H, GRID_W, nkeys), lambda b, hh: (hh, 0, 0, 0))],
        out_specs=pl.BlockSpec((seq, d), lambda b, hh: (b, hh)),
        out_shape=jax.ShapeDtypeStruct((batch * seq, NA_WIDTH), BF16),
        compiler_params=_cparams(("parallel", "arbitrary")),
        name="na_attention",
    )(h, h, h, bias_tables)


def _rms_rows(x, g):
    return x * lax.rsqrt(jnp.mean(x * x, axis=-1, keepdims=True) + RMS_EPS) * g


def _rope_lanes(t, cosf, sinf):
    return t * cosf + pltpu.roll(t, LANES // 2, 1) * sinf


def _mla_prep_kernel(cq_ref, ckv_ref, kr_ref, gq_ref, gkv_ref, wq_ref, wk_ref, wvt_ref, cos_ref, sin_ref,
                     q_ref, k_ref, vt_ref):
    dq = MLA_NOPE_DIM + MLA_ROPE_DIM
    cosf = cos_ref[...]
    sinf = sin_ref[...]
    cqn = _rms_rows(cq_ref[...].astype(F32), gq_ref[...]).astype(BF16)
    ckvn = _rms_rows(ckv_ref[...].astype(F32), gkv_ref[...]).astype(BF16)
    qf = _dot(cqn, wq_ref[...]) * (dq ** -0.5 * math.log2(math.e))
    kf = _dot(ckvn, wk_ref[...])
    vt_ref[...] = _dot_nt(wvt_ref[...], ckvn).astype(BF16)
    kpe = _rope_lanes(kr_ref[...].astype(F32), cosf, sinf).astype(BF16)
    for hh in range(MLA_HEADS):
        c0 = hh * 2 * LANES
        q_ref[:, c0:c0 + LANES] = qf[:, c0:c0 + LANES].astype(BF16)
        q_ref[:, c0 + LANES:c0 + 2 * LANES] = _rope_lanes(qf[:, c0 + LANES:c0 + 2 * LANES], cosf, sinf).astype(BF16)
        k_ref[:, c0:c0 + LANES] = kf[:, hh * LANES:(hh + 1) * LANES].astype(BF16)
        k_ref[:, c0 + LANES:c0 + 2 * LANES] = kpe


def _mla_prep(h, gq, gkv, wq_p, wk, wvt, cosf, sinf, col_cq, seq, tm):
    n = h.shape[0]
    hw = MLA_HEADS * 2 * LANES
    nsb = seq // tm
    b_cq = col_cq // MLA_Q_RANK
    b_ckv = (col_cq + MLA_Q_RANK) // MLA_KV_RANK
    b_kr = (col_cq + MLA_Q_RANK + MLA_KV_RANK) // LANES
    return pl.pallas_call(
        _mla_prep_kernel,
        grid=(n // tm,),
        in_specs=[pl.BlockSpec((tm, MLA_Q_RANK), lambda i: (i, b_cq)),
                  pl.BlockSpec((tm, MLA_KV_RANK), lambda i: (i, b_ckv)),
                  pl.BlockSpec((tm, LANES), lambda i: (i, b_kr)),
                  pl.BlockSpec((1, MLA_Q_RANK), lambda i: (0, 0)),
                  pl.BlockSpec((1, MLA_KV_RANK), lambda i: (0, 0)),
                  pl.BlockSpec((MLA_Q_RANK, hw), lambda i: (0, 0)),
                  pl.BlockSpec((MLA_KV_RANK, MLA_HEADS * LANES), lambda i: (0, 0)),
                  pl.BlockSpec((MLA_HEADS * MLA_V_DIM, MLA_KV_RANK), lambda i: (0, 0)),
                  pl.BlockSpec((tm, LANES), lambda i: (i % nsb, 0)),
                  pl.BlockSpec((tm, LANES), lambda i: (i % nsb, 0))],
        out_specs=[pl.BlockSpec((tm, hw), lambda i: (i, 0)),
                   pl.BlockSpec((tm, hw), lambda i: (i, 0)),
                   pl.BlockSpec((MLA_HEADS * MLA_V_DIM, tm), lambda i: (0, i))],
        out_shape=[jax.ShapeDtypeStruct((n, hw), BF16), jax.ShapeDtypeStruct((n, hw), BF16),
                   jax.ShapeDtypeStruct((MLA_HEADS * MLA_V_DIM, n), BF16)],
        compiler_params=_cparams(("parallel",)),
        name="mla_prep",
    )(h, h, h, gq.reshape(1, -1), gkv.reshape(1, -1), wq_p, wk, wvt, cosf, sinf)


def _cast_stream_specs(w, layer, n_slots, slot_of):
    _, ne, rows, cols = w.shape
    if n_slots >= ne:
        e_per, rb = 1, n_slots // ne
        while rows % rb or (rows // rb) % (2 * SUBLANES):
            rb -= 1
    else:
        assert ne % n_slots == 0
        e_per, rb = ne // n_slots, 1
    n_blocks = (ne // e_per) * rb

    def block_of(*g):
        s = jnp.minimum(slot_of(*g), n_blocks - 1)
        return s // rb, s % rb

    src = pl.BlockSpec((1, e_per, rows // rb, cols), lambda *g: (layer, *block_of(*g), 0))
    dst = pl.BlockSpec((e_per, rows // rb, cols), lambda *g: (*block_of(*g), 0))
    return src, dst, jax.ShapeDtypeStruct((ne, rows, cols), BF16), n_blocks


def _cast_rows_specs(w, lead, n_slots, slot_of):
    rows, cols = w.shape[-2:]
    nb = min(n_slots, rows // (2 * SUBLANES))
    while rows % nb or (rows // nb) % (2 * SUBLANES):
        nb -= 1
    src = pl.BlockSpec((1, rows // nb, cols), lambda *g: (lead, jnp.minimum(slot_of(*g), nb - 1), 0))
    dst = pl.BlockSpec((rows // nb, cols), lambda *g: (jnp.minimum(slot_of(*g), nb - 1), 0))
    return src, dst, jax.ShapeDtypeStruct((rows, cols), BF16), nb


def _cast_blocks(srcs, dsts):
    for src, dst in zip(srcs, dsts):
        dst[...] = src[0].astype(BF16)


def _mla_attn_kernel(q_ref, k_ref, vt_ref, *rest, tk, sub, n_cast):
    cast_src, o_ref, cast_dst = rest[:n_cast], rest[n_cast], rest[n_cast + 1:]
    _cast_blocks(cast_src, cast_dst)
    nchunk = k_ref.shape[0] // tk
    tq = q_ref.shape[0]
    nsub = tq // sub
    qs = [q_ref[s * sub:(s + 1) * sub, :] for s in range(nsub)]
    m = [jnp.full((1, sub), MASK_VALUE, F32) for _ in range(nsub)]
    l = [jnp.zeros((1, sub), F32) for _ in range(nsub)]
    acc = [jnp.zeros((MLA_V_DIM, sub), F32) for _ in range(nsub)]

    def scores(s, c):
        return _dot_nt(k_ref[c * tk:(c + 1) * tk, :], qs[s])

    st_next = [scores(s, 0) for s in range(nsub)]
    for c in range(nchunk):
        for s in range(nsub):
            st = st_next[s]
            m_new = jnp.maximum(m[s], jnp.max(st, axis=0, keepdims=True))
            a = jnp.exp2(m[s] - m_new)
            p = jnp.exp2(st - m_new)
            l[s] = a * l[s] + jnp.sum(p, axis=0, keepdims=True)
            if c + 1 < nchunk:
                st_next[s] = scores(s, c + 1)
            acc[s] = a * acc[s] + _dot(vt_ref[:, c * tk:(c + 1) * tk], p.astype(BF16))
            m[s] = m_new
    for s in range(nsub):
        o_ref[s * sub:(s + 1) * sub, :] = (acc[s] / l[s]).T.astype(o_ref.dtype)


def _mla_attention(q_p, k_p, vt, batch, seq, tq, tk, sub, cast_ws, cast_layer, cast_dense):
    n = q_p.shape[0]
    nqb = seq // tq
    n_slots = batch * MLA_HEADS * nqb

    def slot_of(b, hh, i):
        return (b * MLA_HEADS + hh) * nqb + i

    streams = [_cast_stream_specs(w, cast_layer, n_slots, slot_of) for w in cast_ws]
    streams += [_cast_rows_specs(w, lead, n_slots, slot_of) for w, lead in cast_dense]
    cast_ws = list(cast_ws) + [w for w, _ in cast_dense]
    outs = pl.pallas_call(
        functools.partial(_mla_attn_kernel, tk=tk, sub=sub, n_cast=len(cast_ws)),
        grid=(batch, MLA_HEADS, nqb),
        in_specs=[pl.BlockSpec((tq, 2 * LANES), lambda b, hh, i: (b * nqb + i, hh)),
                  pl.BlockSpec((seq, 2 * LANES), lambda b, hh, i: (b, hh)),
                  pl.BlockSpec((MLA_V_DIM, seq), lambda b, hh, i: (hh, b))] + [s[0] for s in streams],
        out_specs=[pl.BlockSpec((tq, MLA_V_DIM), lambda b, hh, i: (b * nqb + i, hh))] + [s[1] for s in streams],
        out_shape=[jax.ShapeDtypeStruct((n, MLA_HEADS * MLA_V_DIM), BF16)] + [s[2] for s in streams],
        compiler_params=_cparams(("arbitrary", "arbitrary", "arbitrary")),
        name="mla_attention",
    )(q_p, k_p, vt, *cast_ws)
    return outs[0], outs[1:]


def _ret_kernel(lg_ref, q_ref, k_ref, v_ref, g_ref, o_ref, acc_ref, st_ref, *, c_len, group):
    nchunk = q_ref.shape[0] // c_len
    hh = pl.program_id(1)
    lgf = lg_ref[0, hh]
    lgb = lg_ref[1, hh]
    ii = lax.broadcasted_iota(I32, (c_len, c_len), 0).astype(F32)
    jj = lax.broadcasted_iota(I32, (c_len, c_len), 1).astype(F32)
    rel = ii - jj
    dmat = jnp.where(rel >= 0, jnp.exp(lgf * jnp.maximum(rel, 0.0)), jnp.exp(lgb * jnp.maximum(-rel, 0.0)))
    pos = lax.broadcasted_iota(I32, (c_len, 1), 0).astype(F32)
    qdec_f = jnp.exp(lgf * (pos + 1.0))
    kdec_f = jnp.exp(lgf * (c_len - 1.0 - pos))
    qdec_b = jnp.exp(lgb * (c_len - pos))
    kdec_b = jnp.exp(lgb * pos)
    full_chunk = jnp.full((1, RET_V_DIM), float(c_len), F32)
    cdec_f = jnp.exp(lgf * full_chunk)
    cdec_b = jnp.exp(lgb * full_chunk)

    def decayed_keys_t(t0, kdec):
        return (k_ref[pl.ds(t0, c_len), :].astype(F32) * kdec).T.astype(BF16)

    st_ref[...] = jnp.zeros_like(st_ref)

    def bwd_body(i, carry):
        t0s = [pl.multiple_of((nchunk - 1 - (i * group + u)) * c_len, c_len) for u in range(group)]
        upd = [_dot(decayed_keys_t(t0, kdec_b), v_ref[pl.ds(t0, c_len), :]) for t0 in t0s]
        for t0, u_c in zip(t0s, upd):
            st = st_ref[...]
            acc_ref[pl.ds(t0, c_len), :] = _dot(q_ref[pl.ds(t0, c_len), :], st.astype(BF16)) * qdec_b
            st_ref[...] = st * cdec_b + u_c
        return carry

    lax.fori_loop(0, nchunk // group, bwd_body, 0)
    st_ref[...] = jnp.zeros_like(st_ref)

    def fwd_body(i, carry):
        t0s = [pl.multiple_of((i * group + u) * c_len, c_len) for u in range(group)]
        scs = [_dot_nt(q_ref[pl.ds(t0, c_len), :], k_ref[pl.ds(t0, c_len), :]) * dmat for t0 in t0s]
        upd = [_dot(decayed_keys_t(t0, kdec_f), v_ref[pl.ds(t0, c_len), :]) for t0 in t0s]
        for t0, sc, u_c in zip(t0s, scs, upd):
            st = st_ref[...]
            r = (_dot(sc.astype(BF16), v_ref[pl.ds(t0, c_len), :])
                 + _dot(q_ref[pl.ds(t0, c_len), :], st.astype(BF16)) * qdec_f
                 + acc_ref[pl.ds(t0, c_len), :])
            st_ref[...] = st * cdec_f + u_c
            r = r - jnp.mean(r, axis=-1, keepdims=True)
            r = r * lax.rsqrt(jnp.mean(r * r, axis=-1, keepdims=True) + LN_EPS)
            o_ref[pl.ds(t0, c_len), :] = (g_ref[pl.ds(t0, c_len), :].astype(F32) * r).astype(o_ref.dtype)
        return carry

    lax.fori_loop(0, nchunk // group, fwd_body, 0)


def _retention(hc, lg, batch, seq, c_len, group):
    dk, dv, nh = RET_QK_DIM, RET_V_DIM, RET_HEADS
    v_blk0 = (2 * nh * dk) // dv
    return pl.pallas_call(
        functools.partial(_ret_kernel, c_len=c_len, group=group),
        grid=(batch, nh),
        in_specs=[pl.BlockSpec(memory_space=pltpu.SMEM),
                  pl.BlockSpec((seq, dk), lambda b, hh: (b, hh)),
                  pl.BlockSpec((seq, dk), lambda b, hh: (b, nh + hh)),
                  pl.BlockSpec((seq, dv), lambda b, hh: (b, v_blk0 + hh)),
                  pl.BlockSpec((seq, dv), lambda b, hh: (b, v_blk0 + nh + hh))],
        out_specs=pl.BlockSpec((seq, dv), lambda b, hh: (b, hh)),
        scratch_shapes=[pltpu.VMEM((seq, dv), F32), pltpu.VMEM((dk, dv), F32)],
        out_shape=jax.ShapeDtypeStruct((batch * seq, nh * dv), BF16),
        compiler_params=_cparams(("parallel", "arbitrary")),
        name="retention",
    )(lg, hc, hc, hc, hc)


ROUTER_ROWS = 40


def _router_kernel(x_ref, wt_ref, b_ref, tri_ref, ids_ref, wts_ref, cnt_ref, carry_ref):
    i = pl.program_id(0)
    tm = x_ref.shape[0]

    @pl.when(i == 0)
    def _():
        carry_ref[...] = jnp.zeros_like(carry_ref)

    logits = _dot_nt(wt_ref[...], x_ref[...]) + b_ref[...]
    grow = lax.broadcasted_iota(I32, (SUBLANES, tm), 0).astype(F32)
    gl = jnp.where(grow < N_GROUPS, logits[0:SUBLANES], MASK_VALUE)
    gmax = jnp.max(gl, axis=0, keepdims=True)
    gsum = jnp.sum(jnp.exp(gl - gmax), axis=0, keepdims=True)
    p_group = 1.0 / gsum
    g_idx = jnp.min(jnp.where(gl == gmax, grow, float(N_GROUPS)), axis=0, keepdims=True)
    sel = jnp.zeros((EXPERTS_PER_GROUP, tm), F32)
    for g in range(N_GROUPS):
        r0 = SUBLANES + g * EXPERTS_PER_GROUP
        sel = sel + jnp.where(g_idx == float(g), logits[r0:r0 + EXPERTS_PER_GROUP], 0.0)
    erow = lax.broadcasted_iota(I32, (EXPERTS_PER_GROUP, tm), 0).astype(F32)
    smax = jnp.max(sel, axis=0, keepdims=True)
    sexp = jnp.exp(sel - smax)
    probs = sexp / jnp.sum(sexp, axis=0, keepdims=True)
    p1 = jnp.max(probs, axis=0, keepdims=True)
    i1 = jnp.min(jnp.where(probs == p1, erow, float(EXPERTS_PER_GROUP)), axis=0, keepdims=True)
    rest = jnp.where(erow == i1, -1.0, probs)
    p2 = jnp.max(rest, axis=0, keepdims=True)
    i2 = jnp.min(jnp.where(rest == p2, erow, float(EXPERTS_PER_GROUP)), axis=0, keepdims=True)
    denom = p1 + p2
    e0 = g_idx * EXPERTS_PER_GROUP + i1
    e1 = g_idx * EXPERTS_PER_GROUP + i2

    xrow = lax.broadcasted_iota(I32, (N_EXPERTS, tm), 0).astype(F32)
    oh0 = jnp.where(xrow == e0, 1.0, 0.0)
    oh1 = jnp.where(xrow == e1, 1.0, 0.0)
    onehot = oh0 + oh1
    before = _dot(onehot.astype(BF16), tri_ref[...]) + carry_ref[:, 0:1]
    rank0 = jnp.sum(oh0 * before, axis=0, keepdims=True)
    rank1 = jnp.sum(oh1 * before, axis=0, keepdims=True)
    carry_ref[...] = carry_ref[...] + jnp.sum(onehot, axis=1, keepdims=True)

    ids_ref[...] = jnp.zeros_like(ids_ref)
    ids_ref[0:1, :] = e0.astype(I32)
    ids_ref[1:2, :] = e1.astype(I32)
    ids_ref[2:3, :] = rank0.astype(I32)
    ids_ref[3:4, :] = rank1.astype(I32)
    wts_ref[...] = jnp.zeros_like(wts_ref)
    wts_ref[0:1, :] = p_group * p1 / denom
    wts_ref[1:2, :] = p_group * p2 / denom
    cnt_ref[...] = carry_ref[...]


def _router(x, wt, bias, tm):
    n, d = x.shape
    tri = jnp.asarray(np.triu(np.ones((tm, tm), np.float32), 1), BF16)
    return pl.pallas_call(
        _router_kernel,
        grid=(n // tm,),
        in_specs=[pl.BlockSpec((tm, d), lambda i: (i, 0)),
                  pl.BlockSpec((ROUTER_ROWS, d), lambda i: (0, 0)),
                  pl.BlockSpec((ROUTER_ROWS, 1), lambda i: (0, 0)),
                  pl.BlockSpec((tm, tm), lambda i: (0, 0))],
        out_specs=[pl.BlockSpec((SUBLANES, tm), lambda i: (0, i)),
                   pl.BlockSpec((SUBLANES, tm), lambda i: (0, i)),
                   pl.BlockSpec((N_EXPERTS, LANES), lambda i: (0, 0))],
        out_shape=[jax.ShapeDtypeStruct((SUBLANES, n), I32), jax.ShapeDtypeStruct((SUBLANES, n), F32),
                   jax.ShapeDtypeStruct((N_EXPERTS, LANES), F32)],
        scratch_shapes=[pltpu.VMEM((N_EXPERTS, LANES), F32)],
        compiler_params=_cparams(("arbitrary",)),
        name="moe_router",
    )(x, wt, bias, tri)


def _slots_kernel(ids_ref, cnt_ref, slots_ref, blk_ref, *, nblk_pad):
    tm = ids_ref.shape[1]
    cnt = cnt_ref[:, 0:1]
    padded = jnp.floor((cnt + (MOE_BLOCK - 1)) / MOE_BLOCK) * MOE_BLOCK
    er = lax.broadcasted_iota(I32, (N_EXPERTS, N_EXPERTS), 0)
    ec = lax.broadcasted_iota(I32, (N_EXPERTS, N_EXPERTS), 1)
    padded_row = jnp.sum(jnp.where(er == ec, padded, 0.0), axis=0, keepdims=True)
    p_start = jnp.sum(jnp.where(ec < er, padded_row, 0.0), axis=1, keepdims=True)
    p_end = p_start + padded
    xrow = lax.broadcasted_iota(I32, (N_EXPERTS, tm), 0)
    e0 = ids_ref[0:1, :]
    e1 = ids_ref[1:2, :]
    s0 = jnp.sum(jnp.where(xrow == e0, p_start, 0.0), axis=0, keepdims=True).astype(I32) + ids_ref[2:3, :]
    s1 = jnp.sum(jnp.where(xrow == e1, p_start, 0.0), axis=0, keepdims=True).astype(I32) + ids_ref[3:4, :]
    slots_ref[...] = jnp.zeros_like(slots_ref)
    slots_ref[0:1, :] = s0
    slots_ref[1:2, :] = s1
    slots_ref[2:3, :] = s0 | (s1 << 16)
    bstart = (lax.broadcasted_iota(I32, (1, nblk_pad), 1) * MOE_BLOCK).astype(F32)
    blk_e = jnp.minimum(jnp.sum(jnp.where(p_end <= bstart, 1.0, 0.0), axis=0, keepdims=True), N_EXPERTS - 1.0)
    total = jnp.sum(padded, axis=0, keepdims=True)
    erow = lax.broadcasted_iota(I32, (N_EXPERTS, nblk_pad), 0).astype(F32)
    own_end = jnp.sum(jnp.where(erow == blk_e, p_end, 0.0), axis=0, keepdims=True)
    nxt_e = jnp.minimum(jnp.sum(jnp.where(p_end <= own_end, 1.0, 0.0), axis=0, keepdims=True), N_EXPERTS - 1.0)
    nxt_e = jnp.where(own_end < total, nxt_e, -1.0)
    blk_ref[...] = jnp.zeros_like(blk_ref)
    blk_ref[0:1, :] = blk_e.astype(I32)
    blk_ref[1:2, :] = jnp.broadcast_to((total / MOE_BLOCK).astype(I32), (1, nblk_pad))
    blk_ref[2:3, :] = nxt_e.astype(I32)
    lane = lax.broadcasted_iota(I32, (N_EXPERTS, nblk_pad), 1).astype(F32)
    blk_ref[3:4, :] = jnp.sum(jnp.where(erow == lane, p_start + cnt, 0.0), axis=0, keepdims=True).astype(I32)
    blk_ref[4:5, :] = jnp.sum(jnp.where(erow == lane, p_end, 0.0), axis=0, keepdims=True).astype(I32)


def _slots(ids, cnt, tm, nblk_pad):
    n = ids.shape[1]
    return pl.pallas_call(
        functools.partial(_slots_kernel, nblk_pad=nblk_pad),
        grid=(n // tm,),
        in_specs=[pl.BlockSpec((SUBLANES, tm), lambda i: (0, i)),
                  pl.BlockSpec((N_EXPERTS, LANES), lambda i: (0, 0))],
        out_specs=[pl.BlockSpec((SUBLANES, tm), lambda i: (0, i)),
                   pl.BlockSpec((SUBLANES, nblk_pad), lambda i: (0, 0))],
        out_shape=[jax.ShapeDtypeStruct((SUBLANES, n), I32), jax.ShapeDtypeStruct((SUBLANES, nblk_pad), I32)],
        compiler_params=_cparams(("arbitrary",)),
        name="moe_slots",
    )(ids, cnt)


def _slot_tokens_kernel(slots_ref, blk_ref, tok_ref, *, n, cap, nblk_pad):
    def zero(j, carry):
        tok_ref[j] = 0
        return carry

    def scatter(t, carry):
        both = slots_ref[t]
        tok_ref[both & 0xFFFF] = t
        tok_ref[lax.shift_right_logical(both, 16)] = t
        return carry

    for e in range(N_EXPERTS):
        lax.fori_loop(blk_ref[3 * nblk_pad + e], blk_ref[4 * nblk_pad + e], zero, 0)
    lax.fori_loop(blk_ref[nblk_pad] * MOE_BLOCK, cap, zero, 0)
    lax.fori_loop(0, n, scatter, 0, unroll=8)


def _slot_tokens(slots_packed, blk_flat, n, cap, nblk_pad):
    return pl.pallas_call(
        functools.partial(_slot_tokens_kernel, n=n, cap=cap, nblk_pad=nblk_pad),
        grid_spec=pltpu.PrefetchScalarGridSpec(
            num_scalar_prefetch=2,
            grid=(1,),
            in_specs=[],
            out_specs=pl.BlockSpec(memory_space=pltpu.SMEM)),
        out_shape=jax.ShapeDtypeStruct((cap,), I32),
        compiler_params=_cparams(("arbitrary",)),
        name="moe_slot_tokens",
    )(slots_packed, blk_flat)


ROW_BUFS = 3


def _expert_kernel(blk_ref, tok_ref, xpk_hbm, wg_hbm, wu_hbm, wd_hbm, y_ref, xbuf, wgb, wub, wdb,
                   sems, wsems, cnt_ref, *, nblk_pad):
    i = pl.program_id(0)
    n_used = blk_ref[nblk_pad]

    def weight_copies(e, slot):
        return [pltpu.make_async_copy(src.at[e], dst.at[slot], wsems.at[slot])
                for src, dst in ((wg_hbm, wgb), (wu_hbm, wub), (wd_hbm, wdb))]

    def start_rows(block, buf, r_lo=0, r_hi=MOE_BLOCK):
        for r in range(r_lo, r_hi):
            src0 = pl.multiple_of(tok_ref[block * MOE_BLOCK + r] * SUBLANES, SUBLANES)
            pltpu.make_async_copy(xpk_hbm.at[pl.ds(src0, SUBLANES)],
                                  xbuf.at[buf, pl.ds(r * SUBLANES, SUBLANES)], sems.at[buf]).start(priority=r % 2)

    def wait_rows(buf):
        pltpu.make_async_copy(xpk_hbm.at[pl.ds(0, MOE_BLOCK * SUBLANES)], xbuf.at[buf], sems.at[buf]).wait()

    @pl.when(i == 0)
    def _():
        cnt_ref[0] = 0
        for cp in weight_copies(blk_ref[0], 0):
            cp.start(priority=1)
        for ahead in range(ROW_BUFS - 1):
            start_rows(jnp.minimum(ahead, n_used - 1), ahead)

    @pl.when((i < n_used) & ((i == 0) | (blk_ref[i] != blk_ref[jnp.maximum(i - 1, 0)])))
    def _():
        slot = cnt_ref[0] % 2
        cnt_ref[0] = cnt_ref[0] + 1
        for cp in weight_copies(blk_ref[i], slot):
            cp.wait()
        nxt_e = blk_ref[2 * nblk_pad + i]

        @pl.when(nxt_e >= 0)
        def _():
            for cp in weight_copies(nxt_e, 1 - slot):
                cp.start(priority=1)

    @pl.when(i < n_used)
    def _():
        wslot = (cnt_ref[0] + 1) % 2
        buf = i % ROW_BUFS
        nbuf = (i + ROW_BUFS - 1) % ROW_BUFS
        nxt = jnp.minimum(i + ROW_BUFS - 1, n_used - 1)
        wait_rows(buf)
        lo, hi = _unpack_halves(_load_row_tiles(xbuf.at[buf], MOE_BLOCK))
        xb = jnp.concatenate([lo.astype(BF16), hi.astype(BF16)], axis=-1)
        g = _dot(xb, wgb[wslot])
        start_rows(nxt, nbuf, 0, MOE_BLOCK // 2)
        u = _dot(xb, wub[wslot])
        start_rows(nxt, nbuf, MOE_BLOCK // 2, MOE_BLOCK)
        hmid = (g * jax.nn.sigmoid(g) * u).astype(BF16)
        _store_row_tiles(y_ref, _pack_halves(_dot(hmid, wdb[wslot])))

    @pl.when(i == n_used - 1)
    def _():
        for ahead in range(1, ROW_BUFS):
            wait_rows((i + ahead) % ROW_BUFS)

    @pl.when(i >= n_used)
    def _():
        y_ref[...] = jnp.zeros_like(y_ref)


def _experts(blk_flat, slot_tok, xpk, wg, wu, wd, nblk_pad):
    d, de = wg.shape[1], wg.shape[2]
    assert d == 2 * SUBLANES * LANES and xpk.shape[1] == LANES
    cap = slot_tok.shape[0]
    tile_rows = MOE_BLOCK * SUBLANES
    return pl.pallas_call(
        functools.partial(_expert_kernel, nblk_pad=nblk_pad),
        grid_spec=pltpu.PrefetchScalarGridSpec(
            num_scalar_prefetch=2,
            grid=(cap // MOE_BLOCK,),
            in_specs=[pl.BlockSpec(memory_space=pl.ANY), pl.BlockSpec(memory_space=pl.ANY),
                      pl.BlockSpec(memory_space=pl.ANY), pl.BlockSpec(memory_space=pl.ANY)],
            out_specs=pl.BlockSpec((tile_rows, LANES), lambda i, blk, tok: (i, 0)),
            scratch_shapes=[pltpu.VMEM((ROW_BUFS, tile_rows, LANES), U32),
                            pltpu.VMEM((2, d, de), BF16), pltpu.VMEM((2, d, de), BF16), pltpu.VMEM((2, de, d), BF16),
                            pltpu.SemaphoreType.DMA((ROW_BUFS,)), pltpu.SemaphoreType.DMA((2,)),
                            pltpu.SMEM((1,), I32)]),
        out_shape=jax.ShapeDtypeStruct((cap * SUBLANES, LANES), U32),
        compiler_params=_cparams(("arbitrary",)),
        name="moe_experts",
    )(blk_flat, slot_tok, xpk, wg, wu, wd)


def _tail_kernel(slots_ref, x_ref, xb_ref, p_ref, wts_ref, wgate_ref, bgate_ref, wproj_ref, g_ref, b_ref,
                 yb_hbm, y_ref, ybf_ref, rows_ref, sems):
    i = pl.program_id(0)
    nsteps = pl.num_programs(0)
    tm = x_ref.shape[0]
    n = nsteps * tm

    def start_rows(step, buf):
        for t in range(tm):
            for which in range(2):
                src0 = pl.multiple_of(slots_ref[which * n + step * tm + t] * SUBLANES, SUBLANES)
                pltpu.make_async_copy(yb_hbm.at[pl.ds(src0, SUBLANES)],
                                      rows_ref.at[buf, which, pl.ds(t * SUBLANES, SUBLANES)], sems.at[buf]).start()

    def wait_rows(buf):
        for which in range(2):
            pltpu.make_async_copy(yb_hbm.at[pl.ds(0, tm * SUBLANES)], rows_ref.at[buf, which],
                                  sems.at[buf]).wait()

    @pl.when(i == 0)
    def _():
        start_rows(0, 0)

    buf = i % 2
    nxt = jnp.minimum(i + 1, nsteps - 1)
    wait_rows(buf)
    gate_pre = _dot(xb_ref[...], wgate_ref[...])
    proj = _dot(p_ref[0].astype(BF16), wproj_ref[...])
    start_rows(nxt, 1 - buf)
    ple = jax.nn.sigmoid(gate_pre + bgate_ref[...]) * proj
    w = wts_ref[...]
    lo0, hi0 = _unpack_halves(_load_row_tiles(rows_ref.at[buf, 0], tm))
    lo1, hi1 = _unpack_halves(_load_row_tiles(rows_ref.at[buf, 1], tm))
    w0 = w[:, 0:1]
    w1 = w[:, 1:2]
    ffn = jnp.concatenate([lo0 * w0 + lo1 * w1, hi0 * w0 + hi1 * w1], axis=-1)
    z = DN_ALPHA * x_ref[...] + ffn + ple
    y = _layer_norm_rows(z, g_ref[...], b_ref[...])
    y_ref[...] = y
    ybf_ref[...] = y.astype(BF16)

    @pl.when(i == nsteps - 1)
    def _():
        wait_rows(1 - buf)


def _layer_tail(slots_flat, x, xb, p, layer, wts_t, wgate, bgate, wproj, g, b, yb, tm):
    n, d = x.shape
    pd = p.shape[2]
    return pl.pallas_call(
        _tail_kernel,
        grid_spec=pltpu.PrefetchScalarGridSpec(
            num_scalar_prefetch=1,
            grid=(n // tm,),
            in_specs=[pl.BlockSpec((tm, d), lambda i, s: (i, 0)),
                      pl.BlockSpec((tm, d), lambda i, s: (i, 0)),
                      pl.BlockSpec((1, tm, pd), lambda i, s: (layer, i, 0)),
                      pl.BlockSpec((tm, 2), lambda i, s: (i, 0)),
                      pl.BlockSpec((d, d), lambda i, s: (0, 0)),
                      pl.BlockSpec((1, d), lambda i, s: (0, 0)),
                      pl.BlockSpec((pd, d), lambda i, s: (0, 0)),
                      pl.BlockSpec((1, d), lambda i, s: (0, 0)),
                      pl.BlockSpec((1, d), lambda i, s: (0, 0)),
                      pl.BlockSpec(memory_space=pl.ANY)],
            out_specs=[pl.BlockSpec((tm, d), lambda i, s: (i, 0)),
                       pl.BlockSpec((tm, d), lambda i, s: (i, 0))],
            scratch_shapes=[pltpu.VMEM((2, 2, tm * SUBLANES, LANES), U32), pltpu.SemaphoreType.DMA((2,))]),
        out_shape=[jax.ShapeDtypeStruct((n, d), F32), jax.ShapeDtypeStruct((n, d), BF16)],
        compiler_params=_cparams(("arbitrary",)),
        name="layer_tail",
    )(slots_flat, x, xb, p, wts_t, wgate, bgate.reshape(1, d), wproj, g.reshape(1, d), b.reshape(1, d), yb)


def _rope_table(seq, dim):
    pos = jnp.arange(seq, dtype=F32)
    inv = jnp.exp(jnp.arange(0, dim, 2, dtype=F32) * (-math.log(ROPE_BASE) / dim))
    ang = pos[:, None] * inv[None, :]
    return jnp.cos(ang), jnp.sin(ang)


def _moe_layer(x, xb, xpk, p, layer, w_group, b_group, w_router, b_router, w_gate, w_up, w_down,
               ple_w_proj, ple_w_gate, ple_b_gate, ln_g, ln_b):
    n, d = x.shape
    nblk = -(-(2 * n) // MOE_BLOCK) + N_EXPERTS
    nblk_pad = -(-nblk // LANES) * LANES
    cap = nblk * MOE_BLOCK
    wt = jnp.zeros((ROUTER_ROWS, d), F32)
    wt = wt.at[0:N_GROUPS].set(w_group.T)
    wt = wt.at[SUBLANES:].set(w_router.transpose(0, 2, 1).reshape(N_EXPERTS, d))
    bias = jnp.zeros((ROUTER_ROWS, 1), F32)
    bias = bias.at[0:N_GROUPS, 0].set(b_group)
    bias = bias.at[SUBLANES:, 0].set(b_router.reshape(N_EXPERTS))
    ids, wts, cnt = _router(xb, wt.astype(BF16), bias, tm=512)
    slots, blk = _slots(ids, cnt, tm=min(2048, n), nblk_pad=nblk_pad)
    slots_flat = slots[0:2].reshape(2 * n)
    blk_flat = blk[0:5].reshape(5 * nblk_pad)
    assert cap <= 1 << 16
    slot_tok = _slot_tokens(slots[2], blk_flat, n, cap, nblk_pad)
    yb = _experts(blk_flat, slot_tok, xpk, w_gate, w_up, w_down, nblk_pad)
    return _layer_tail(slots_flat, x, xb, p, layer, wts[0:2].T, ple_w_gate, ple_b_gate, ple_w_proj, ln_g, ln_b,
                       yb, tm=256)


def _mixer_ab(x, xres, batch, seq, w_in, rpb, q_norm, w_uq, kv_norm, w_ukv, w_out_all, j, ln_g, ln_b,
              expert_ws, layer, dense_early, dense_late):
    d = x.shape[1]
    o1 = 3 * NA_WIDTH
    o2 = o1 + MLA_Q_RANK
    o3 = o2 + MLA_KV_RANK
    half = MLA_ROPE_DIM // 2
    kr = w_in[:, o3:o3 + MLA_ROPE_DIM]
    kr_sw = jnp.concatenate([kr[:, half:], kr[:, :half]], axis=1)
    width = -(-(o3 + 2 * MLA_ROPE_DIM) // 1024) * 1024
    w_in_p = jnp.concatenate([w_in, kr_sw, jnp.zeros((d, width - o3 - 2 * MLA_ROPE_DIM), F32)], axis=1)
    h, early_b = _matmul(x, w_in_p.astype(BF16), BF16, tm=1024, tn=1024, cast_ws=dense_early)
    a_out = _na_attention(h, _na_bias_tables(rpb), batch, seq)
    dq = MLA_NOPE_DIM + MLA_ROPE_DIM
    wq = w_uq.reshape(MLA_Q_RANK, MLA_HEADS, dq)
    wq_pe = wq[:, :, MLA_NOPE_DIM:]
    wq_p = jnp.concatenate([wq, wq_pe[:, :, half:], wq_pe[:, :, :half]], axis=2)
    wq_p = wq_p.reshape(MLA_Q_RANK, MLA_HEADS * 2 * LANES).astype(BF16)
    wkv = w_ukv.reshape(MLA_KV_RANK, MLA_HEADS, MLA_NOPE_DIM + MLA_V_DIM)
    wk = wkv[:, :, :MLA_NOPE_DIM].reshape(MLA_KV_RANK, MLA_HEADS * MLA_NOPE_DIM).astype(BF16)
    wvt = wkv[:, :, MLA_NOPE_DIM:].reshape(MLA_KV_RANK, MLA_HEADS * MLA_V_DIM).T.astype(BF16)
    cos, sin = _rope_table(seq, MLA_ROPE_DIM)
    zpad = jnp.zeros((seq, LANES - MLA_ROPE_DIM), F32)
    cosf = jnp.concatenate([cos, cos, zpad], axis=1)
    sinf = jnp.concatenate([-sin, sin, zpad], axis=1)
    q_p, k_p, vt = _mla_prep(h, q_norm, kv_norm, wq_p, wk, wvt, cosf, sinf, o1, seq, tm=512)
    b_out, cast_out = _mla_attention(q_p, k_p, vt, batch, seq, tq=min(1024, seq), tk=1024, sub=256,
                                     cast_ws=expert_ws, cast_layer=layer,
                                     cast_dense=[(w_out_all, j)] + list(dense_late))
    expert_wb, w_out_b, late_b = cast_out[:len(expert_ws)], cast_out[len(expert_ws)], cast_out[len(expert_ws) + 1:]
    outs = _proj_ln([a_out, b_out], [w_out_b[:NA_WIDTH], w_out_b[NA_WIDTH:]], xres, ln_g, ln_b, tm=512, nk=1)
    return outs, expert_wb, early_b, late_b


def _mixer_c(xb, xres, batch, seq, w_in, log_rate_f, log_rate_b, w_out, ln_g, ln_b, expert_ws, layer):
    cosr, sinr = _rope_table(seq, RET_QK_DIM)
    n_q = RET_HEADS * RET_QK_DIM
    hc, expert_wb = _matmul_rope(xb, w_in, cosr, sinr, BF16, tm=min(1024, seq), tn=1024,
                                 n_q_cols=n_q, n_rope_cols=2 * n_q, gate_col0=2 * n_q + RET_HEADS * RET_V_DIM,
                                 head_w=RET_QK_DIM, q_scale=RET_QK_DIM ** -0.5,
                                 cast_ws=expert_ws, cast_layer=layer, cast_j=8)
    lg = jnp.stack([jnp.log1p(-jnp.exp(log_rate_f.astype(F32))), jnp.log1p(-jnp.exp(log_rate_b.astype(F32)))])
    r = _retention(hc, lg, batch, seq, c_len=256, group=min(8, seq // 256))
    return _proj_ln([r], [w_out], xres, ln_g, ln_b, tm=512, nk=1), expert_wb


def kernel(x, p, ab_w_in, ab_rpb, ab_q_norm, ab_w_uq, ab_kv_norm, ab_w_ukv, ab_w_out, c_w_in, c_log_rate_f,
           c_log_rate_b, c_w_out, ln1_g, ln1_b, moe_w_group, moe_b_group, moe_w_router, moe_b_router,
           moe_w_gate, moe_w_up, moe_w_down, ple_w_proj, ple_w_gate, ple_b_gate, ln2_g, ln2_b):
    batch, seq, d = x.shape
    n = batch * seq
    xf = x.reshape(n, d)
    p_flat = p.reshape(DEPTH, n, -1)
    expert_ws = (moe_w_gate, moe_w_up, moe_w_down)
    assert DEPTH == 2
    dense_late = [(c_w_out, 0)] + [(w, i) for i in range(DEPTH) for w in (ple_w_gate, ple_w_proj)]
    (xf, xb, xpk), (wg, wu, wd), (c_w_in_b,), late_b = _mixer_ab(
        xf, xf, batch, seq, ab_w_in[0], ab_rpb[0], ab_q_norm[0], ab_w_uq[0], ab_kv_norm[0], ab_w_ukv[0],
        ab_w_out, 0, ln1_g[0], ln1_b[0], expert_ws, 0, dense_early=[(c_w_in, 0)], dense_late=dense_late)
    c_w_out_b, ple_b = late_b[0], late_b[1:]
    xf, xb = _moe_layer(xf, xb, xpk, p_flat, 0, moe_w_group[0], moe_b_group[0], moe_w_router[0],
                        moe_b_router[0], wg, wu, wd, ple_b[1], ple_b[0], ple_b_gate[0], ln2_g[0], ln2_b[0])
    (xf, xb, xpk), (wg, wu, wd) = _mixer_c(
        xb, xf, batch, seq, c_w_in_b, c_log_rate_f[0], c_log_rate_b[0], c_w_out_b, ln1_g[1], ln1_b[1],
        expert_ws, 1)
    xf, xb = _moe_layer(xf, xb, xpk, p_flat, 1, moe_w_group[1], moe_b_group[1], moe_w_router[1],
                        moe_b_router[1], wg, wu, wd, ple_b[3], ple_b[2], ple_b_gate[1], ln2_g[1], ln2_b[1])
    return xf.reshape(batch, seq, d)
```

```python
import functools
import math

import numpy as np
import jax
import jax.numpy as jnp
from jax import lax
from jax.experimental import pallas as pl
from jax.experimental.pallas import tpu as pltpu

DEPTH = 2
GRID_W = 64
NA_HEADS = 8
NA_HEAD_DIM = 128
NA_WIN_H = 8
NA_WIN_W = 16
MLA_HEADS = 8
MLA_Q_RANK = 512
MLA_KV_RANK = 256
MLA_NOPE_DIM = 128
MLA_ROPE_DIM = 64
MLA_V_DIM = 128
RET_HEADS = 8
RET_QK_DIM = 256
RET_V_DIM = 512
RET_CHUNK = 128
N_GROUPS = 4
EXPERTS_PER_GROUP = 8
N_EXPERTS = N_GROUPS * EXPERTS_PER_GROUP
D_EXPERT = 512
MOE_BLOCK = 256
ROPE_BASE = 10000.0
LN_EPS = 1e-5
RMS_EPS = 1e-6
DN_ALPHA = (2 * DEPTH) ** 0.25
NA_WIDTH = NA_HEADS * NA_HEAD_DIM

LANES = 128
SUBLANES = 8
VMEM_LIMIT_BYTES = 60 * 1024 * 1024
MASK_VALUE = -1e30

F32 = jnp.float32
BF16 = jnp.bfloat16
I32 = jnp.int32
U32 = jnp.uint32


def _cparams(sem):
    return pltpu.CompilerParams(dimension_semantics=sem, vmem_limit_bytes=VMEM_LIMIT_BYTES)


def _dot(a, b):
    return jnp.dot(a, b, preferred_element_type=F32)


def _dot_nt(a, b, precision=None):
    return lax.dot_general(a, b, (((1,), (1,)), ((), ())), preferred_element_type=F32,
                           precision=precision)


def _pack_halves(y):
    c = y.shape[1] // 2
    bits = pltpu.bitcast(y.astype(BF16).astype(F32), U32)
    return (bits[:, :c] >> 16) | (bits[:, c:] & jnp.uint32(0xFFFF0000))


def _unpack_halves(w):
    lo = pltpu.bitcast(w << 16, F32)
    hi = pltpu.bitcast(w & jnp.uint32(0xFFFF0000), F32)
    return lo, hi


def _store_row_tiles(ref, packed):
    m = packed.shape[0]
    for s in range(SUBLANES):
        ref[pl.ds(s, m, stride=SUBLANES), :] = packed[:, s * LANES:(s + 1) * LANES]


def _load_row_tiles(ref, m):
    return jnp.concatenate([ref[pl.ds(s, m, stride=SUBLANES), :] for s in range(SUBLANES)], axis=-1)


def _mm_kernel(x_ref, w_ref, *rest, n_cast):
    cast_src, o_ref, cast_dst = rest[:n_cast], rest[n_cast], rest[n_cast + 1:]
    _cast_blocks(cast_src, cast_dst)
    o_ref[...] = _dot(x_ref[...].astype(BF16), w_ref[...]).astype(o_ref.dtype)


def _matmul(x, w, out_dtype, tm, tn, cast_ws):
    m, k = x.shape
    n = w.shape[1]
    nj = n // tn
    streams = [_cast_rows_specs(cw, lead, (m // tm) * nj, lambda i, j: i * nj + j) for cw, lead in cast_ws]
    outs = pl.pallas_call(
        functools.partial(_mm_kernel, n_cast=len(cast_ws)),
        grid=(m // tm, nj),
        in_specs=[pl.BlockSpec((tm, k), lambda i, j: (i, 0)),
                  pl.BlockSpec((k, tn), lambda i, j: (0, j))] + [s[0] for s in streams],
        out_specs=[pl.BlockSpec((tm, tn), lambda i, j: (i, j))] + [s[1] for s in streams],
        out_shape=[jax.ShapeDtypeStruct((m, n), out_dtype)] + [s[2] for s in streams],
        compiler_params=_cparams(("arbitrary", "arbitrary")),
        name="matmul",
    )(x, w, *[cw for cw, _ in cast_ws])
    return outs[0], outs[1:]


def _mm_rope_kernel(x_ref, w_ref, cos_ref, sin_ref, *rest, n_q_tiles, n_rope_tiles, first_gate_tile,
                    head_w, q_scale, n_cast):
    cast_src, o_ref, cast_dst = rest[:n_cast], rest[n_cast], rest[n_cast + 1:]
    j = pl.program_id(1)
    _cast_blocks(cast_src, cast_dst)
    acc = _dot(x_ref[...].astype(BF16), w_ref[...])

    @pl.when((j >= n_rope_tiles) & (j < first_gate_tile))
    def _():
        o_ref[...] = acc.astype(o_ref.dtype)

    @pl.when(j >= first_gate_tile)
    def _():
        o_ref[...] = (acc * jax.nn.sigmoid(acc)).astype(o_ref.dtype)

    @pl.when(j < n_rope_tiles)
    def _():
        scale = jnp.where(j < n_q_tiles, q_scale, 1.0)
        cos = cos_ref[...] * scale
        sin = sin_ref[...] * scale
        half = head_w // 2
        for c0 in range(0, acc.shape[1], head_w):
            x1 = acc[:, c0:c0 + half]
            x2 = acc[:, c0 + half:c0 + head_w]
            o_ref[:, c0:c0 + half] = (x1 * cos - x2 * sin).astype(o_ref.dtype)
            o_ref[:, c0 + half:c0 + head_w] = (x2 * cos + x1 * sin).astype(o_ref.dtype)


def _matmul_rope(x, w, cos, sin, out_dtype, tm, tn, n_q_cols, n_rope_cols, gate_col0, head_w, q_scale,
                 cast_ws, cast_layer, cast_j):
    m, k = x.shape
    n = w.shape[1]
    nsb = cos.shape[0] // tm
    streams = [_cast_stream_specs(cw, cast_layer, (m // tm) * cast_j,
                                  lambda i, j: i * cast_j + jnp.minimum(j, cast_j - 1)) for cw in cast_ws]
    outs = pl.pallas_call(
        functools.partial(_mm_rope_kernel, n_q_tiles=n_q_cols // tn, n_rope_tiles=n_rope_cols // tn,
                          first_gate_tile=gate_col0 // tn, head_w=head_w, q_scale=q_scale,
                          n_cast=len(cast_ws)),
        grid=(m // tm, n // tn),
        in_specs=[pl.BlockSpec((tm, k), lambda i, j: (i, 0)),
                  pl.BlockSpec((k, tn), lambda i, j: (0, j)),
                  pl.BlockSpec((tm, head_w // 2), lambda i, j: (i % nsb, 0)),
                  pl.BlockSpec((tm, head_w // 2), lambda i, j: (i % nsb, 0))] + [s[0] for s in streams],
        out_specs=[pl.BlockSpec((tm, tn), lambda i, j: (i, j))] + [s[1] for s in streams],
        out_shape=[jax.ShapeDtypeStruct((m, n), out_dtype)] + [s[2] for s in streams],
        compiler_params=_cparams(("arbitrary", "arbitrary")),
        name="matmul_rope",
    )(x, w, cos, sin, *cast_ws)
    return outs[0], outs[1:]


def _layer_norm_rows(z, g, b):
    mean = jnp.mean(z, axis=-1, keepdims=True)
    zc = z - mean
    var = jnp.mean(zc * zc, axis=-1, keepdims=True)
    return zc * lax.rsqrt(var + LN_EPS) * g + b


def _proj_ln_kernel(*refs, n_act, nk):
    acts = refs[:n_act]
    ws = refs[n_act:2 * n_act]
    x_ref, g_ref, b_ref, y_ref, yb_ref, yp_ref = refs[2 * n_act:2 * n_act + 6]
    k = pl.program_id(1)
    tm = x_ref.shape[0]
    n_split = 2
    hm = tm // n_split

    def product(rows):
        part = _dot(acts[0][rows, :], ws[0][...])
        for a, w in zip(acts[1:], ws[1:]):
            part = part + _dot(a[rows, :], w[...])
        return part

    def finish(rows, h, proj):
        z = DN_ALPHA * x_ref[rows, :] + proj
        y = _layer_norm_rows(z, g_ref[...], b_ref[...])
        y_ref[rows, :] = y
        yb_ref[rows, :] = y.astype(BF16)
        _store_row_tiles(yp_ref.at[pl.ds(h * hm * SUBLANES, hm * SUBLANES)], _pack_halves(y))

    if nk == 1:
        for h in range(n_split):
            rows = pl.ds(h * hm, hm)
            finish(rows, h, product(rows))
        return
    acc_ref = refs[2 * n_act + 6]

    @pl.when(k == 0)
    def _():
        acc_ref[...] = product(pl.ds(0, tm))

    @pl.when((k > 0) & (k < nk - 1))
    def _():
        acc_ref[...] = acc_ref[...] + product(pl.ds(0, tm))

    @pl.when(k == nk - 1)
    def _():
        for h in range(n_split):
            rows = pl.ds(h * hm, hm)
            finish(rows, h, acc_ref[rows, :] + product(rows))


def _proj_ln(acts, ws, x, g, b, tm, nk):
    m, d = x.shape
    n_act = len(acts)
    in_specs = []
    for a in acts:
        kk = a.shape[1] // nk
        in_specs.append(pl.BlockSpec((tm, kk), lambda i, k: (i, k)))
    for w in ws:
        kk = w.shape[0] // nk
        mode = pl.Buffered(1) if nk == 1 else None
        in_specs.append(pl.BlockSpec((kk, d), lambda i, k: (k, 0), pipeline_mode=mode))
    in_specs += [pl.BlockSpec((tm, d), lambda i, k: (i, 0)),
                 pl.BlockSpec((1, d), lambda i, k: (0, 0)),
                 pl.BlockSpec((1, d), lambda i, k: (0, 0))]
    return pl.pallas_call(
        functools.partial(_proj_ln_kernel, n_act=n_act, nk=nk),
        grid=(m // tm, nk),
        in_specs=in_specs,
        out_specs=[pl.BlockSpec((tm, d), lambda i, k: (i, 0)),
                   pl.BlockSpec((tm, d), lambda i, k: (i, 0)),
                   pl.BlockSpec((tm * SUBLANES, LANES), lambda i, k: (i, 0))],
        out_shape=[jax.ShapeDtypeStruct((m, d), F32), jax.ShapeDtypeStruct((m, d), BF16),
                   jax.ShapeDtypeStruct((m * SUBLANES, LANES), U32)],
        scratch_shapes=[pltpu.VMEM((tm, d), F32)] if nk > 1 else [],
        compiler_params=_cparams(("parallel", "arbitrary")),
        name="proj_ln",
    )(*acts, *ws, x, g.reshape(1, d), b.reshape(1, d))


def _na_bias_tables(rpb):
    nh = rpb.shape[0]
    c = np.arange(GRID_W)
    cs = np.clip(c - NA_WIN_W // 2, 0, GRID_W - NA_WIN_W)
    kc = np.arange(GRID_W)
    valid = (kc[None, :] >= cs[:, None]) & (kc[None, :] < cs[:, None] + NA_WIN_W)
    dc = kc[None, :] - c[:, None] + NA_WIN_W - 1
    onehot = (dc[:, :, None] == np.arange(2 * NA_WIN_W - 1)[None, None, :]) & valid[:, :, None]
    cols = jnp.einsum("hrd,ckd->hrck", rpb.astype(F32), jnp.asarray(onehot, F32),
                      precision=lax.Precision.HIGHEST)
    cols = jnp.where(jnp.asarray(valid)[None, None], cols, MASK_VALUE)
    tabs = jnp.stack([cols[:, off:off + NA_WIN_H] for off in range(NA_WIN_H)], axis=1)
    return tabs.transpose(0, 1, 3, 2, 4).reshape(nh, NA_WIN_H, GRID_W, NA_WIN_H * GRID_W)


def _na_kernel(q_ref, k_ref, v_ref, bias_ref, o_ref, *, rows, group):
    scale = NA_HEAD_DIM ** -0.5
    nkeys = NA_WIN_H * GRID_W

    def body(i, carry):
        geom, scores = [], []
        for u in range(group):
            r = i * group + u
            rs = jnp.clip(r - NA_WIN_H // 2, 0, rows - NA_WIN_H)
            off = rs - r + NA_WIN_H - 1
            q0 = pl.multiple_of(r * GRID_W, GRID_W)
            k0 = pl.multiple_of(rs * GRID_W, GRID_W)
            geom.append((q0, k0))
            s = _dot_nt(q_ref[pl.ds(q0, GRID_W), :], k_ref[pl.ds(k0, nkeys), :])
            scores.append(s * scale + bias_ref[0, off])
        for (q0, k0), s in zip(geom, scores):
            m = jnp.max(s, axis=-1, keepdims=True)
            p = jnp.exp(s - m)
            l = jnp.sum(p, axis=-1, keepdims=True)
            o = _dot(p.astype(BF16), v_ref[pl.ds(k0, nkeys), :]) / l
            o_ref[pl.ds(q0, GRID_W), :] = o.astype(o_ref.dtype)
        return carry

    lax.fori_loop(0, rows // group, body, 0)


def _na_attention(h, bias_tables, batch, seq):
    rows = seq // GRID_W
    d = NA_HEAD_DIM
    nkeys = NA_WIN_H * GRID_W
    return pl.pallas_call(
        functools.partial(_na_kernel, rows=rows, group=min(32, rows)),
        grid=(batch, NA_HEADS),
        in_specs=[pl.BlockSpec((seq, d), lambda b, hh: (b, hh)),
                  pl.BlockSpec((seq, d), lambda b, hh: (b, NA_HEADS + hh)),
                  pl.BlockSpec((seq, d), lambda b, hh: (b, 2 * NA_HEADS + hh)),
                  pl.BlockSpec((1, NA_WIN_H, GRID_W, nkeys), lambda b, hh: (hh, 0, 0, 0))],
        out_specs=pl.BlockSpec((seq, d), lambda b, hh: (b, hh)),
        out_shape=jax.ShapeDtypeStruct((batch * seq, NA_WIDTH), BF16),
        compiler_params=_cparams(("parallel", "arbitrary")),
        name="na_attention",
    )(h, h, h, bias_tables)


def _rms_rows(x, g):
    return x * lax.rsqrt(jnp.mean(x * x, axis=-1, keepdims=True) + RMS_EPS) * g


def _rope_lanes(t, cosf, sinf):
    return t * cosf + pltpu.roll(t, LANES // 2, 1) * sinf


def _mla_prep_kernel(cq_ref, ckv_ref, kr_ref, gq_ref, gkv_ref, wq_ref, wk_ref, wvt_ref, cos_ref, sin_ref,
                     q_ref, k_ref, vt_ref):
    dq = MLA_NOPE_DIM + MLA_ROPE_DIM
    cosf = cos_ref[...]
    sinf = sin_ref[...]
    cqn = _rms_rows(cq_ref[...].astype(F32), gq_ref[...]).astype(BF16)
    ckvn = _rms_rows(ckv_ref[...].astype(F32), gkv_ref[...]).astype(BF16)
    qf = _dot(cqn, wq_ref[...]) * (dq ** -0.5 * math.log2(math.e))
    kf = _dot(ckvn, wk_ref[...])
    vt_ref[...] = _dot_nt(wvt_ref[...], ckvn).astype(BF16)
    kpe = _rope_lanes(kr_ref[...].astype(F32), cosf, sinf).astype(BF16)
    for hh in range(MLA_HEADS):
        c0 = hh * 2 * LANES
        q_ref[:, c0:c0 + LANES] = qf[:, c0:c0 + LANES].astype(BF16)
        q_ref[:, c0 + LANES:c0 + 2 * LANES] = _rope_lanes(qf[:, c0 + LANES:c0 + 2 * LANES], cosf, sinf).astype(BF16)
        k_ref[:, c0:c0 + LANES] = kf[:, hh * LANES:(hh + 1) * LANES].astype(BF16)
        k_ref[:, c0 + LANES:c0 + 2 * LANES] = kpe


def _mla_prep(h, gq, gkv, wq_p, wk, wvt, cosf, sinf, col_cq, seq, tm):
    n = h.shape[0]
    hw = MLA_HEADS * 2 * LANES
    nsb = seq // tm
    b_cq = col_cq // MLA_Q_RANK
    b_ckv = (col_cq + MLA_Q_RANK) // MLA_KV_RANK
    b_kr = (col_cq + MLA_Q_RANK + MLA_KV_RANK) // LANES
    return pl.pallas_call(
        _mla_prep_kernel,
        grid=(n // tm,),
        in_specs=[pl.BlockSpec((tm, MLA_Q_RANK), lambda i: (i, b_cq)),
                  pl.BlockSpec((tm, MLA_KV_RANK), lambda i: (i, b_ckv)),
                  pl.BlockSpec((tm, LANES), lambda i: (i, b_kr)),
                  pl.BlockSpec((1, MLA_Q_RANK), lambda i: (0, 0)),
                  pl.BlockSpec((1, MLA_KV_RANK), lambda i: (0, 0)),
                  pl.BlockSpec((MLA_Q_RANK, hw), lambda i: (0, 0)),
                  pl.BlockSpec((MLA_KV_RANK, MLA_HEADS * LANES), lambda i: (0, 0)),
                  pl.BlockSpec((MLA_HEADS * MLA_V_DIM, MLA_KV_RANK), lambda i: (0, 0)),
                  pl.BlockSpec((tm, LANES), lambda i: (i % nsb, 0)),
                  pl.BlockSpec((tm, LANES), lambda i: (i % nsb, 0))],
        out_specs=[pl.BlockSpec((tm, hw), lambda i: (i, 0)),
                   pl.BlockSpec((tm, hw), lambda i: (i, 0)),
                   pl.BlockSpec((MLA_HEADS * MLA_V_DIM, tm), lambda i: (0, i))],
        out_shape=[jax.ShapeDtypeStruct((n, hw), BF16), jax.ShapeDtypeStruct((n, hw), BF16),
                   jax.ShapeDtypeStruct((MLA_HEADS * MLA_V_DIM, n), BF16)],
        compiler_params=_cparams(("parallel",)),
        name="mla_prep",
    )(h, h, h, gq.reshape(1, -1), gkv.reshape(1, -1), wq_p, wk, wvt, cosf, sinf)


def _cast_stream_specs(w, layer, n_slots, slot_of):
    _, ne, rows, cols = w.shape
    if n_slots >= ne:
        e_per, rb = 1, n_slots // ne
        while rows % rb or (rows // rb) % (2 * SUBLANES):
            rb -= 1
    else:
        assert ne % n_slots == 0
        e_per, rb = ne // n_slots, 1
    n_blocks = (ne // e_per) * rb

    def block_of(*g):
        s = jnp.minimum(slot_of(*g), n_blocks - 1)
        return s // rb, s % rb

    src = pl.BlockSpec((1, e_per, rows // rb, cols), lambda *g: (layer, *block_of(*g), 0))
    dst = pl.BlockSpec((e_per, rows // rb, cols), lambda *g: (*block_of(*g), 0))
    return src, dst, jax.ShapeDtypeStruct((ne, rows, cols), BF16), n_blocks


def _cast_rows_specs(w, lead, n_slots, slot_of):
    rows, cols = w.shape[-2:]
    nb = min(n_slots, rows // (2 * SUBLANES))
    while rows % nb or (rows // nb) % (2 * SUBLANES):
        nb -= 1
    src = pl.BlockSpec((1, rows // nb, cols), lambda *g: (lead, jnp.minimum(slot_of(*g), nb - 1), 0))
    dst = pl.BlockSpec((rows // nb, cols), lambda *g: (jnp.minimum(slot_of(*g), nb - 1), 0))
    return src, dst, jax.ShapeDtypeStruct((rows, cols), BF16), nb


def _cast_blocks(srcs, dsts):
    for src, dst in zip(srcs, dsts):
        dst[...] = src[0].astype(BF16)


def _mla_attn_kernel(q_ref, k_ref, vt_ref, *rest, tk, sub, n_cast):
    cast_src, o_ref, cast_dst = rest[:n_cast], rest[n_cast], rest[n_cast + 1:]
    _cast_blocks(cast_src, cast_dst)
    nchunk = k_ref.shape[0] // tk
    tq = q_ref.shape[0]
    nsub = tq // sub
    qs = [q_ref[s * sub:(s + 1) * sub, :] for s in range(nsub)]
    m = [jnp.full((1, sub), MASK_VALUE, F32) for _ in range(nsub)]
    l = [jnp.zeros((1, sub), F32) for _ in range(nsub)]
    acc = [jnp.zeros((MLA_V_DIM, sub), F32) for _ in range(nsub)]

    def scores(s, c):
        return _dot_nt(k_ref[c * tk:(c + 1) * tk, :], qs[s])

    st_next = [scores(s, 0) for s in range(nsub)]
    for c in range(nchunk):
        for s in range(nsub):
            st = st_next[s]
            m_new = jnp.maximum(m[s], jnp.max(st, axis=0, keepdims=True))
            a = jnp.exp2(m[s] - m_new)
            p = jnp.exp2(st - m_new)
            l[s] = a * l[s] + jnp.sum(p, axis=0, keepdims=True)
            if c + 1 < nchunk:
                st_next[s] = scores(s, c + 1)
            acc[s] = a * acc[s] + _dot(vt_ref[:, c * tk:(c + 1) * tk], p.astype(BF16))
            m[s] = m_new
    for s in range(nsub):
        o_ref[s * sub:(s + 1) * sub, :] = (acc[s] / l[s]).T.astype(o_ref.dtype)


def _mla_attention(q_p, k_p, vt, batch, seq, tq, tk, sub, cast_ws, cast_layer, cast_dense):
    n = q_p.shape[0]
    nqb = seq // tq
    n_slots = batch * MLA_HEADS * nqb

    def slot_of(b, hh, i):
        return (b * MLA_HEADS + hh) * nqb + i

    streams = [_cast_stream_specs(w, cast_layer, n_slots, slot_of) for w in cast_ws]
    streams += [_cast_rows_specs(w, lead, n_slots, slot_of) for w, lead in cast_dense]
    cast_ws = list(cast_ws) + [w for w, _ in cast_dense]
    outs = pl.pallas_call(
        functools.partial(_mla_attn_kernel, tk=tk, sub=sub, n_cast=len(cast_ws)),
        grid=(batch, MLA_HEADS, nqb),
        in_specs=[pl.BlockSpec((tq, 2 * LANES), lambda b, hh, i: (b * nqb + i, hh)),
                  pl.BlockSpec((seq, 2 * LANES), lambda b, hh, i: (b, hh)),
                  pl.BlockSpec((MLA_V_DIM, seq), lambda b, hh, i: (hh, b))] + [s[0] for s in streams],
        out_specs=[pl.BlockSpec((tq, MLA_V_DIM), lambda b, hh, i: (b * nqb + i, hh))] + [s[1] for s in streams],
        out_shape=[jax.ShapeDtypeStruct((n, MLA_HEADS * MLA_V_DIM), BF16)] + [s[2] for s in streams],
        compiler_params=_cparams(("arbitrary", "arbitrary", "arbitrary")),
        name="mla_attention",
    )(q_p, k_p, vt, *cast_ws)
    return outs[0], outs[1:]


def _ret_kernel(lg_ref, q_ref, k_ref, v_ref, g_ref, o_ref, acc_ref, st_ref, *, c_len, group):
    nchunk = q_ref.shape[0] // c_len
    hh = pl.program_id(1)
    lgf = lg_ref[0, hh]
    lgb = lg_ref[1, hh]
    ii = lax.broadcasted_iota(I32, (c_len, c_len), 0).astype(F32)
    jj = lax.broadcasted_iota(I32, (c_len, c_len), 1).astype(F32)
    rel = ii - jj
    dmat = jnp.where(rel >= 0, jnp.exp(lgf * jnp.maximum(rel, 0.0)), jnp.exp(lgb * jnp.maximum(-rel, 0.0)))
    pos = lax.broadcasted_iota(I32, (c_len, 1), 0).astype(F32)
    qdec_f = jnp.exp(lgf * (pos + 1.0))
    kdec_f = jnp.exp(lgf * (c_len - 1.0 - pos))
    qdec_b = jnp.exp(lgb * (c_len - pos))
    kdec_b = jnp.exp(lgb * pos)
    full_chunk = jnp.full((1, RET_V_DIM), float(c_len), F32)
    cdec_f = jnp.exp(lgf * full_chunk)
    cdec_b = jnp.exp(lgb * full_chunk)

    def decayed_keys_t(t0, kdec):
        return (k_ref[pl.ds(t0, c_len), :].astype(F32) * kdec).T.astype(BF16)

    st_ref[...] = jnp.zeros_like(st_ref)

    def bwd_body(i, carry):
        t0s = [pl.multiple_of((nchunk - 1 - (i * group + u)) * c_len, c_len) for u in range(group)]
        upd = [_dot(decayed_keys_t(t0, kdec_b), v_ref[pl.ds(t0, c_len), :]) for t0 in t0s]
        for t0, u_c in zip(t0s, upd):
            st = st_ref[...]
            acc_ref[pl.ds(t0, c_len), :] = _dot(q_ref[pl.ds(t0, c_len), :], st.astype(BF16)) * qdec_b
            st_ref[...] = st * cdec_b + u_c
        return carry

    lax.fori_loop(0, nchunk // group, bwd_body, 0)
    st_ref[...] = jnp.zeros_like(st_ref)

    def fwd_body(i, carry):
        t0s = [pl.multiple_of((i * group + u) * c_len, c_len) for u in range(group)]
        scs = [_dot_nt(q_ref[pl.ds(t0, c_len), :], k_ref[pl.ds(t0, c_len), :]) * dmat for t0 in t0s]
        upd = [_dot(decayed_keys_t(t0, kdec_f), v_ref[pl.ds(t0, c_len), :]) for t0 in t0s]
        for t0, sc, u_c in zip(t0s, scs, upd):
            st = st_ref[...]
            r = (_dot(sc.astype(BF16), v_ref[pl.ds(t0, c_len), :])
                 + _dot(q_ref[pl.ds(t0, c_len), :], st.astype(BF16)) * qdec_f
                 + acc_ref[pl.ds(t0, c_len), :])
            st_ref[...] = st * cdec_f + u_c
            r = r - jnp.mean(r, axis=-1, keepdims=True)
            r = r * lax.rsqrt(jnp.mean(r * r, axis=-1, keepdims=True) + LN_EPS)
            o_ref[pl.ds(t0, c_len), :] = (g_ref[pl.ds(t0, c_len), :].astype(F32) * r).astype(o_ref.dtype)
        return carry

    lax.fori_loop(0, nchunk // group, fwd_body, 0)


def _retention(hc, lg, batch, seq, c_len, group):
    dk, dv, nh = RET_QK_DIM, RET_V_DIM, RET_HEADS
    v_blk0 = (2 * nh * dk) // dv
    return pl.pallas_call(
        functools.partial(_ret_kernel, c_len=c_len, group=group),
        grid=(batch, nh),
        in_specs=[pl.BlockSpec(memory_space=pltpu.SMEM),
                  pl.BlockSpec((seq, dk), lambda b, hh: (b, hh)),
                  pl.BlockSpec((seq, dk), lambda b, hh: (b, nh + hh)),
                  pl.BlockSpec((seq, dv), lambda b, hh: (b, v_blk0 + hh)),
                  pl.BlockSpec((seq, dv), lambda b, hh: (b, v_blk0 + nh + hh))],
        out_specs=pl.BlockSpec((seq, dv), lambda b, hh: (b, hh)),
        scratch_shapes=[pltpu.VMEM((seq, dv), F32), pltpu.VMEM((dk, dv), F32)],
        out_shape=jax.ShapeDtypeStruct((batch * seq, nh * dv), BF16),
        compiler_params=_cparams(("parallel", "arbitrary")),
        name="retention",
    )(lg, hc, hc, hc, hc)


ROUTER_ROWS = 40


def _router_kernel(x_ref, wt_ref, b_ref, tri_ref, ids_ref, wts_ref, cnt_ref, carry_ref):
    i = pl.program_id(0)
    tm = x_ref.shape[0]

    @pl.when(i == 0)
    def _():
        carry_ref[...] = jnp.zeros_like(carry_ref)

    logits = _dot_nt(wt_ref[...], x_ref[...]) + b_ref[...]
    grow = lax.broadcasted_iota(I32, (SUBLANES, tm), 0).astype(F32)
    gl = jnp.where(grow < N_GROUPS, logits[0:SUBLANES], MASK_VALUE)
    gmax = jnp.max(gl, axis=0, keepdims=True)
    gsum = jnp.sum(jnp.exp(gl - gmax), axis=0, keepdims=True)
    p_group = 1.0 / gsum
    g_idx = jnp.min(jnp.where(gl == gmax, grow, float(N_GROUPS)), axis=0, keepdims=True)
    sel = jnp.zeros((EXPERTS_PER_GROUP, tm), F32)
    for g in range(N_GROUPS):
        r0 = SUBLANES + g * EXPERTS_PER_GROUP
        sel = sel + jnp.where(g_idx == float(g), logits[r0:r0 + EXPERTS_PER_GROUP], 0.0)
    erow = lax.broadcasted_iota(I32, (EXPERTS_PER_GROUP, tm), 0).astype(F32)
    smax = jnp.max(sel, axis=0, keepdims=True)
    sexp = jnp.exp(sel - smax)
    probs = sexp / jnp.sum(sexp, axis=0, keepdims=True)
    p1 = jnp.max(probs, axis=0, keepdims=True)
    i1 = jnp.min(jnp.where(probs == p1, erow, float(EXPERTS_PER_GROUP)), axis=0, keepdims=True)
    rest = jnp.where(erow == i1, -1.0, probs)
    p2 = jnp.max(rest, axis=0, keepdims=True)
    i2 = jnp.min(jnp.where(rest == p2, erow, float(EXPERTS_PER_GROUP)), axis=0, keepdims=True)
    denom = p1 + p2
    e0 = g_idx * EXPERTS_PER_GROUP + i1
    e1 = g_idx * EXPERTS_PER_GROUP + i2

    xrow = lax.broadcasted_iota(I32, (N_EXPERTS, tm), 0).astype(F32)
    oh0 = jnp.where(xrow == e0, 1.0, 0.0)
    oh1 = jnp.where(xrow == e1, 1.0, 0.0)
    onehot = oh0 + oh1
    before = _dot(onehot.astype(BF16), tri_ref[...]) + carry_ref[:, 0:1]
    rank0 = jnp.sum(oh0 * before, axis=0, keepdims=True)
    rank1 = jnp.sum(oh1 * before, axis=0, keepdims=True)
    carry_ref[...] = carry_ref[...] + jnp.sum(onehot, axis=1, keepdims=True)

    ids_ref[...] = jnp.zeros_like(ids_ref)
    ids_ref[0:1, :] = e0.astype(I32)
    ids_ref[1:2, :] = e1.astype(I32)
    ids_ref[2:3, :] = rank0.astype(I32)
    ids_ref[3:4, :] = rank1.astype(I32)
    wts_ref[...] = jnp.zeros_like(wts_ref)
    wts_ref[0:1, :] = p_group * p1 / denom
    wts_ref[1:2, :] = p_group * p2 / denom
    cnt_ref[...] = carry_ref[...]


def _router(x, wt, bias, tm):
    n, d = x.shape
    tri = jnp.asarray(np.triu(np.ones((tm, tm), np.float32), 1), BF16)
    return pl.pallas_call(
        _router_kernel,
        grid=(n // tm,),
        in_specs=[pl.BlockSpec((tm, d), lambda i: (i, 0)),
                  pl.BlockSpec((ROUTER_ROWS, d), lambda i: (0, 0)),
                  pl.BlockSpec((ROUTER_ROWS, 1), lambda i: (0, 0)),
                  pl.BlockSpec((tm, tm), lambda i: (0, 0))],
        out_specs=[pl.BlockSpec((SUBLANES, tm), lambda i: (0, i)),
                   pl.BlockSpec((SUBLANES, tm), lambda i: (0, i)),
                   pl.BlockSpec((N_EXPERTS, LANES), lambda i: (0, 0))],
        out_shape=[jax.ShapeDtypeStruct((SUBLANES, n), I32), jax.ShapeDtypeStruct((SUBLANES, n), F32),
                   jax.ShapeDtypeStruct((N_EXPERTS, LANES), F32)],
        scratch_shapes=[pltpu.VMEM((N_EXPERTS, LANES), F32)],
        compiler_params=_cparams(("arbitrary",)),
        name="moe_router",
    )(x, wt, bias, tri)


def _slots_kernel(ids_ref, cnt_ref, slots_ref, blk_ref, *, nblk_pad):
    tm = ids_ref.shape[1]
    cnt = cnt_ref[:, 0:1]
    padded = jnp.floor((cnt + (MOE_BLOCK - 1)) / MOE_BLOCK) * MOE_BLOCK
    er = lax.broadcasted_iota(I32, (N_EXPERTS, N_EXPERTS), 0)
    ec = lax.broadcasted_iota(I32, (N_EXPERTS, N_EXPERTS), 1)
    padded_row = jnp.sum(jnp.where(er == ec, padded, 0.0), axis=0, keepdims=True)
    p_start = jnp.sum(jnp.where(ec < er, padded_row, 0.0), axis=1, keepdims=True)
    p_end = p_start + padded
    xrow = lax.broadcasted_iota(I32, (N_EXPERTS, tm), 0)
    e0 = ids_ref[0:1, :]
    e1 = ids_ref[1:2, :]
    s0 = jnp.sum(jnp.where(xrow == e0, p_start, 0.0), axis=0, keepdims=True).astype(I32) + ids_ref[2:3, :]
    s1 = jnp.sum(jnp.where(xrow == e1, p_start, 0.0), axis=0, keepdims=True).astype(I32) + ids_ref[3:4, :]
    slots_ref[...] = jnp.zeros_like(slots_ref)
    slots_ref[0:1, :] = s0
    slots_ref[1:2, :] = s1
    slots_ref[2:3, :] = s0 | (s1 << 16)
    bstart = (lax.broadcasted_iota(I32, (1, nblk_pad), 1) * MOE_BLOCK).astype(F32)
    blk_e = jnp.minimum(jnp.sum(jnp.where(p_end <= bstart, 1.0, 0.0), axis=0, keepdims=True), N_EXPERTS - 1.0)
    total = jnp.sum(padded, axis=0, keepdims=True)
    erow = lax.broadcasted_iota(I32, (N_EXPERTS, nblk_pad), 0).astype(F32)
    own_end = jnp.sum(jnp.where(erow == blk_e, p_end, 0.0), axis=0, keepdims=True)
    nxt_e = jnp.minimum(jnp.sum(jnp.where(p_end <= own_end, 1.0, 0.0), axis=0, keepdims=True), N_EXPERTS - 1.0)
    nxt_e = jnp.where(own_end < total, nxt_e, -1.0)
    blk_ref[...] = jnp.zeros_like(blk_ref)
    blk_ref[0:1, :] = blk_e.astype(I32)
    blk_ref[1:2, :] = jnp.broadcast_to((total / MOE_BLOCK).astype(I32), (1, nblk_pad))
    blk_ref[2:3, :] = nxt_e.astype(I32)
    lane = lax.broadcasted_iota(I32, (N_EXPERTS, nblk_pad), 1).astype(F32)
    blk_ref[3:4, :] = jnp.sum(jnp.where(erow == lane, p_start + cnt, 0.0), axis=0, keepdims=True).astype(I32)
    blk_ref[4:5, :] = jnp.sum(jnp.where(erow == lane, p_end, 0.0), axis=0, keepdims=True).astype(I32)


def _slots(ids, cnt, tm, nblk_pad):
    n = ids.shape[1]
    return pl.pallas_call(
        functools.partial(_slots_kernel, nblk_pad=nblk_pad),
        grid=(n // tm,),
        in_specs=[pl.BlockSpec((SUBLANES, tm), lambda i: (0, i)),
                  pl.BlockSpec((N_EXPERTS, LANES), lambda i: (0, 0))],
        out_specs=[pl.BlockSpec((SUBLANES, tm), lambda i: (0, i)),
                   pl.BlockSpec((SUBLANES, nblk_pad), lambda i: (0, 0))],
        out_shape=[jax.ShapeDtypeStruct((SUBLANES, n), I32), jax.ShapeDtypeStruct((SUBLANES, nblk_pad), I32)],
        compiler_params=_cparams(("arbitrary",)),
        name="moe_slots",
    )(ids, cnt)


def _slot_tokens_kernel(slots_ref, blk_ref, tok_ref, *, n, cap, nblk_pad):
    def zero(j, carry):
        tok_ref[j] = 0
        return carry

    def scatter(t, carry):
        both = slots_ref[t]
        tok_ref[both & 0xFFFF] = t
        tok_ref[lax.shift_right_logical(both, 16)] = t
        return carry

    for e in range(N_EXPERTS):
        lax.fori_loop(blk_ref[3 * nblk_pad + e], blk_ref[4 * nblk_pad + e], zero, 0)
    lax.fori_loop(blk_ref[nblk_pad] * MOE_BLOCK, cap, zero, 0)
    lax.fori_loop(0, n, scatter, 0, unroll=8)


def _slot_tokens(slots_packed, blk_flat, n, cap, nblk_pad):
    return pl.pallas_call(
        functools.partial(_slot_tokens_kernel, n=n, cap=cap, nblk_pad=nblk_pad),
        grid_spec=pltpu.PrefetchScalarGridSpec(
            num_scalar_prefetch=2,
            grid=(1,),
            in_specs=[],
            out_specs=pl.BlockSpec(memory_space=pltpu.SMEM)),
        out_shape=jax.ShapeDtypeStruct((cap,), I32),
        compiler_params=_cparams(("arbitrary",)),
        name="moe_slot_tokens",
    )(slots_packed, blk_flat)


ROW_BUFS = 3


def _expert_kernel(blk_ref, tok_ref, xpk_hbm, wg_hbm, wu_hbm, wd_hbm, y_ref, xbuf, wgb, wub, wdb,
                   sems, wsems, cnt_ref, *, nblk_pad):
    i = pl.program_id(0)
    n_used = blk_ref[nblk_pad]

    def weight_copies(e, slot):
        return [pltpu.make_async_copy(src.at[e], dst.at[slot], wsems.at[slot])
                for src, dst in ((wg_hbm, wgb), (wu_hbm, wub), (wd_hbm, wdb))]

    def start_rows(block, buf, r_lo=0, r_hi=MOE_BLOCK):
        for r in range(r_lo, r_hi):
            src0 = pl.multiple_of(tok_ref[block * MOE_BLOCK + r] * SUBLANES, SUBLANES)
            pltpu.make_async_copy(xpk_hbm.at[pl.ds(src0, SUBLANES)],
                                  xbuf.at[buf, pl.ds(r * SUBLANES, SUBLANES)], sems.at[buf]).start(priority=r % 2)

    def wait_rows(buf):
        pltpu.make_async_copy(xpk_hbm.at[pl.ds(0, MOE_BLOCK * SUBLANES)], xbuf.at[buf], sems.at[buf]).wait()

    @pl.when(i == 0)
    def _():
        cnt_ref[0] = 0
        for cp in weight_copies(blk_ref[0], 0):
            cp.start(priority=1)
        for ahead in range(ROW_BUFS - 1):
            start_rows(jnp.minimum(ahead, n_used - 1), ahead)

    @pl.when((i < n_used) & ((i == 0) | (blk_ref[i] != blk_ref[jnp.maximum(i - 1, 0)])))
    def _():
        slot = cnt_ref[0] % 2
        cnt_ref[0] = cnt_ref[0] + 1
        for cp in weight_copies(blk_ref[i], slot):
            cp.wait()
        nxt_e = blk_ref[2 * nblk_pad + i]

        @pl.when(nxt_e >= 0)
        def _():
            for cp in weight_copies(nxt_e, 1 - slot):
                cp.start(priority=1)

    @pl.when(i < n_used)
    def _():
        wslot = (cnt_ref[0] + 1) % 2
        buf = i % ROW_BUFS
        nbuf = (i + ROW_BUFS - 1) % ROW_BUFS
        nxt = jnp.minimum(i + ROW_BUFS - 1, n_used - 1)
        wait_rows(buf)
        lo, hi = _unpack_halves(_load_row_tiles(xbuf.at[buf], MOE_BLOCK))
        xb = jnp.concatenate([lo.astype(BF16), hi.astype(BF16)], axis=-1)
        g = _dot(xb, wgb[wslot])
        start_rows(nxt, nbuf, 0, MOE_BLOCK // 2)
        u = _dot(xb, wub[wslot])
        start_rows(nxt, nbuf, MOE_BLOCK // 2, MOE_BLOCK)
        hmid = (g * jax.nn.sigmoid(g) * u).astype(BF16)
        _store_row_tiles(y_ref, _pack_halves(_dot(hmid, wdb[wslot])))

    @pl.when(i == n_used - 1)
    def _():
        for ahead in range(1, ROW_BUFS):
            wait_rows((i + ahead) % ROW_BUFS)

    @pl.when(i >= n_used)
    def _():
        y_ref[...] = jnp.zeros_like(y_ref)


def _experts(blk_flat, slot_tok, xpk, wg, wu, wd, nblk_pad):
    d, de = wg.shape[1], wg.shape[2]
    assert d == 2 * SUBLANES * LANES and xpk.shape[1] == LANES
    cap = slot_tok.shape[0]
    tile_rows = MOE_BLOCK * SUBLANES
    return pl.pallas_call(
        functools.partial(_expert_kernel, nblk_pad=nblk_pad),
        grid_spec=pltpu.PrefetchScalarGridSpec(
            num_scalar_prefetch=2,
            grid=(cap // MOE_BLOCK,),
            in_specs=[pl.BlockSpec(memory_space=pl.ANY), pl.BlockSpec(memory_space=pl.ANY),
                      pl.BlockSpec(memory_space=pl.ANY), pl.BlockSpec(memory_space=pl.ANY)],
            out_specs=pl.BlockSpec((tile_rows, LANES), lambda i, blk, tok: (i, 0)),
            scratch_shapes=[pltpu.VMEM((ROW_BUFS, tile_rows, LANES), U32),
                            pltpu.VMEM((2, d, de), BF16), pltpu.VMEM((2, d, de), BF16), pltpu.VMEM((2, de, d), BF16),
                            pltpu.SemaphoreType.DMA((ROW_BUFS,)), pltpu.SemaphoreType.DMA((2,)),
                            pltpu.SMEM((1,), I32)]),
        out_shape=jax.ShapeDtypeStruct((cap * SUBLANES, LANES), U32),
        compiler_params=_cparams(("arbitrary",)),
        name="moe_experts",
    )(blk_flat, slot_tok, xpk, wg, wu, wd)


def _tail_kernel(slots_ref, x_ref, xb_ref, p_ref, wts_ref, wgate_ref, bgate_ref, wproj_ref, g_ref, b_ref,
                 yb_hbm, y_ref, ybf_ref, rows_ref, sems):
    i = pl.program_id(0)
    nsteps = pl.num_programs(0)
    tm = x_ref.shape[0]
    n = nsteps * tm

    def start_rows(step, buf):
        for t in range(tm):
            for which in range(2):
                src0 = pl.multiple_of(slots_ref[which * n + step * tm + t] * SUBLANES, SUBLANES)
                pltpu.make_async_copy(yb_hbm.at[pl.ds(src0, SUBLANES)],
                                      rows_ref.at[buf, which, pl.ds(t * SUBLANES, SUBLANES)], sems.at[buf]).start()

    def wait_rows(buf):
        for which in range(2):
            pltpu.make_async_copy(yb_hbm.at[pl.ds(0, tm * SUBLANES)], rows_ref.at[buf, which],
                                  sems.at[buf]).wait()

    @pl.when(i == 0)
    def _():
        start_rows(0, 0)

    buf = i % 2
    nxt = jnp.minimum(i + 1, nsteps - 1)
    wait_rows(buf)
    gate_pre = _dot(xb_ref[...], wgate_ref[...])
    proj = _dot(p_ref[0].astype(BF16), wproj_ref[...])
    start_rows(nxt, 1 - buf)
    ple = jax.nn.sigmoid(gate_pre + bgate_ref[...]) * proj
    w = wts_ref[...]
    lo0, hi0 = _unpack_halves(_load_row_tiles(rows_ref.at[buf, 0], tm))
    lo1, hi1 = _unpack_halves(_load_row_tiles(rows_ref.at[buf, 1], tm))
    w0 = w[:, 0:1]
    w1 = w[:, 1:2]
    ffn = jnp.concatenate([lo0 * w0 + lo1 * w1, hi0 * w0 + hi1 * w1], axis=-1)
    z = DN_ALPHA * x_ref[...] + ffn + ple
    y = _layer_norm_rows(z, g_ref[...], b_ref[...])
    y_ref[...] = y
    ybf_ref[...] = y.astype(BF16)

    @pl.when(i == nsteps - 1)
    def _():
        wait_rows(1 - buf)


def _layer_tail(slots_flat, x, xb, p, layer, wts_t, wgate, bgate, wproj, g, b, yb, tm):
    n, d = x.shape
    pd = p.shape[2]
    return pl.pallas_call(
        _tail_kernel,
        grid_spec=pltpu.PrefetchScalarGridSpec(
            num_scalar_prefetch=1,
            grid=(n // tm,),
            in_specs=[pl.BlockSpec((tm, d), lambda i, s: (i, 0)),
                      pl.BlockSpec((tm, d), lambda i, s: (i, 0)),
                      pl.BlockSpec((1, tm, pd), lambda i, s: (layer, i, 0)),
                      pl.BlockSpec((tm, 2), lambda i, s: (i, 0)),
                      pl.BlockSpec((d, d), lambda i, s: (0, 0)),
                      pl.BlockSpec((1, d), lambda i, s: (0, 0)),
                      pl.BlockSpec((pd, d), lambda i, s: (0, 0)),
                      pl.BlockSpec((1, d), lambda i, s: (0, 0)),
                      pl.BlockSpec((1, d), lambda i, s: (0, 0)),
                      pl.BlockSpec(memory_space=pl.ANY)],
            out_specs=[pl.BlockSpec((tm, d), lambda i, s: (i, 0)),
                       pl.BlockSpec((tm, d), lambda i, s: (i, 0))],
            scratch_shapes=[pltpu.VMEM((2, 2, tm * SUBLANES, LANES), U32), pltpu.SemaphoreType.DMA((2,))]),
        out_shape=[jax.ShapeDtypeStruct((n, d), F32), jax.ShapeDtypeStruct((n, d), BF16)],
        compiler_params=_cparams(("arbitrary",)),
        name="layer_tail",
    )(slots_flat, x, xb, p, wts_t, wgate, bgate.reshape(1, d), wproj, g.reshape(1, d), b.reshape(1, d), yb)


def _rope_table(seq, dim):
    pos = jnp.arange(seq, dtype=F32)
    inv = jnp.exp(jnp.arange(0, dim, 2, dtype=F32) * (-math.log(ROPE_BASE) / dim))
    ang = pos[:, None] * inv[None, :]
    return jnp.cos(ang), jnp.sin(ang)


def _moe_layer(x, xb, xpk, p, layer, w_group, b_group, w_router, b_router, w_gate, w_up, w_down,
               ple_w_proj, ple_w_gate, ple_b_gate, ln_g, ln_b):
    n, d = x.shape
    nblk = -(-(2 * n) // MOE_BLOCK) + N_EXPERTS
    nblk_pad = -(-nblk // LANES) * LANES
    cap = nblk * MOE_BLOCK
    wt = jnp.zeros((ROUTER_ROWS, d), F32)
    wt = wt.at[0:N_GROUPS].set(w_group.T)
    wt = wt.at[SUBLANES:].set(w_router.transpose(0, 2, 1).reshape(N_EXPERTS, d))
    bias = jnp.zeros((ROUTER_ROWS, 1), F32)
    bias = bias.at[0:N_GROUPS, 0].set(b_group)
    bias = bias.at[SUBLANES:, 0].set(b_router.reshape(N_EXPERTS))
    ids, wts, cnt = _router(xb, wt.astype(BF16), bias, tm=512)
    slots, blk = _slots(ids, cnt, tm=min(2048, n), nblk_pad=nblk_pad)
    slots_flat = slots[0:2].reshape(2 * n)
    blk_flat = blk[0:5].reshape(5 * nblk_pad)
    assert cap <= 1 << 16
    slot_tok = _slot_tokens(slots[2], blk_flat, n, cap, nblk_pad)
    yb = _experts(blk_flat, slot_tok, xpk, w_gate, w_up, w_down, nblk_pad)
    return _layer_tail(slots_flat, x, xb, p, layer, wts[0:2].T, ple_w_gate, ple_b_gate, ple_w_proj, ln_g, ln_b,
                       yb, tm=256)


def _mixer_ab(x, xres, batch, seq, w_in, rpb, q_norm, w_uq, kv_norm, w_ukv, w_out_all, j, ln_g, ln_b,
              expert_ws, layer, dense_early, dense_late):
    d = x.shape[1]
    o1 = 3 * NA_WIDTH
    o2 = o1 + MLA_Q_RANK
    o3 = o2 + MLA_KV_RANK
    half = MLA_ROPE_DIM // 2
    kr = w_in[:, o3:o3 + MLA_ROPE_DIM]
    kr_sw = jnp.concatenate([kr[:, half:], kr[:, :half]], axis=1)
    width = -(-(o3 + 2 * MLA_ROPE_DIM) // 1024) * 1024
    w_in_p = jnp.concatenate([w_in, kr_sw, jnp.zeros((d, width - o3 - 2 * MLA_ROPE_DIM), F32)], axis=1)
    h, early_b = _matmul(x, w_in_p.astype(BF16), BF16, tm=1024, tn=1024, cast_ws=dense_early)
    a_out = _na_attention(h, _na_bias_tables(rpb), batch, seq)
    dq = MLA_NOPE_DIM + MLA_ROPE_DIM
    wq = w_uq.reshape(MLA_Q_RANK, MLA_HEADS, dq)
    wq_pe = wq[:, :, MLA_NOPE_DIM:]
    wq_p = jnp.concatenate([wq, wq_pe[:, :, half:], wq_pe[:, :, :half]], axis=2)
    wq_p = wq_p.reshape(MLA_Q_RANK, MLA_HEADS * 2 * LANES).astype(BF16)
    wkv = w_ukv.reshape(MLA_KV_RANK, MLA_HEADS, MLA_NOPE_DIM + MLA_V_DIM)
    wk = wkv[:, :, :MLA_NOPE_DIM].reshape(MLA_KV_RANK, MLA_HEADS * MLA_NOPE_DIM).astype(BF16)
    wvt = wkv[:, :, MLA_NOPE_DIM:].reshape(MLA_KV_RANK, MLA_HEADS * MLA_V_DIM).T.astype(BF16)
    cos, sin = _rope_table(seq, MLA_ROPE_DIM)
    zpad = jnp.zeros((seq, LANES - MLA_ROPE_DIM), F32)
    cosf = jnp.concatenate([cos, cos, zpad], axis=1)
    sinf = jnp.concatenate([-sin, sin, zpad], axis=1)
    q_p, k_p, vt = _mla_prep(h, q_norm, kv_norm, wq_p, wk, wvt, cosf, sinf, o1, seq, tm=512)
    b_out, cast_out = _mla_attention(q_p, k_p, vt, batch, seq, tq=min(1024, seq), tk=1024, sub=256,
                                     cast_ws=expert_ws, cast_layer=layer,
                                     cast_dense=[(w_out_all, j)] + list(dense_late))
    expert_wb, w_out_b, late_b = cast_out[:len(expert_ws)], cast_out[len(expert_ws)], cast_out[len(expert_ws) + 1:]
    outs = _proj_ln([a_out, b_out], [w_out_b[:NA_WIDTH], w_out_b[NA_WIDTH:]], xres, ln_g, ln_b, tm=512, nk=1)
    return outs, expert_wb, early_b, late_b


def _mixer_c(xb, xres, batch, seq, w_in, log_rate_f, log_rate_b, w_out, ln_g, ln_b, expert_ws, layer):
    cosr, sinr = _rope_table(seq, RET_QK_DIM)
    n_q = RET_HEADS * RET_QK_DIM
    hc, expert_wb = _matmul_rope(xb, w_in, cosr, sinr, BF16, tm=min(1024, seq), tn=1024,
                                 n_q_cols=n_q, n_rope_cols=2 * n_q, gate_col0=2 * n_q + RET_HEADS * RET_V_DIM,
                                 head_w=RET_QK_DIM, q_scale=RET_QK_DIM ** -0.5,
                                 cast_ws=expert_ws, cast_layer=layer, cast_j=8)
    lg = jnp.stack([jnp.log1p(-jnp.exp(log_rate_f.astype(F32))), jnp.log1p(-jnp.exp(log_rate_b.astype(F32)))])
    r = _retention(hc, lg, batch, seq, c_len=256, group=min(8, seq // 256))
    return _proj_ln([r], [w_out], xres, ln_g, ln_b, tm=512, nk=1), expert_wb


def kernel(x, p, ab_w_in, ab_rpb, ab_q_norm, ab_w_uq, ab_kv_norm, ab_w_ukv, ab_w_out, c_w_in, c_log_rate_f,
           c_log_rate_b, c_w_out, ln1_g, ln1_b, moe_w_group, moe_b_group, moe_w_router, moe_b_router,
           moe_w_gate, moe_w_up, moe_w_down, ple_w_proj, ple_w_gate, ple_b_gate, ln2_g, ln2_b):
    batch, seq, d = x.shape
    n = batch * seq
    xf = x.reshape(n, d)
    p_flat = p.reshape(DEPTH, n, -1)
    expert_ws = (moe_w_gate, moe_w_up, moe_w_down)
    assert DEPTH == 2
    dense_late = [(c_w_out, 0)] + [(w, i) for i in range(DEPTH) for w in (ple_w_gate, ple_w_proj)]
    (xf, xb, xpk), (wg, wu, wd), (c_w_in_b,), late_b = _mixer_ab(
        xf, xf, batch, seq, ab_w_in[0], ab_rpb[0], ab_q_norm[0], ab_w_uq[0], ab_kv_norm[0], ab_w_ukv[0],
        ab_w_out, 0, ln1_g[0], ln1_b[0], expert_ws, 0, dense_early=[(c_w_in, 0)], dense_late=dense_late)
    c_w_out_b, ple_b = late_b[0], late_b[1:]
    xf, xb = _moe_layer(xf, xb, xpk, p_flat, 0, moe_w_group[0], moe_b_group[0], moe_w_router[0],
                        moe_b_router[0], wg, wu, wd, ple_b[1], ple_b[0], ple_b_gate[0], ln2_g[0], ln2_b[0])
    (xf, xb, xpk), (wg, wu, wd) = _mixer_c(
        xb, xf, batch, seq, c_w_in_b, c_log_rate_f[0], c_log_rate_b[0], c_w_out_b, ln1_g[1], ln1_b[1],
        expert_ws, 1)
    xf, xb = _moe_layer(xf, xb, xpk, p_flat, 1, moe_w_group[1], moe_b_group[1], moe_w_router[1],
                        moe_b_router[1], wg, wu, wd, ple_b[3], ple_b[2], ple_b_gate[1], ln2_g[1], ln2_b[1])
    return xf.reshape(batch, seq, d)
```

```python
import functools
import math

import numpy as np
import jax
import jax.numpy as jnp
from jax import lax
from jax.experimental import pallas as pl
from jax.experimental.pallas import tpu as pltpu

DEPTH = 2
GRID_W = 64
NA_HEADS = 8
NA_HEAD_DIM = 128
NA_WIN_H = 8
NA_WIN_W = 16
MLA_HEADS = 8
MLA_Q_RANK = 512
MLA_KV_RANK = 256
MLA_NOPE_DIM = 128
MLA_ROPE_DIM = 64
MLA_V_DIM = 128
RET_HEADS = 8
RET_QK_DIM = 256
RET_V_DIM = 512
RET_CHUNK = 128
N_GROUPS = 4
EXPERTS_PER_GROUP = 8
N_EXPERTS = N_GROUPS * EXPERTS_PER_GROUP
D_EXPERT = 512
MOE_BLOCK = 128
ROPE_BASE = 10000.0
LN_EPS = 1e-5
RMS_EPS = 1e-6
DN_ALPHA = (2 * DEPTH) ** 0.25
NA_WIDTH = NA_HEADS * NA_HEAD_DIM

LANES = 128
SUBLANES = 8
VMEM_LIMIT_BYTES = 60 * 1024 * 1024
MASK_VALUE = -1e30

F32 = jnp.float32
BF16 = jnp.bfloat16
I32 = jnp.int32
U32 = jnp.uint32


def _cparams(sem):
    return pltpu.CompilerParams(dimension_semantics=sem, vmem_limit_bytes=VMEM_LIMIT_BYTES)


def _dot(a, b):
    return jnp.dot(a, b, preferred_element_type=F32)


def _dot_nt(a, b, precision=None):
    return lax.dot_general(a, b, (((1,), (1,)), ((), ())), preferred_element_type=F32,
                           precision=precision)


def _pack_halves(y):
    c = y.shape[1] // 2
    bits = pltpu.bitcast(y.astype(BF16).astype(F32), U32)
    return (bits[:, :c] >> 16) | (bits[:, c:] & jnp.uint32(0xFFFF0000))


def _unpack_halves(w):
    lo = pltpu.bitcast(w << 16, F32)
    hi = pltpu.bitcast(w & jnp.uint32(0xFFFF0000), F32)
    return lo, hi


def _store_row_tiles(ref, packed):
    m = packed.shape[0]
    for s in range(SUBLANES):
        ref[pl.ds(s, m, stride=SUBLANES), :] = packed[:, s * LANES:(s + 1) * LANES]


def _load_row_tiles(ref, m):
    return jnp.concatenate([ref[pl.ds(s, m, stride=SUBLANES), :] for s in range(SUBLANES)], axis=-1)


def _mm_kernel(x_ref, w_ref, *rest, n_cast):
    cast_src, o_ref, cast_dst = rest[:n_cast], rest[n_cast], rest[n_cast + 1:]
    _cast_blocks(cast_src, cast_dst)
    o_ref[...] = _dot(x_ref[...].astype(BF16), w_ref[...]).astype(o_ref.dtype)


def _matmul(x, w, out_dtype, tm, tn, cast_ws):
    m, k = x.shape
    n = w.shape[1]
    nj = n // tn
    streams = [_cast_rows_specs(cw, lead, (m // tm) * nj, lambda i, j: i * nj + j) for cw, lead in cast_ws]
    outs = pl.pallas_call(
        functools.partial(_mm_kernel, n_cast=len(cast_ws)),
        grid=(m // tm, nj),
        in_specs=[pl.BlockSpec((tm, k), lambda i, j: (i, 0)),
                  pl.BlockSpec((k, tn), lambda i, j: (0, j))] + [s[0] for s in streams],
        out_specs=[pl.BlockSpec((tm, tn), lambda i, j: (i, j))] + [s[1] for s in streams],
        out_shape=[jax.ShapeDtypeStruct((m, n), out_dtype)] + [s[2] for s in streams],
        compiler_params=_cparams(("arbitrary", "arbitrary")),
        name="matmul",
    )(x, w, *[cw for cw, _ in cast_ws])
    return outs[0], outs[1:]


def _mm_rope_kernel(x_ref, w_ref, cos_ref, sin_ref, *rest, n_q_tiles, n_rope_tiles, first_gate_tile,
                    head_w, q_scale, n_cast):
    cast_src, o_ref, cast_dst = rest[:n_cast], rest[n_cast], rest[n_cast + 1:]
    j = pl.program_id(1)
    _cast_blocks(cast_src, cast_dst)
    acc = _dot(x_ref[...].astype(BF16), w_ref[...])

    @pl.when((j >= n_rope_tiles) & (j < first_gate_tile))
    def _():
        o_ref[...] = acc.astype(o_ref.dtype)

    @pl.when(j >= first_gate_tile)
    def _():
        o_ref[...] = (acc * jax.nn.sigmoid(acc)).astype(o_ref.dtype)

    @pl.when(j < n_rope_tiles)
    def _():
        scale = jnp.where(j < n_q_tiles, q_scale, 1.0)
        cos = cos_ref[...] * scale
        sin = sin_ref[...] * scale
        half = head_w // 2
        for c0 in range(0, acc.shape[1], head_w):
            x1 = acc[:, c0:c0 + half]
            x2 = acc[:, c0 + half:c0 + head_w]
            o_ref[:, c0:c0 + half] = (x1 * cos - x2 * sin).astype(o_ref.dtype)
            o_ref[:, c0 + half:c0 + head_w] = (x2 * cos + x1 * sin).astype(o_ref.dtype)


def _matmul_rope(x, w, cos, sin, out_dtype, tm, tn, n_q_cols, n_rope_cols, gate_col0, head_w, q_scale,
                 cast_ws, cast_layer, cast_j):
    m, k = x.shape
    n = w.shape[1]
    nsb = cos.shape[0] // tm
    streams = [_cast_stream_specs(cw, cast_layer, (m // tm) * cast_j,
                                  lambda i, j: i * cast_j + jnp.minimum(j, cast_j - 1)) for cw in cast_ws]
    outs = pl.pallas_call(
        functools.partial(_mm_rope_kernel, n_q_tiles=n_q_cols // tn, n_rope_tiles=n_rope_cols // tn,
                          first_gate_tile=gate_col0 // tn, head_w=head_w, q_scale=q_scale,
                          n_cast=len(cast_ws)),
        grid=(m // tm, n // tn),
        in_specs=[pl.BlockSpec((tm, k), lambda i, j: (i, 0)),
                  pl.BlockSpec((k, tn), lambda i, j: (0, j)),
                  pl.BlockSpec((tm, head_w // 2), lambda i, j: (i % nsb, 0)),
                  pl.BlockSpec((tm, head_w // 2), lambda i, j: (i % nsb, 0))] + [s[0] for s in streams],
        out_specs=[pl.BlockSpec((tm, tn), lambda i, j: (i, j))] + [s[1] for s in streams],
        out_shape=[jax.ShapeDtypeStruct((m, n), out_dtype)] + [s[2] for s in streams],
        compiler_params=_cparams(("arbitrary", "arbitrary")),
        name="matmul_rope",
    )(x, w, cos, sin, *cast_ws)
    return outs[0], outs[1:]


def _layer_norm_rows(z, g, b):
    mean = jnp.mean(z, axis=-1, keepdims=True)
    zc = z - mean
    var = jnp.mean(zc * zc, axis=-1, keepdims=True)
    return zc * lax.rsqrt(var + LN_EPS) * g + b


def _proj_ln_kernel(*refs, n_act, nk):
    acts = refs[:n_act]
    ws = refs[n_act:2 * n_act]
    x_ref, g_ref, b_ref, y_ref, yb_ref, yp_ref = refs[2 * n_act:2 * n_act + 6]
    k = pl.program_id(1)
    tm = x_ref.shape[0]
    n_split = 2
    hm = tm // n_split

    def product(rows):
        part = _dot(acts[0][rows, :], ws[0][...])
        for a, w in zip(acts[1:], ws[1:]):
            part = part + _dot(a[rows, :], w[...])
        return part

    def finish(rows, h, proj):
        z = DN_ALPHA * x_ref[rows, :] + proj
        y = _layer_norm_rows(z, g_ref[...], b_ref[...])
        y_ref[rows, :] = y
        yb_ref[rows, :] = y.astype(BF16)
        _store_row_tiles(yp_ref.at[pl.ds(h * hm * SUBLANES, hm * SUBLANES)], _pack_halves(y))

    if nk == 1:
        for h in range(n_split):
            rows = pl.ds(h * hm, hm)
            finish(rows, h, product(rows))
        return
    acc_ref = refs[2 * n_act + 6]

    @pl.when(k == 0)
    def _():
        acc_ref[...] = product(pl.ds(0, tm))

    @pl.when((k > 0) & (k < nk - 1))
    def _():
        acc_ref[...] = acc_ref[...] + product(pl.ds(0, tm))

    @pl.when(k == nk - 1)
    def _():
        for h in range(n_split):
            rows = pl.ds(h * hm, hm)
            finish(rows, h, acc_ref[rows, :] + product(rows))


def _proj_ln(acts, ws, x, g, b, tm, nk):
    m, d = x.shape
    n_act = len(acts)
    in_specs = []
    for a in acts:
        kk = a.shape[1] // nk
        in_specs.append(pl.BlockSpec((tm, kk), lambda i, k: (i, k)))
    for w in ws:
        kk = w.shape[0] // nk
        mode = pl.Buffered(1) if nk == 1 else None
        in_specs.append(pl.BlockSpec((kk, d), lambda i, k: (k, 0), pipeline_mode=mode))
    in_specs += [pl.BlockSpec((tm, d), lambda i, k: (i, 0)),
                 pl.BlockSpec((1, d), lambda i, k: (0, 0)),
                 pl.BlockSpec((1, d), lambda i, k: (0, 0))]
    return pl.pallas_call(
        functools.partial(_proj_ln_kernel, n_act=n_act, nk=nk),
        grid=(m // tm, nk),
        in_specs=in_specs,
        out_specs=[pl.BlockSpec((tm, d), lambda i, k: (i, 0)),
                   pl.BlockSpec((tm, d), lambda i, k: (i, 0)),
                   pl.BlockSpec((tm * SUBLANES, LANES), lambda i, k: (i, 0))],
        out_shape=[jax.ShapeDtypeStruct((m, d), F32), jax.ShapeDtypeStruct((m, d), BF16),
                   jax.ShapeDtypeStruct((m * SUBLANES, LANES), U32)],
        scratch_shapes=[pltpu.VMEM((tm, d), F32)] if nk > 1 else [],
        compiler_params=_cparams(("parallel", "arbitrary")),
        name="proj_ln",
    )(*acts, *ws, x, g.reshape(1, d), b.reshape(1, d))


def _na_bias_tables(rpb):
    nh = rpb.shape[0]
    c = np.arange(GRID_W)
    cs = np.clip(c - NA_WIN_W // 2, 0, GRID_W - NA_WIN_W)
    kc = np.arange(GRID_W)
    valid = (kc[None, :] >= cs[:, None]) & (kc[None, :] < cs[:, None] + NA_WIN_W)
    dc = kc[None, :] - c[:, None] + NA_WIN_W - 1
    onehot = (dc[:, :, None] == np.arange(2 * NA_WIN_W - 1)[None, None, :]) & valid[:, :, None]
    cols = jnp.einsum("hrd,ckd->hrck", rpb.astype(F32), jnp.asarray(onehot, F32),
                      precision=lax.Precision.HIGHEST)
    cols = jnp.where(jnp.asarray(valid)[None, None], cols, MASK_VALUE)
    tabs = jnp.stack([cols[:, off:off + NA_WIN_H] for off in range(NA_WIN_H)], axis=1)
    return tabs.transpose(0, 1, 3, 2, 4).reshape(nh, NA_WIN_H, GRID_W, NA_WIN_H * GRID_W)


def _na_kernel(q_ref, k_ref, v_ref, bias_ref, o_ref, *, rows, group):
    scale = NA_HEAD_DIM ** -0.5
    nkeys = NA_WIN_H * GRID_W

    def body(i, carry):
        geom, scores = [], []
        for u in range(group):
            r = i * group + u
            rs = jnp.clip(r - NA_WIN_H // 2, 0, rows - NA_WIN_H)
            off = rs - r + NA_WIN_H - 1
            q0 = pl.multiple_of(r * GRID_W, GRID_W)
            k0 = pl.multiple_of(rs * GRID_W, GRID_W)
            geom.append((q0, k0))
            s = _dot_nt(q_ref[pl.ds(q0, GRID_W), :], k_ref[pl.ds(k0, nkeys), :])
            scores.append(s * scale + bias_ref[0, off])
        for (q0, k0), s in zip(geom, scores):
            m = jnp.max(s, axis=-1, keepdims=True)
            p = jnp.exp(s - m)
            l = jnp.sum(p, axis=-1, keepdims=True)
            o = _dot(p.astype(BF16), v_ref[pl.ds(k0, nkeys), :]) / l
            o_ref[pl.ds(q0, GRID_W), :] = o.astype(o_ref.dtype)
        return carry

    lax.fori_loop(0, rows // group, body, 0)


def _na_attention(h, bias_tables, batch, seq):
    rows = seq // GRID_W
    d = NA_HEAD_DIM
    nkeys = NA_WIN_H * GRID_W
    return pl.pallas_call(
        functools.partial(_na_kernel, rows=rows, group=min(32, rows)),
        grid=(batch, NA_HEADS),
        in_specs=[pl.BlockSpec((seq, d), lambda b, hh: (b, hh)),
                  pl.BlockSpec((seq, d), lambda b, hh: (b, NA_HEADS + hh)),
                  pl.BlockSpec((seq, d), lambda b, hh: (b, 2 * NA_HEADS + hh)),
                  pl.BlockSpec((1, NA_WIN_H, GRID_W, nkeys), lambda b, hh: (hh, 0, 0, 0))],
        out_specs=pl.BlockSpec((seq, d), lambda b, hh: (b, hh)),
        out_shape=jax.ShapeDtypeStruct((batch * seq, NA_WIDTH), BF16),
        compiler_params=_cparams(("parallel", "arbitrary")),
        name="na_attention",
    )(h, h, h, bias_tables)


def _rms_rows(x, g):
    return x * lax.rsqrt(jnp.mean(x * x, axis=-1, keepdims=True) + RMS_EPS) * g


def _rope_lanes(t, cosf, sinf):
    return t * cosf + pltpu.roll(t, LANES // 2, 1) * sinf


def _mla_prep_kernel(cq_ref, ckv_ref, kr_ref, gq_ref, gkv_ref, wq_ref, wk_ref, wvt_ref, cos_ref, sin_ref,
                     q_ref, k_ref, vt_ref):
    dq = MLA_NOPE_DIM + MLA_ROPE_DIM
    cosf = cos_ref[...]
    sinf = sin_ref[...]
    cqn = _rms_rows(cq_ref[...].astype(F32), gq_ref[...]).astype(BF16)
    ckvn = _rms_rows(ckv_ref[...].astype(F32), gkv_ref[...]).astype(BF16)
    qf = _dot(cqn, wq_ref[...]) * (dq ** -0.5 * math.log2(math.e))
    kf = _dot(ckvn, wk_ref[...])
    vt_ref[...] = _dot_nt(wvt_ref[...], ckvn).astype(BF16)
    kpe = _rope_lanes(kr_ref[...].astype(F32), cosf, sinf).astype(BF16)
    for hh in range(MLA_HEADS):
        c0 = hh * 2 * LANES
        q_ref[:, c0:c0 + LANES] = qf[:, c0:c0 + LANES].astype(BF16)
        q_ref[:, c0 + LANES:c0 + 2 * LANES] = _rope_lanes(qf[:, c0 + LANES:c0 + 2 * LANES], cosf, sinf).astype(BF16)
        k_ref[:, c0:c0 + LANES] = kf[:, hh * LANES:(hh + 1) * LANES].astype(BF16)
        k_ref[:, c0 + LANES:c0 + 2 * LANES] = kpe


def _mla_prep(h, gq, gkv, wq_p, wk, wvt, cosf, sinf, col_cq, seq, tm):
    n = h.shape[0]
    hw = MLA_HEADS * 2 * LANES
    nsb = seq // tm
    b_cq = col_cq // MLA_Q_RANK
    b_ckv = (col_cq + MLA_Q_RANK) // MLA_KV_RANK
    b_kr = (col_cq + MLA_Q_RANK + MLA_KV_RANK) // LANES
    return pl.pallas_call(
        _mla_prep_kernel,
        grid=(n // tm,),
        in_specs=[pl.BlockSpec((tm, MLA_Q_RANK), lambda i: (i, b_cq)),
                  pl.BlockSpec((tm, MLA_KV_RANK), lambda i: (i, b_ckv)),
                  pl.BlockSpec((tm, LANES), lambda i: (i, b_kr)),
                  pl.BlockSpec((1, MLA_Q_RANK), lambda i: (0, 0)),
                  pl.BlockSpec((1, MLA_KV_RANK), lambda i: (0, 0)),
                  pl.BlockSpec((MLA_Q_RANK, hw), lambda i: (0, 0)),
                  pl.BlockSpec((MLA_KV_RANK, MLA_HEADS * LANES), lambda i: (0, 0)),
                  pl.BlockSpec((MLA_HEADS * MLA_V_DIM, MLA_KV_RANK), lambda i: (0, 0)),
                  pl.BlockSpec((tm, LANES), lambda i: (i % nsb, 0)),
                  pl.BlockSpec((tm, LANES), lambda i: (i % nsb, 0))],
        out_specs=[pl.BlockSpec((tm, hw), lambda i: (i, 0)),
                   pl.BlockSpec((tm, hw), lambda i: (i, 0)),
                   pl.BlockSpec((MLA_HEADS * MLA_V_DIM, tm), lambda i: (0, i))],
        out_shape=[jax.ShapeDtypeStruct((n, hw), BF16), jax.ShapeDtypeStruct((n, hw), BF16),
                   jax.ShapeDtypeStruct((MLA_HEADS * MLA_V_DIM, n), BF16)],
        compiler_params=_cparams(("parallel",)),
        name="mla_prep",
    )(h, h, h, gq.reshape(1, -1), gkv.reshape(1, -1), wq_p, wk, wvt, cosf, sinf)


def _cast_stream_specs(w, layer, n_slots, slot_of):
    _, ne, rows, cols = w.shape
    if n_slots >= ne:
        e_per, rb = 1, n_slots // ne
        while rows % rb or (rows // rb) % (2 * SUBLANES):
            rb -= 1
    else:
        assert ne % n_slots == 0
        e_per, rb = ne // n_slots, 1
    n_blocks = (ne // e_per) * rb

    def block_of(*g):
        s = jnp.minimum(slot_of(*g), n_blocks - 1)
        return s // rb, s % rb

    src = pl.BlockSpec((1, e_per, rows // rb, cols), lambda *g: (layer, *block_of(*g), 0))
    dst = pl.BlockSpec((e_per, rows // rb, cols), lambda *g: (*block_of(*g), 0))
    return src, dst, jax.ShapeDtypeStruct((ne, rows, cols), BF16), n_blocks


def _cast_rows_specs(w, lead, n_slots, slot_of):
    rows, cols = w.shape[-2:]
    nb = min(n_slots, rows // (2 * SUBLANES))
    while rows % nb or (rows // nb) % (2 * SUBLANES):
        nb -= 1
    src = pl.BlockSpec((1, rows // nb, cols), lambda *g: (lead, jnp.minimum(slot_of(*g), nb - 1), 0))
    dst = pl.BlockSpec((rows // nb, cols), lambda *g: (jnp.minimum(slot_of(*g), nb - 1), 0))
    return src, dst, jax.ShapeDtypeStruct((rows, cols), BF16), nb


def _cast_blocks(srcs, dsts):
    for src, dst in zip(srcs, dsts):
        dst[...] = src[0].astype(BF16)


def _mla_attn_kernel(q_ref, k_ref, vt_ref, *rest, tk, sub, n_cast):
    cast_src, o_ref, cast_dst = rest[:n_cast], rest[n_cast], rest[n_cast + 1:]
    _cast_blocks(cast_src, cast_dst)
    nchunk = k_ref.shape[0] // tk
    tq = q_ref.shape[0]
    nsub = tq // sub
    qs = [q_ref[s * sub:(s + 1) * sub, :] for s in range(nsub)]
    m = [jnp.full((1, sub), MASK_VALUE, F32) for _ in range(nsub)]
    l = [jnp.zeros((1, sub), F32) for _ in range(nsub)]
    acc = [jnp.zeros((MLA_V_DIM, sub), F32) for _ in range(nsub)]

    def scores(s, c):
        return _dot_nt(k_ref[c * tk:(c + 1) * tk, :], qs[s])

    st_next = [scores(s, 0) for s in range(nsub)]
    for c in range(nchunk):
        for s in range(nsub):
            st = st_next[s]
            m_new = jnp.maximum(m[s], jnp.max(st, axis=0, keepdims=True))
            a = jnp.exp2(m[s] - m_new)
            p = jnp.exp2(st - m_new)
            l[s] = a * l[s] + jnp.sum(p, axis=0, keepdims=True)
            if c + 1 < nchunk:
                st_next[s] = scores(s, c + 1)
            acc[s] = a * acc[s] + _dot(vt_ref[:, c * tk:(c + 1) * tk], p.astype(BF16))
            m[s] = m_new
    for s in range(nsub):
        o_ref[s * sub:(s + 1) * sub, :] = (acc[s] / l[s]).T.astype(o_ref.dtype)


def _mla_attention(q_p, k_p, vt, batch, seq, tq, tk, sub, cast_ws, cast_layer, cast_dense):
    n = q_p.shape[0]
    nqb = seq // tq
    n_slots = batch * MLA_HEADS * nqb

    def slot_of(b, hh, i):
        return (b * MLA_HEADS + hh) * nqb + i

    streams = [_cast_stream_specs(w, cast_layer, n_slots, slot_of) for w in cast_ws]
    streams += [_cast_rows_specs(w, lead, n_slots, slot_of) for w, lead in cast_dense]
    cast_ws = list(cast_ws) + [w for w, _ in cast_dense]
    outs = pl.pallas_call(
        functools.partial(_mla_attn_kernel, tk=tk, sub=sub, n_cast=len(cast_ws)),
        grid=(batch, MLA_HEADS, nqb),
        in_specs=[pl.BlockSpec((tq, 2 * LANES), lambda b, hh, i: (b * nqb + i, hh)),
                  pl.BlockSpec((seq, 2 * LANES), lambda b, hh, i: (b, hh)),
                  pl.BlockSpec((MLA_V_DIM, seq), lambda b, hh, i: (hh, b))] + [s[0] for s in streams],
        out_specs=[pl.BlockSpec((tq, MLA_V_DIM), lambda b, hh, i: (b * nqb + i, hh))] + [s[1] for s in streams],
        out_shape=[jax.ShapeDtypeStruct((n, MLA_HEADS * MLA_V_DIM), BF16)] + [s[2] for s in streams],
        compiler_params=_cparams(("arbitrary", "arbitrary", "arbitrary")),
        name="mla_attention",
    )(q_p, k_p, vt, *cast_ws)
    return outs[0], outs[1:]


def _ret_kernel(lg_ref, q_ref, k_ref, v_ref, g_ref, o_ref, acc_ref, st_ref, *, c_len, group):
    nchunk = q_ref.shape[0] // c_len
    hh = pl.program_id(1)
    lgf = lg_ref[0, hh]
    lgb = lg_ref[1, hh]
    ii = lax.broadcasted_iota(I32, (c_len, c_len), 0).astype(F32)
    jj = lax.broadcasted_iota(I32, (c_len, c_len), 1).astype(F32)
    rel = ii - jj
    dmat = jnp.where(rel >= 0, jnp.exp(lgf * jnp.maximum(rel, 0.0)), jnp.exp(lgb * jnp.maximum(-rel, 0.0)))
    pos = lax.broadcasted_iota(I32, (c_len, 1), 0).astype(F32)
    qdec_f = jnp.exp(lgf * (pos + 1.0))
    kdec_f = jnp.exp(lgf * (c_len - 1.0 - pos))
    qdec_b = jnp.exp(lgb * (c_len - pos))
    kdec_b = jnp.exp(lgb * pos)
    full_chunk = jnp.full((1, RET_V_DIM), float(c_len), F32)
    cdec_f = jnp.exp(lgf * full_chunk)
    cdec_b = jnp.exp(lgb * full_chunk)

    def decayed_keys_t(t0, kdec):
        return (k_ref[pl.ds(t0, c_len), :].astype(F32) * kdec).T.astype(BF16)

    st_ref[...] = jnp.zeros_like(st_ref)

    def bwd_body(i, carry):
        t0s = [pl.multiple_of((nchunk - 1 - (i * group + u)) * c_len, c_len) for u in range(group)]
        upd = [_dot(decayed_keys_t(t0, kdec_b), v_ref[pl.ds(t0, c_len), :]) for t0 in t0s]
        for t0, u_c in zip(t0s, upd):
            st = st_ref[...]
            acc_ref[pl.ds(t0, c_len), :] = _dot(q_ref[pl.ds(t0, c_len), :], st.astype(BF16)) * qdec_b
            st_ref[...] = st * cdec_b + u_c
        return carry

    lax.fori_loop(0, nchunk // group, bwd_body, 0)
    st_ref[...] = jnp.zeros_like(st_ref)

    def fwd_body(i, carry):
        t0s = [pl.multiple_of((i * group + u) * c_len, c_len) for u in range(group)]
        scs = [_dot_nt(q_ref[pl.ds(t0, c_len), :], k_ref[pl.ds(t0, c_len), :]) * dmat for t0 in t0s]
        upd = [_dot(decayed_keys_t(t0, kdec_f), v_ref[pl.ds(t0, c_len), :]) for t0 in t0s]
        for t0, sc, u_c in zip(t0s, scs, upd):
            st = st_ref[...]
            r = (_dot(sc.astype(BF16), v_ref[pl.ds(t0, c_len), :])
                 + _dot(q_ref[pl.ds(t0, c_len), :], st.astype(BF16)) * qdec_f
                 + acc_ref[pl.ds(t0, c_len), :])
            st_ref[...] = st * cdec_f + u_c
            r = r - jnp.mean(r, axis=-1, keepdims=True)
            r = r * lax.rsqrt(jnp.mean(r * r, axis=-1, keepdims=True) + LN_EPS)
            o_ref[pl.ds(t0, c_len), :] = (g_ref[pl.ds(t0, c_len), :].astype(F32) * r).astype(o_ref.dtype)
        return carry

    lax.fori_loop(0, nchunk // group, fwd_body, 0)


def _retention(hc, lg, batch, seq, c_len, group):
    dk, dv, nh = RET_QK_DIM, RET_V_DIM, RET_HEADS
    v_blk0 = (2 * nh * dk) // dv
    return pl.pallas_call(
        functools.partial(_ret_kernel, c_len=c_len, group=group),
        grid=(batch, nh),
        in_specs=[pl.BlockSpec(memory_space=pltpu.SMEM),
                  pl.BlockSpec((seq, dk), lambda b, hh: (b, hh)),
                  pl.BlockSpec((seq, dk), lambda b, hh: (b, nh + hh)),
                  pl.BlockSpec((seq, dv), lambda b, hh: (b, v_blk0 + hh)),
                  pl.BlockSpec((seq, dv), lambda b, hh: (b, v_blk0 + nh + hh))],
        out_specs=pl.BlockSpec((seq, dv), lambda b, hh: (b, hh)),
        scratch_shapes=[pltpu.VMEM((seq, dv), F32), pltpu.VMEM((dk, dv), F32)],
        out_shape=jax.ShapeDtypeStruct((batch * seq, nh * dv), BF16),
        compiler_params=_cparams(("parallel", "arbitrary")),
        name="retention",
    )(lg, hc, hc, hc, hc)


ROUTER_ROWS = 40


def _router_kernel(x_ref, wt_ref, b_ref, tri_ref, ids_ref, wts_ref, cnt_ref, carry_ref):
    i = pl.program_id(0)
    tm = x_ref.shape[0]

    @pl.when(i == 0)
    def _():
        carry_ref[...] = jnp.zeros_like(carry_ref)

    logits = _dot_nt(wt_ref[...], x_ref[...]) + b_ref[...]
    grow = lax.broadcasted_iota(I32, (SUBLANES, tm), 0).astype(F32)
    gl = jnp.where(grow < N_GROUPS, logits[0:SUBLANES], MASK_VALUE)
    gmax = jnp.max(gl, axis=0, keepdims=True)
    gsum = jnp.sum(jnp.exp(gl - gmax), axis=0, keepdims=True)
    p_group = 1.0 / gsum
    g_idx = jnp.min(jnp.where(gl == gmax, grow, float(N_GROUPS)), axis=0, keepdims=True)
    sel = jnp.zeros((EXPERTS_PER_GROUP, tm), F32)
    for g in range(N_GROUPS):
        r0 = SUBLANES + g * EXPERTS_PER_GROUP
        sel = sel + jnp.where(g_idx == float(g), logits[r0:r0 + EXPERTS_PER_GROUP], 0.0)
    erow = lax.broadcasted_iota(I32, (EXPERTS_PER_GROUP, tm), 0).astype(F32)
    smax = jnp.max(sel, axis=0, keepdims=True)
    sexp = jnp.exp(sel - smax)
    probs = sexp / jnp.sum(sexp, axis=0, keepdims=True)
    p1 = jnp.max(probs, axis=0, keepdims=True)
    i1 = jnp.min(jnp.where(probs == p1, erow, float(EXPERTS_PER_GROUP)), axis=0, keepdims=True)
    rest = jnp.where(erow == i1, -1.0, probs)
    p2 = jnp.max(rest, axis=0, keepdims=True)
    i2 = jnp.min(jnp.where(rest == p2, erow, float(EXPERTS_PER_GROUP)), axis=0, keepdims=True)
    denom = p1 + p2
    e0 = g_idx * EXPERTS_PER_GROUP + i1
    e1 = g_idx * EXPERTS_PER_GROUP + i2

    xrow = lax.broadcasted_iota(I32, (N_EXPERTS, tm), 0).astype(F32)
    oh0 = jnp.where(xrow == e0, 1.0, 0.0)
    oh1 = jnp.where(xrow == e1, 1.0, 0.0)
    onehot = oh0 + oh1
    before = _dot(onehot.astype(BF16), tri_ref[...]) + carry_ref[:, 0:1]
    rank0 = jnp.sum(oh0 * before, axis=0, keepdims=True)
    rank1 = jnp.sum(oh1 * before, axis=0, keepdims=True)
    carry_ref[...] = carry_ref[...] + jnp.sum(onehot, axis=1, keepdims=True)

    ids_ref[...] = jnp.zeros_like(ids_ref)
    ids_ref[0:1, :] = e0.astype(I32)
    ids_ref[1:2, :] = e1.astype(I32)
    ids_ref[2:3, :] = rank0.astype(I32)
    ids_ref[3:4, :] = rank1.astype(I32)
    wts_ref[...] = jnp.zeros_like(wts_ref)
    wts_ref[0:1, :] = p_group * p1 / denom
    wts_ref[1:2, :] = p_group * p2 / denom
    cnt_ref[...] = carry_ref[...]


def _router(x, wt, bias, tm):
    n, d = x.shape
    tri = jnp.asarray(np.triu(np.ones((tm, tm), np.float32), 1), BF16)
    return pl.pallas_call(
        _router_kernel,
        grid=(n // tm,),
        in_specs=[pl.BlockSpec((tm, d), lambda i: (i, 0)),
                  pl.BlockSpec((ROUTER_ROWS, d), lambda i: (0, 0)),
                  pl.BlockSpec((ROUTER_ROWS, 1), lambda i: (0, 0)),
                  pl.BlockSpec((tm, tm), lambda i: (0, 0))],
        out_specs=[pl.BlockSpec((SUBLANES, tm), lambda i: (0, i)),
                   pl.BlockSpec((SUBLANES, tm), lambda i: (0, i)),
                   pl.BlockSpec((N_EXPERTS, LANES), lambda i: (0, 0))],
        out_shape=[jax.ShapeDtypeStruct((SUBLANES, n), I32), jax.ShapeDtypeStruct((SUBLANES, n), F32),
                   jax.ShapeDtypeStruct((N_EXPERTS, LANES), F32)],
        scratch_shapes=[pltpu.VMEM((N_EXPERTS, LANES), F32)],
        compiler_params=_cparams(("arbitrary",)),
        name="moe_router",
    )(x, wt, bias, tri)


def _slots_kernel(ids_ref, cnt_ref, slots_ref, blk_ref, *, nblk_pad):
    tm = ids_ref.shape[1]
    cnt = cnt_ref[:, 0:1]
    padded = jnp.floor((cnt + (MOE_BLOCK - 1)) / MOE_BLOCK) * MOE_BLOCK
    er = lax.broadcasted_iota(I32, (N_EXPERTS, N_EXPERTS), 0)
    ec = lax.broadcasted_iota(I32, (N_EXPERTS, N_EXPERTS), 1)
    padded_row = jnp.sum(jnp.where(er == ec, padded, 0.0), axis=0, keepdims=True)
    p_start = jnp.sum(jnp.where(ec < er, padded_row, 0.0), axis=1, keepdims=True)
    p_end = p_start + padded
    xrow = lax.broadcasted_iota(I32, (N_EXPERTS, tm), 0)
    e0 = ids_ref[0:1, :]
    e1 = ids_ref[1:2, :]
    s0 = jnp.sum(jnp.where(xrow == e0, p_start, 0.0), axis=0, keepdims=True).astype(I32) + ids_ref[2:3, :]
    s1 = jnp.sum(jnp.where(xrow == e1, p_start, 0.0), axis=0, keepdims=True).astype(I32) + ids_ref[3:4, :]
    slots_ref[...] = jnp.zeros_like(slots_ref)
    slots_ref[0:1, :] = s0
    slots_ref[1:2, :] = s1
    slots_ref[2:3, :] = s0 | (s1 << 16)
    bstart = (lax.broadcasted_iota(I32, (1, nblk_pad), 1) * MOE_BLOCK).astype(F32)
    blk_e = jnp.minimum(jnp.sum(jnp.where(p_end <= bstart, 1.0, 0.0), axis=0, keepdims=True), N_EXPERTS - 1.0)
    total = jnp.sum(padded, axis=0, keepdims=True)
    erow = lax.broadcasted_iota(I32, (N_EXPERTS, nblk_pad), 0).astype(F32)
    own_end = jnp.sum(jnp.where(erow == blk_e, p_end, 0.0), axis=0, keepdims=True)
    nxt_e = jnp.minimum(jnp.sum(jnp.where(p_end <= own_end, 1.0, 0.0), axis=0, keepdims=True), N_EXPERTS - 1.0)
    nxt_e = jnp.where(own_end < total, nxt_e, -1.0)
    blk_ref[...] = jnp.zeros_like(blk_ref)
    blk_ref[0:1, :] = blk_e.astype(I32)
    blk_ref[1:2, :] = jnp.broadcast_to((total / MOE_BLOCK).astype(I32), (1, nblk_pad))
    blk_ref[2:3, :] = nxt_e.astype(I32)
    lane = lax.broadcasted_iota(I32, (N_EXPERTS, nblk_pad), 1).astype(F32)
    blk_ref[3:4, :] = jnp.sum(jnp.where(erow == lane, p_start + cnt, 0.0), axis=0, keepdims=True).astype(I32)
    blk_ref[4:5, :] = jnp.sum(jnp.where(erow == lane, p_end, 0.0), axis=0, keepdims=True).astype(I32)


def _slots(ids, cnt, tm, nblk_pad):
    n = ids.shape[1]
    return pl.pallas_call(
        functools.partial(_slots_kernel, nblk_pad=nblk_pad),
        grid=(n // tm,),
        in_specs=[pl.BlockSpec((SUBLANES, tm), lambda i: (0, i)),
                  pl.BlockSpec((N_EXPERTS, LANES), lambda i: (0, 0))],
        out_specs=[pl.BlockSpec((SUBLANES, tm), lambda i: (0, i)),
                   pl.BlockSpec((SUBLANES, nblk_pad), lambda i: (0, 0))],
        out_shape=[jax.ShapeDtypeStruct((SUBLANES, n), I32), jax.ShapeDtypeStruct((SUBLANES, nblk_pad), I32)],
        compiler_params=_cparams(("arbitrary",)),
        name="moe_slots",
    )(ids, cnt)


def _slot_tokens_kernel(slots_ref, blk_ref, tok_ref, *, n, cap, nblk_pad):
    def zero(j, carry):
        tok_ref[j] = 0
        return carry

    def scatter(t, carry):
        both = slots_ref[t]
        tok_ref[both & 0xFFFF] = t
        tok_ref[lax.shift_right_logical(both, 16)] = t
        return carry

    for e in range(N_EXPERTS):
        lax.fori_loop(blk_ref[3 * nblk_pad + e], blk_ref[4 * nblk_pad + e], zero, 0)
    lax.fori_loop(blk_ref[nblk_pad] * MOE_BLOCK, cap, zero, 0)
    lax.fori_loop(0, n, scatter, 0, unroll=8)


def _slot_tokens(slots_packed, blk_flat, n, cap, nblk_pad):
    return pl.pallas_call(
        functools.partial(_slot_tokens_kernel, n=n, cap=cap, nblk_pad=nblk_pad),
        grid_spec=pltpu.PrefetchScalarGridSpec(
            num_scalar_prefetch=2,
            grid=(1,),
            in_specs=[],
            out_specs=pl.BlockSpec(memory_space=pltpu.SMEM)),
        out_shape=jax.ShapeDtypeStruct((cap,), I32),
        compiler_params=_cparams(("arbitrary",)),
        name="moe_slot_tokens",
    )(slots_packed, blk_flat)


ROW_BUFS = 3


def _expert_kernel(blk_ref, tok_ref, xpk_hbm, wg_hbm, wu_hbm, wd_hbm, y_ref, xbuf, wgb, wub, wdb,
                   sems, wsems, cnt_ref, *, nblk_pad):
    i = pl.program_id(0)
    n_used = blk_ref[nblk_pad]

    def weight_copies(e, slot):
        return [pltpu.make_async_copy(src.at[e], dst.at[slot], wsems.at[slot])
                for src, dst in ((wg_hbm, wgb), (wu_hbm, wub), (wd_hbm, wdb))]

    def start_rows(block, buf, r_lo=0, r_hi=MOE_BLOCK):
        for r in range(r_lo, r_hi):
            src0 = pl.multiple_of(tok_ref[block * MOE_BLOCK + r] * SUBLANES, SUBLANES)
            pltpu.make_async_copy(xpk_hbm.at[pl.ds(src0, SUBLANES)],
                                  xbuf.at[buf, pl.ds(r * SUBLANES, SUBLANES)], sems.at[buf]).start(priority=r % 2)

    def wait_rows(buf):
        pltpu.make_async_copy(xpk_hbm.at[pl.ds(0, MOE_BLOCK * SUBLANES)], xbuf.at[buf], sems.at[buf]).wait()

    @pl.when(i == 0)
    def _():
        cnt_ref[0] = 0
        for cp in weight_copies(blk_ref[0], 0):
            cp.start(priority=1)
        for ahead in range(ROW_BUFS - 1):
            start_rows(jnp.minimum(ahead, n_used - 1), ahead)

    @pl.when((i < n_used) & ((i == 0) | (blk_ref[i] != blk_ref[jnp.maximum(i - 1, 0)])))
    def _():
        slot = cnt_ref[0] % 2
        cnt_ref[0] = cnt_ref[0] + 1
        for cp in weight_copies(blk_ref[i], slot):
            cp.wait()
        nxt_e = blk_ref[2 * nblk_pad + i]

        @pl.when(nxt_e >= 0)
        def _():
            for cp in weight_copies(nxt_e, 1 - slot):
                cp.start(priority=1)

    @pl.when(i < n_used)
    def _():
        wslot = (cnt_ref[0] + 1) % 2
        buf = i % ROW_BUFS
        nbuf = (i + ROW_BUFS - 1) % ROW_BUFS
        nxt = jnp.minimum(i + ROW_BUFS - 1, n_used - 1)
        wait_rows(buf)
        lo, hi = _unpack_halves(_load_row_tiles(xbuf.at[buf], MOE_BLOCK))
        xb = jnp.concatenate([lo.astype(BF16), hi.astype(BF16)], axis=-1)
        g = _dot(xb, wgb[wslot])
        start_rows(nxt, nbuf, 0, MOE_BLOCK // 2)
        u = _dot(xb, wub[wslot])
        start_rows(nxt, nbuf, MOE_BLOCK // 2, MOE_BLOCK)
        hmid = (g * jax.nn.sigmoid(g) * u).astype(BF16)
        _store_row_tiles(y_ref, _pack_halves(_dot(hmid, wdb[wslot])))

    @pl.when(i == n_used - 1)
    def _():
        for ahead in range(1, ROW_BUFS):
            wait_rows((i + ahead) % ROW_BUFS)

    @pl.when(i >= n_used)
    def _():
        y_ref[...] = jnp.zeros_like(y_ref)


def _experts(blk_flat, slot_tok, xpk, wg, wu, wd, nblk_pad):
    d, de = wg.shape[1], wg.shape[2]
    assert d == 2 * SUBLANES * LANES and xpk.shape[1] == LANES
    cap = slot_tok.shape[0]
    tile_rows = MOE_BLOCK * SUBLANES
    return pl.pallas_call(
        functools.partial(_expert_kernel, nblk_pad=nblk_pad),
        grid_spec=pltpu.PrefetchScalarGridSpec(
            num_scalar_prefetch=2,
            grid=(cap // MOE_BLOCK,),
            in_specs=[pl.BlockSpec(memory_space=pl.ANY), pl.BlockSpec(memory_space=pl.ANY),
                      pl.BlockSpec(memory_space=pl.ANY), pl.BlockSpec(memory_space=pl.ANY)],
            out_specs=pl.BlockSpec((tile_rows, LANES), lambda i, blk, tok: (i, 0)),
            scratch_shapes=[pltpu.VMEM((ROW_BUFS, tile_rows, LANES), U32),
                            pltpu.VMEM((2, d, de), BF16), pltpu.VMEM((2, d, de), BF16), pltpu.VMEM((2, de, d), BF16),
                            pltpu.SemaphoreType.DMA((ROW_BUFS,)), pltpu.SemaphoreType.DMA((2,)),
                            pltpu.SMEM((1,), I32)]),
        out_shape=jax.ShapeDtypeStruct((cap * SUBLANES, LANES), U32),
        compiler_params=_cparams(("arbitrary",)),
        name="moe_experts",
    )(blk_flat, slot_tok, xpk, wg, wu, wd)


def _tail_kernel(slots_ref, x_ref, xb_ref, p_ref, wts_ref, wgate_ref, bgate_ref, wproj_ref, g_ref, b_ref,
                 yb_hbm, y_ref, ybf_ref, rows_ref, sems):
    i = pl.program_id(0)
    nsteps = pl.num_programs(0)
    tm = x_ref.shape[0]
    n = nsteps * tm

    def start_rows(step, buf):
        for t in range(tm):
            for which in range(2):
                src0 = pl.multiple_of(slots_ref[which * n + step * tm + t] * SUBLANES, SUBLANES)
                pltpu.make_async_copy(yb_hbm.at[pl.ds(src0, SUBLANES)],
                                      rows_ref.at[buf, which, pl.ds(t * SUBLANES, SUBLANES)],
                                      sems.at[buf]).start(priority=which)

    def wait_rows(buf):
        for which in range(2):
            pltpu.make_async_copy(yb_hbm.at[pl.ds(0, tm * SUBLANES)], rows_ref.at[buf, which],
                                  sems.at[buf]).wait()

    @pl.when(i == 0)
    def _():
        start_rows(0, 0)

    buf = i % 2
    nxt = jnp.minimum(i + 1, nsteps - 1)
    wait_rows(buf)
    gate_pre = _dot(xb_ref[...], wgate_ref[...])
    proj = _dot(p_ref[0].astype(BF16), wproj_ref[...])
    start_rows(nxt, 1 - buf)
    ple = jax.nn.sigmoid(gate_pre + bgate_ref[...]) * proj
    w = wts_ref[...]
    lo0, hi0 = _unpack_halves(_load_row_tiles(rows_ref.at[buf, 0], tm))
    lo1, hi1 = _unpack_halves(_load_row_tiles(rows_ref.at[buf, 1], tm))
    w0 = w[:, 0:1]
    w1 = w[:, 1:2]
    ffn = jnp.concatenate([lo0 * w0 + lo1 * w1, hi0 * w0 + hi1 * w1], axis=-1)
    z = DN_ALPHA * x_ref[...] + ffn + ple
    y = _layer_norm_rows(z, g_ref[...], b_ref[...])
    y_ref[...] = y
    ybf_ref[...] = y.astype(BF16)

    @pl.when(i == nsteps - 1)
    def _():
        wait_rows(1 - buf)


def _layer_tail(slots_flat, x, xb, p, layer, wts_t, wgate, bgate, wproj, g, b, yb, tm):
    n, d = x.shape
    pd = p.shape[2]
    return pl.pallas_call(
        _tail_kernel,
        grid_spec=pltpu.PrefetchScalarGridSpec(
            num_scalar_prefetch=1,
            grid=(n // tm,),
            in_specs=[pl.BlockSpec((tm, d), lambda i, s: (i, 0)),
                      pl.BlockSpec((tm, d), lambda i, s: (i, 0)),
                      pl.BlockSpec((1, tm, pd), lambda i, s: (layer, i, 0)),
                      pl.BlockSpec((tm, 2), lambda i, s: (i, 0)),
                      pl.BlockSpec((d, d), lambda i, s: (0, 0)),
                      pl.BlockSpec((1, d), lambda i, s: (0, 0)),
                      pl.BlockSpec((pd, d), lambda i, s: (0, 0)),
                      pl.BlockSpec((1, d), lambda i, s: (0, 0)),
                      pl.BlockSpec((1, d), lambda i, s: (0, 0)),
                      pl.BlockSpec(memory_space=pl.ANY)],
            out_specs=[pl.BlockSpec((tm, d), lambda i, s: (i, 0)),
                       pl.BlockSpec((tm, d), lambda i, s: (i, 0))],
            scratch_shapes=[pltpu.VMEM((2, 2, tm * SUBLANES, LANES), U32), pltpu.SemaphoreType.DMA((2,))]),
        out_shape=[jax.ShapeDtypeStruct((n, d), F32), jax.ShapeDtypeStruct((n, d), BF16)],
        compiler_params=_cparams(("arbitrary",)),
        name="layer_tail",
    )(slots_flat, x, xb, p, wts_t, wgate, bgate.reshape(1, d), wproj, g.reshape(1, d), b.reshape(1, d), yb)


def _rope_table(seq, dim):
    pos = jnp.arange(seq, dtype=F32)
    inv = jnp.exp(jnp.arange(0, dim, 2, dtype=F32) * (-math.log(ROPE_BASE) / dim))
    ang = pos[:, None] * inv[None, :]
    return jnp.cos(ang), jnp.sin(ang)


def _moe_layer(x, xb, xpk, p, layer, w_group, b_group, w_router, b_router, w_gate, w_up, w_down,
               ple_w_proj, ple_w_gate, ple_b_gate, ln_g, ln_b):
    n, d = x.shape
    nblk = -(-(2 * n) // MOE_BLOCK) + N_EXPERTS
    nblk_pad = -(-nblk // LANES) * LANES
    cap = nblk * MOE_BLOCK
    wt = jnp.zeros((ROUTER_ROWS, d), F32)
    wt = wt.at[0:N_GROUPS].set(w_group.T)
    wt = wt.at[SUBLANES:].set(w_router.transpose(0, 2, 1).reshape(N_EXPERTS, d))
    bias = jnp.zeros((ROUTER_ROWS, 1), F32)
    bias = bias.at[0:N_GROUPS, 0].set(b_group)
    bias = bias.at[SUBLANES:, 0].set(b_router.reshape(N_EXPERTS))
    ids, wts, cnt = _router(xb, wt.astype(BF16), bias, tm=512)
    slots, blk = _slots(ids, cnt, tm=min(2048, n), nblk_pad=nblk_pad)
    slots_flat = slots[0:2].reshape(2 * n)
    blk_flat = blk[0:5].reshape(5 * nblk_pad)
    assert cap <= 1 << 16
    slot_tok = _slot_tokens(slots[2], blk_flat, n, cap, nblk_pad)
    yb = _experts(blk_flat, slot_tok, xpk, w_gate, w_up, w_down, nblk_pad)
    return _layer_tail(slots_flat, x, xb, p, layer, wts[0:2].T, ple_w_gate, ple_b_gate, ple_w_proj, ln_g, ln_b,
                       yb, tm=512)


def _mixer_ab(x, xres, batch, seq, w_in, rpb, q_norm, w_uq, kv_norm, w_ukv, w_out_all, j, ln_g, ln_b,
              expert_ws, layer, dense_early, dense_late):
    d = x.shape[1]
    o1 = 3 * NA_WIDTH
    o2 = o1 + MLA_Q_RANK
    o3 = o2 + MLA_KV_RANK
    half = MLA_ROPE_DIM // 2
    kr = w_in[:, o3:o3 + MLA_ROPE_DIM]
    kr_sw = jnp.concatenate([kr[:, half:], kr[:, :half]], axis=1)
    width = -(-(o3 + 2 * MLA_ROPE_DIM) // 1024) * 1024
    w_in_p = jnp.concatenate([w_in, kr_sw, jnp.zeros((d, width - o3 - 2 * MLA_ROPE_DIM), F32)], axis=1)
    h, early_b = _matmul(x, w_in_p.astype(BF16), BF16, tm=1024, tn=1024, cast_ws=dense_early)
    a_out = _na_attention(h, _na_bias_tables(rpb), batch, seq)
    dq = MLA_NOPE_DIM + MLA_ROPE_DIM
    wq = w_uq.reshape(MLA_Q_RANK, MLA_HEADS, dq)
    wq_pe = wq[:, :, MLA_NOPE_DIM:]
    wq_p = jnp.concatenate([wq, wq_pe[:, :, half:], wq_pe[:, :, :half]], axis=2)
    wq_p = wq_p.reshape(MLA_Q_RANK, MLA_HEADS * 2 * LANES).astype(BF16)
    wkv = w_ukv.reshape(MLA_KV_RANK, MLA_HEADS, MLA_NOPE_DIM + MLA_V_DIM)
    wk = wkv[:, :, :MLA_NOPE_DIM].reshape(MLA_KV_RANK, MLA_HEADS * MLA_NOPE_DIM).astype(BF16)
    wvt = wkv[:, :, MLA_NOPE_DIM:].reshape(MLA_KV_RANK, MLA_HEADS * MLA_V_DIM).T.astype(BF16)
    cos, sin = _rope_table(seq, MLA_ROPE_DIM)
    zpad = jnp.zeros((seq, LANES - MLA_ROPE_DIM), F32)
    cosf = jnp.concatenate([cos, cos, zpad], axis=1)
    sinf = jnp.concatenate([-sin, sin, zpad], axis=1)
    q_p, k_p, vt = _mla_prep(h, q_norm, kv_norm, wq_p, wk, wvt, cosf, sinf, o1, seq, tm=512)
    b_out, cast_out = _mla_attention(q_p, k_p, vt, batch, seq, tq=min(1024, seq), tk=1024, sub=256,
                                     cast_ws=expert_ws, cast_layer=layer,
                                     cast_dense=[(w_out_all, j)] + list(dense_late))
    expert_wb, w_out_b, late_b = cast_out[:len(expert_ws)], cast_out[len(expert_ws)], cast_out[len(expert_ws) + 1:]
    outs = _proj_ln([a_out, b_out], [w_out_b[:NA_WIDTH], w_out_b[NA_WIDTH:]], xres, ln_g, ln_b, tm=512, nk=1)
    return outs, expert_wb, early_b, late_b


def _mixer_c(xb, xres, batch, seq, w_in, log_rate_f, log_rate_b, w_out, ln_g, ln_b, expert_ws, layer):
    cosr, sinr = _rope_table(seq, RET_QK_DIM)
    n_q = RET_HEADS * RET_QK_DIM
    hc, expert_wb = _matmul_rope(xb, w_in, cosr, sinr, BF16, tm=min(1024, seq), tn=1024,
                                 n_q_cols=n_q, n_rope_cols=2 * n_q, gate_col0=2 * n_q + RET_HEADS * RET_V_DIM,
                                 head_w=RET_QK_DIM, q_scale=RET_QK_DIM ** -0.5,
                                 cast_ws=expert_ws, cast_layer=layer, cast_j=8)
    lg = jnp.stack([jnp.log1p(-jnp.exp(log_rate_f.astype(F32))), jnp.log1p(-jnp.exp(log_rate_b.astype(F32)))])
    r = _retention(hc, lg, batch, seq, c_len=256, group=min(8, seq // 256))
    return _proj_ln([r], [w_out], xres, ln_g, ln_b, tm=512, nk=1), expert_wb


def kernel(x, p, ab_w_in, ab_rpb, ab_q_norm, ab_w_uq, ab_kv_norm, ab_w_ukv, ab_w_out, c_w_in, c_log_rate_f,
           c_log_rate_b, c_w_out, ln1_g, ln1_b, moe_w_group, moe_b_group, moe_w_router, moe_b_router,
           moe_w_gate, moe_w_up, moe_w_down, ple_w_proj, ple_w_gate, ple_b_gate, ln2_g, ln2_b):
    batch, seq, d = x.shape
    n = batch * seq
    xf = x.reshape(n, d)
    p_flat = p.reshape(DEPTH, n, -1)
    expert_ws = (moe_w_gate, moe_w_up, moe_w_down)
    assert DEPTH == 2
    dense_late = [(c_w_out, 0)] + [(w, i) for i in range(DEPTH) for w in (ple_w_gate, ple_w_proj)]
    (xf, xb, xpk), (wg, wu, wd), (c_w_in_b,), late_b = _mixer_ab(
        xf, xf, batch, seq, ab_w_in[0], ab_rpb[0], ab_q_norm[0], ab_w_uq[0], ab_kv_norm[0], ab_w_ukv[0],
        ab_w_out, 0, ln1_g[0], ln1_b[0], expert_ws, 0, dense_early=[(c_w_in, 0)], dense_late=dense_late)
    c_w_out_b, ple_b = late_b[0], late_b[1:]
    xf, xb = _moe_layer(xf, xb, xpk, p_flat, 0, moe_w_group[0], moe_b_group[0], moe_w_router[0],
                        moe_b_router[0], wg, wu, wd, ple_b[1], ple_b[0], ple_b_gate[0], ln2_g[0], ln2_b[0])
    (xf, xb, xpk), (wg, wu, wd) = _mixer_c(
        xb, xf, batch, seq, c_w_in_b, c_log_rate_f[0], c_log_rate_b[0], c_w_out_b, ln1_g[1], ln1_b[1],
        expert_ws, 1)
    xf, xb = _moe_layer(xf, xb, xpk, p_flat, 1, moe_w_group[1], moe_b_group[1], moe_w_router[1],
                        moe_b_router[1], wg, wu, wd, ple_b[3], ple_b[2], ple_b_gate[1], ln2_g[1], ln2_b[1])
    return xf.reshape(batch, seq, d)
```
